```python
import math
import jax
import jax.numpy as jnp
from jax import lax
import numpy as np

D_MODEL = 1024
BATCH = 4
SEQ = 4096
DEPTH = 2

CTX_LEN = 256
GRID_W = 64
NORM_EPS = 1e-6
ROPE_THETA = 10000.0

GDN_HEADS = 8
GDN_DK = 64
GDN_DV = 64
GDN_CHUNK = 64
CONV_K = 5

SWA_Q_HEADS = 8
SWA_KV_HEADS = 2
SWA_HEAD_DIM = 64
SWA_WINDOW = 128
SWA_BLOCK = 128

MLA_HEADS = 8
MLA_Q_RANK = 384
MLA_KV_RANK = 256
MLA_NOPE = 64
MLA_ROPE = 32
MLA_V = 64

DIFF_HEADS = 4
DIFF_HEAD_DIM = 64

Q_BLOCK = 128

D_FF = 2816
N_EXPERTS = 8
TOP_K = 2
D_FF_EXPERT = 3584

N_EVEN = (DEPTH + 1) // 2
N_ODD = DEPTH // 2

GDN_QKV = GDN_HEADS * (2 * GDN_DK + GDN_DV)
GDN_Z = GDN_HEADS * GDN_DV
SWA_Q = SWA_Q_HEADS * SWA_HEAD_DIM
SWA_KV = SWA_KV_HEADS * SWA_HEAD_DIM
EVEN_COLS = (GDN_QKV, GDN_Z, 2 * GDN_HEADS, 2 * GDN_HEADS, SWA_Q, SWA_KV, SWA_KV)
EVEN_IN = sum(EVEN_COLS)
EVEN_MIX = GDN_Z + SWA_Q

DIFF_QK = DIFF_HEADS * 2 * DIFF_HEAD_DIM
DIFF_VW = DIFF_HEADS * 2 * DIFF_HEAD_DIM
ODD_COLS = (MLA_Q_RANK, MLA_KV_RANK, MLA_ROPE, DIFF_QK, DIFF_QK, DIFF_VW)
ODD_IN = sum(ODD_COLS)
ODD_MIX = MLA_HEADS * MLA_V + DIFF_VW

kernel_name = 'hybrid_diffusion_trunk'


def split_cols(a, sizes):
    return jnp.split(a, np.cumsum(sizes)[:-1].tolist(), axis=-1)


def rmsnorm(x, w):
    xf = x.astype(jnp.float32)
    y = xf * lax.rsqrt(jnp.mean(xf * xf, axis=-1, keepdims=True) + NORM_EPS)
    return (y * w.astype(jnp.float32)).astype(x.dtype)


def l2norm(x):
    xf = x.astype(jnp.float32)
    return xf * lax.rsqrt(jnp.sum(xf * xf, axis=-1, keepdims=True) + NORM_EPS)


def modulate(h, shift, scale):
    return h * (1 + scale) + shift


def axial_rope_tables(rows, rot_dim):
    row = jnp.repeat(jnp.arange(rows, dtype=jnp.float32), GRID_W)
    col = jnp.tile(jnp.arange(GRID_W, dtype=jnp.float32), rows)
    half = rot_dim // 2
    inv_freq = ROPE_THETA ** (-jnp.arange(0, half, 2, dtype=jnp.float32) / half)
    ang_r = row[:, None] * inv_freq[None, :]
    ang_c = col[:, None] * inv_freq[None, :]
    ang = jnp.concatenate([ang_r, ang_r, ang_c, ang_c], axis=-1)
    return jnp.cos(ang), jnp.sin(ang)


def rotate_half(x):
    x1, x2 = jnp.split(x, 2, axis=-1)
    return jnp.concatenate([-x2, x1], axis=-1)


def apply_axial_rope(x, cos, sin):
    half = x.shape[-1] // 2
    rot = jnp.concatenate([rotate_half(x[..., :half]), rotate_half(x[..., half:])], axis=-1)
    return x * cos[None, :, None, :].astype(x.dtype) + rot * sin[None, :, None, :].astype(x.dtype)


def centred_dwconv(x, w):
    pad = CONV_K // 2
    return lax.conv_general_dilated(x, w[:, None, :].astype(x.dtype), window_strides=(1,),
                                    padding=[(pad, pad)], dimension_numbers=('NWC', 'WIO', 'NWC'),
                                    feature_group_count=x.shape[-1])


def gated_delta_chunked(q, k, v, g, beta, state0):
    B, T, H, Dk = q.shape
    Dv = v.shape[-1]
    C = GDN_CHUNK
    N = T // C

    def to_chunks(a):
        a = a.reshape(B, N, C, H, *a.shape[3:])
        return jnp.moveaxis(a, (1, 3), (0, 2))

    qc, kc, vc = to_chunks(q), to_chunks(k), to_chunks(v)
    gc = jnp.cumsum(to_chunks(g), axis=-1)
    bc = to_chunks(beta)
    idx = jnp.arange(C)
    lower_incl = idx[:, None] >= idx[None, :]
    strict = idx[:, None] > idx[None, :]
    decay = jnp.exp(jnp.where(lower_incl, gc[..., :, None] - gc[..., None, :], -jnp.inf))
    kb = kc * bc[..., None]
    vb = vc * bc[..., None]
    L = jnp.where(strict, jnp.einsum('nbhid,nbhjd->nbhij', kb, kc) * decay, 0.0)
    eye = jnp.eye(C, dtype=jnp.float32)
    t_inv = lax.linalg.triangular_solve(eye + L, jnp.broadcast_to(eye, L.shape), left_side=True, lower=True)
    u = t_inv @ vb
    w = t_inv @ (kb * jnp.exp(gc)[..., None])
    attn_intra = jnp.where(lower_incl, jnp.einsum('nbhid,nbhjd->nbhij', qc, kc) * decay, 0.0)
    g_last = gc[..., -1]
    k_tail = kc * jnp.exp(g_last[..., None] - gc)[..., None]
    q_head = qc * jnp.exp(gc)[..., None]

    def step(S, inp):
        q_h, k_t, u_i, w_i, a_i, gl = inp
        v_new = u_i - w_i @ S
        o = q_h @ S + a_i @ v_new
        S = S * jnp.exp(gl)[..., None, None] + jnp.swapaxes(k_t, -1, -2) @ v_new
        return S, o

    s_final, o = lax.scan(step, state0, (q_head, k_tail, u, w, attn_intra, g_last))
    o = jnp.moveaxis(o, (0, 2), (1, 3)).reshape(B, T, H, Dv)
    return o, s_final


def window_gqa_latent(q, k, v, k_ctx, v_ctx, sink):
    B, T, Hq, D = q.shape
    Hk = k.shape[2]
    G = Hq // Hk
    W = SWA_BLOCK
    nb = T // W
    Tc = k_ctx.shape[1]
    scale = D ** -0.5
    qb = q.reshape(B, nb, W, Hk, G, D)

    def band(a):
        ap = jnp.pad(a, ((0, 0), (W, W), (0, 0), (0, 0))).reshape(B, nb + 2, W, Hk, D)
        return jnp.concatenate([ap[:, :-2], ap[:, 1:-1], ap[:, 2:]], axis=2)

    k_win, v_win = band(k), band(v)
    i_abs = jnp.arange(nb)[:, None] * W + jnp.arange(W)[None, :]
    j_abs = jnp.arange(nb)[:, None] * W - W + jnp.arange(3 * W)[None, :]
    valid = ((jnp.abs(i_abs[:, :, None] - j_abs[:, None, :]) <= SWA_WINDOW)
             & (j_abs >= 0)[:, None, :] & (j_abs < T)[:, None, :])
    s_win = jnp.einsum('bnqhgd,bnkhd->bnhgqk', qb, k_win).astype(jnp.float32) * scale
    s_win = jnp.where(valid[None, :, None, None], s_win, -jnp.inf)
    s_ctx = jnp.einsum('bnqhgd,bchd->bnhgqc', qb, k_ctx).astype(jnp.float32) * scale
    s_sink = jnp.broadcast_to(sink.astype(jnp.float32).reshape(1, 1, Hk, G, 1, 1), s_ctx.shape[:-1] + (1,))
    p = jax.nn.softmax(jnp.concatenate([s_sink, s_ctx, s_win], axis=-1), axis=-1).astype(v.dtype)
    o = (jnp.einsum('bnhgqc,bchd->bnqhgd', p[..., 1:1 + Tc], v_ctx)
         + jnp.einsum('bnhgqk,bnkhd->bnqhgd', p[..., 1 + Tc:], v_win))
    return o.reshape(B, T, Hq * D)


def gqa_sink_dense(q, k, v, sink):
    B, Tq, Hq, D = q.shape
    Hk = k.shape[2]
    G = Hq // Hk
    qg = q.reshape(B, Tq, Hk, G, D)
    s = jnp.einsum('bqhgd,bkhd->bhgqk', qg, k).astype(jnp.float32) * D ** -0.5
    s_sink = jnp.broadcast_to(sink.astype(jnp.float32).reshape(1, Hk, G, 1, 1), s.shape[:-1] + (1,))
    p = jax.nn.softmax(jnp.concatenate([s_sink, s], axis=-1), axis=-1)[..., 1:].astype(v.dtype)
    o = jnp.einsum('bhgqk,bkhd->bqhgd', p, v)
    return o.reshape(B, Tq, Hq * D)


def sweep_query_blocks(fn, *qs):
    B, T = qs[0].shape[:2]
    nb = T // Q_BLOCK
    blocks = tuple(jnp.moveaxis(a.reshape(B, nb, Q_BLOCK, *a.shape[2:]), 1, 0) for a in qs)
    out = lax.map(lambda xs: fn(*xs), blocks)
    return jnp.moveaxis(out, 0, 1).reshape(B, T, *out.shape[3:])


def mla_attend(q_nope, q_rope, k_nope, k_rope, v):
    scale = (MLA_NOPE + MLA_ROPE) ** -0.5
    s = (jnp.einsum('bqhd,bshd->bhqs', q_nope, k_nope)
         + jnp.einsum('bqhr,bsr->bhqs', q_rope, k_rope)).astype(jnp.float32) * scale
    p = jax.nn.softmax(s, axis=-1).astype(v.dtype)
    return jnp.einsum('bhqs,bshd->bqhd', p, v)


def diff_attend(q1, q2, k1, k2, v, lam):
    scale = DIFF_HEAD_DIM ** -0.5
    s1 = jnp.einsum('bqhd,bshd->bhqs', q1, k1).astype(jnp.float32) * scale
    s2 = jnp.einsum('bqhd,bshd->bhqs', q2, k2).astype(jnp.float32) * scale
    a = jax.nn.softmax(s1, axis=-1) - lam * jax.nn.softmax(s2, axis=-1)
    return jnp.einsum('bhqs,bshe->bqhe', a.astype(v.dtype), v)


def even_mixer(h_c, h_l, w_in, conv_w, a_log, dt_bias, gdn_norm, sink, w_out, cos, sin, need_ctx):
    B, Tc, _ = h_c.shape
    T = h_l.shape[1]
    n_all = Tc + T
    proj = jnp.concatenate([h_c, h_l], axis=1) @ w_in
    qkv, z, ga, gb, sq, sk, sv = split_cols(proj, EVEN_COLS)

    g = -jnp.exp(a_log.astype(jnp.float32)) * jax.nn.softplus(
        ga.astype(jnp.float32).reshape(B, n_all, 2, GDN_HEADS) + dt_bias.astype(jnp.float32))
    beta = jax.nn.sigmoid(gb.astype(jnp.float32).reshape(B, n_all, 2, GDN_HEADS))

    def qkv_heads(u):
        n = u.shape[1]
        u = jax.nn.silu(centred_dwconv(u, conv_w))
        q, k, v = split_cols(u, (GDN_HEADS * GDN_DK, GDN_HEADS * GDN_DK, GDN_HEADS * GDN_DV))
        q = l2norm(q.reshape(B, n, GDN_HEADS, GDN_DK)) * GDN_DK ** -0.5
        k = l2norm(k.reshape(B, n, GDN_HEADS, GDN_DK))
        v = v.reshape(B, n, GDN_HEADS, GDN_DV).astype(jnp.float32)
        return q, k, v

    qc, kc, vc = qkv_heads(qkv[:, :Tc])
    ql, kl, vl = qkv_heads(qkv[:, Tc:])
    g_c, g_l = g[:, :Tc], g[:, Tc:]
    b_c, b_l = beta[:, :Tc], beta[:, Tc:]
    s0 = jnp.zeros((B, GDN_HEADS, GDN_DK, GDN_DV), jnp.float32)
    rev = lambda t: jnp.flip(t, axis=1)
    oc_f, s_f = gated_delta_chunked(qc, kc, vc, g_c[:, :, 0], b_c[:, :, 0], s0)
    oc_b, s_b = gated_delta_chunked(rev(qc), rev(kc), rev(vc), rev(g_c[:, :, 1]), rev(b_c[:, :, 1]), s0)
    ol_f, _ = gated_delta_chunked(ql, kl, vl, g_l[:, :, 0], b_l[:, :, 0], s_f)
    ol_b, _ = gated_delta_chunked(rev(ql), rev(kl), rev(vl), rev(g_l[:, :, 1]), rev(b_l[:, :, 1]), s_b)

    def gdn_out(o, zz):
        n = o.shape[1]
        o = rmsnorm(o, gdn_norm).astype(zz.dtype).reshape(B, n, GDN_Z)
        return o * jax.nn.silu(zz)

    lat_gdn = gdn_out(ol_f + rev(ol_b), z[:, Tc:])

    sq_c = sq[:, :Tc].reshape(B, Tc, SWA_Q_HEADS, SWA_HEAD_DIM)
    sk_c = sk[:, :Tc].reshape(B, Tc, SWA_KV_HEADS, SWA_HEAD_DIM)
    sv_c = sv[:, :Tc].reshape(B, Tc, SWA_KV_HEADS, SWA_HEAD_DIM)
    sq_l = apply_axial_rope(sq[:, Tc:].reshape(B, T, SWA_Q_HEADS, SWA_HEAD_DIM), cos, sin)
    sk_l = apply_axial_rope(sk[:, Tc:].reshape(B, T, SWA_KV_HEADS, SWA_HEAD_DIM), cos, sin)
    sv_l = sv[:, Tc:].reshape(B, T, SWA_KV_HEADS, SWA_HEAD_DIM)
    lat_swa = window_gqa_latent(sq_l, sk_l, sv_l, sk_c, sv_c, sink)

    y_l = jnp.concatenate([lat_gdn, lat_swa], axis=-1) @ w_out
    y_c = None
    if need_ctx:
        ctx_gdn = gdn_out(oc_f + rev(oc_b), z[:, :Tc])
        ctx_swa = gqa_sink_dense(sq_c, sk_c, sv_c, sink)
        y_c = jnp.concatenate([ctx_gdn, ctx_swa], axis=-1) @ w_out
    return y_c, y_l


def odd_mixer(h_c, h_l, w_in, q_norm, kv_norm, w_uq, w_ukv, lam_p, lam_init, subln, w_out,
              cos_d, sin_d, cos_r, sin_r, need_ctx):
    B, Tc, _ = h_c.shape
    T = h_l.shape[1]
    n_all = Tc + T
    proj = jnp.concatenate([h_c, h_l], axis=1) @ w_in
    cq, ckv, kr, dq, dk, dv = split_cols(proj, ODD_COLS)

    q_m = (rmsnorm(cq, q_norm) @ w_uq).reshape(B, n_all, MLA_HEADS, MLA_NOPE + MLA_ROPE)
    kv_m = (rmsnorm(ckv, kv_norm) @ w_ukv).reshape(B, n_all, MLA_HEADS, MLA_NOPE + MLA_V)
    qn, qr = q_m[..., :MLA_NOPE], q_m[..., MLA_NOPE:]
    kn, vm = kv_m[..., :MLA_NOPE], kv_m[..., MLA_NOPE:]
    qr_l = apply_axial_rope(qr[:, Tc:], cos_r, sin_r)
    kr_l = apply_axial_rope(kr[:, Tc:, None, :], cos_r, sin_r)[:, :, 0]
    kr_all = jnp.concatenate([kr[:, :Tc], kr_l], axis=1)
    mla_l = sweep_query_blocks(lambda a, b: mla_attend(a, b, kn, kr_all, vm), qn[:, Tc:], qr_l)
    mla_l = mla_l.reshape(B, T, MLA_HEADS * MLA_V)

    dq = dq.reshape(B, n_all, 2 * DIFF_HEADS, DIFF_HEAD_DIM)
    dk = dk.reshape(B, n_all, 2 * DIFF_HEADS, DIFF_HEAD_DIM)
    dv = dv.reshape(B, n_all, DIFF_HEADS, 2 * DIFF_HEAD_DIM)
    dq_l = apply_axial_rope(dq[:, Tc:], cos_d, sin_d).reshape(B, T, DIFF_HEADS, 2, DIFF_HEAD_DIM)
    dk_l = apply_axial_rope(dk[:, Tc:], cos_d, sin_d)
    dk_all = jnp.concatenate([dk[:, :Tc], dk_l], axis=1).reshape(B, n_all, DIFF_HEADS, 2, DIFF_HEAD_DIM)
    k1, k2 = dk_all[..., 0, :], dk_all[..., 1, :]
    lp = lam_p.astype(jnp.float32)
    lam = jnp.exp(jnp.sum(lp[0] * lp[1])) - jnp.exp(jnp.sum(lp[2] * lp[3])) + lam_init

    def diff_post(o):
        n = o.shape[1]
        return (rmsnorm(o, subln) * (1.0 - lam_init)).reshape(B, n, DIFF_VW)

    diff_l = sweep_query_blocks(lambda a, b: diff_attend(a, b, k1, k2, dv, lam),
                                dq_l[..., 0, :], dq_l[..., 1, :])
    y_l = jnp.concatenate([mla_l, diff_post(diff_l)], axis=-1) @ w_out
    y_c = None
    if need_ctx:
        mla_c = mla_attend(qn[:, :Tc], qr[:, :Tc], kn[:, :Tc], kr[:, :Tc], vm[:, :Tc])
        dq_c = dq[:, :Tc].reshape(B, Tc, DIFF_HEADS, 2, DIFF_HEAD_DIM)
        diff_c = diff_attend(dq_c[..., 0, :], dq_c[..., 1, :], k1[:, :Tc], k2[:, :Tc], dv[:, :Tc], lam)
        y_c = jnp.concatenate([mla_c.reshape(B, Tc, MLA_HEADS * MLA_V), diff_post(diff_c)], axis=-1) @ w_out
    return y_c, y_l


def swiglu(h, w_gate, w_up, w_down):
    return (jax.nn.silu(h @ w_gate) * (h @ w_up)) @ w_down


def moe_swiglu(h, w_router, w_gate, w_up, w_down):
    logits = (h @ w_router).astype(jnp.float32)
    top_val, top_idx = lax.top_k(logits, TOP_K)
    top_w = jax.nn.softmax(top_val, axis=-1)
    gates = jnp.sum(jax.nn.one_hot(top_idx, N_EXPERTS, dtype=jnp.float32) * top_w[..., None], axis=-2)
    gates = gates.astype(h.dtype)
    out = jnp.zeros_like(h)
    for e in range(N_EXPERTS):
        out = out + gates[:, e:e + 1] * swiglu(h, w_gate[e], w_up[e], w_down[e])
    return out


def setup_inputs(seed: int = 0) -> dict:
    key = jax.random.key(seed)
    keys = list(jax.random.split(key, 40))
    D = D_MODEL

    def nrm(shape, scale):
        return jax.random.normal(keys.pop(), shape, jnp.float32) * scale

    def gain(shape):
        return 1.0 + nrm(shape, 0.02)

    inp = {}
    inp['x'] = nrm((BATCH, SEQ, D), 1.0)
    inp['c'] = nrm((BATCH, D), 1.0)
    inp['ctx'] = nrm((BATCH, CTX_LEN, D), 1.0)
    inp['c_ctx'] = nrm((D,), 1.0)
    inp['e_mod_w'] = nrm((N_EVEN, D, 6 * D), D ** -0.5)
    inp['e_mod_b'] = nrm((N_EVEN, 6 * D), 0.02)
    inp['e_norms'] = gain((N_EVEN, 4, D))
    inp['e_w_in'] = nrm((N_EVEN, D, EVEN_IN), D ** -0.5)
    inp['e_conv_w'] = nrm((N_EVEN, CONV_K, GDN_QKV), CONV_K ** -0.5)
    inp['e_a_log'] = jnp.log(jax.random.uniform(keys.pop(), (N_EVEN, 2, GDN_HEADS), jnp.float32, 1.0, 16.0))
    inp['e_dt_bias'] = 1.0 + nrm((N_EVEN, 2, GDN_HEADS), 0.1)
    inp['e_gdn_norm'] = gain((N_EVEN, GDN_DV))
    inp['e_sink'] = nrm((N_EVEN, SWA_Q_HEADS), 0.5)
    inp['e_w_out'] = nrm((N_EVEN, EVEN_MIX, D), EVEN_MIX ** -0.5)
    inp['e_ffn_gate'] = nrm((N_EVEN, D, D_FF), D ** -0.5)
    inp['e_ffn_up'] = nrm((N_EVEN, D, D_FF), D ** -0.5)
    inp['e_ffn_down'] = nrm((N_EVEN, D_FF, D), D_FF ** -0.5)
    inp['o_mod_w'] = nrm((N_ODD, D, 6 * D), D ** -0.5)
    inp['o_mod_b'] = nrm((N_ODD, 6 * D), 0.02)
    inp['o_norms'] = gain((N_ODD, 4, D))
    inp['o_w_in'] = nrm((N_ODD, D, ODD_IN), D ** -0.5)
    inp['o_q_norm'] = gain((N_ODD, MLA_Q_RANK))
    inp['o_kv_norm'] = gain((N_ODD, MLA_KV_RANK))
    inp['o_w_uq'] = nrm((N_ODD, MLA_Q_RANK, MLA_HEADS * (MLA_NOPE + MLA_ROPE)), MLA_Q_RANK ** -0.5)
    inp['o_w_ukv'] = nrm((N_ODD, MLA_KV_RANK, MLA_HEADS * (MLA_NOPE + MLA_V)), MLA_KV_RANK ** -0.5)
    inp['o_lambda'] = nrm((N_ODD, 4, DIFF_HEAD_DIM), 0.1)
    inp['o_subln'] = gain((N_ODD, 2 * DIFF_HEAD_DIM))
    inp['o_w_out'] = nrm((N_ODD, ODD_MIX, D), ODD_MIX ** -0.5)
    inp['o_router'] = nrm((N_ODD, D, N_EXPERTS), D ** -0.5)
    inp['o_exp_gate'] = nrm((N_ODD, N_EXPERTS, D, D_FF_EXPERT), D ** -0.5)
    inp['o_exp_up'] = nrm((N_ODD, N_EXPERTS, D, D_FF_EXPERT), D ** -0.5)
    inp['o_exp_down'] = nrm((N_ODD, N_EXPERTS, D_FF_EXPERT, D), D_FF_EXPERT ** -0.5)
    return inp


def reference(x, c, ctx, c_ctx,
              e_mod_w, e_mod_b, e_norms, e_w_in, e_conv_w, e_a_log, e_dt_bias, e_gdn_norm, e_sink, e_w_out,
              e_ffn_gate, e_ffn_up, e_ffn_down,
              o_mod_w, o_mod_b, o_norms, o_w_in, o_q_norm, o_kv_norm, o_w_uq, o_w_ukv, o_lambda, o_subln,
              o_w_out, o_router, o_exp_gate, o_exp_up, o_exp_down):
    B, T, D = x.shape
    Tc = ctx.shape[1]
    rows = T // GRID_W
    cos_s, sin_s = axial_rope_tables(rows, SWA_HEAD_DIM)
    cos_d, sin_d = axial_rope_tables(rows, DIFF_HEAD_DIM)
    cos_r, sin_r = axial_rope_tables(rows, MLA_ROPE)
    silu_c = jax.nn.silu(c)
    silu_cc = jax.nn.silu(c_ctx)
    h_ctx = ctx
    for i in range(DEPTH):
        last = i == DEPTH - 1
        j = i // 2
        even = i % 2 == 0
        mod_w, mod_b, norms = (e_mod_w[j], e_mod_b[j], e_norms[j]) if even else (o_mod_w[j], o_mod_b[j], o_norms[j])
        m_l = jnp.split((silu_c @ mod_w + mod_b)[:, None, :], 6, axis=-1)
        m_c = jnp.split((silu_cc @ mod_w + mod_b)[None, None, :], 6, axis=-1)

        a_c = modulate(rmsnorm(h_ctx, norms[0]), m_c[0], m_c[1])
        a_l = modulate(rmsnorm(x, norms[0]), m_l[0], m_l[1])
        if even:
            y_c, y_l = even_mixer(a_c, a_l, e_w_in[j], e_conv_w[j], e_a_log[j], e_dt_bias[j], e_gdn_norm[j],
                                  e_sink[j], e_w_out[j], cos_s, sin_s, not last)
        else:
            lam_init = 0.8 - 0.6 * math.exp(-0.3 * i)
            y_c, y_l = odd_mixer(a_c, a_l, o_w_in[j], o_q_norm[j], o_kv_norm[j], o_w_uq[j], o_w_ukv[j],
                                 o_lambda[j], lam_init, o_subln[j], o_w_out[j], cos_d, sin_d, cos_r, sin_r,
                                 not last)
        x = x + m_l[2] * rmsnorm(y_l, norms[1])
        if not last:
            h_ctx = h_ctx + m_c[2] * rmsnorm(y_c, norms[1])

        f_in = modulate(rmsnorm(x, norms[2]), m_l[3], m_l[4])
        if not last:
            f_in = jnp.concatenate([modulate(rmsnorm(h_ctx, norms[2]), m_c[3], m_c[4]), f_in], axis=1)
        if even:
            f_out = swiglu(f_in, e_ffn_gate[j], e_ffn_up[j], e_ffn_down[j])
        else:
            f_out = moe_swiglu(f_in.reshape(-1, D), o_router[j], o_exp_gate[j], o_exp_up[j],
                               o_exp_down[j]).reshape(f_in.shape)
        x = x + m_l[5] * rmsnorm(f_out[:, -T:], norms[3])
        if not last:
            h_ctx = h_ctx + m_c[5] * rmsnorm(f_out[:, :Tc], norms[3])
    return x
```

```python
import functools
import math

import jax
import jax.numpy as jnp
import numpy as np
from jax import lax
from jax.experimental import pallas as pl
from jax.experimental.pallas import tpu as pltpu

F32 = jnp.float32
BF16 = jnp.bfloat16

D_MODEL = 1024
CTX_LEN = 256
GRID_W = 64
NORM_EPS = 1e-6
ROPE_THETA = 10000.0

GDN_HEADS = 8
GDN_DK = 64
GDN_DV = 64
GDN_CHUNK = 64
CONV_K = 5
GDN_GROUP = 4
GDN_LANES = GDN_GROUP * GDN_DK

SWA_Q_HEADS = 8
SWA_KV_HEADS = 2
SWA_HEAD_DIM = 64
SWA_WINDOW = 128
SWA_BLOCK = 128

MLA_HEADS = 8
MLA_Q_RANK = 384
MLA_KV_RANK = 256
MLA_NOPE = 64
MLA_ROPE = 32
MLA_V = 64

DIFF_HEADS = 4
DIFF_HEAD_DIM = 64

D_FF = 2816
N_EXPERTS = 8
TOP_K = 2
D_FF_EXPERT = 3584

ROW_TILE = 256
VMEM_LIMIT = 56 * 1024 * 1024


def _dot(a, b):
    return jnp.dot(a, b, preferred_element_type=F32)


def _dot_nt(a, b):
    return lax.dot_general(a, b, (((1,), (1,)), ((), ())), preferred_element_type=F32)


def _dot_tn(a, b):
    return lax.dot_general(a, b, (((0,), (0,)), ((), ())), preferred_element_type=F32)


def _split_bf16(x):
    hi = x.astype(BF16)
    lo = (x - hi.astype(F32)).astype(BF16)
    return hi, lo


def _block_diag(x, head_of_lane):
    zero = jnp.zeros_like(x)
    return jnp.concatenate([jnp.where(head_of_lane == h, x, zero) for h in range(GDN_GROUP)], axis=0)


def _expand_heads(x4, head_of_lane):
    c = x4.shape[0]
    out = jnp.broadcast_to(x4[:, GDN_GROUP - 1:GDN_GROUP], (c, GDN_LANES))
    for h in range(GDN_GROUP - 2, -1, -1):
        out = jnp.where(head_of_lane == h, jnp.broadcast_to(x4[:, h:h + 1], (c, GDN_LANES)), out)
    return out


def _gdn_group(q4, k4, v4, g4, b4, s4, reverse):
    C = GDN_CHUNK
    row = lax.broadcasted_iota(jnp.int32, (C, GDN_LANES), 0)
    lane = lax.broadcasted_iota(jnp.int32, (C, GDN_LANES), 1)
    head_of_lane = lane // GDN_DK
    col = lane % GDN_DK
    sgn = jnp.where(reverse, -1, 1)
    ahead = (row - col) * sgn
    incl = ahead >= 0
    strict = ahead > 0
    r64 = lax.broadcasted_iota(jnp.int32, (C, C), 0)
    c64 = lax.broadcasted_iota(jnp.int32, (C, C), 1)
    tri = jnp.where((r64 - c64) * sgn >= 0, 1.0, 0.0).astype(BF16)
    ones = jnp.ones((C, C), BF16)

    gE = _expand_heads(g4, head_of_lane)
    bE = _expand_heads(b4, head_of_lane)

    g_hi, g_lo = _split_bf16(gE)
    gc = _dot(tri, g_hi) + _dot(tri, g_lo)
    before_j = (col - row) * sgn >= 0
    gr_hi, gr_lo = _split_bf16(jnp.where(before_j, gE, 0.0))
    gc_row = _dot(ones, gr_hi) + _dot(ones, gr_lo)
    decay = jnp.where(incl, jnp.exp(gc - gc_row), 0.0)

    g_last = jnp.where(reverse, gc[0:1, :], gc[C - 1:C, :])
    eg = jnp.exp(gc)
    kb = k4 * bE
    vb = v4 * bE
    kbg = kb * eg
    qh = q4 * eg
    kt = k4 * jnp.exp(g_last - gc)

    hol = head_of_lane

    def bd(x):
        return _block_diag(x, hol)

    gram = _dot_nt(jnp.concatenate([kb, q4], axis=0).astype(BF16), bd(k4.astype(BF16)))
    L = jnp.where(strict, gram[:C] * decay, 0.0)
    A = gram[C:] * decay

    eye = jnp.where(row == col, 1.0, 0.0)
    T = eye - L
    P = L.astype(BF16)
    n_sq = int(math.log2(C)) - 1
    for _ in range(n_sq):
        P32 = _dot(P, bd(P))
        P = P32.astype(BF16)
        T = T + _dot(T.astype(BF16), bd(P))
    Tb = T.astype(BF16)
    u = _dot(Tb, bd(vb.astype(BF16)))
    w = _dot(Tb, bd(kbg.astype(BF16)))

    ws_qs = _dot(jnp.concatenate([w, qh], axis=0).astype(BF16), bd(s4.astype(BF16)))
    v_new = u - ws_qs[:C]
    o = ws_qs[C:] + _dot(A.astype(BF16), bd(v_new.astype(BF16)))
    full = _dot_tn(kt.astype(BF16), v_new.astype(BF16))
    upd = jnp.zeros((GDN_DK, GDN_LANES), F32)
    for h in range(GDN_GROUP):
        upd = upd + jnp.where(hol == h, full[h * GDN_DK:(h + 1) * GDN_DK, :], 0.0)
    s_new = s4 * jnp.exp(g_last) + upd
    return o, s_new


def _gdn_kernel(qf_ref, kf_ref, vf_ref, gf_ref, bf_ref, qb_ref, kb_ref, vb_ref, gb_ref, bb_ref,
                of_ref, ob_ref, s_ref):
    @pl.when(pl.program_id(1) == 0)
    def _():
        s_ref[...] = jnp.zeros_like(s_ref)

    n_grp = GDN_HEADS // GDN_GROUP
    for d, (q_ref, k_ref, v_ref, g_ref, b_ref, o_ref) in enumerate((
            (qf_ref, kf_ref, vf_ref, gf_ref, bf_ref, of_ref),
            (qb_ref, kb_ref, vb_ref, gb_ref, bb_ref, ob_ref))):
        for grp in range(n_grp):
            ls = slice(grp * GDN_LANES, (grp + 1) * GDN_LANES)
            hs = slice(grp * GDN_GROUP, (grp + 1) * GDN_GROUP)
            o, s_new = _gdn_group(q_ref[0, :, ls], k_ref[0, :, ls], v_ref[0, :, ls],
                                  g_ref[0, 0, :, hs], b_ref[0, 0, :, hs],
                                  s_ref[d * n_grp + grp], d == 1)
            o_ref[0, 0, :, ls] = o
            s_ref[d * n_grp + grp] = s_new


def _gdn(q, k, v, g, beta, n_ctx_chunks):
    B, S, HD = q.shape
    n_chunks = S // GDN_CHUNK
    last = n_chunks - 1

    def fwd_c(s):
        return s

    def bwd_c(s):
        return jnp.where(s < n_ctx_chunks, n_ctx_chunks - 1 - s, last + n_ctx_chunks - s)

    def seq_spec(cmap):
        return pl.BlockSpec((1, GDN_CHUNK, HD), lambda b, s: (b, cmap(s), 0))

    def gate_spec(cmap, d):
        return pl.BlockSpec((1, 1, GDN_CHUNK, GDN_HEADS), lambda b, s: (b, d, cmap(s), 0))

    def out_spec(cmap, d):
        return pl.BlockSpec((1, 1, GDN_CHUNK, HD), lambda b, s: (b, d, cmap(s), 0))

    of, ob = pl.pallas_call(
        _gdn_kernel,
        grid=(B, n_chunks),
        in_specs=[seq_spec(fwd_c), seq_spec(fwd_c), seq_spec(fwd_c), gate_spec(fwd_c, 0), gate_spec(fwd_c, 0),
                  seq_spec(bwd_c), seq_spec(bwd_c), seq_spec(bwd_c), gate_spec(bwd_c, 1), gate_spec(bwd_c, 1)],
        out_specs=[out_spec(fwd_c, 0), out_spec(bwd_c, 0)],
        out_shape=[jax.ShapeDtypeStruct((B, 1, S, HD), F32), jax.ShapeDtypeStruct((B, 1, S, HD), F32)],
        scratch_shapes=[pltpu.VMEM((2 * GDN_HEADS // GDN_GROUP, GDN_DK, GDN_LANES), F32)],
        compiler_params=pltpu.CompilerParams(dimension_semantics=("parallel", "arbitrary"),
                                             vmem_limit_bytes=VMEM_LIMIT),
        name="gdn_scan",
    )(q, k, v, g, beta, q, k, v, g, beta)
    return of[:, 0], ob[:, 0]


def _rms(x, w):
    return x * lax.rsqrt(jnp.mean(x * x, axis=-1, keepdims=True) + NORM_EPS) * w


def _params(*sem):
    return pltpu.CompilerParams(dimension_semantics=sem, vmem_limit_bytes=VMEM_LIMIT)


def _row_spec(width, off=0):
    return pl.BlockSpec((1, ROW_TILE, width), lambda b, t: (b, t + off, 0))


def _mod_spec(off=0):
    return pl.BlockSpec((1, 1, 6, D_MODEL), lambda b, t: (b, jnp.minimum(t + off, 1), 0, 0))


def _const_spec(shape):
    return pl.BlockSpec(shape, lambda b, t: (0,) * len(shape))


def _norm_proj_kernel(x_ref, mod_ref, nw_ref, w_ref, o_ref, *, shift_idx, scale_idx):
    a = (_rms(x_ref[0], nw_ref[...]) * (1.0 + mod_ref[0, 0, scale_idx:scale_idx + 1, :])
         + mod_ref[0, 0, shift_idx:shift_idx + 1, :])
    o_ref[0] = _dot(a.astype(BF16), w_ref[...]).astype(o_ref.dtype)


def _norm_proj(xa, mod, norm_w, w, shift_idx, scale_idx):
    B, S, D = xa.shape
    N = w.shape[1]
    return pl.pallas_call(
        functools.partial(_norm_proj_kernel, shift_idx=shift_idx, scale_idx=scale_idx),
        grid=(B, S // ROW_TILE),
        in_specs=[_row_spec(D), _mod_spec(), _const_spec((1, D)), _const_spec((D, N))],
        out_specs=_row_spec(N),
        out_shape=jax.ShapeDtypeStruct((B, S, N), F32),
        compiler_params=_params("parallel", "parallel"),
        name="norm_proj",
    )(xa, mod, norm_w, w)


def _matmul_kernel(a_ref, w_ref, o_ref):
    o_ref[0] = _dot(a_ref[0], w_ref[...]).astype(o_ref.dtype)


def _matmul(a, w, out_dtype):
    B, S, K = a.shape
    N = w.shape[1]
    return pl.pallas_call(
        _matmul_kernel,
        grid=(B, S // ROW_TILE),
        in_specs=[_row_spec(K), _const_spec((K, N))],
        out_specs=_row_spec(N),
        out_shape=jax.ShapeDtypeStruct((B, S, N), out_dtype),
        compiler_params=_params("parallel", "parallel"),
        name="matmul",
    )(a, w)


def _mix_out_kernel(a_ref, x_ref, mod_ref, nw_ref, w_ref, o_ref, *, gate_idx):
    y = _dot(a_ref[0], w_ref[...])
    o_ref[0] = x_ref[0] + mod_ref[0, 0, gate_idx:gate_idx + 1, :] * _rms(y, nw_ref[...])


def _mix_out(a, xa, mod, norm_w, w, gate_idx, x_off):
    B, Sa, K = a.shape
    D = xa.shape[-1]
    return pl.pallas_call(
        functools.partial(_mix_out_kernel, gate_idx=gate_idx),
        grid=(B, Sa // ROW_TILE),
        in_specs=[_row_spec(K), _row_spec(D, x_off), _mod_spec(x_off), _const_spec((1, D)), _const_spec((K, D))],
        out_specs=_row_spec(D),
        out_shape=jax.ShapeDtypeStruct((B, Sa, D), F32),
        compiler_params=_params("parallel", "parallel"),
        name="mix_out",
    )(a, xa, mod, norm_w, w)


def _ffn_kernel(x_ref, mod_ref, n_in_ref, n_out_ref, wg_ref, wu_ref, wd_ref, o_ref):
    x = x_ref[0]
    f = (_rms(x, n_in_ref[...]) * (1.0 + mod_ref[0, 0, 4:5, :]) + mod_ref[0, 0, 3:4, :]).astype(BF16)
    h = jax.nn.silu(_dot(f, wg_ref[...])) * _dot(f, wu_ref[...])
    y = _dot(h.astype(BF16), wd_ref[...])
    o_ref[0] = x + mod_ref[0, 0, 5:6, :] * _rms(y, n_out_ref[...])


def _ffn(xa, mod, n_in, n_out, wg, wu, wd):
    B, S, D = xa.shape
    F = wg.shape[1]
    return pl.pallas_call(
        _ffn_kernel,
        grid=(B, S // ROW_TILE),
        in_specs=[_row_spec(D), _mod_spec(), _const_spec((1, D)), _const_spec((1, D)),
                  _const_spec((D, F)), _const_spec((D, F)), _const_spec((F, D))],
        out_specs=_row_spec(D),
        out_shape=jax.ShapeDtypeStruct((B, S, D), F32),
        compiler_params=_params("parallel", "parallel"),
        name="dense_ffn",
    )(xa, mod, n_in, n_out, wg, wu, wd)


def _norm_residual_kernel(a_ref, b_ref, x_ref, mod_ref, nw_ref, o_ref, *, gate_idx):
    y = a_ref[0] + b_ref[0]
    o_ref[0] = x_ref[0] + mod_ref[0, 0, gate_idx:gate_idx + 1, :] * _rms(y, nw_ref[...])


def _norm_residual(a, b, x, mod, norm_w, gate_idx):
    B, T, D = x.shape
    return pl.pallas_call(
        functools.partial(_norm_residual_kernel, gate_idx=gate_idx),
        grid=(B, T // ROW_TILE),
        in_specs=[_row_spec(D), _row_spec(D), _row_spec(D), _mod_spec(1), _const_spec((1, D))],
        out_specs=_row_spec(D),
        out_shape=jax.ShapeDtypeStruct((B, T, D), F32),
        compiler_params=_params("parallel", "parallel"),
        name="norm_residual",
    )(a, b, x, mod, norm_w)


NEG_BIG = -1e30


def _swa_kernel(q_ref, k_ref, v_ref, sink_ref, o_ref):
    W = SWA_BLOCK
    n = pl.program_id(1)
    nb = pl.num_programs(1)
    prev = jnp.maximum(n - 1, 0)
    nxt = jnp.minimum(n + 1, nb - 1)

    def gather_keys(ref):
        def blk(i):
            return ref[0, pl.ds(pl.multiple_of(CTX_LEN + i * W, W), W), :]
        return jnp.concatenate([ref[0, 0:CTX_LEN, :], blk(prev), blk(n), blk(nxt)], axis=0)

    k_all = gather_keys(k_ref)
    v_all = gather_keys(v_ref)
    nk = CTX_LEN + 3 * W
    ii = lax.broadcasted_iota(jnp.int32, (W, nk), 0)
    jj = lax.broadcasted_iota(jnp.int32, (W, nk), 1) - CTX_LEN
    valid = ((jj < 0)
             | ((jj >= 0) & (jj < W) & (jj >= ii) & (n > 0))
             | ((jj >= W) & (jj < 2 * W))
             | ((jj >= 2 * W) & (jj - 2 * W <= ii) & (n < nb - 1)))
    for h in range(SWA_Q_HEADS):
        ls = slice(h * 128, (h + 1) * 128)
        s = jnp.where(valid, _dot_nt(q_ref[0, :, ls], k_all), NEG_BIG)
        sk = sink_ref[:, h:h + 1]
        m = jnp.maximum(jnp.max(s, axis=-1, keepdims=True), sk)
        p = jnp.exp(s - m)
        denom = jnp.sum(p, axis=-1, keepdims=True) + jnp.exp(sk - m)
        o_ref[0, :, ls] = (_dot(p.astype(BF16), v_all) / denom).astype(o_ref.dtype)


def _swa(q_pad, k, v, sink):
    B, S, _ = k.shape
    T = S - CTX_LEN
    W = SWA_BLOCK
    return pl.pallas_call(
        _swa_kernel,
        grid=(B, T // W),
        in_specs=[pl.BlockSpec((1, W, 1024), lambda b, n: (b, n + CTX_LEN // W, 0)),
                  pl.BlockSpec((1, S, 128), lambda b, n: (b, 0, 0)),
                  pl.BlockSpec((1, S, 128), lambda b, n: (b, 0, 0)),
                  pl.BlockSpec((1, SWA_Q_HEADS), lambda b, n: (0, 0))],
        out_specs=pl.BlockSpec((1, W, 1024), lambda b, n: (b, n, 0)),
        out_shape=jax.ShapeDtypeStruct((B, T, 1024), BF16),
        compiler_params=_params("parallel", "parallel"),
        name="swa",
    )(q_pad, k, v, sink)


def _swa_ctx_kernel(q_ref, k_ref, v_ref, sink_ref, o_ref):
    k_all = k_ref[0]
    v_all = v_ref[0]
    for h in range(SWA_Q_HEADS):
        ls = slice(h * 128, (h + 1) * 128)
        s = _dot_nt(q_ref[0, :, ls], k_all)
        sk = sink_ref[:, h:h + 1]
        m = jnp.maximum(jnp.max(s, axis=-1, keepdims=True), sk)
        p = jnp.exp(s - m)
        denom = jnp.sum(p, axis=-1, keepdims=True) + jnp.exp(sk - m)
        o_ref[0, :, ls] = (_dot(p.astype(BF16), v_all) / denom).astype(o_ref.dtype)


def _swa_ctx(q_pad, k, v, sink):
    B = k.shape[0]
    return pl.pallas_call(
        _swa_ctx_kernel,
        grid=(B, 1),
        in_specs=[pl.BlockSpec((1, CTX_LEN, 1024), lambda b, n: (b, 0, 0)),
                  pl.BlockSpec((1, CTX_LEN, 128), lambda b, n: (b, 0, 0)),
                  pl.BlockSpec((1, CTX_LEN, 128), lambda b, n: (b, 0, 0)),
                  pl.BlockSpec((1, SWA_Q_HEADS), lambda b, n: (0, 0))],
        out_specs=pl.BlockSpec((1, CTX_LEN, 1024), lambda b, n: (b, 0, 0)),
        out_shape=jax.ShapeDtypeStruct((B, CTX_LEN, 1024), BF16),
        compiler_params=_params("parallel", "parallel"),
        name="swa_ctx",
    )(q_pad, k, v, sink)


def _mla_kernel(q_ref, k_ref, v_ref, o_ref):
    s = _dot_nt(q_ref[0], k_ref[0])
    m = jnp.max(s, axis=-1, keepdims=True)
    p = jnp.exp(s - m).astype(BF16)
    acc = _dot(p, v_ref[0])
    o_ref[0] = (acc / acc[:, MLA_V:MLA_V + 1]).astype(o_ref.dtype)


def _mla(q, k, v_ext):
    B, S, _ = k.shape
    T = S - CTX_LEN
    off = CTX_LEN // ROW_TILE
    return pl.pallas_call(
        _mla_kernel,
        grid=(B, MLA_HEADS, T // ROW_TILE),
        in_specs=[pl.BlockSpec((1, ROW_TILE, 128), lambda b, h, t: (b, t + off, h)),
                  pl.BlockSpec((1, S, 128), lambda b, h, t: (b, 0, h)),
                  pl.BlockSpec((1, S, 128), lambda b, h, t: (b, 0, h))],
        out_specs=pl.BlockSpec((1, ROW_TILE, 128), lambda b, h, t: (b, t, h)),
        out_shape=jax.ShapeDtypeStruct((B, T, MLA_HEADS * 128), BF16),
        compiler_params=_params("parallel", "parallel", "parallel"),
        name="mla_attn",
    )(q, k, v_ext)


def _diff_kernel(q_ref, k_ref, v_ref, lam_ref, sub_ref, o_ref, *, post_scale):
    q = q_ref[0]
    k = k_ref[0]
    v = v_ref[0]
    lo = lax.broadcasted_iota(jnp.int32, q.shape, 1) < DIFF_HEAD_DIM
    zero = jnp.zeros_like(q)
    vw = 2 * DIFF_HEAD_DIM

    def branch(qm):
        s = _dot_nt(qm, k)
        m = jnp.max(s, axis=-1, keepdims=True)
        acc = _dot(jnp.exp(s - m).astype(BF16), v)
        return acc[:, :vw] / acc[:, vw:vw + 1]

    a = branch(jnp.where(lo, q, zero)) - lam_ref[...] * branch(jnp.where(lo, zero, q))
    o_ref[0] = (_rms(a, sub_ref[...]) * post_scale).astype(o_ref.dtype)


def _diff(q, k, v_ext, lam, subln, post_scale):
    B, S, _ = k.shape
    T = S - CTX_LEN
    off = CTX_LEN // ROW_TILE
    return pl.pallas_call(
        functools.partial(_diff_kernel, post_scale=post_scale),
        grid=(B, DIFF_HEADS, T // ROW_TILE),
        in_specs=[pl.BlockSpec((1, ROW_TILE, 128), lambda b, h, t: (b, t + off, h)),
                  pl.BlockSpec((1, S, 128), lambda b, h, t: (b, 0, h)),
                  pl.BlockSpec((1, S, 256), lambda b, h, t: (b, 0, h)),
                  pl.BlockSpec((1, 1), lambda b, h, t: (0, 0)),
                  pl.BlockSpec((1, 128), lambda b, h, t: (0, 0))],
        out_specs=pl.BlockSpec((1, ROW_TILE, 128), lambda b, h, t: (b, t, h)),
        out_shape=jax.ShapeDtypeStruct((B, T, DIFF_HEADS * 128), BF16),
        compiler_params=_params("parallel", "parallel", "parallel"),
        name="diff_attn",
    )(q, k, v_ext, lam, subln)


MOE_TM = 512
MOE_TF = 512


def _moe_pre_kernel(x_ref, mod_ref, nw_ref, wr_ref, f_ref, lg_ref):
    f = _rms(x_ref[0], nw_ref[...]) * (1.0 + mod_ref[0, 0, 4:5, :]) + mod_ref[0, 0, 3:4, :]
    f_ref[0] = f.astype(BF16)
    lg_ref[0] = jnp.dot(f, wr_ref[...], precision=lax.Precision.HIGHEST, preferred_element_type=F32)


def _moe_pre(x, mod, norm_w, w_router_pad):
    B, T, D = x.shape
    return pl.pallas_call(
        _moe_pre_kernel,
        grid=(B, T // ROW_TILE),
        in_specs=[_row_spec(D), _mod_spec(1), _const_spec((1, D)), _const_spec((D, 128))],
        out_specs=[_row_spec(D), _row_spec(128)],
        out_shape=[jax.ShapeDtypeStruct((B, T, D), BF16), jax.ShapeDtypeStruct((B, T, 128), F32)],
        compiler_params=_params("parallel", "parallel"),
        name="moe_pre",
    )(x, mod, norm_w, w_router_pad)


def _moe_kernel(te_ref, ta_ref, x_ref, rw_ref, wg_ref, wu_ref, wd_ref, o_ref, acc_ref):
    i = pl.program_id(0)
    f = pl.program_id(1)

    @pl.when(f == 0)
    def _():
        acc_ref[...] = jnp.zeros_like(acc_ref)

    @pl.when(ta_ref[i] > 0)
    def _():
        x = x_ref[...]
        h = jax.nn.silu(_dot(x, wg_ref[0])) * _dot(x, wu_ref[0])
        acc_ref[...] += _dot(h.astype(BF16), wd_ref[0])

    @pl.when(f == pl.num_programs(1) - 1)
    def _():
        o_ref[...] = acc_ref[...] * rw_ref[...]


def _moe_experts(tile_expert, tile_active, x_sorted, row_w, wg, wu, wd):
    R, D = x_sorted.shape
    E, _, F = wg.shape
    nf = F // MOE_TF

    def f_eff(i, f, ta):
        return jnp.where(ta[i] > 0, f, nf - 1)

    grid_spec = pltpu.PrefetchScalarGridSpec(
        num_scalar_prefetch=2,
        grid=(R // MOE_TM, nf),
        in_specs=[pl.BlockSpec((MOE_TM, D), lambda i, f, te, ta: (i, 0)),
                  pl.BlockSpec((MOE_TM, 1), lambda i, f, te, ta: (i, 0)),
                  pl.BlockSpec((1, D, MOE_TF), lambda i, f, te, ta: (te[i], 0, f_eff(i, f, ta))),
                  pl.BlockSpec((1, D, MOE_TF), lambda i, f, te, ta: (te[i], 0, f_eff(i, f, ta))),
                  pl.BlockSpec((1, MOE_TF, D), lambda i, f, te, ta: (te[i], f_eff(i, f, ta), 0))],
        out_specs=pl.BlockSpec((MOE_TM, D), lambda i, f, te, ta: (i, 0)),
        scratch_shapes=[pltpu.VMEM((MOE_TM, D), F32)],
    )
    return pl.pallas_call(
        _moe_kernel,
        grid_spec=grid_spec,
        out_shape=jax.ShapeDtypeStruct((R, D), F32),
        compiler_params=_params("arbitrary", "arbitrary"),
        name="moe_experts",
    )(tile_expert, tile_active, x_sorted, row_w, wg, wu, wd)


def _route(logits):
    N = logits.shape[0]
    R = N * TOP_K + N_EXPERTS * MOE_TM
    top_val, top_idx = lax.top_k(logits, TOP_K)
    top_w = jax.nn.softmax(top_val, axis=-1)
    e_flat = top_idx.reshape(-1)
    onehot = (e_flat[:, None] == jnp.arange(N_EXPERTS)[None, :]).astype(jnp.int32)
    pos = jnp.sum((jnp.cumsum(onehot, axis=0) - onehot) * onehot, axis=-1)
    counts = jnp.sum(onehot, axis=0)
    padded = ((counts + MOE_TM - 1) // MOE_TM) * MOE_TM
    ends = jnp.cumsum(padded)
    starts = ends - padded
    dest = starts[e_flat] + pos
    token = jnp.arange(N * TOP_K, dtype=jnp.int32) // TOP_K
    row_token = jnp.zeros((R,), jnp.int32).at[dest].set(token)
    row_w = jnp.zeros((R,), F32).at[dest].set(top_w.reshape(-1))
    tile_start = jnp.arange(R // MOE_TM, dtype=jnp.int32) * MOE_TM
    tile_active = (tile_start < ends[-1]).astype(jnp.int32)
    te = jnp.minimum(jnp.searchsorted(ends, tile_start, side="right"), N_EXPERTS - 1).astype(jnp.int32)
    n_active = ends[-1] // MOE_TM
    last_e = te[jnp.maximum(n_active - 1, 0)]
    tile_expert = jnp.where(tile_active > 0, te, last_e)
    return dest.reshape(N, TOP_K), row_token, row_w[:, None], tile_expert, tile_active


def _rope_tables(rows, rot_dim):
    row = jnp.repeat(jnp.arange(rows, dtype=F32), GRID_W)
    col = jnp.tile(jnp.arange(GRID_W, dtype=F32), rows)
    half = rot_dim // 2
    inv_freq = ROPE_THETA ** (-jnp.arange(0, half, 2, dtype=F32) / half)
    ang_r = row[:, None] * inv_freq[None, :]
    ang_c = col[:, None] * inv_freq[None, :]
    ang = jnp.concatenate([ang_r, ang_r, ang_c, ang_c], axis=-1)
    return jnp.cos(ang), jnp.sin(ang)


def _rotate_half(x):
    x1, x2 = jnp.split(x, 2, axis=-1)
    return jnp.concatenate([-x2, x1], axis=-1)


def _rope(x, cos, sin):
    half = x.shape[-1] // 2
    rot = jnp.concatenate([_rotate_half(x[..., :half]), _rotate_half(x[..., half:])], axis=-1)
    return x * cos[None, :, None, :] + rot * sin[None, :, None, :]


def _rope_latents(x, cos, sin):
    return jnp.concatenate([x[:, :CTX_LEN], _rope(x[:, CTX_LEN:], cos, sin)], axis=1)


def _dwconv(x, w):
    pad = CONV_K // 2
    return lax.conv_general_dilated(x, w[:, None, :], window_strides=(1,), padding=[(pad, pad)],
                                    dimension_numbers=("NWC", "WIO", "NWC"), feature_group_count=x.shape[-1])


def _l2norm(x):
    return x * lax.rsqrt(jnp.sum(x * x, axis=-1, keepdims=True) + NORM_EPS)


def _rms_glue(x, w):
    return x * lax.rsqrt(jnp.mean(x * x, axis=-1, keepdims=True) + NORM_EPS) * w


def _mod_table(c, c_ctx, mod_w, mod_b):
    B = c.shape[0]
    hp = lax.Precision.HIGHEST
    m_l = (jnp.dot(jax.nn.silu(c), mod_w, precision=hp) + mod_b).reshape(B, 1, 6, D_MODEL)
    m_c = (jnp.dot(jax.nn.silu(c_ctx), mod_w, precision=hp) + mod_b).reshape(1, 1, 6, D_MODEL)
    return jnp.concatenate([jnp.broadcast_to(m_c, m_l.shape), m_l], axis=1)


def _even_layer(xa, mod, norms, w_in, conv_w, a_log, dt_bias, gdn_norm, sink, w_out, ffn_gate, ffn_up, ffn_down,
                cos_s, sin_s):
    B, S, D = xa.shape
    T = S - CTX_LEN
    proj = _norm_proj(xa, mod, norms[0:1], w_in.astype(BF16), 0, 1)
    c0 = GDN_HEADS * (2 * GDN_DK + GDN_DV)
    c1 = c0 + GDN_HEADS * GDN_DV
    c2 = c1 + 2 * GDN_HEADS
    c3 = c2 + 2 * GDN_HEADS
    c4 = c3 + SWA_Q_HEADS * SWA_HEAD_DIM
    c5 = c4 + SWA_KV_HEADS * SWA_HEAD_DIM
    qkv, z, ga, gb = proj[..., :c0], proj[..., c0:c1], proj[..., c1:c2], proj[..., c2:c3]
    sq, sk, sv = proj[..., c3:c4], proj[..., c4:c5], proj[..., c5:]

    g = -jnp.exp(a_log) * jax.nn.softplus(ga.reshape(B, S, 2, GDN_HEADS) + dt_bias)
    beta = jax.nn.sigmoid(gb.reshape(B, S, 2, GDN_HEADS))
    u = jnp.concatenate([_dwconv(qkv[:, :CTX_LEN], conv_w), _dwconv(qkv[:, CTX_LEN:], conv_w)], axis=1)
    u = jax.nn.silu(u)
    hk = GDN_HEADS * GDN_DK
    q = (_l2norm(u[..., :hk].reshape(B, S, GDN_HEADS, GDN_DK)) * GDN_DK ** -0.5).reshape(B, S, hk)
    k = _l2norm(u[..., hk:2 * hk].reshape(B, S, GDN_HEADS, GDN_DK)).reshape(B, S, hk)
    v = u[..., 2 * hk:]
    o_f, o_b = _gdn(q, k, v, jnp.moveaxis(g, 2, 1), jnp.moveaxis(beta, 2, 1), CTX_LEN // GDN_CHUNK)
    o = (o_f + o_b).reshape(B, S, GDN_HEADS, GDN_DV)
    gdn = _rms_glue(o, gdn_norm).reshape(B, S, GDN_HEADS * GDN_DV) * jax.nn.silu(z)

    sq4 = _rope_latents(sq.reshape(B, S, SWA_Q_HEADS, SWA_HEAD_DIM), cos_s, sin_s) * SWA_HEAD_DIM ** -0.5
    sk4 = _rope_latents(sk.reshape(B, S, SWA_KV_HEADS, SWA_HEAD_DIM), cos_s, sin_s)
    grp = SWA_Q_HEADS // SWA_KV_HEADS
    zq = jnp.zeros_like(sq4)
    kv_of_head = (jnp.arange(SWA_Q_HEADS) // grp)[None, None, :, None]
    q_pad = jnp.concatenate([jnp.where(kv_of_head == 0, sq4, zq), jnp.where(kv_of_head == 1, sq4, zq)], axis=-1)
    q_pad = q_pad.reshape(B, S, SWA_Q_HEADS * 128).astype(BF16)
    k_b = sk4.reshape(B, S, 128).astype(BF16)
    v_b = sv.astype(BF16)
    sink2 = sink.reshape(1, SWA_Q_HEADS)
    o_lat = _swa(q_pad, k_b, v_b, sink2)
    o_ctx = _swa_ctx(q_pad[:, :CTX_LEN], k_b[:, :CTX_LEN], v_b[:, :CTX_LEN], sink2)
    o_all = jnp.concatenate([o_ctx, o_lat], axis=1).reshape(B, S, SWA_Q_HEADS, 2, SWA_HEAD_DIM)
    swa = jnp.where(kv_of_head == 0, o_all[..., 0, :], o_all[..., 1, :]).reshape(B, S, SWA_Q_HEADS * SWA_HEAD_DIM)

    mix = jnp.concatenate([gdn.astype(BF16), swa], axis=-1)
    xa = _mix_out(mix, xa, mod, norms[1:2], w_out.astype(BF16), 2, 0)
    return _ffn(xa, mod, norms[2:3], norms[3:4], ffn_gate.astype(BF16), ffn_up.astype(BF16), ffn_down.astype(BF16))


def _odd_layer_last(xa, mod, norms, w_in, q_norm, kv_norm, w_uq, w_ukv, lam_p, lam_init, subln, w_out,
                    router, exp_gate, exp_up, exp_down, cos_d, sin_d, cos_r, sin_r):
    B, S, D = xa.shape
    T = S - CTX_LEN
    H = MLA_HEADS
    proj = _norm_proj(xa, mod, norms[0:1], w_in.astype(BF16), 0, 1)
    c0 = MLA_Q_RANK
    c1 = c0 + MLA_KV_RANK
    c2 = c1 + MLA_ROPE
    dw = DIFF_HEADS * 2 * DIFF_HEAD_DIM
    cq, ckv, kr = proj[..., :c0], proj[..., c0:c1], proj[..., c1:c2]
    dq, dk, dv = proj[..., c2:c2 + dw], proj[..., c2 + dw:c2 + 2 * dw], proj[..., c2 + 2 * dw:]

    qd = MLA_NOPE + MLA_ROPE
    w_uq_pad = jnp.pad(w_uq.reshape(MLA_Q_RANK, H, qd), ((0, 0), (0, 0), (0, 128 - qd))).reshape(MLA_Q_RANK, H * 128)
    q_m = _matmul(_rms_glue(cq, q_norm).astype(BF16), w_uq_pad.astype(BF16), F32).reshape(B, S, H, 128)
    kv_m = _matmul(_rms_glue(ckv, kv_norm).astype(BF16), w_ukv.astype(BF16), F32).reshape(B, S, H, MLA_NOPE + MLA_V)
    scale = qd ** -0.5
    qr = _rope_latents(q_m[..., MLA_NOPE:qd], cos_r, sin_r)
    q_cat = jnp.concatenate([q_m[..., :MLA_NOPE], qr, q_m[..., qd:]], axis=-1) * scale
    kr_all = _rope_latents(kr[:, :, None, :], cos_r, sin_r)
    k_cat = jnp.concatenate([kv_m[..., :MLA_NOPE], jnp.broadcast_to(kr_all, (B, S, H, MLA_ROPE)),
                             jnp.zeros((B, S, H, 128 - qd), F32)], axis=-1)
    v_ext = jnp.concatenate([kv_m[..., MLA_NOPE:], jnp.ones((B, S, H, 128 - MLA_V), F32)], axis=-1)
    mla = _mla(q_cat.reshape(B, S, H * 128).astype(BF16), k_cat.reshape(B, S, H * 128).astype(BF16),
               v_ext.reshape(B, S, H * 128).astype(BF16))

    dscale = DIFF_HEAD_DIM ** -0.5
    dq4 = _rope_latents(dq.reshape(B, S, 2 * DIFF_HEADS, DIFF_HEAD_DIM), cos_d, sin_d) * dscale
    dk4 = _rope_latents(dk.reshape(B, S, 2 * DIFF_HEADS, DIFF_HEAD_DIM), cos_d, sin_d)
    dv4 = dv.reshape(B, S, DIFF_HEADS, 2 * DIFF_HEAD_DIM)
    dv_ext = jnp.concatenate([dv4, jnp.ones_like(dv4)], axis=-1).reshape(B, S, DIFF_HEADS * 256)
    lam = (jnp.exp(jnp.sum(lam_p[0] * lam_p[1])) - jnp.exp(jnp.sum(lam_p[2] * lam_p[3])) + lam_init).reshape(1, 1)
    diff = _diff(dq4.reshape(B, S, dw).astype(BF16), dk4.reshape(B, S, dw).astype(BF16), dv_ext.astype(BF16),
                 lam, subln.reshape(1, 128), 1.0 - lam_init)

    w_mla = jnp.pad(w_out[:H * MLA_V].reshape(H, MLA_V, D), ((0, 0), (0, 128 - MLA_V), (0, 0))).reshape(H * 128, D)
    w_out_pad = jnp.concatenate([w_mla, w_out[H * MLA_V:]], axis=0).astype(BF16)
    x = _mix_out(jnp.concatenate([mla, diff], axis=-1), xa, mod, norms[1:2], w_out_pad, 2, CTX_LEN // ROW_TILE)

    w_router_pad = jnp.pad(router, ((0, 0), (0, 128 - N_EXPERTS)))
    f_in, logits = _moe_pre(x, mod, norms[2:3], w_router_pad)
    dest, row_token, row_w, tile_expert, tile_active = _route(logits.reshape(B * T, 128)[:, :N_EXPERTS])
    x_sorted = f_in.reshape(B * T, D)[row_token]
    y = _moe_experts(tile_expert, tile_active, x_sorted, row_w,
                     exp_gate.astype(BF16), exp_up.astype(BF16), exp_down.astype(BF16))
    y0 = y[dest[:, 0]].reshape(B, T, D)
    y1 = y[dest[:, 1]].reshape(B, T, D)
    return _norm_residual(y0, y1, x, mod, norms[3:4], 5)


def kernel(x, c, ctx, c_ctx, e_mod_w, e_mod_b, e_norms, e_w_in, e_conv_w, e_a_log, e_dt_bias, e_gdn_norm, e_sink, e_w_out, e_ffn_gate, e_ffn_up, e_ffn_down, o_mod_w, o_mod_b, o_norms, o_w_in, o_q_norm, o_kv_norm, o_w_uq, o_w_ukv, o_lambda, o_subln, o_w_out, o_router, o_exp_gate, o_exp_up, o_exp_down):
    B, T, D = x.shape
    rows = T // GRID_W
    cos_s, sin_s = _rope_tables(rows, SWA_HEAD_DIM)
    cos_d, sin_d = _rope_tables(rows, DIFF_HEAD_DIM)
    cos_r, sin_r = _rope_tables(rows, MLA_ROPE)
    xa = jnp.concatenate([ctx, x], axis=1)
    mod_e = _mod_table(c, c_ctx, e_mod_w[0], e_mod_b[0])
    xa = _even_layer(xa, mod_e, e_norms[0], e_w_in[0], e_conv_w[0], e_a_log[0], e_dt_bias[0], e_gdn_norm[0],
                     e_sink[0], e_w_out[0], e_ffn_gate[0], e_ffn_up[0], e_ffn_down[0], cos_s, sin_s)
    mod_o = _mod_table(c, c_ctx, o_mod_w[0], o_mod_b[0])
    lam_init = 0.8 - 0.6 * math.exp(-0.3 * 1)
    return _odd_layer_last(xa, mod_o, o_norms[0], o_w_in[0], o_q_norm[0], o_kv_norm[0], o_w_uq[0], o_w_ukv[0],
                           o_lambda[0], lam_init, o_subln[0], o_w_out[0], o_router[0], o_exp_gate[0],
                           o_exp_up[0], o_exp_down[0], cos_d, sin_d, cos_r, sin_r)
```

```python
import functools
import math

import jax
import jax.numpy as jnp
from jax import lax
from jax.experimental import pallas as pl
from jax.experimental.pallas import tpu as pltpu

F32 = jnp.float32
BF16 = jnp.bfloat16

D_MODEL = 1024
CTX_LEN = 256
GRID_W = 64
NORM_EPS = 1e-6
ROPE_THETA = 10000.0

GDN_HEADS = 8
GDN_DK = 64
GDN_DV = 64
GDN_CHUNK = 64
CONV_K = 5
GDN_GROUP = 4
GDN_LANES = GDN_GROUP * GDN_DK
GDN_HD = GDN_HEADS * GDN_DK

SWA_Q_HEADS = 8
SWA_KV_HEADS = 2
SWA_HEAD_DIM = 64
SWA_WINDOW = 128
SWA_BLOCK = 128

MLA_HEADS = 8
MLA_Q_RANK = 384
MLA_KV_RANK = 256
MLA_NOPE = 64
MLA_ROPE = 32
MLA_V = 64

DIFF_HEADS = 4
DIFF_HEAD_DIM = 64

D_FF = 2816
N_EXPERTS = 8
TOP_K = 2
D_FF_EXPERT = 3584

LANES = 128
SUBLANES = 8
ROW_TILE = 256
ATTN_TQ = 512
VMEM_LIMIT = 56 * 1024 * 1024
NEG_BIG = -1e30


def _dot(a, b):
    return jnp.dot(a, b, preferred_element_type=F32)


def _dot_nt(a, b):
    return lax.dot_general(a, b, (((1,), (1,)), ((), ())), preferred_element_type=F32)


def _dot_tn(a, b):
    return lax.dot_general(a, b, (((0,), (0,)), ((), ())), preferred_element_type=F32)


def _split_bf16(x):
    hi = x.astype(BF16)
    lo = (x - hi.astype(F32)).astype(BF16)
    return hi, lo


def _dot_split(x, w):
    hi, lo = _split_bf16(x)
    return _dot(hi, w) + _dot(lo, w)


def _dot_rsplit(w, x):
    hi, lo = _split_bf16(x)
    return _dot(w, hi) + _dot(w, lo)


def _rms(x, w):
    return x * lax.rsqrt(jnp.mean(x * x, axis=-1, keepdims=True) + NORM_EPS) * w


def _silu(x):
    return x * jax.nn.sigmoid(x)


def _rope_slab(x, cos, sin, quarter):
    lane = lax.broadcasted_iota(jnp.int32, x.shape, 1)
    fwd = pltpu.roll(x, quarter, 1)
    back = pltpu.roll(x, LANES - quarter, 1)
    rot = jnp.where(lane % (2 * quarter) < quarter, -back, fwd)
    return x * cos + rot * sin


def _params(*sem):
    return pltpu.CompilerParams(dimension_semantics=sem, vmem_limit_bytes=VMEM_LIMIT)


def _row_spec(width, tile=ROW_TILE):
    return pl.BlockSpec((1, tile, width), lambda b, t: (b, t, 0))


def _mod_spec(n_lat_tiles):
    return pl.BlockSpec((1, 1, 6, D_MODEL), lambda b, t: (b, jnp.where(t < n_lat_tiles, 1, 0), 0, 0))


def _const_spec(shape):
    return pl.BlockSpec(shape, lambda b, t: (0,) * len(shape))


def _table_spec():
    return pl.BlockSpec((ROW_TILE, LANES), lambda b, t: (t, 0))


E_QKV = (0, 3 * GDN_HD)
E_Z = (E_QKV[1], E_QKV[1] + GDN_HD)
E_GATE = (E_Z[1], E_Z[1] + LANES)
E_Q = (E_GATE[1], E_GATE[1] + SWA_Q_HEADS * LANES)
E_K = (E_Q[1], E_Q[1] + LANES)
E_V = (E_K[1], E_K[1] + LANES)


def _even_proj_kernel(x_ref, mod_ref, nw_ref, w_ref, cos_ref, sin_ref,
                      qkv_ref, z_ref, gate_ref, q_ref, k_ref, v_ref):
    a = (_rms(x_ref[0], nw_ref[...]) * (1.0 + mod_ref[0, 0, 1:2, :]) + mod_ref[0, 0, 0:1, :]).astype(BF16)

    def proj(cols):
        return _dot(a, w_ref[:, cols[0]:cols[1]])

    qkv_ref[0] = proj(E_QKV)
    z_ref[0] = proj(E_Z).astype(BF16)
    gate_ref[0] = proj(E_GATE)
    cos = cos_ref[...]
    sin = sin_ref[...]
    quarter = SWA_HEAD_DIM // 4
    qp = proj(E_Q)
    for h in range(SWA_Q_HEADS):
        ls = slice(h * LANES, (h + 1) * LANES)
        q_ref[0, :, ls] = _rope_slab(qp[:, ls], cos, sin, quarter).astype(BF16)
    k_ref[0] = _rope_slab(proj(E_K), cos, sin, quarter).astype(BF16)
    v_ref[0] = proj(E_V).astype(BF16)


def _even_proj(xa, mod, norm_w, w, cos, sin, n_lat_tiles):
    B, S, D = xa.shape
    widths = (E_QKV[1] - E_QKV[0], GDN_HD, LANES, SWA_Q_HEADS * LANES, LANES, LANES)
    dtypes = (F32, BF16, F32, BF16, BF16, BF16)
    return pl.pallas_call(
        _even_proj_kernel,
        grid=(B, S // ROW_TILE),
        in_specs=[_row_spec(D), _mod_spec(n_lat_tiles), _const_spec((1, D)), _const_spec(w.shape),
                  _table_spec(), _table_spec()],
        out_specs=[_row_spec(n) for n in widths],
        out_shape=[jax.ShapeDtypeStruct((B, S, n), dt) for n, dt in zip(widths, dtypes)],
        compiler_params=_params("parallel", "parallel"),
        name="even_proj",
    )(xa, mod, norm_w, w, cos, sin)


def _gdn_prep_kernel(x_ref, prev_ref, next_ref, cw_ref, bd_ref, q_ref, k_ref, v_ref, *, n_lat_tiles):
    t = pl.program_id(1)
    has_prev = jnp.logical_and(t > 0, t < n_lat_tiles)
    has_next = t < n_lat_tiles - 1
    prev = jnp.where(has_prev, prev_ref[0], 0.0)
    nxt = jnp.where(has_next, next_ref[0], 0.0)
    xe = jnp.concatenate([prev, x_ref[0], nxt], axis=0)
    pad = CONV_K // 2
    y = jnp.zeros(x_ref.shape[1:], F32)
    for kk in range(CONV_K):
        lo = SUBLANES + kk - pad
        y = y + cw_ref[kk:kk + 1, :] * xe[lo:lo + ROW_TILE, :]
    u = _silu(y)

    def headnorm(a):
        return a * lax.rsqrt(_dot_split(a * a, bd_ref[...]) + NORM_EPS)

    q_ref[0] = headnorm(u[:, :GDN_HD]) * GDN_DK ** -0.5
    k_ref[0] = headnorm(u[:, GDN_HD:2 * GDN_HD])
    v_ref[0] = u[:, 2 * GDN_HD:]


def _gdn_prep(qkv, conv_w, head_ones, n_lat_tiles):
    B, S, W = qkv.shape
    per_tile = ROW_TILE // SUBLANES
    n8 = S // SUBLANES
    return pl.pallas_call(
        functools.partial(_gdn_prep_kernel, n_lat_tiles=n_lat_tiles),
        grid=(B, S // ROW_TILE),
        in_specs=[_row_spec(W),
                  pl.BlockSpec((1, SUBLANES, W), lambda b, t: (b, jnp.maximum(t * per_tile - 1, 0), 0)),
                  pl.BlockSpec((1, SUBLANES, W), lambda b, t: (b, jnp.minimum((t + 1) * per_tile, n8 - 1), 0)),
                  _const_spec(conv_w.shape), _const_spec(head_ones.shape)],
        out_specs=[_row_spec(GDN_HD)] * 3,
        out_shape=[jax.ShapeDtypeStruct((B, S, GDN_HD), F32)] * 3,
        compiler_params=_params("parallel", "parallel"),
        name="gdn_prep",
    )(qkv, qkv, qkv, conv_w, head_ones)


def _block_diag(x, head_of_lane):
    zero = jnp.zeros_like(x)
    return jnp.concatenate([jnp.where(head_of_lane == h, x, zero) for h in range(GDN_GROUP)], axis=0)


def _expand_heads(x4, head_of_lane):
    c = x4.shape[0]
    out = jnp.broadcast_to(x4[:, GDN_GROUP - 1:GDN_GROUP], (c, GDN_LANES))
    for h in range(GDN_GROUP - 2, -1, -1):
        out = jnp.where(head_of_lane == h, jnp.broadcast_to(x4[:, h:h + 1], (c, GDN_LANES)), out)
    return out


def _gdn_groups(items):
    C = GDN_CHUNK
    row = lax.broadcasted_iota(jnp.int32, (C, GDN_LANES), 0)
    lane = lax.broadcasted_iota(jnp.int32, (C, GDN_LANES), 1)
    hol = lane // GDN_DK
    col = lane % GDN_DK
    r64 = lax.broadcasted_iota(jnp.int32, (C, C), 0)
    c64 = lax.broadcasted_iota(jnp.int32, (C, C), 1)
    ones = jnp.ones((C, C), BF16)
    eye = jnp.where(row == col, 1.0, 0.0)

    def bd(x):
        return _block_diag(x, hol)

    st = []
    for (q4, k4, v4, g4, b4, s4, reverse) in items:
        sgn = -1 if reverse else 1
        ahead = (row - col) * sgn
        incl = ahead >= 0
        strict = ahead > 0
        tri = jnp.where((r64 - c64) * sgn >= 0, 1.0, 0.0).astype(BF16)
        gE = _expand_heads(g4, hol)
        bE = _expand_heads(b4, hol)
        gc = _dot_rsplit(tri, gE)
        gc_row = _dot_rsplit(ones, jnp.where((col - row) * sgn >= 0, gE, 0.0))
        decay = jnp.where(incl, jnp.exp(gc - gc_row), 0.0)
        g_last = gc[0:1, :] if reverse else gc[C - 1:C, :]
        eg = jnp.exp(gc)
        kb = k4 * bE
        st.append(dict(q=q4, k=k4, s=s4, incl=incl, strict=strict, decay=decay, g_last=g_last,
                       kb=kb, vb=v4 * bE, kbg=kb * eg, qh=q4 * eg, kt=k4 * jnp.exp(g_last - gc)))

    for d in st:
        gram = _dot_nt(jnp.concatenate([d["kb"], d["q"]], axis=0).astype(BF16), bd(d["k"].astype(BF16)))
        L = jnp.where(d["strict"], gram[:C] * d["decay"], 0.0)
        d["A"] = gram[C:] * d["decay"]
        d["T"] = eye - L
        d["P"] = L.astype(BF16)
    for _ in range(int(math.log2(C)) - 1):
        for d in st:
            d["P"] = _dot(d["P"], bd(d["P"])).astype(BF16)
        for d in st:
            d["T"] = d["T"] + _dot(d["T"].astype(BF16), bd(d["P"]))
    for d in st:
        Tb = d["T"].astype(BF16)
        d["u"] = _dot(Tb, bd(d["vb"].astype(BF16)))
        d["w"] = _dot(Tb, bd(d["kbg"].astype(BF16)))
    for d in st:
        d["ws_qs"] = _dot(jnp.concatenate([d["w"], d["qh"]], axis=0).astype(BF16), bd(d["s"].astype(BF16)))
    out = []
    for d in st:
        v_new = d["u"] - d["ws_qs"][:C]
        vb16 = v_new.astype(BF16)
        o = d["ws_qs"][C:] + _dot(d["A"].astype(BF16), bd(vb16))
        full = _dot_tn(d["kt"].astype(BF16), vb16)
        upd = jnp.zeros((GDN_DK, GDN_LANES), F32)
        for h in range(GDN_GROUP):
            upd = upd + jnp.where(hol == h, full[h * GDN_DK:(h + 1) * GDN_DK, :], 0.0)
        out.append((o, d["s"] * jnp.exp(d["g_last"]) + upd))
    return out


def _gdn_kernel(qf_ref, kf_ref, vf_ref, gf_ref, qb_ref, kb_ref, vb_ref, gb_ref, al_ref, dtb_ref,
                of_ref, ob_ref, s_ref):
    @pl.when(pl.program_id(1) == 0)
    def _():
        s_ref[...] = jnp.zeros_like(s_ref)

    n_grp = GDN_HEADS // GDN_GROUP
    n_gate = 2 * GDN_HEADS
    items = []
    for d, (q_ref, k_ref, v_ref, g_ref) in enumerate(((qf_ref, kf_ref, vf_ref, gf_ref),
                                                      (qb_ref, kb_ref, vb_ref, gb_ref))):
        raw = g_ref[0]
        xs = raw + dtb_ref[...]
        softplus = jnp.maximum(xs, 0.0) + jnp.log(1.0 + jnp.exp(-jnp.abs(xs)))
        g = -jnp.exp(al_ref[...]) * softplus
        beta = jax.nn.sigmoid(raw)
        for grp in range(n_grp):
            ls = slice(grp * GDN_LANES, (grp + 1) * GDN_LANES)
            c0 = d * GDN_HEADS + grp * GDN_GROUP
            items.append((q_ref[0, :, ls], k_ref[0, :, ls], v_ref[0, :, ls],
                          g[:, c0:c0 + GDN_GROUP], beta[:, n_gate + c0:n_gate + c0 + GDN_GROUP],
                          s_ref[d * n_grp + grp], d == 1))
    res = _gdn_groups(items)
    for d, o_ref in enumerate((of_ref, ob_ref)):
        for grp in range(n_grp):
            o, s_new = res[d * n_grp + grp]
            o_ref[0, :, grp * GDN_LANES:(grp + 1) * GDN_LANES] = o
            s_ref[d * n_grp + grp] = s_new


def _gdn(q, k, v, gate, a_log_row, dt_bias_row, n_lat_chunks):
    B, S, HD = q.shape
    n_chunks = S // GDN_CHUNK
    n_ctx = n_chunks - n_lat_chunks

    def fwd_c(s):
        return jnp.where(s < n_ctx, n_lat_chunks + s, s - n_ctx)

    def bwd_c(s):
        return n_chunks - 1 - s

    def spec(cmap, width):
        return pl.BlockSpec((1, GDN_CHUNK, width), lambda b, s: (b, cmap(s), 0))

    return pl.pallas_call(
        _gdn_kernel,
        grid=(B, n_chunks),
        in_specs=[spec(fwd_c, HD), spec(fwd_c, HD), spec(fwd_c, HD), spec(fwd_c, LANES),
                  spec(bwd_c, HD), spec(bwd_c, HD), spec(bwd_c, HD), spec(bwd_c, LANES),
                  _const_spec((1, LANES)), _const_spec((1, LANES))],
        out_specs=[spec(fwd_c, HD), spec(bwd_c, HD)],
        out_shape=[jax.ShapeDtypeStruct((B, S, HD), F32)] * 2,
        scratch_shapes=[pltpu.VMEM((2 * GDN_HEADS // GDN_GROUP, GDN_DK, GDN_LANES), F32)],
        compiler_params=_params("parallel", "arbitrary"),
        name="gdn_scan",
    )(q, k, v, gate, q, k, v, gate, a_log_row, dt_bias_row)


def _swa_kernel(q_ref, k_ref, v_ref, sink_ref, o_ref, *, n_lat_blocks):
    W = SWA_BLOCK
    H = SWA_Q_HEADS
    n = pl.program_id(1)
    is_lat = n < n_lat_blocks
    nl = jnp.minimum(n, n_lat_blocks - 1)
    prev = jnp.maximum(nl - 1, 0)
    nxt = jnp.minimum(nl + 1, n_lat_blocks - 1)
    ctx0 = n_lat_blocks * W

    def keys(ref):
        def blk(i):
            return ref[0, pl.ds(pl.multiple_of(i * W, W), W), :]
        return jnp.concatenate([ref[0, ctx0:ctx0 + CTX_LEN, :], blk(prev), blk(nl), blk(nxt)], axis=0)

    k_all = keys(k_ref)
    v_all = keys(v_ref)
    nk = CTX_LEN + 3 * W
    v_ext = jnp.concatenate([v_all, jnp.ones((nk, LANES), BF16)], axis=1)
    q = jnp.concatenate([q_ref[0, :, h * LANES:(h + 1) * LANES] for h in range(H)], axis=0)
    s = _dot_nt(q, k_all)
    ii = lax.broadcasted_iota(jnp.int32, (H * W, nk), 0) % W
    jj = lax.broadcasted_iota(jnp.int32, (H * W, nk), 1) - CTX_LEN
    in_window = (((jj >= 0) & (jj < W) & (jj >= ii) & (nl > 0))
                 | ((jj >= W) & (jj < 2 * W))
                 | ((jj >= 2 * W) & (jj - 2 * W <= ii) & (nl < n_lat_blocks - 1)))
    valid = (jj < 0) | (in_window & is_lat)
    s = jnp.where(valid, s, NEG_BIG)
    sk = sink_ref[...]
    m = jnp.maximum(jnp.max(s, axis=-1, keepdims=True), sk)
    acc = _dot(jnp.exp(s - m).astype(BF16), v_ext)
    o = acc[:, :LANES] / (acc[:, LANES:LANES + 1] + jnp.exp(sk - m))
    for h in range(H):
        o_ref[0, :, h * LANES:(h + 1) * LANES] = o[h * W:(h + 1) * W].astype(o_ref.dtype)


def _swa(q_pad, k, v, sink_rows, n_lat_blocks):
    B, S, _ = k.shape
    W = SWA_BLOCK
    QW = SWA_Q_HEADS * LANES
    return pl.pallas_call(
        functools.partial(_swa_kernel, n_lat_blocks=n_lat_blocks),
        grid=(B, S // W),
        in_specs=[pl.BlockSpec((1, W, QW), lambda b, n: (b, n, 0)),
                  pl.BlockSpec((1, S, LANES), lambda b, n: (b, 0, 0)),
                  pl.BlockSpec((1, S, LANES), lambda b, n: (b, 0, 0)),
                  _const_spec(sink_rows.shape)],
        out_specs=pl.BlockSpec((1, W, QW), lambda b, n: (b, n, 0)),
        out_shape=jax.ShapeDtypeStruct((B, S, QW), BF16),
        compiler_params=_params("parallel", "parallel"),
        name="swa",
    )(q_pad, k, v, sink_rows)


def _mix_out0_kernel(of_ref, ob_ref, z_ref, swa_ref, x_ref, mod_ref, nw_ref, gn_ref, bd_ref, wg_ref, ws_ref,
                     o_ref):
    o = of_ref[0] + ob_ref[0]
    ms = _dot_split(o * o, bd_ref[...]) * (1.0 / GDN_DV)
    gdn = o * lax.rsqrt(ms + NORM_EPS) * gn_ref[...] * _silu(z_ref[0].astype(F32))
    y = _dot(gdn.astype(BF16), wg_ref[...]) + _dot(swa_ref[0], ws_ref[...])
    o_ref[0] = x_ref[0] + mod_ref[0, 0, 2:3, :] * _rms(y, nw_ref[...])


def _mix_out0(o_f, o_b, z, swa, xa, mod, norm_w, gdn_norm_row, head_ones, w_gdn, w_swa, n_lat_tiles):
    B, S, D = xa.shape
    return pl.pallas_call(
        _mix_out0_kernel,
        grid=(B, S // ROW_TILE),
        in_specs=[_row_spec(GDN_HD), _row_spec(GDN_HD), _row_spec(GDN_HD), _row_spec(swa.shape[-1]), _row_spec(D),
                  _mod_spec(n_lat_tiles), _const_spec((1, D)), _const_spec(gdn_norm_row.shape),
                  _const_spec(head_ones.shape), _const_spec(w_gdn.shape), _const_spec(w_swa.shape)],
        out_specs=_row_spec(D),
        out_shape=jax.ShapeDtypeStruct((B, S, D), F32),
        compiler_params=_params("parallel", "parallel"),
        name="mix_out0",
    )(o_f, o_b, z, swa, xa, mod, norm_w, gdn_norm_row, head_ones, w_gdn, w_swa)


def _ffn_kernel(x_ref, mod_ref, n_in_ref, n_out_ref, wg_ref, wu_ref, wd_ref, o_ref):
    x = x_ref[0]
    f = (_rms(x, n_in_ref[...]) * (1.0 + mod_ref[0, 0, 4:5, :]) + mod_ref[0, 0, 3:4, :]).astype(BF16)
    h = _silu(_dot(f, wg_ref[...])) * _dot(f, wu_ref[...])
    y = _dot(h.astype(BF16), wd_ref[...])
    o_ref[0] = x + mod_ref[0, 0, 5:6, :] * _rms(y, n_out_ref[...])


def _ffn(xa, mod, n_in, n_out, wg, wu, wd, n_lat_tiles):
    B, S, D = xa.shape
    return pl.pallas_call(
        _ffn_kernel,
        grid=(B, S // ROW_TILE),
        in_specs=[_row_spec(D), _mod_spec(n_lat_tiles), _const_spec((1, D)), _const_spec((1, D)),
                  _const_spec(wg.shape), _const_spec(wu.shape), _const_spec(wd.shape)],
        out_specs=_row_spec(D),
        out_shape=jax.ShapeDtypeStruct((B, S, D), F32),
        compiler_params=_params("parallel", "parallel"),
        name="dense_ffn",
    )(xa, mod, n_in, n_out, wg, wu, wd)


O_CQ = (0, MLA_Q_RANK)
O_CKV = (O_CQ[1], O_CQ[1] + MLA_KV_RANK)
O_KR = (O_CKV[1], O_CKV[1] + LANES)
O_DQ = (O_KR[1], O_KR[1] + DIFF_HEADS * LANES)
O_DK = (O_DQ[1], O_DQ[1] + DIFF_HEADS * LANES)
O_DV = (O_DK[1], O_DK[1] + DIFF_HEADS * LANES)


def _odd_proj_kernel(x_ref, mod_ref, nw_ref, w_ref, cos_ref, sin_ref, cosr_ref, sinr_ref,
                     cq_ref, ckv_ref, kr_ref, dq_ref, dk_ref, dv_ref):
    a = (_rms(x_ref[0], nw_ref[...]) * (1.0 + mod_ref[0, 0, 1:2, :]) + mod_ref[0, 0, 0:1, :]).astype(BF16)

    def proj(cols):
        return _dot(a, w_ref[:, cols[0]:cols[1]])

    cq_ref[0] = proj(O_CQ)
    ckv_ref[0] = proj(O_CKV)
    kr_ref[0] = _rope_slab(proj(O_KR), cosr_ref[...], sinr_ref[...], MLA_ROPE // 4)
    cos = cos_ref[...]
    sin = sin_ref[...]
    quarter = DIFF_HEAD_DIM // 4
    dq = proj(O_DQ)
    dk = proj(O_DK)
    dv = proj(O_DV)
    ones = jnp.ones((ROW_TILE, LANES), BF16)
    for h in range(DIFF_HEADS):
        ls = slice(h * LANES, (h + 1) * LANES)
        dq_ref[0, :, ls] = _rope_slab(dq[:, ls], cos, sin, quarter).astype(BF16)
        dk_ref[0, :, ls] = _rope_slab(dk[:, ls], cos, sin, quarter).astype(BF16)
        dv_ref[0, :, 2 * h * LANES:(2 * h + 1) * LANES] = dv[:, ls].astype(BF16)
        dv_ref[0, :, (2 * h + 1) * LANES:(2 * h + 2) * LANES] = ones


def _odd_proj(xa, mod, norm_w, w, cos, sin, cos_r, sin_r, n_lat_tiles):
    B, S, D = xa.shape
    widths = (MLA_Q_RANK, MLA_KV_RANK, LANES, DIFF_HEADS * LANES, DIFF_HEADS * LANES, 2 * DIFF_HEADS * LANES)
    dtypes = (F32, F32, F32, BF16, BF16, BF16)
    return pl.pallas_call(
        _odd_proj_kernel,
        grid=(B, S // ROW_TILE),
        in_specs=[_row_spec(D), _mod_spec(n_lat_tiles), _const_spec((1, D)), _const_spec(w.shape),
                  _table_spec(), _table_spec(), _table_spec(), _table_spec()],
        out_specs=[_row_spec(n) for n in widths],
        out_shape=[jax.ShapeDtypeStruct((B, S, n), dt) for n, dt in zip(widths, dtypes)],
        compiler_params=_params("parallel", "parallel"),
        name="odd_proj",
    )(xa, mod, norm_w, w, cos, sin, cos_r, sin_r)


def _mla_up_kernel(cq_ref, ckv_ref, kr_ref, qn_ref, kvn_ref, wq_ref, wk_ref, wv_ref, cos_ref, sin_ref,
                   q_ref, k_ref, v_ref):
    cq = _rms(cq_ref[0], qn_ref[...]).astype(BF16)
    ckv = _rms(ckv_ref[0], kvn_ref[...]).astype(BF16)
    qp = _dot(cq, wq_ref[...])
    kp = _dot(ckv, wk_ref[...])
    vp = _dot(ckv, wv_ref[...])
    kr = kr_ref[0]
    cos = cos_ref[...]
    sin = sin_ref[...]
    lane = lax.broadcasted_iota(jnp.int32, (ROW_TILE, LANES), 1)
    ones_hi = jnp.where(lane >= MLA_V, 1.0, 0.0)
    for h in range(MLA_HEADS):
        ls = slice(h * LANES, (h + 1) * LANES)
        q_ref[0, :, ls] = _rope_slab(qp[:, ls], cos, sin, MLA_ROPE // 4).astype(BF16)
        k_ref[0, :, ls] = (kp[:, ls] + kr).astype(BF16)
        v_ref[0, :, ls] = (vp[:, ls] + ones_hi).astype(BF16)


def _mla_up(cq, ckv, kr, q_norm, kv_norm, wq, wk, wv, cos_r, sin_r):
    B, S, _ = cq.shape
    HW = MLA_HEADS * LANES
    return pl.pallas_call(
        _mla_up_kernel,
        grid=(B, S // ROW_TILE),
        in_specs=[_row_spec(MLA_Q_RANK), _row_spec(MLA_KV_RANK), _row_spec(LANES),
                  _const_spec(q_norm.shape), _const_spec(kv_norm.shape),
                  _const_spec(wq.shape), _const_spec(wk.shape), _const_spec(wv.shape),
                  _table_spec(), _table_spec()],
        out_specs=[_row_spec(HW)] * 3,
        out_shape=[jax.ShapeDtypeStruct((B, S, HW), BF16)] * 3,
        compiler_params=_params("parallel", "parallel"),
        name="mla_up",
    )(cq, ckv, kr, q_norm, kv_norm, wq, wk, wv, cos_r, sin_r)


def _mla_kernel(q_ref, k_ref, v_ref, o_ref):
    s = _dot_nt(q_ref[0], k_ref[0])
    m = jnp.max(s, axis=-1, keepdims=True)
    acc = _dot(jnp.exp(s - m).astype(BF16), v_ref[0])
    o_ref[0] = (acc / acc[:, MLA_V:MLA_V + 1]).astype(o_ref.dtype)


def _mla(q, k, v_ext, T):
    B, S, _ = k.shape
    return pl.pallas_call(
        _mla_kernel,
        grid=(B, MLA_HEADS, T // ATTN_TQ),
        in_specs=[pl.BlockSpec((1, ATTN_TQ, LANES), lambda b, h, t: (b, t, h)),
                  pl.BlockSpec((1, S, LANES), lambda b, h, t: (b, 0, h)),
                  pl.BlockSpec((1, S, LANES), lambda b, h, t: (b, 0, h))],
        out_specs=pl.BlockSpec((1, ATTN_TQ, LANES), lambda b, h, t: (b, t, h)),
        out_shape=jax.ShapeDtypeStruct((B, T, MLA_HEADS * LANES), BF16),
        compiler_params=_params("parallel", "parallel", "parallel"),
        name="mla_attn",
    )(q, k, v_ext)


def _diff_kernel(q_ref, k_ref, v_ref, lam_ref, sub_ref, o_ref, *, post_scale):
    q = q_ref[0]
    tq = q.shape[0]
    lo = lax.broadcasted_iota(jnp.int32, q.shape, 1) < DIFF_HEAD_DIM
    zero = jnp.zeros_like(q)
    qs = jnp.concatenate([jnp.where(lo, q, zero), jnp.where(lo, zero, q)], axis=0)
    s = _dot_nt(qs, k_ref[0])
    m = jnp.max(s, axis=-1, keepdims=True)
    acc = _dot(jnp.exp(s - m).astype(BF16), v_ref[0])
    vw = 2 * DIFF_HEAD_DIM
    att = acc[:, :vw] / acc[:, vw:vw + 1]
    a = att[:tq] - lam_ref[...] * att[tq:]
    o_ref[0] = (_rms(a, sub_ref[...]) * post_scale).astype(o_ref.dtype)


def _diff(q, k, v_ext, lam, subln, post_scale, T):
    B, S, _ = k.shape
    tq = ATTN_TQ // 2
    return pl.pallas_call(
        functools.partial(_diff_kernel, post_scale=post_scale),
        grid=(B, DIFF_HEADS, T // tq),
        in_specs=[pl.BlockSpec((1, tq, LANES), lambda b, h, t: (b, t, h)),
                  pl.BlockSpec((1, S, LANES), lambda b, h, t: (b, 0, h)),
                  pl.BlockSpec((1, S, 2 * LANES), lambda b, h, t: (b, 0, h)),
                  pl.BlockSpec((1, 1), lambda b, h, t: (0, 0)),
                  pl.BlockSpec((1, LANES), lambda b, h, t: (0, 0))],
        out_specs=pl.BlockSpec((1, tq, LANES), lambda b, h, t: (b, t, h)),
        out_shape=jax.ShapeDtypeStruct((B, T, DIFF_HEADS * LANES), BF16),
        compiler_params=_params("parallel", "parallel", "parallel"),
        name="diff_attn",
    )(q, k, v_ext, lam, subln)


def _mix_out1_kernel(mla_ref, diff_ref, x_ref, mod_ref, n1_ref, n2_ref, w1_ref, w2_ref, wr_ref,
                     xo_ref, f_ref, lg_ref):
    y = _dot(mla_ref[0], w1_ref[...]) + _dot(diff_ref[0], w2_ref[...])
    x = x_ref[0] + mod_ref[0, 0, 2:3, :] * _rms(y, n1_ref[...])
    xo_ref[0] = x
    f = _rms(x, n2_ref[...]) * (1.0 + mod_ref[0, 0, 4:5, :]) + mod_ref[0, 0, 3:4, :]
    f_ref[0] = f.astype(BF16)
    lg_ref[0] = jnp.dot(f, wr_ref[...], precision=lax.Precision.HIGHEST, preferred_element_type=F32)


def _mix_out1(mla, diff, xa, mod, n1, n2, w1, w2, wr, n_lat_tiles):
    B, T, _ = mla.shape
    D = xa.shape[-1]
    return pl.pallas_call(
        _mix_out1_kernel,
        grid=(B, T // ROW_TILE),
        in_specs=[_row_spec(mla.shape[-1]), _row_spec(diff.shape[-1]), _row_spec(D), _mod_spec(n_lat_tiles),
                  _const_spec((1, D)), _const_spec((1, D)), _const_spec(w1.shape), _const_spec(w2.shape),
                  _const_spec(wr.shape)],
        out_specs=[_row_spec(D), _row_spec(D), _row_spec(LANES)],
        out_shape=[jax.ShapeDtypeStruct((B, T, D), F32), jax.ShapeDtypeStruct((B, T, D), BF16),
                   jax.ShapeDtypeStruct((B, T, LANES), F32)],
        compiler_params=_params("parallel", "parallel"),
        name="mix_out1",
    )(mla, diff, xa, mod, n1, n2, w1, w2, wr)


MOE_TM = 1024
MOE_TF = 512


def _moe_kernel(te_ref, ta_ref, x_ref, rw_ref, wg_ref, wu_ref, wd_ref, o_ref, acc_ref):
    i = pl.program_id(0)
    f = pl.program_id(1)

    @pl.when(f == 0)
    def _():
        acc_ref[...] = jnp.zeros_like(acc_ref)

    @pl.when(ta_ref[i] > 0)
    def _():
        x = x_ref[...]
        h = _silu(_dot(x, wg_ref[0].astype(BF16))) * _dot(x, wu_ref[0].astype(BF16))
        acc_ref[...] += _dot(h.astype(BF16), wd_ref[0].astype(BF16))

    @pl.when(f == pl.num_programs(1) - 1)
    def _():
        o_ref[...] = acc_ref[...] * rw_ref[...]


def _moe_experts(tile_expert, tile_active, x_sorted, row_w, wg, wu, wd):
    R, D = x_sorted.shape
    E, _, F = wg.shape
    nf = F // MOE_TF

    def f_eff(i, f, ta):
        return jnp.where(ta[i] > 0, f, nf - 1)

    grid_spec = pltpu.PrefetchScalarGridSpec(
        num_scalar_prefetch=2,
        grid=(R // MOE_TM, nf),
        in_specs=[pl.BlockSpec((MOE_TM, D), lambda i, f, te, ta: (i, 0)),
                  pl.BlockSpec((MOE_TM, 1), lambda i, f, te, ta: (i, 0)),
                  pl.BlockSpec((1, D, MOE_TF), lambda i, f, te, ta: (te[i], 0, f_eff(i, f, ta))),
                  pl.BlockSpec((1, D, MOE_TF), lambda i, f, te, ta: (te[i], 0, f_eff(i, f, ta))),
                  pl.BlockSpec((1, MOE_TF, D), lambda i, f, te, ta: (te[i], f_eff(i, f, ta), 0))],
        out_specs=pl.BlockSpec((MOE_TM, D), lambda i, f, te, ta: (i, 0)),
        scratch_shapes=[pltpu.VMEM((MOE_TM, D), F32)],
    )
    return pl.pallas_call(
        _moe_kernel,
        grid_spec=grid_spec,
        out_shape=jax.ShapeDtypeStruct((R, D), F32),
        compiler_params=_params("arbitrary", "arbitrary"),
        name="moe_experts",
    )(tile_expert, tile_active, x_sorted, row_w, wg, wu, wd)


def _norm_residual_kernel(a_ref, b_ref, x_ref, mod_ref, nw_ref, o_ref):
    y = a_ref[0] + b_ref[0]
    o_ref[0] = x_ref[0] + mod_ref[0, 0, 5:6, :] * _rms(y, nw_ref[...])


def _norm_residual(a, b, x, mod, norm_w, n_lat_tiles):
    B, T, D = x.shape
    return pl.pallas_call(
        _norm_residual_kernel,
        grid=(B, T // ROW_TILE),
        in_specs=[_row_spec(D), _row_spec(D), _row_spec(D), _mod_spec(n_lat_tiles), _const_spec((1, D))],
        out_specs=_row_spec(D),
        out_shape=jax.ShapeDtypeStruct((B, T, D), F32),
        compiler_params=_params("parallel", "parallel"),
        name="norm_residual",
    )(a, b, x, mod, norm_w)


def _route(logits):
    N = logits.shape[0]
    M = N * TOP_K
    R = M + N_EXPERTS * MOE_TM
    top_val, top_idx = lax.top_k(logits, TOP_K)
    top_w = jax.nn.softmax(top_val, axis=-1).reshape(-1)
    e_flat = top_idx.reshape(-1)
    onehot = (e_flat[:, None] == jnp.arange(N_EXPERTS)[None, :]).astype(jnp.int32)
    csum = jnp.cumsum(onehot, axis=0)
    pos = jnp.sum((csum - onehot) * onehot, axis=-1)
    counts = csum[-1]
    padded = ((counts + MOE_TM - 1) // MOE_TM) * MOE_TM
    ends = jnp.cumsum(padded)
    starts = ends - padded
    ustarts = jnp.cumsum(counts) - counts
    dest = starts[e_flat] + pos
    order = jnp.argsort(e_flat, stable=True).astype(jnp.int32)
    tile_start = jnp.arange(R // MOE_TM, dtype=jnp.int32) * MOE_TM
    tile_active = (tile_start < ends[-1]).astype(jnp.int32)
    te = jnp.minimum(jnp.searchsorted(ends, tile_start, side="right"), N_EXPERTS - 1).astype(jnp.int32)
    last_e = te[jnp.maximum(ends[-1] // MOE_TM - 1, 0)]
    tile_expert = jnp.where(tile_active > 0, te, last_e)
    rows = jnp.arange(R, dtype=jnp.int32)
    e_row = tile_expert[rows // MOE_TM]
    within = rows - starts[e_row]
    valid = (within < counts[e_row]) & (rows < ends[-1])
    src = order[jnp.clip(ustarts[e_row] + within, 0, M - 1)]
    row_token = jnp.where(valid, src // TOP_K, 0)
    row_w = jnp.where(valid, top_w[src], 0.0)
    return dest.reshape(N, TOP_K), row_token, row_w[:, None], tile_expert, tile_active


def _gather_rows_bf16(x, idx):
    N, D = x.shape
    xu = lax.bitcast_convert_type(x.reshape(N, D // 2, 2), jnp.uint32)
    return lax.bitcast_convert_type(xu[idx], BF16).reshape(idx.shape[0], D)


def _rope_tables(rows, rot_dim):
    row = jnp.repeat(jnp.arange(rows, dtype=F32), GRID_W)
    col = jnp.tile(jnp.arange(GRID_W, dtype=F32), rows)
    half = rot_dim // 2
    inv_freq = ROPE_THETA ** (-jnp.arange(0, half, 2, dtype=F32) / half)
    ang_r = row[:, None] * inv_freq[None, :]
    ang_c = col[:, None] * inv_freq[None, :]
    ang = jnp.concatenate([ang_r, ang_r, ang_c, ang_c], axis=-1)
    return jnp.cos(ang), jnp.sin(ang)


def _stream_tables(cos, sin, lane0):
    T, R = cos.shape
    if lane0 == 0:
        cos_l, sin_l = jnp.tile(cos, (1, LANES // R)), jnp.tile(sin, (1, LANES // R))
    else:
        pad = ((0, 0), (lane0, LANES - lane0 - R))
        cos_l = jnp.pad(cos - 1.0, pad) + 1.0
        sin_l = jnp.pad(sin, pad)
    ctx = ((0, CTX_LEN), (0, 0))
    return jnp.pad(cos_l - 1.0, ctx) + 1.0, jnp.pad(sin_l, ctx)


def _mod_table(c, c_ctx, mod_w, mod_b):
    B = c.shape[0]
    hp = lax.Precision.HIGHEST
    m_l = (jnp.dot(jax.nn.silu(c), mod_w, precision=hp) + mod_b).reshape(B, 1, 6, D_MODEL)
    m_c = (jnp.dot(jax.nn.silu(c_ctx), mod_w, precision=hp) + mod_b).reshape(1, 1, 6, D_MODEL)
    return jnp.concatenate([jnp.broadcast_to(m_c, m_l.shape), m_l], axis=1)


def _pad_cols(w, n):
    return jnp.pad(w, ((0, 0), (0, n - w.shape[1])))


def _even_layer(xa, mod, norms, w_in, conv_w, a_log, dt_bias, gdn_norm, sink, w_out, ffn_gate, ffn_up, ffn_down,
                cos, sin, n_lat_tiles):
    B, S, D = xa.shape
    T = n_lat_tiles * ROW_TILE
    n_gate = 2 * GDN_HEADS
    c_qkv = 3 * GDN_HD
    c_z = c_qkv + GDN_HD
    c_ga = c_z + n_gate
    c_gb = c_ga + n_gate
    c_sq = c_gb + SWA_Q_HEADS * SWA_HEAD_DIM
    c_sk = c_sq + SWA_KV_HEADS * SWA_HEAD_DIM
    grp = SWA_Q_HEADS // SWA_KV_HEADS
    wq = w_in[:, c_gb:c_sq].reshape(D, SWA_Q_HEADS, SWA_HEAD_DIM) * SWA_HEAD_DIM ** -0.5
    kv_of_head = (jnp.arange(SWA_Q_HEADS) // grp)[None, :, None]
    wq = jnp.concatenate([jnp.where(kv_of_head == 0, wq, 0.0), jnp.where(kv_of_head == 1, wq, 0.0)], axis=-1)
    w_all = jnp.concatenate([w_in[:, :c_z], _pad_cols(w_in[:, c_z:c_gb], LANES),
                             wq.reshape(D, SWA_Q_HEADS * LANES), w_in[:, c_sq:]], axis=1).astype(BF16)
    qkv, z, gate, q_pad, k, v = _even_proj(xa, mod, norms[0:1], w_all, cos, sin, n_lat_tiles)

    head_ones = (jnp.arange(GDN_HD)[:, None] // GDN_DK == jnp.arange(GDN_HD)[None, :] // GDN_DK).astype(BF16)
    conv_w8 = jnp.pad(conv_w, ((0, SUBLANES - CONV_K), (0, 0)))
    gq, gk, gv = _gdn_prep(qkv, conv_w8, head_ones, n_lat_tiles)
    a_log_row = _pad_cols(a_log.reshape(1, n_gate), LANES)
    dt_bias_row = _pad_cols(dt_bias.reshape(1, n_gate), LANES)
    o_f, o_b = _gdn(gq, gk, gv, gate, a_log_row, dt_bias_row, T // GDN_CHUNK)

    sink_rows = jnp.repeat(sink, SWA_BLOCK)[:, None]
    swa = _swa(q_pad, k, v, sink_rows, T // SWA_BLOCK)

    w_s = w_out[GDN_HD:].reshape(SWA_Q_HEADS, SWA_HEAD_DIM, D)
    w_s = jnp.concatenate([jnp.where(kv_of_head.reshape(-1, 1, 1) == 0, w_s, 0.0),
                           jnp.where(kv_of_head.reshape(-1, 1, 1) == 1, w_s, 0.0)], axis=1)
    gdn_norm_row = jnp.tile(gdn_norm, GDN_HEADS)[None, :]
    xa = _mix_out0(o_f, o_b, z, swa, xa, mod, norms[1:2], gdn_norm_row, head_ones,
                   w_out[:GDN_HD].astype(BF16), w_s.reshape(SWA_Q_HEADS * LANES, D).astype(BF16), n_lat_tiles)
    return _ffn(xa, mod, norms[2:3], norms[3:4], ffn_gate.astype(BF16), ffn_up.astype(BF16),
                ffn_down.astype(BF16), n_lat_tiles)


def _odd_layer_last(xa, mod, norms, w_in, q_norm, kv_norm, w_uq, w_ukv, lam_p, lam_init, subln, w_out,
                    router, exp_gate, exp_up, exp_down, cos, sin, cos_r, sin_r, n_lat_tiles):
    B, S, D = xa.shape
    T = n_lat_tiles * ROW_TILE
    H = MLA_HEADS
    c0 = MLA_Q_RANK
    c1 = c0 + MLA_KV_RANK
    c2 = c1 + MLA_ROPE
    dw = DIFF_HEADS * 2 * DIFF_HEAD_DIM
    w_kr = jnp.pad(w_in[:, c1:c2], ((0, 0), (MLA_NOPE, LANES - MLA_NOPE - MLA_ROPE)))
    w_all = jnp.concatenate([w_in[:, :c1], w_kr, w_in[:, c2:c2 + dw] * DIFF_HEAD_DIM ** -0.5,
                             w_in[:, c2 + dw:]], axis=1).astype(BF16)
    cq, ckv, kr, dq, dk, dv = _odd_proj(xa, mod, norms[0:1], w_all, cos, sin, cos_r, sin_r, n_lat_tiles)

    qd = MLA_NOPE + MLA_ROPE
    wq = jnp.pad(w_uq.reshape(MLA_Q_RANK, H, qd) * qd ** -0.5, ((0, 0), (0, 0), (0, LANES - qd)))
    wkv = w_ukv.reshape(MLA_KV_RANK, H, MLA_NOPE + MLA_V)
    wk = jnp.pad(wkv[..., :MLA_NOPE], ((0, 0), (0, 0), (0, LANES - MLA_NOPE)))
    wv = jnp.pad(wkv[..., MLA_NOPE:], ((0, 0), (0, 0), (0, LANES - MLA_V)))
    q_cat, k_cat, v_ext = _mla_up(cq, ckv, kr, q_norm[None, :], kv_norm[None, :],
                                  wq.reshape(MLA_Q_RANK, H * LANES).astype(BF16),
                                  wk.reshape(MLA_KV_RANK, H * LANES).astype(BF16),
                                  wv.reshape(MLA_KV_RANK, H * LANES).astype(BF16), cos_r, sin_r)
    mla = _mla(q_cat, k_cat, v_ext, T)

    lam = (jnp.exp(jnp.sum(lam_p[0] * lam_p[1])) - jnp.exp(jnp.sum(lam_p[2] * lam_p[3])) + lam_init).reshape(1, 1)
    diff = _diff(dq, dk, dv, lam, subln[None, :], 1.0 - lam_init, T)

    w_mla = jnp.pad(w_out[:H * MLA_V].reshape(H, MLA_V, D), ((0, 0), (0, LANES - MLA_V), (0, 0)))
    x, f_in, logits = _mix_out1(mla, diff, xa, mod, norms[1:2], norms[2:3],
                                w_mla.reshape(H * LANES, D).astype(BF16), w_out[H * MLA_V:].astype(BF16),
                                _pad_cols(router, LANES), n_lat_tiles)

    dest, row_token, row_w, tile_expert, tile_active = _route(logits.reshape(B * T, LANES)[:, :N_EXPERTS])
    x_sorted = _gather_rows_bf16(f_in.reshape(B * T, D), row_token)
    y = _moe_experts(tile_expert, tile_active, x_sorted, row_w, exp_gate, exp_up, exp_down)
    y0 = y[dest[:, 0]].reshape(B, T, D)
    y1 = y[dest[:, 1]].reshape(B, T, D)
    return _norm_residual(y0, y1, x, mod, norms[3:4], n_lat_tiles)


def kernel(x, c, ctx, c_ctx, e_mod_w, e_mod_b, e_norms, e_w_in, e_conv_w, e_a_log, e_dt_bias, e_gdn_norm, e_sink, e_w_out, e_ffn_gate, e_ffn_up, e_ffn_down, o_mod_w, o_mod_b, o_norms, o_w_in, o_q_norm, o_kv_norm, o_w_uq, o_w_ukv, o_lambda, o_subln, o_w_out, o_router, o_exp_gate, o_exp_up, o_exp_down):
    B, T, D = x.shape
    n_lat_tiles = T // ROW_TILE
    rows = T // GRID_W
    cos64, sin64 = _stream_tables(*_rope_tables(rows, SWA_HEAD_DIM), 0)
    cos_r, sin_r = _stream_tables(*_rope_tables(rows, MLA_ROPE), MLA_NOPE)
    xa = jnp.concatenate([x, ctx], axis=1)
    mod_e = _mod_table(c, c_ctx, e_mod_w[0], e_mod_b[0])
    xa = _even_layer(xa, mod_e, e_norms[0], e_w_in[0], e_conv_w[0], e_a_log[0], e_dt_bias[0], e_gdn_norm[0],
                     e_sink[0], e_w_out[0], e_ffn_gate[0], e_ffn_up[0], e_ffn_down[0], cos64, sin64, n_lat_tiles)
    mod_o = _mod_table(c, c_ctx, o_mod_w[0], o_mod_b[0])
    lam_init = 0.8 - 0.6 * math.exp(-0.3 * 1)
    return _odd_layer_last(xa, mod_o, o_norms[0], o_w_in[0], o_q_norm[0], o_kv_norm[0], o_w_uq[0], o_w_ukv[0],
                           o_lambda[0], lam_init, o_subln[0], o_w_out[0], o_router[0], o_exp_gate[0],
                           o_exp_up[0], o_exp_down[0], cos64, sin64, cos_r, sin_r, n_lat_tiles)
```

```python
import functools
import math

import jax
import jax.numpy as jnp
from jax import lax
from jax.experimental import pallas as pl
from jax.experimental.pallas import tpu as pltpu

F32 = jnp.float32
BF16 = jnp.bfloat16

D_MODEL = 1024
CTX_LEN = 256
GRID_W = 64
NORM_EPS = 1e-6
ROPE_THETA = 10000.0

GDN_HEADS = 8
GDN_DK = 64
GDN_DV = 64
GDN_CHUNK = 64
CONV_K = 5
GDN_GROUP = 4
GDN_LANES = GDN_GROUP * GDN_DK
GDN_HD = GDN_HEADS * GDN_DK

SWA_Q_HEADS = 8
SWA_KV_HEADS = 2
SWA_HEAD_DIM = 64
SWA_WINDOW = 128
SWA_BLOCK = 128

MLA_HEADS = 8
MLA_Q_RANK = 384
MLA_KV_RANK = 256
MLA_NOPE = 64
MLA_ROPE = 32
MLA_V = 64

DIFF_HEADS = 4
DIFF_HEAD_DIM = 64

D_FF = 2816
N_EXPERTS = 8
TOP_K = 2
D_FF_EXPERT = 3584

LANES = 128
SUBLANES = 8
ROW_TILE = 256
ATTN_TQ = 1024
ATTN_SUB = 128
VMEM_LIMIT = 56 * 1024 * 1024
NEG_BIG = -1e30


def _dot(a, b):
    return jnp.dot(a, b, preferred_element_type=F32)


def _dot_nt(a, b):
    return lax.dot_general(a, b, (((1,), (1,)), ((), ())), preferred_element_type=F32)


def _dot_tn(a, b):
    return lax.dot_general(a, b, (((0,), (0,)), ((), ())), preferred_element_type=F32)


def _split_bf16(x):
    hi = x.astype(BF16)
    lo = (x - hi.astype(F32)).astype(BF16)
    return hi, lo


def _dot_split(x, w):
    hi, lo = _split_bf16(x)
    return _dot(hi, w) + _dot(lo, w)


def _dot_rsplit(w, x):
    hi, lo = _split_bf16(x)
    return _dot(w, hi) + _dot(w, lo)


def _rms(x, w):
    return x * lax.rsqrt(jnp.mean(x * x, axis=-1, keepdims=True) + NORM_EPS) * w


def _silu(x):
    return x * jax.nn.sigmoid(x)


def _rope_slab(x, cos, sin, quarter):
    lane = lax.broadcasted_iota(jnp.int32, x.shape, 1)
    fwd = pltpu.roll(x, quarter, 1)
    back = pltpu.roll(x, LANES - quarter, 1)
    rot = jnp.where(lane % (2 * quarter) < quarter, -back, fwd)
    return x * cos + rot * sin


def _params(*sem):
    return pltpu.CompilerParams(dimension_semantics=sem, vmem_limit_bytes=VMEM_LIMIT)


def _row_spec(width, tile=ROW_TILE):
    return pl.BlockSpec((1, tile, width), lambda b, t: (b, t, 0))


def _mod_spec(n_lat_tiles):
    return pl.BlockSpec((1, 1, 6, D_MODEL), lambda b, t: (b, jnp.where(t < n_lat_tiles, 1, 0), 0, 0))


def _const_spec(shape):
    return pl.BlockSpec(shape, lambda b, t: (0,) * len(shape))


def _table_spec():
    return pl.BlockSpec((ROW_TILE, LANES), lambda b, t: (t, 0))


E_QKV = (0, 3 * GDN_HD)
E_Z = (E_QKV[1], E_QKV[1] + GDN_HD)
E_GATE = (E_Z[1], E_Z[1] + LANES)
E_Q = (E_GATE[1], E_GATE[1] + SWA_Q_HEADS * LANES)
E_K = (E_Q[1], E_Q[1] + LANES)
E_V = (E_K[1], E_K[1] + LANES)


def _even_proj_kernel(x_ref, mod_ref, nw_ref, w_ref, cos_ref, sin_ref,
                      qkv_ref, z_ref, gate_ref, q_ref, k_ref, v_ref):
    a = (_rms(x_ref[0], nw_ref[...]) * (1.0 + mod_ref[0, 0, 1:2, :]) + mod_ref[0, 0, 0:1, :]).astype(BF16)

    def proj(cols):
        return _dot(a, w_ref[:, cols[0]:cols[1]])

    qkv_ref[0] = proj(E_QKV)
    z_ref[0] = proj(E_Z).astype(BF16)
    gate_ref[0] = proj(E_GATE)
    cos = cos_ref[...]
    sin = sin_ref[...]
    quarter = SWA_HEAD_DIM // 4
    qp = proj(E_Q)
    for h in range(SWA_Q_HEADS):
        ls = slice(h * LANES, (h + 1) * LANES)
        q_ref[0, :, ls] = _rope_slab(qp[:, ls], cos, sin, quarter).astype(BF16)
    k_ref[0] = _rope_slab(proj(E_K), cos, sin, quarter).astype(BF16)
    v_ref[0] = proj(E_V).astype(BF16)


def _even_proj(xa, mod, norm_w, w, cos, sin, n_lat_tiles):
    B, S, D = xa.shape
    widths = (E_QKV[1] - E_QKV[0], GDN_HD, LANES, SWA_Q_HEADS * LANES, LANES, LANES)
    dtypes = (F32, BF16, F32, BF16, BF16, BF16)
    return pl.pallas_call(
        _even_proj_kernel,
        grid=(B, S // ROW_TILE),
        in_specs=[_row_spec(D), _mod_spec(n_lat_tiles), _const_spec((1, D)), _const_spec(w.shape),
                  _table_spec(), _table_spec()],
        out_specs=[_row_spec(n) for n in widths],
        out_shape=[jax.ShapeDtypeStruct((B, S, n), dt) for n, dt in zip(widths, dtypes)],
        compiler_params=_params("parallel", "parallel"),
        name="even_proj",
    )(xa, mod, norm_w, w, cos, sin)


def _gdn_prep_kernel(x_ref, prev_ref, next_ref, cw_ref, bd_ref, q_ref, k_ref, v_ref, *, n_lat_tiles):
    t = pl.program_id(1)
    has_prev = jnp.logical_and(t > 0, t < n_lat_tiles)
    has_next = t < n_lat_tiles - 1
    prev = jnp.where(has_prev, prev_ref[0], 0.0)
    nxt = jnp.where(has_next, next_ref[0], 0.0)
    xe = jnp.concatenate([prev, x_ref[0], nxt], axis=0)
    pad = CONV_K // 2
    y = jnp.zeros(x_ref.shape[1:], F32)
    for kk in range(CONV_K):
        lo = SUBLANES + kk - pad
        y = y + cw_ref[kk:kk + 1, :] * xe[lo:lo + ROW_TILE, :]
    u = _silu(y)

    def headnorm(a):
        return a * lax.rsqrt(_dot_split(a * a, bd_ref[...]) + NORM_EPS)

    q_ref[0] = headnorm(u[:, :GDN_HD]) * GDN_DK ** -0.5
    k_ref[0] = headnorm(u[:, GDN_HD:2 * GDN_HD])
    v_ref[0] = u[:, 2 * GDN_HD:]


def _gdn_prep(qkv, conv_w, head_ones, n_lat_tiles):
    B, S, W = qkv.shape
    per_tile = ROW_TILE // SUBLANES
    n8 = S // SUBLANES
    return pl.pallas_call(
        functools.partial(_gdn_prep_kernel, n_lat_tiles=n_lat_tiles),
        grid=(B, S // ROW_TILE),
        in_specs=[_row_spec(W),
                  pl.BlockSpec((1, SUBLANES, W), lambda b, t: (b, jnp.maximum(t * per_tile - 1, 0), 0)),
                  pl.BlockSpec((1, SUBLANES, W), lambda b, t: (b, jnp.minimum((t + 1) * per_tile, n8 - 1), 0)),
                  _const_spec(conv_w.shape), _const_spec(head_ones.shape)],
        out_specs=[_row_spec(GDN_HD)] * 3,
        out_shape=[jax.ShapeDtypeStruct((B, S, GDN_HD), F32)] * 3,
        compiler_params=_params("parallel", "parallel"),
        name="gdn_prep",
    )(qkv, qkv, qkv, conv_w, head_ones)


def _block_diag(x, head_of_lane):
    zero = jnp.zeros_like(x)
    return jnp.concatenate([jnp.where(head_of_lane == h, x, zero) for h in range(GDN_GROUP)], axis=0)


def _expand_heads(x4, head_of_lane):
    c = x4.shape[0]
    out = jnp.broadcast_to(x4[:, GDN_GROUP - 1:GDN_GROUP], (c, GDN_LANES))
    for h in range(GDN_GROUP - 2, -1, -1):
        out = jnp.where(head_of_lane == h, jnp.broadcast_to(x4[:, h:h + 1], (c, GDN_LANES)), out)
    return out


def _gdn_groups(items):
    C = GDN_CHUNK
    row = lax.broadcasted_iota(jnp.int32, (C, GDN_LANES), 0)
    lane = lax.broadcasted_iota(jnp.int32, (C, GDN_LANES), 1)
    hol = lane // GDN_DK
    col = lane % GDN_DK
    r64 = lax.broadcasted_iota(jnp.int32, (C, C), 0)
    c64 = lax.broadcasted_iota(jnp.int32, (C, C), 1)
    ones = jnp.ones((C, C), BF16)
    eye = jnp.where(row == col, 1.0, 0.0)

    def bd(x):
        return _block_diag(x, hol)

    st = []
    for (q4, k4, v4, g4, b4, s4, reverse) in items:
        sgn = -1 if reverse else 1
        ahead = (row - col) * sgn
        incl = ahead >= 0
        strict = ahead > 0
        tri = jnp.where((r64 - c64) * sgn >= 0, 1.0, 0.0).astype(BF16)
        gE = _expand_heads(g4, hol)
        bE = _expand_heads(b4, hol)
        gc = _dot_rsplit(tri, gE)
        gc_row = _dot_rsplit(ones, jnp.where((col - row) * sgn >= 0, gE, 0.0))
        decay = jnp.where(incl, jnp.exp(gc - gc_row), 0.0)
        g_last = gc[0:1, :] if reverse else gc[C - 1:C, :]
        eg = jnp.exp(gc)
        kb = k4 * bE
        st.append(dict(q=q4, k=k4, s=s4, incl=incl, strict=strict, decay=decay, g_last=g_last,
                       kb=kb, vb=v4 * bE, kbg=kb * eg, qh=q4 * eg, kt=k4 * jnp.exp(g_last - gc)))

    for d in st:
        gram = _dot_nt(jnp.concatenate([d["kb"], d["q"]], axis=0).astype(BF16), bd(d["k"].astype(BF16)))
        L = jnp.where(d["strict"], gram[:C] * d["decay"], 0.0)
        d["A"] = gram[C:] * d["decay"]
        d["T"] = eye - L
        d["P"] = L.astype(BF16)
    for _ in range(int(math.log2(C)) - 1):
        for d in st:
            d["P"] = _dot(d["P"], bd(d["P"])).astype(BF16)
        for d in st:
            d["T"] = d["T"] + _dot(d["T"].astype(BF16), bd(d["P"]))
    for d in st:
        Tb = d["T"].astype(BF16)
        d["u"] = _dot(Tb, bd(d["vb"].astype(BF16)))
        d["w"] = _dot(Tb, bd(d["kbg"].astype(BF16)))
    for d in st:
        d["ws_qs"] = _dot(jnp.concatenate([d["w"], d["qh"]], axis=0).astype(BF16), bd(d["s"].astype(BF16)))
    out = []
    for d in st:
        v_new = d["u"] - d["ws_qs"][:C]
        vb16 = v_new.astype(BF16)
        o = d["ws_qs"][C:] + _dot(d["A"].astype(BF16), bd(vb16))
        full = _dot_tn(d["kt"].astype(BF16), vb16)
        upd = jnp.zeros((GDN_DK, GDN_LANES), F32)
        for h in range(GDN_GROUP):
            upd = upd + jnp.where(hol == h, full[h * GDN_DK:(h + 1) * GDN_DK, :], 0.0)
        out.append((o, d["s"] * jnp.exp(d["g_last"]) + upd))
    return out


def _gdn_kernel(qf_ref, kf_ref, vf_ref, gf_ref, qb_ref, kb_ref, vb_ref, gb_ref, al_ref, dtb_ref,
                of_ref, ob_ref, s_ref):
    @pl.when(pl.program_id(1) == 0)
    def _():
        s_ref[...] = jnp.zeros_like(s_ref)

    n_grp = GDN_HEADS // GDN_GROUP
    n_gate = 2 * GDN_HEADS
    items = []
    for d, (q_ref, k_ref, v_ref, g_ref) in enumerate(((qf_ref, kf_ref, vf_ref, gf_ref),
                                                      (qb_ref, kb_ref, vb_ref, gb_ref))):
        raw = g_ref[0]
        xs = raw + dtb_ref[...]
        softplus = jnp.maximum(xs, 0.0) + jnp.log(1.0 + jnp.exp(-jnp.abs(xs)))
        g = -jnp.exp(al_ref[...]) * softplus
        beta = jax.nn.sigmoid(raw)
        for grp in range(n_grp):
            ls = slice(grp * GDN_LANES, (grp + 1) * GDN_LANES)
            c0 = d * GDN_HEADS + grp * GDN_GROUP
            items.append((q_ref[0, :, ls], k_ref[0, :, ls], v_ref[0, :, ls],
                          g[:, c0:c0 + GDN_GROUP], beta[:, n_gate + c0:n_gate + c0 + GDN_GROUP],
                          s_ref[d * n_grp + grp], d == 1))
    res = _gdn_groups(items)
    for d, o_ref in enumerate((of_ref, ob_ref)):
        for grp in range(n_grp):
            o, s_new = res[d * n_grp + grp]
            o_ref[0, :, grp * GDN_LANES:(grp + 1) * GDN_LANES] = o
            s_ref[d * n_grp + grp] = s_new


def _gdn(q, k, v, gate, a_log_row, dt_bias_row, n_lat_chunks):
    B, S, HD = q.shape
    n_chunks = S // GDN_CHUNK
    n_ctx = n_chunks - n_lat_chunks

    def fwd_c(s):
        return jnp.where(s < n_ctx, n_lat_chunks + s, s - n_ctx)

    def bwd_c(s):
        return n_chunks - 1 - s

    def spec(cmap, width):
        return pl.BlockSpec((1, GDN_CHUNK, width), lambda b, s: (b, cmap(s), 0))

    return pl.pallas_call(
        _gdn_kernel,
        grid=(B, n_chunks),
        in_specs=[spec(fwd_c, HD), spec(fwd_c, HD), spec(fwd_c, HD), spec(fwd_c, LANES),
                  spec(bwd_c, HD), spec(bwd_c, HD), spec(bwd_c, HD), spec(bwd_c, LANES),
                  _const_spec((1, LANES)), _const_spec((1, LANES))],
        out_specs=[spec(fwd_c, HD), spec(bwd_c, HD)],
        out_shape=[jax.ShapeDtypeStruct((B, S, HD), F32)] * 2,
        scratch_shapes=[pltpu.VMEM((2 * GDN_HEADS // GDN_GROUP, GDN_DK, GDN_LANES), F32)],
        compiler_params=_params("parallel", "arbitrary"),
        name="gdn_scan",
    )(q, k, v, gate, q, k, v, gate, a_log_row, dt_bias_row)


def _swa_kernel(q_ref, k_ref, v_ref, sink_ref, o_ref, *, n_lat_blocks):
    W = SWA_BLOCK
    H = SWA_Q_HEADS
    n = pl.program_id(1)
    is_lat = n < n_lat_blocks
    nl = jnp.minimum(n, n_lat_blocks - 1)
    prev = jnp.maximum(nl - 1, 0)
    nxt = jnp.minimum(nl + 1, n_lat_blocks - 1)
    ctx0 = n_lat_blocks * W

    def keys(ref):
        def blk(i):
            return ref[0, pl.ds(pl.multiple_of(i * W, W), W), :]
        return jnp.concatenate([ref[0, ctx0:ctx0 + CTX_LEN, :], blk(prev), blk(nl), blk(nxt)], axis=0)

    k_all = keys(k_ref)
    v_all = keys(v_ref)
    nk = CTX_LEN + 3 * W
    v_ext = jnp.concatenate([v_all, jnp.ones((nk, LANES), BF16)], axis=1)
    q = jnp.concatenate([q_ref[0, :, h * LANES:(h + 1) * LANES] for h in range(H)], axis=0)
    s = _dot_nt(q, k_all)
    ii = lax.broadcasted_iota(jnp.int32, (H * W, nk), 0) % W
    jj = lax.broadcasted_iota(jnp.int32, (H * W, nk), 1) - CTX_LEN
    in_window = (((jj >= 0) & (jj < W) & (jj >= ii) & (nl > 0))
                 | ((jj >= W) & (jj < 2 * W))
                 | ((jj >= 2 * W) & (jj - 2 * W <= ii) & (nl < n_lat_blocks - 1)))
    valid = (jj < 0) | (in_window & is_lat)
    s = jnp.where(valid, s, NEG_BIG)
    sk = sink_ref[...]
    m = jnp.maximum(jnp.max(s, axis=-1, keepdims=True), sk)
    acc = _dot(jnp.exp(s - m).astype(BF16), v_ext)
    o = acc[:, :LANES] / (acc[:, LANES:LANES + 1] + jnp.exp(sk - m))
    for h in range(H):
        o_ref[0, :, h * LANES:(h + 1) * LANES] = o[h * W:(h + 1) * W].astype(o_ref.dtype)


def _swa(q_pad, k, v, sink_rows, n_lat_blocks):
    B, S, _ = k.shape
    W = SWA_BLOCK
    QW = SWA_Q_HEADS * LANES
    return pl.pallas_call(
        functools.partial(_swa_kernel, n_lat_blocks=n_lat_blocks),
        grid=(B, S // W),
        in_specs=[pl.BlockSpec((1, W, QW), lambda b, n: (b, n, 0)),
                  pl.BlockSpec((1, S, LANES), lambda b, n: (b, 0, 0)),
                  pl.BlockSpec((1, S, LANES), lambda b, n: (b, 0, 0)),
                  _const_spec(sink_rows.shape)],
        out_specs=pl.BlockSpec((1, W, QW), lambda b, n: (b, n, 0)),
        out_shape=jax.ShapeDtypeStruct((B, S, QW), BF16),
        compiler_params=_params("parallel", "parallel"),
        name="swa",
    )(q_pad, k, v, sink_rows)


def _mix_out0_kernel(of_ref, ob_ref, z_ref, swa_ref, x_ref, mod_ref, nw_ref, gn_ref, bd_ref, wg_ref, ws_ref,
                     o_ref):
    o = of_ref[0] + ob_ref[0]
    ms = _dot_split(o * o, bd_ref[...]) * (1.0 / GDN_DV)
    gdn = o * lax.rsqrt(ms + NORM_EPS) * gn_ref[...] * _silu(z_ref[0].astype(F32))
    y = _dot(gdn.astype(BF16), wg_ref[...]) + _dot(swa_ref[0], ws_ref[...])
    o_ref[0] = x_ref[0] + mod_ref[0, 0, 2:3, :] * _rms(y, nw_ref[...])


def _mix_out0(o_f, o_b, z, swa, xa, mod, norm_w, gdn_norm_row, head_ones, w_gdn, w_swa, n_lat_tiles):
    B, S, D = xa.shape
    return pl.pallas_call(
        _mix_out0_kernel,
        grid=(B, S // ROW_TILE),
        in_specs=[_row_spec(GDN_HD), _row_spec(GDN_HD), _row_spec(GDN_HD), _row_spec(swa.shape[-1]), _row_spec(D),
                  _mod_spec(n_lat_tiles), _const_spec((1, D)), _const_spec(gdn_norm_row.shape),
                  _const_spec(head_ones.shape), _const_spec(w_gdn.shape), _const_spec(w_swa.shape)],
        out_specs=_row_spec(D),
        out_shape=jax.ShapeDtypeStruct((B, S, D), F32),
        compiler_params=_params("parallel", "parallel"),
        name="mix_out0",
    )(o_f, o_b, z, swa, xa, mod, norm_w, gdn_norm_row, head_ones, w_gdn, w_swa)


def _ffn_kernel(x_ref, mod_ref, n_in_ref, n_out_ref, wg_ref, wu_ref, wd_ref, o_ref):
    x = x_ref[0]
    f = (_rms(x, n_in_ref[...]) * (1.0 + mod_ref[0, 0, 4:5, :]) + mod_ref[0, 0, 3:4, :]).astype(BF16)
    h = _silu(_dot(f, wg_ref[...])) * _dot(f, wu_ref[...])
    y = _dot(h.astype(BF16), wd_ref[...])
    o_ref[0] = x + mod_ref[0, 0, 5:6, :] * _rms(y, n_out_ref[...])


def _ffn(xa, mod, n_in, n_out, wg, wu, wd, n_lat_tiles):
    B, S, D = xa.shape
    return pl.pallas_call(
        _ffn_kernel,
        grid=(B, S // ROW_TILE),
        in_specs=[_row_spec(D), _mod_spec(n_lat_tiles), _const_spec((1, D)), _const_spec((1, D)),
                  _const_spec(wg.shape), _const_spec(wu.shape), _const_spec(wd.shape)],
        out_specs=_row_spec(D),
        out_shape=jax.ShapeDtypeStruct((B, S, D), F32),
        compiler_params=_params("parallel", "parallel"),
        name="dense_ffn",
    )(xa, mod, n_in, n_out, wg, wu, wd)


O_CQ = (0, MLA_Q_RANK)
O_CKV = (O_CQ[1], O_CQ[1] + MLA_KV_RANK)
O_KR = (O_CKV[1], O_CKV[1] + LANES)
O_DQ = (O_KR[1], O_KR[1] + DIFF_HEADS * LANES)
O_DK = (O_DQ[1], O_DQ[1] + DIFF_HEADS * LANES)
O_DV = (O_DK[1], O_DK[1] + DIFF_HEADS * LANES)


def _odd_proj_kernel(x_ref, mod_ref, nw_ref, w_ref, cos_ref, sin_ref, cosr_ref, sinr_ref,
                     cq_ref, ckv_ref, kr_ref, dq_ref, dk_ref, dv_ref):
    a = (_rms(x_ref[0], nw_ref[...]) * (1.0 + mod_ref[0, 0, 1:2, :]) + mod_ref[0, 0, 0:1, :]).astype(BF16)

    def proj(cols):
        return _dot(a, w_ref[:, cols[0]:cols[1]])

    cq_ref[0] = proj(O_CQ)
    ckv_ref[0] = proj(O_CKV)
    kr_ref[0] = _rope_slab(proj(O_KR), cosr_ref[...], sinr_ref[...], MLA_ROPE // 4)
    cos = cos_ref[...]
    sin = sin_ref[...]
    quarter = DIFF_HEAD_DIM // 4
    dq = proj(O_DQ)
    dk = proj(O_DK)
    dv = proj(O_DV)
    ones = jnp.ones((ROW_TILE, LANES), BF16)
    for h in range(DIFF_HEADS):
        ls = slice(h * LANES, (h + 1) * LANES)
        dq_ref[0, :, ls] = _rope_slab(dq[:, ls], cos, sin, quarter).astype(BF16)
        dk_ref[0, :, ls] = _rope_slab(dk[:, ls], cos, sin, quarter).astype(BF16)
        dv_ref[0, :, 2 * h * LANES:(2 * h + 1) * LANES] = dv[:, ls].astype(BF16)
        dv_ref[0, :, (2 * h + 1) * LANES:(2 * h + 2) * LANES] = ones


def _odd_proj(xa, mod, norm_w, w, cos, sin, cos_r, sin_r, n_lat_tiles):
    B, S, D = xa.shape
    widths = (MLA_Q_RANK, MLA_KV_RANK, LANES, DIFF_HEADS * LANES, DIFF_HEADS * LANES, 2 * DIFF_HEADS * LANES)
    dtypes = (F32, F32, F32, BF16, BF16, BF16)
    return pl.pallas_call(
        _odd_proj_kernel,
        grid=(B, S // ROW_TILE),
        in_specs=[_row_spec(D), _mod_spec(n_lat_tiles), _const_spec((1, D)), _const_spec(w.shape),
                  _table_spec(), _table_spec(), _table_spec(), _table_spec()],
        out_specs=[_row_spec(n) for n in widths],
        out_shape=[jax.ShapeDtypeStruct((B, S, n), dt) for n, dt in zip(widths, dtypes)],
        compiler_params=_params("parallel", "parallel"),
        name="odd_proj",
    )(xa, mod, norm_w, w, cos, sin, cos_r, sin_r)


def _mla_up_kernel(cq_ref, ckv_ref, kr_ref, qn_ref, kvn_ref, wq_ref, wk_ref, wv_ref, cos_ref, sin_ref,
                   q_ref, k_ref, v_ref):
    cq = _rms(cq_ref[0], qn_ref[...]).astype(BF16)
    ckv = _rms(ckv_ref[0], kvn_ref[...]).astype(BF16)
    qp = _dot(cq, wq_ref[...])
    kp = _dot(ckv, wk_ref[...])
    vp = _dot(ckv, wv_ref[...])
    kr = kr_ref[0]
    cos = cos_ref[...]
    sin = sin_ref[...]
    lane = lax.broadcasted_iota(jnp.int32, (ROW_TILE, LANES), 1)
    ones_hi = jnp.where(lane >= MLA_V, 1.0, 0.0)
    for h in range(MLA_HEADS):
        ls = slice(h * LANES, (h + 1) * LANES)
        q_ref[0, :, ls] = _rope_slab(qp[:, ls], cos, sin, MLA_ROPE // 4).astype(BF16)
        k_ref[0, :, ls] = (kp[:, ls] + kr).astype(BF16)
        v_ref[0, :, ls] = (vp[:, ls] + ones_hi).astype(BF16)


def _mla_up(cq, ckv, kr, q_norm, kv_norm, wq, wk, wv, cos_r, sin_r):
    B, S, _ = cq.shape
    HW = MLA_HEADS * LANES
    return pl.pallas_call(
        _mla_up_kernel,
        grid=(B, S // ROW_TILE),
        in_specs=[_row_spec(MLA_Q_RANK), _row_spec(MLA_KV_RANK), _row_spec(LANES),
                  _const_spec(q_norm.shape), _const_spec(kv_norm.shape),
                  _const_spec(wq.shape), _const_spec(wk.shape), _const_spec(wv.shape),
                  _table_spec(), _table_spec()],
        out_specs=[_row_spec(HW)] * 3,
        out_shape=[jax.ShapeDtypeStruct((B, S, HW), BF16)] * 3,
        compiler_params=_params("parallel", "parallel"),
        name="mla_up",
    )(cq, ckv, kr, q_norm, kv_norm, wq, wk, wv, cos_r, sin_r)


def _one_ahead(n_sub, scores, finish):
    s_prev = scores(0)
    for i in range(1, n_sub):
        s_next = scores(i)
        finish(i - 1, s_prev)
        s_prev = s_next
    finish(n_sub - 1, s_prev)


def _mla_kernel(q_ref, k_ref, v_ref, o_ref):
    k = k_ref[0]
    v = v_ref[0]

    def scores(i):
        return _dot_nt(q_ref[0, i * ATTN_SUB:(i + 1) * ATTN_SUB, :], k)

    def finish(i, s):
        m = jnp.max(s, axis=-1, keepdims=True)
        acc = _dot(jnp.exp(s - m).astype(BF16), v)
        o_ref[0, i * ATTN_SUB:(i + 1) * ATTN_SUB, :] = (acc / acc[:, MLA_V:MLA_V + 1]).astype(o_ref.dtype)

    _one_ahead(q_ref.shape[1] // ATTN_SUB, scores, finish)


def _mla(q, k, v_ext, T):
    B, S, _ = k.shape
    return pl.pallas_call(
        _mla_kernel,
        grid=(B, MLA_HEADS, T // ATTN_TQ),
        in_specs=[pl.BlockSpec((1, ATTN_TQ, LANES), lambda b, h, t: (b, t, h)),
                  pl.BlockSpec((1, S, LANES), lambda b, h, t: (b, 0, h)),
                  pl.BlockSpec((1, S, LANES), lambda b, h, t: (b, 0, h))],
        out_specs=pl.BlockSpec((1, ATTN_TQ, LANES), lambda b, h, t: (b, t, h)),
        out_shape=jax.ShapeDtypeStruct((B, T, MLA_HEADS * LANES), BF16),
        compiler_params=_params("parallel", "parallel", "parallel"),
        name="mla_attn",
    )(q, k, v_ext)


def _diff_kernel(q_ref, k_ref, v_ref, lam_ref, sub_ref, o_ref, *, post_scale):
    k = k_ref[0]
    v = v_ref[0]
    sub = ATTN_SUB
    vw = 2 * DIFF_HEAD_DIM
    lo = lax.broadcasted_iota(jnp.int32, (sub, LANES), 1) < DIFF_HEAD_DIM
    zero = jnp.zeros((sub, LANES), BF16)

    def scores(i):
        q = q_ref[0, i * sub:(i + 1) * sub, :]
        return _dot_nt(jnp.concatenate([jnp.where(lo, q, zero), jnp.where(lo, zero, q)], axis=0), k)

    def finish(i, s):
        m = jnp.max(s, axis=-1, keepdims=True)
        acc = _dot(jnp.exp(s - m).astype(BF16), v)
        att = acc[:, :vw] / acc[:, vw:vw + 1]
        a = att[:sub] - lam_ref[...] * att[sub:]
        o_ref[0, i * sub:(i + 1) * sub, :] = (_rms(a, sub_ref[...]) * post_scale).astype(o_ref.dtype)

    _one_ahead(q_ref.shape[1] // sub, scores, finish)


def _diff(q, k, v_ext, lam, subln, post_scale, T):
    B, S, _ = k.shape
    tq = ATTN_TQ // 2
    return pl.pallas_call(
        functools.partial(_diff_kernel, post_scale=post_scale),
        grid=(B, DIFF_HEADS, T // tq),
        in_specs=[pl.BlockSpec((1, tq, LANES), lambda b, h, t: (b, t, h)),
                  pl.BlockSpec((1, S, LANES), lambda b, h, t: (b, 0, h)),
                  pl.BlockSpec((1, S, 2 * LANES), lambda b, h, t: (b, 0, h)),
                  pl.BlockSpec((1, 1), lambda b, h, t: (0, 0)),
                  pl.BlockSpec((1, LANES), lambda b, h, t: (0, 0))],
        out_specs=pl.BlockSpec((1, tq, LANES), lambda b, h, t: (b, t, h)),
        out_shape=jax.ShapeDtypeStruct((B, T, DIFF_HEADS * LANES), BF16),
        compiler_params=_params("parallel", "parallel", "parallel"),
        name="diff_attn",
    )(q, k, v_ext, lam, subln)


def _mix_out1_kernel(mla_ref, diff_ref, x_ref, mod_ref, n1_ref, n2_ref, w1_ref, w2_ref, wr_ref,
                     xo_ref, f_ref, lg_ref):
    y = _dot(mla_ref[0], w1_ref[...]) + _dot(diff_ref[0], w2_ref[...])
    x = x_ref[0] + mod_ref[0, 0, 2:3, :] * _rms(y, n1_ref[...])
    xo_ref[0] = x
    f = _rms(x, n2_ref[...]) * (1.0 + mod_ref[0, 0, 4:5, :]) + mod_ref[0, 0, 3:4, :]
    f_ref[0] = f
    lg_ref[0] = jnp.dot(f, wr_ref[...], precision=lax.Precision.HIGHEST, preferred_element_type=F32)


def _mix_out1(mla, diff, xa, mod, n1, n2, w1, w2, wr, n_lat_tiles):
    B, T, _ = mla.shape
    D = xa.shape[-1]
    return pl.pallas_call(
        _mix_out1_kernel,
        grid=(B, T // ROW_TILE),
        in_specs=[_row_spec(mla.shape[-1]), _row_spec(diff.shape[-1]), _row_spec(D), _mod_spec(n_lat_tiles),
                  _const_spec((1, D)), _const_spec((1, D)), _const_spec(w1.shape), _const_spec(w2.shape),
                  _const_spec(wr.shape)],
        out_specs=[_row_spec(D), _row_spec(D), _row_spec(LANES)],
        out_shape=[jax.ShapeDtypeStruct((B, T, D), F32), jax.ShapeDtypeStruct((B, T, D), F32),
                   jax.ShapeDtypeStruct((B, T, LANES), F32)],
        compiler_params=_params("parallel", "parallel"),
        name="mix_out1",
    )(mla, diff, xa, mod, n1, n2, w1, w2, wr)


MOE_TM = 1024
MOE_TF = 512


def _moe_kernel(te_ref, ta_ref, x_ref, rw_ref, wg_ref, wu_ref, wd_ref, o_ref, acc_ref):
    i = pl.program_id(0)
    f = pl.program_id(1)

    @pl.when(f == 0)
    def _():
        acc_ref[...] = jnp.zeros_like(acc_ref)

    @pl.when(ta_ref[i] > 0)
    def _():
        x = x_ref[...].astype(BF16)
        h = _silu(_dot(x, wg_ref[0].astype(BF16))) * _dot(x, wu_ref[0].astype(BF16))
        acc_ref[...] += _dot(h.astype(BF16), wd_ref[0].astype(BF16))

    @pl.when(f == pl.num_programs(1) - 1)
    def _():
        o_ref[...] = acc_ref[...] * rw_ref[...]


def _moe_experts(tile_expert, tile_active, x_sorted, row_w, wg, wu, wd):
    R, D = x_sorted.shape
    E, _, F = wg.shape
    nf = F // MOE_TF

    def f_eff(i, f, ta):
        return jnp.where(ta[i] > 0, f, nf - 1)

    grid_spec = pltpu.PrefetchScalarGridSpec(
        num_scalar_prefetch=2,
        grid=(R // MOE_TM, nf),
        in_specs=[pl.BlockSpec((MOE_TM, D), lambda i, f, te, ta: (i, 0)),
                  pl.BlockSpec((MOE_TM, 1), lambda i, f, te, ta: (i, 0)),
                  pl.BlockSpec((1, D, MOE_TF), lambda i, f, te, ta: (te[i], 0, f_eff(i, f, ta))),
                  pl.BlockSpec((1, D, MOE_TF), lambda i, f, te, ta: (te[i], 0, f_eff(i, f, ta))),
                  pl.BlockSpec((1, MOE_TF, D), lambda i, f, te, ta: (te[i], f_eff(i, f, ta), 0))],
        out_specs=pl.BlockSpec((MOE_TM, D), lambda i, f, te, ta: (i, 0)),
        scratch_shapes=[pltpu.VMEM((MOE_TM, D), F32)],
    )
    return pl.pallas_call(
        _moe_kernel,
        grid_spec=grid_spec,
        out_shape=jax.ShapeDtypeStruct((R, D), F32),
        compiler_params=_params("arbitrary", "arbitrary"),
        name="moe_experts",
    )(tile_expert, tile_active, x_sorted, row_w, wg, wu, wd)


def _norm_residual_kernel(a_ref, b_ref, x_ref, mod_ref, nw_ref, o_ref):
    y = a_ref[0] + b_ref[0]
    o_ref[0] = x_ref[0] + mod_ref[0, 0, 5:6, :] * _rms(y, nw_ref[...])


def _norm_residual(a, b, x, mod, norm_w, n_lat_tiles):
    B, T, D = x.shape
    return pl.pallas_call(
        _norm_residual_kernel,
        grid=(B, T // ROW_TILE),
        in_specs=[_row_spec(D), _row_spec(D), _row_spec(D), _mod_spec(n_lat_tiles), _const_spec((1, D))],
        out_specs=_row_spec(D),
        out_shape=jax.ShapeDtypeStruct((B, T, D), F32),
        compiler_params=_params("parallel", "parallel"),
        name="norm_residual",
    )(a, b, x, mod, norm_w)


def _route(logits):
    N = logits.shape[0]
    M = N * TOP_K
    R = M + N_EXPERTS * MOE_TM
    top_val, top_idx = lax.top_k(logits, TOP_K)
    top_w = jax.nn.softmax(top_val, axis=-1).reshape(-1)
    e_flat = top_idx.reshape(-1)
    onehot = (e_flat[:, None] == jnp.arange(N_EXPERTS)[None, :]).astype(jnp.int32)
    csum = jnp.cumsum(onehot, axis=0)
    pos = jnp.sum((csum - onehot) * onehot, axis=-1)
    counts = csum[-1]
    padded = ((counts + MOE_TM - 1) // MOE_TM) * MOE_TM
    ends = jnp.cumsum(padded)
    starts = ends - padded
    ustarts = jnp.cumsum(counts) - counts
    dest = starts[e_flat] + pos
    order = jnp.argsort(e_flat, stable=True).astype(jnp.int32)
    tile_start = jnp.arange(R // MOE_TM, dtype=jnp.int32) * MOE_TM
    tile_active = (tile_start < ends[-1]).astype(jnp.int32)
    te = jnp.minimum(jnp.searchsorted(ends, tile_start, side="right"), N_EXPERTS - 1).astype(jnp.int32)
    last_e = te[jnp.maximum(ends[-1] // MOE_TM - 1, 0)]
    tile_expert = jnp.where(tile_active > 0, te, last_e)
    rows = jnp.arange(R, dtype=jnp.int32)
    e_row = tile_expert[rows // MOE_TM]
    within = rows - starts[e_row]
    valid = (within < counts[e_row]) & (rows < ends[-1])
    src = order[jnp.clip(ustarts[e_row] + within, 0, M - 1)]
    row_token = jnp.where(valid, src // TOP_K, 0)
    row_w = jnp.where(valid, top_w[src], 0.0)
    return dest.reshape(N, TOP_K), row_token, row_w[:, None], tile_expert, tile_active


def _rope_tables(rows, rot_dim):
    row = jnp.repeat(jnp.arange(rows, dtype=F32), GRID_W)
    col = jnp.tile(jnp.arange(GRID_W, dtype=F32), rows)
    half = rot_dim // 2
    inv_freq = ROPE_THETA ** (-jnp.arange(0, half, 2, dtype=F32) / half)
    ang_r = row[:, None] * inv_freq[None, :]
    ang_c = col[:, None] * inv_freq[None, :]
    ang = jnp.concatenate([ang_r, ang_r, ang_c, ang_c], axis=-1)
    return jnp.cos(ang), jnp.sin(ang)


def _stream_tables(cos, sin, lane0):
    T, R = cos.shape
    if lane0 == 0:
        cos_l, sin_l = jnp.tile(cos, (1, LANES // R)), jnp.tile(sin, (1, LANES // R))
    else:
        pad = ((0, 0), (lane0, LANES - lane0 - R))
        cos_l = jnp.pad(cos - 1.0, pad) + 1.0
        sin_l = jnp.pad(sin, pad)
    ctx = ((0, CTX_LEN), (0, 0))
    return jnp.pad(cos_l - 1.0, ctx) + 1.0, jnp.pad(sin_l, ctx)


def _mod_table(c, c_ctx, mod_w, mod_b):
    B = c.shape[0]
    hp = lax.Precision.HIGHEST
    m_l = (jnp.dot(jax.nn.silu(c), mod_w, precision=hp) + mod_b).reshape(B, 1, 6, D_MODEL)
    m_c = (jnp.dot(jax.nn.silu(c_ctx), mod_w, precision=hp) + mod_b).reshape(1, 1, 6, D_MODEL)
    return jnp.concatenate([jnp.broadcast_to(m_c, m_l.shape), m_l], axis=1)


def _pad_cols(w, n):
    return jnp.pad(w, ((0, 0), (0, n - w.shape[1])))


def _even_layer(xa, mod, norms, w_in, conv_w, a_log, dt_bias, gdn_norm, sink, w_out, ffn_gate, ffn_up, ffn_down,
                cos, sin, n_lat_tiles):
    B, S, D = xa.shape
    T = n_lat_tiles * ROW_TILE
    n_gate = 2 * GDN_HEADS
    c_qkv = 3 * GDN_HD
    c_z = c_qkv + GDN_HD
    c_ga = c_z + n_gate
    c_gb = c_ga + n_gate
    c_sq = c_gb + SWA_Q_HEADS * SWA_HEAD_DIM
    c_sk = c_sq + SWA_KV_HEADS * SWA_HEAD_DIM
    grp = SWA_Q_HEADS // SWA_KV_HEADS
    wq = w_in[:, c_gb:c_sq].reshape(D, SWA_Q_HEADS, SWA_HEAD_DIM) * SWA_HEAD_DIM ** -0.5
    kv_of_head = (jnp.arange(SWA_Q_HEADS) // grp)[None, :, None]
    wq = jnp.concatenate([jnp.where(kv_of_head == 0, wq, 0.0), jnp.where(kv_of_head == 1, wq, 0.0)], axis=-1)
    w_all = jnp.concatenate([w_in[:, :c_z], _pad_cols(w_in[:, c_z:c_gb], LANES),
                             wq.reshape(D, SWA_Q_HEADS * LANES), w_in[:, c_sq:]], axis=1).astype(BF16)
    qkv, z, gate, q_pad, k, v = _even_proj(xa, mod, norms[0:1], w_all, cos, sin, n_lat_tiles)

    head_ones = (jnp.arange(GDN_HD)[:, None] // GDN_DK == jnp.arange(GDN_HD)[None, :] // GDN_DK).astype(BF16)
    conv_w8 = jnp.pad(conv_w, ((0, SUBLANES - CONV_K), (0, 0)))
    gq, gk, gv = _gdn_prep(qkv, conv_w8, head_ones, n_lat_tiles)
    a_log_row = _pad_cols(a_log.reshape(1, n_gate), LANES)
    dt_bias_row = _pad_cols(dt_bias.reshape(1, n_gate), LANES)
    o_f, o_b = _gdn(gq, gk, gv, gate, a_log_row, dt_bias_row, T // GDN_CHUNK)

    sink_rows = jnp.repeat(sink, SWA_BLOCK)[:, None]
    swa = _swa(q_pad, k, v, sink_rows, T // SWA_BLOCK)

    w_s = w_out[GDN_HD:].reshape(SWA_Q_HEADS, SWA_HEAD_DIM, D)
    w_s = jnp.concatenate([jnp.where(kv_of_head.reshape(-1, 1, 1) == 0, w_s, 0.0),
                           jnp.where(kv_of_head.reshape(-1, 1, 1) == 1, w_s, 0.0)], axis=1)
    gdn_norm_row = jnp.tile(gdn_norm, GDN_HEADS)[None, :]
    xa = _mix_out0(o_f, o_b, z, swa, xa, mod, norms[1:2], gdn_norm_row, head_ones,
                   w_out[:GDN_HD].astype(BF16), w_s.reshape(SWA_Q_HEADS * LANES, D).astype(BF16), n_lat_tiles)
    return _ffn(xa, mod, norms[2:3], norms[3:4], ffn_gate.astype(BF16), ffn_up.astype(BF16),
                ffn_down.astype(BF16), n_lat_tiles)


def _odd_layer_last(xa, mod, norms, w_in, q_norm, kv_norm, w_uq, w_ukv, lam_p, lam_init, subln, w_out,
                    router, exp_gate, exp_up, exp_down, cos, sin, cos_r, sin_r, n_lat_tiles):
    B, S, D = xa.shape
    T = n_lat_tiles * ROW_TILE
    H = MLA_HEADS
    c0 = MLA_Q_RANK
    c1 = c0 + MLA_KV_RANK
    c2 = c1 + MLA_ROPE
    dw = DIFF_HEADS * 2 * DIFF_HEAD_DIM
    w_kr = jnp.pad(w_in[:, c1:c2], ((0, 0), (MLA_NOPE, LANES - MLA_NOPE - MLA_ROPE)))
    w_all = jnp.concatenate([w_in[:, :c1], w_kr, w_in[:, c2:c2 + dw] * DIFF_HEAD_DIM ** -0.5,
                             w_in[:, c2 + dw:]], axis=1).astype(BF16)
    cq, ckv, kr, dq, dk, dv = _odd_proj(xa, mod, norms[0:1], w_all, cos, sin, cos_r, sin_r, n_lat_tiles)

    qd = MLA_NOPE + MLA_ROPE
    wq = jnp.pad(w_uq.reshape(MLA_Q_RANK, H, qd) * qd ** -0.5, ((0, 0), (0, 0), (0, LANES - qd)))
    wkv = w_ukv.reshape(MLA_KV_RANK, H, MLA_NOPE + MLA_V)
    wk = jnp.pad(wkv[..., :MLA_NOPE], ((0, 0), (0, 0), (0, LANES - MLA_NOPE)))
    wv = jnp.pad(wkv[..., MLA_NOPE:], ((0, 0), (0, 0), (0, LANES - MLA_V)))
    q_cat, k_cat, v_ext = _mla_up(cq, ckv, kr, q_norm[None, :], kv_norm[None, :],
                                  wq.reshape(MLA_Q_RANK, H * LANES).astype(BF16),
                                  wk.reshape(MLA_KV_RANK, H * LANES).astype(BF16),
                                  wv.reshape(MLA_KV_RANK, H * LANES).astype(BF16), cos_r, sin_r)
    mla = _mla(q_cat, k_cat, v_ext, T)

    lam = (jnp.exp(jnp.sum(lam_p[0] * lam_p[1])) - jnp.exp(jnp.sum(lam_p[2] * lam_p[3])) + lam_init).reshape(1, 1)
    diff = _diff(dq, dk, dv, lam, subln[None, :], 1.0 - lam_init, T)

    w_mla = jnp.pad(w_out[:H * MLA_V].reshape(H, MLA_V, D), ((0, 0), (0, LANES - MLA_V), (0, 0)))
    x, f_in, logits = _mix_out1(mla, diff, xa, mod, norms[1:2], norms[2:3],
                                w_mla.reshape(H * LANES, D).astype(BF16), w_out[H * MLA_V:].astype(BF16),
                                _pad_cols(router, LANES), n_lat_tiles)

    dest, row_token, row_w, tile_expert, tile_active = _route(logits.reshape(B * T, LANES)[:, :N_EXPERTS])
    x_sorted = f_in.reshape(B * T, D)[row_token]
    y = _moe_experts(tile_expert, tile_active, x_sorted, row_w, exp_gate, exp_up, exp_down)
    y0 = y[dest[:, 0]].reshape(B, T, D)
    y1 = y[dest[:, 1]].reshape(B, T, D)
    return _norm_residual(y0, y1, x, mod, norms[3:4], n_lat_tiles)


def kernel(x, c, ctx, c_ctx, e_mod_w, e_mod_b, e_norms, e_w_in, e_conv_w, e_a_log, e_dt_bias, e_gdn_norm, e_sink, e_w_out, e_ffn_gate, e_ffn_up, e_ffn_down, o_mod_w, o_mod_b, o_norms, o_w_in, o_q_norm, o_kv_norm, o_w_uq, o_w_ukv, o_lambda, o_subln, o_w_out, o_router, o_exp_gate, o_exp_up, o_exp_down):
    B, T, D = x.shape
    n_lat_tiles = T // ROW_TILE
    rows = T // GRID_W
    cos64, sin64 = _stream_tables(*_rope_tables(rows, SWA_HEAD_DIM), 0)
    cos_r, sin_r = _stream_tables(*_rope_tables(rows, MLA_ROPE), MLA_NOPE)
    xa = jnp.concatenate([x, ctx], axis=1)
    mod_e = _mod_table(c, c_ctx, e_mod_w[0], e_mod_b[0])
    xa = _even_layer(xa, mod_e, e_norms[0], e_w_in[0], e_conv_w[0], e_a_log[0], e_dt_bias[0], e_gdn_norm[0],
                     e_sink[0], e_w_out[0], e_ffn_gate[0], e_ffn_up[0], e_ffn_down[0], cos64, sin64, n_lat_tiles)
    mod_o = _mod_table(c, c_ctx, o_mod_w[0], o_mod_b[0])
    lam_init = 0.8 - 0.6 * math.exp(-0.3 * 1)
    return _odd_layer_last(xa, mod_o, o_norms[0], o_w_in[0], o_q_norm[0], o_kv_norm[0], o_w_uq[0], o_w_ukv[0],
                           o_lambda[0], lam_init, o_subln[0], o_w_out[0], o_router[0], o_exp_gate[0],
                           o_exp_up[0], o_exp_down[0], cos64, sin64, cos_r, sin_r, n_lat_tiles)
```

```python
import functools
import math

import jax
import jax.numpy as jnp
from jax import lax
from jax.experimental import pallas as pl
from jax.experimental.pallas import tpu as pltpu

F32 = jnp.float32
BF16 = jnp.bfloat16

D_MODEL = 1024
CTX_LEN = 256
GRID_W = 64
NORM_EPS = 1e-6
ROPE_THETA = 10000.0

GDN_HEADS = 8
GDN_DK = 64
GDN_DV = 64
GDN_CHUNK = 64
CONV_K = 5
GDN_GROUP = 4
GDN_LANES = GDN_GROUP * GDN_DK
GDN_HD = GDN_HEADS * GDN_DK
GDN_NB = 4

SWA_Q_HEADS = 8
SWA_KV_HEADS = 2
SWA_HEAD_DIM = 64
SWA_WINDOW = 128
SWA_BLOCK = 128

MLA_HEADS = 8
MLA_Q_RANK = 384
MLA_KV_RANK = 256
MLA_NOPE = 64
MLA_ROPE = 32
MLA_V = 64

DIFF_HEADS = 4
DIFF_HEAD_DIM = 64

D_FF = 2816
N_EXPERTS = 8
TOP_K = 2
D_FF_EXPERT = 3584

LANES = 128
SUBLANES = 8
ROW_TILE = 256
SWA_HEADS_PER_SUB = 2
ATTN_TQ = 1024
ATTN_SUB = 128
VMEM_LIMIT = 56 * 1024 * 1024
NEG_BIG = -1e30


def _dot(a, b):
    return jnp.dot(a, b, preferred_element_type=F32)


def _dot_nt(a, b):
    return lax.dot_general(a, b, (((1,), (1,)), ((), ())), preferred_element_type=F32)


def _dot_tn(a, b):
    return lax.dot_general(a, b, (((0,), (0,)), ((), ())), preferred_element_type=F32)


def _split_bf16(x):
    hi = x.astype(BF16)
    lo = (x - hi.astype(F32)).astype(BF16)
    return hi, lo


def _dot_split(x, w):
    hi, lo = _split_bf16(x)
    return _dot(hi, w) + _dot(lo, w)


def _dot_rsplit(w, x):
    hi, lo = _split_bf16(x)
    return _dot(w, hi) + _dot(w, lo)


def _rms(x, w):
    return x * lax.rsqrt(jnp.mean(x * x, axis=-1, keepdims=True) + NORM_EPS) * w


def _silu(x):
    return x * jax.nn.sigmoid(x)


def _rope_slab(x, cos, sin, quarter):
    lane = lax.broadcasted_iota(jnp.int32, x.shape, 1)
    fwd = pltpu.roll(x, quarter, 1)
    back = pltpu.roll(x, LANES - quarter, 1)
    rot = jnp.where(lane % (2 * quarter) < quarter, -back, fwd)
    return x * cos + rot * sin


def _params(*sem):
    return pltpu.CompilerParams(dimension_semantics=sem, vmem_limit_bytes=VMEM_LIMIT)


def _row_spec(width, tile=ROW_TILE):
    return pl.BlockSpec((1, tile, width), lambda b, t: (b, t, 0))


def _mod_spec(n_lat_tiles):
    return pl.BlockSpec((1, 1, 6, D_MODEL), lambda b, t: (b, jnp.where(t < n_lat_tiles, 1, 0), 0, 0))


def _const_spec(shape):
    return pl.BlockSpec(shape, lambda b, t: (0,) * len(shape))


def _table_spec():
    return pl.BlockSpec((ROW_TILE, LANES), lambda b, t: (t, 0))


E_QKV = (0, 3 * GDN_HD)
E_Z = (E_QKV[1], E_QKV[1] + GDN_HD)
E_GATE = (E_Z[1], E_Z[1] + LANES)
E_Q = (E_GATE[1], E_GATE[1] + SWA_Q_HEADS * LANES)
E_K = (E_Q[1], E_Q[1] + LANES)
E_V = (E_K[1], E_K[1] + LANES)


def _even_proj_kernel(x_ref, mod_ref, nw_ref, w_ref, cos_ref, sin_ref,
                      qkv_ref, z_ref, gate_ref, q_ref, k_ref, v_ref):
    a = (_rms(x_ref[0], nw_ref[...]) * (1.0 + mod_ref[0, 0, 1:2, :]) + mod_ref[0, 0, 0:1, :]).astype(BF16)

    def proj(cols):
        return _dot(a, w_ref[:, cols[0]:cols[1]])

    qkv_ref[0] = proj(E_QKV)
    z_ref[0] = proj(E_Z).astype(BF16)
    gate_ref[0] = proj(E_GATE)
    cos = cos_ref[...]
    sin = sin_ref[...]
    quarter = SWA_HEAD_DIM // 4
    qp = proj(E_Q)
    for h in range(SWA_Q_HEADS):
        ls = slice(h * LANES, (h + 1) * LANES)
        q_ref[0, :, ls] = _rope_slab(qp[:, ls], cos, sin, quarter).astype(BF16)
    k_ref[0] = _rope_slab(proj(E_K), cos, sin, quarter).astype(BF16)
    v_ref[0] = proj(E_V).astype(BF16)


def _even_proj(xa, mod, norm_w, w, cos, sin, n_lat_tiles):
    B, S, D = xa.shape
    widths = (E_QKV[1] - E_QKV[0], GDN_HD, LANES, SWA_Q_HEADS * LANES, LANES, LANES)
    dtypes = (F32, BF16, F32, BF16, BF16, BF16)
    return pl.pallas_call(
        _even_proj_kernel,
        grid=(B, S // ROW_TILE),
        in_specs=[_row_spec(D), _mod_spec(n_lat_tiles), _const_spec((1, D)), _const_spec(w.shape),
                  _table_spec(), _table_spec()],
        out_specs=[_row_spec(n) for n in widths],
        out_shape=[jax.ShapeDtypeStruct((B, S, n), dt) for n, dt in zip(widths, dtypes)],
        compiler_params=_params("parallel", "parallel"),
        name="even_proj",
    )(xa, mod, norm_w, w, cos, sin)


def _gdn_prep_kernel(x_ref, prev_ref, next_ref, cw_ref, bd_ref, q_ref, k_ref, v_ref, *, n_lat_tiles):
    t = pl.program_id(1)
    has_prev = jnp.logical_and(t > 0, t < n_lat_tiles)
    has_next = t < n_lat_tiles - 1
    prev = jnp.where(has_prev, prev_ref[0], 0.0)
    nxt = jnp.where(has_next, next_ref[0], 0.0)
    xe = jnp.concatenate([prev, x_ref[0], nxt], axis=0)
    pad = CONV_K // 2
    y = jnp.zeros(x_ref.shape[1:], F32)
    for kk in range(CONV_K):
        lo = SUBLANES + kk - pad
        y = y + cw_ref[kk:kk + 1, :] * xe[lo:lo + ROW_TILE, :]
    u = _silu(y)

    def headnorm(a):
        return a * lax.rsqrt(_dot_split(a * a, bd_ref[...]) + NORM_EPS)

    q_ref[0] = headnorm(u[:, :GDN_HD]) * GDN_DK ** -0.5
    k_ref[0] = headnorm(u[:, GDN_HD:2 * GDN_HD])
    v_ref[0] = u[:, 2 * GDN_HD:]


def _gdn_prep(qkv, conv_w, head_ones, n_lat_tiles):
    B, S, W = qkv.shape
    per_tile = ROW_TILE // SUBLANES
    n8 = S // SUBLANES
    return pl.pallas_call(
        functools.partial(_gdn_prep_kernel, n_lat_tiles=n_lat_tiles),
        grid=(B, S // ROW_TILE),
        in_specs=[_row_spec(W),
                  pl.BlockSpec((1, SUBLANES, W), lambda b, t: (b, jnp.maximum(t * per_tile - 1, 0), 0)),
                  pl.BlockSpec((1, SUBLANES, W), lambda b, t: (b, jnp.minimum((t + 1) * per_tile, n8 - 1), 0)),
                  _const_spec(conv_w.shape), _const_spec(head_ones.shape)],
        out_specs=[_row_spec(GDN_HD)] * 3,
        out_shape=[jax.ShapeDtypeStruct((B, S, GDN_HD), F32)] * 3,
        compiler_params=_params("parallel", "parallel"),
        name="gdn_prep",
    )(qkv, qkv, qkv, conv_w, head_ones)


def _block_diag(x, head_of_lane):
    zero = jnp.zeros_like(x)
    return jnp.concatenate([jnp.where(head_of_lane == h, x, zero) for h in range(GDN_GROUP)], axis=0)


def _expand_heads(x4, head_of_lane):
    c = x4.shape[0]
    out = jnp.broadcast_to(x4[:, GDN_GROUP - 1:GDN_GROUP], (c, GDN_LANES))
    for h in range(GDN_GROUP - 2, -1, -1):
        out = jnp.where(head_of_lane == h, jnp.broadcast_to(x4[:, h:h + 1], (c, GDN_LANES)), out)
    return out


def _gdn_groups(items):
    C = GDN_CHUNK
    row = lax.broadcasted_iota(jnp.int32, (C, GDN_LANES), 0)
    lane = lax.broadcasted_iota(jnp.int32, (C, GDN_LANES), 1)
    hol = lane // GDN_DK
    col = lane % GDN_DK
    r64 = lax.broadcasted_iota(jnp.int32, (C, C), 0)
    c64 = lax.broadcasted_iota(jnp.int32, (C, C), 1)
    ones = jnp.ones((C, C), BF16)
    eye = jnp.where(row == col, 1.0, 0.0)

    def bd(x):
        return _block_diag(x, hol)

    st = []
    for (q4, k4, v4, g4, b4, s4, reverse) in items:
        sgn = -1 if reverse else 1
        ahead = (row - col) * sgn
        incl = ahead >= 0
        strict = ahead > 0
        tri = jnp.where((r64 - c64) * sgn >= 0, 1.0, 0.0).astype(BF16)
        gE = _expand_heads(g4, hol)
        bE = _expand_heads(b4, hol)
        gc = _dot_rsplit(tri, gE)
        gc_row = _dot_rsplit(ones, jnp.where((col - row) * sgn >= 0, gE, 0.0))
        decay = jnp.where(incl, jnp.exp(gc - gc_row), 0.0)
        g_last = gc[0:1, :] if reverse else gc[C - 1:C, :]
        eg = jnp.exp(gc)
        kb = k4 * bE
        st.append(dict(q=q4, k=k4, s=s4, incl=incl, strict=strict, decay=decay, g_last=g_last,
                       kb=kb, vb=v4 * bE, kbg=kb * eg, qh=q4 * eg, kt=k4 * jnp.exp(g_last - gc)))

    for d in st:
        gram = _dot_nt(jnp.concatenate([d["kb"], d["q"]], axis=0).astype(BF16), bd(d["k"].astype(BF16)))
        L = jnp.where(d["strict"], gram[:C] * d["decay"], 0.0)
        d["A"] = gram[C:] * d["decay"]
        d["T"] = eye - L
        d["P"] = L.astype(BF16)
    for _ in range(int(math.log2(C)) - 1):
        for d in st:
            d["P"] = _dot(d["P"], bd(d["P"])).astype(BF16)
        for d in st:
            d["T"] = d["T"] + _dot(d["T"].astype(BF16), bd(d["P"]))
    for d in st:
        Tb = d["T"].astype(BF16)
        d["u"] = _dot(Tb, bd(d["vb"].astype(BF16)))
        d["w"] = _dot(Tb, bd(d["kbg"].astype(BF16)))
    for d in st:
        d["ws_qs"] = _dot(jnp.concatenate([d["w"], d["qh"]], axis=0).astype(BF16), bd(d["s"].astype(BF16)))
    out = []
    for d in st:
        v_new = d["u"] - d["ws_qs"][:C]
        vb16 = v_new.astype(BF16)
        o = d["ws_qs"][C:] + _dot(d["A"].astype(BF16), bd(vb16))
        full = _dot_tn(d["kt"].astype(BF16), vb16)
        upd = jnp.zeros((GDN_DK, GDN_LANES), F32)
        for h in range(GDN_GROUP):
            upd = upd + jnp.where(hol == h, full[h * GDN_DK:(h + 1) * GDN_DK, :], 0.0)
        out.append((o, d["s"] * jnp.exp(d["g_last"]) + upd))
    return out


def _gdn_kernel(qf_ref, kf_ref, vf_ref, gf_ref, qb_ref, kb_ref, vb_ref, gb_ref, al_ref, dtb_ref,
                of_ref, ob_ref, s_ref):
    @pl.when(pl.program_id(1) == 0)
    def _():
        s_ref[...] = jnp.zeros_like(s_ref)

    n_grp = GDN_HEADS // GDN_GROUP
    n_gate = 2 * GDN_HEADS
    items = []
    slots = []
    for nb in range(GDN_NB):
        for d, (q_ref, k_ref, v_ref, g_ref, o_ref) in enumerate(((qf_ref, kf_ref, vf_ref, gf_ref, of_ref),
                                                                 (qb_ref, kb_ref, vb_ref, gb_ref, ob_ref))):
            raw = g_ref[nb]
            xs = raw + dtb_ref[...]
            softplus = jnp.maximum(xs, 0.0) + jnp.log(1.0 + jnp.exp(-jnp.abs(xs)))
            g = -jnp.exp(al_ref[...]) * softplus
            beta = jax.nn.sigmoid(raw)
            for grp in range(n_grp):
                ls = slice(grp * GDN_LANES, (grp + 1) * GDN_LANES)
                c0 = d * GDN_HEADS + grp * GDN_GROUP
                slot = (nb * 2 + d) * n_grp + grp
                items.append((q_ref[nb, :, ls], k_ref[nb, :, ls], v_ref[nb, :, ls],
                              g[:, c0:c0 + GDN_GROUP], beta[:, n_gate + c0:n_gate + c0 + GDN_GROUP],
                              s_ref[slot], d == 1))
                slots.append((o_ref, nb, ls, slot))
    for (o_ref, nb, ls, slot), (o, s_new) in zip(slots, _gdn_groups(items)):
        o_ref[nb, :, ls] = o
        s_ref[slot] = s_new


def _gdn(q, k, v, gate, a_log_row, dt_bias_row, n_lat_chunks):
    B, S, HD = q.shape
    n_chunks = S // GDN_CHUNK
    n_ctx = n_chunks - n_lat_chunks

    def fwd_c(s):
        return jnp.where(s < n_ctx, n_lat_chunks + s, s - n_ctx)

    def bwd_c(s):
        return n_chunks - 1 - s

    def spec(cmap, width):
        return pl.BlockSpec((GDN_NB, GDN_CHUNK, width), lambda b, s: (b, cmap(s), 0))

    return pl.pallas_call(
        _gdn_kernel,
        grid=(B // GDN_NB, n_chunks),
        in_specs=[spec(fwd_c, HD), spec(fwd_c, HD), spec(fwd_c, HD), spec(fwd_c, LANES),
                  spec(bwd_c, HD), spec(bwd_c, HD), spec(bwd_c, HD), spec(bwd_c, LANES),
                  _const_spec((1, LANES)), _const_spec((1, LANES))],
        out_specs=[spec(fwd_c, HD), spec(bwd_c, HD)],
        out_shape=[jax.ShapeDtypeStruct((B, S, HD), F32)] * 2,
        scratch_shapes=[pltpu.VMEM((GDN_NB * 2 * GDN_HEADS // GDN_GROUP, GDN_DK, GDN_LANES), F32)],
        compiler_params=_params("parallel", "arbitrary"),
        name="gdn_scan",
    )(q, k, v, gate, q, k, v, gate, a_log_row, dt_bias_row)


def _swa_kernel(q_ref, k_ref, v_ref, sink_ref, o_ref, *, n_lat_blocks):
    W = SWA_BLOCK
    H = SWA_Q_HEADS
    n = pl.program_id(1)
    is_lat = n < n_lat_blocks
    nl = jnp.minimum(n, n_lat_blocks - 1)
    prev = jnp.maximum(nl - 1, 0)
    nxt = jnp.minimum(nl + 1, n_lat_blocks - 1)
    ctx0 = n_lat_blocks * W

    def keys(ref):
        def blk(i):
            return ref[0, pl.ds(pl.multiple_of(i * W, W), W), :]
        return jnp.concatenate([ref[0, ctx0:ctx0 + CTX_LEN, :], blk(prev), blk(nl), blk(nxt)], axis=0)

    k_all = keys(k_ref)
    v_all = keys(v_ref)
    nk = CTX_LEN + 3 * W
    v_ext = jnp.concatenate([v_all, jnp.ones((nk, LANES), BF16)], axis=1)
    hp = SWA_HEADS_PER_SUB
    ii = lax.broadcasted_iota(jnp.int32, (hp * W, nk), 0) % W
    jj = lax.broadcasted_iota(jnp.int32, (hp * W, nk), 1) - CTX_LEN
    in_window = (((jj >= 0) & (jj < W) & (jj >= ii) & (nl > 0))
                 | ((jj >= W) & (jj < 2 * W))
                 | ((jj >= 2 * W) & (jj - 2 * W <= ii) & (nl < n_lat_blocks - 1)))
    valid = (jj < 0) | (in_window & is_lat)

    def scores(i):
        q = jnp.concatenate([q_ref[0, :, h * LANES:(h + 1) * LANES] for h in range(i * hp, (i + 1) * hp)], axis=0)
        return jnp.where(valid, _dot_nt(q, k_all), NEG_BIG)

    def finish(i, s):
        sk = sink_ref[i * hp * W:(i + 1) * hp * W, :]
        m = jnp.maximum(jnp.max(s, axis=-1, keepdims=True), sk)
        acc = _dot(jnp.exp(s - m).astype(BF16), v_ext)
        o = acc[:, :LANES] / (acc[:, LANES:LANES + 1] + jnp.exp(sk - m))
        for j in range(hp):
            h = i * hp + j
            o_ref[0, :, h * LANES:(h + 1) * LANES] = o[j * W:(j + 1) * W].astype(o_ref.dtype)

    _one_ahead(H // hp, scores, finish)


def _swa(q_pad, k, v, sink_rows, n_lat_blocks):
    B, S, _ = k.shape
    W = SWA_BLOCK
    QW = SWA_Q_HEADS * LANES
    return pl.pallas_call(
        functools.partial(_swa_kernel, n_lat_blocks=n_lat_blocks),
        grid=(B, S // W),
        in_specs=[pl.BlockSpec((1, W, QW), lambda b, n: (b, n, 0)),
                  pl.BlockSpec((1, S, LANES), lambda b, n: (b, 0, 0)),
                  pl.BlockSpec((1, S, LANES), lambda b, n: (b, 0, 0)),
                  _const_spec(sink_rows.shape)],
        out_specs=pl.BlockSpec((1, W, QW), lambda b, n: (b, n, 0)),
        out_shape=jax.ShapeDtypeStruct((B, S, QW), BF16),
        compiler_params=_params("parallel", "parallel"),
        name="swa",
    )(q_pad, k, v, sink_rows)


def _mix_out0_kernel(of_ref, ob_ref, z_ref, swa_ref, x_ref, mod_ref, nw_ref, gn_ref, bd_ref, wg_ref, ws_ref,
                     o_ref):
    o = of_ref[0] + ob_ref[0]
    ms = _dot_split(o * o, bd_ref[...]) * (1.0 / GDN_DV)
    gdn = o * lax.rsqrt(ms + NORM_EPS) * gn_ref[...] * _silu(z_ref[0].astype(F32))
    y = _dot(gdn.astype(BF16), wg_ref[...]) + _dot(swa_ref[0], ws_ref[...])
    o_ref[0] = x_ref[0] + mod_ref[0, 0, 2:3, :] * _rms(y, nw_ref[...])


def _mix_out0(o_f, o_b, z, swa, xa, mod, norm_w, gdn_norm_row, head_ones, w_gdn, w_swa, n_lat_tiles):
    B, S, D = xa.shape
    return pl.pallas_call(
        _mix_out0_kernel,
        grid=(B, S // ROW_TILE),
        in_specs=[_row_spec(GDN_HD), _row_spec(GDN_HD), _row_spec(GDN_HD), _row_spec(swa.shape[-1]), _row_spec(D),
                  _mod_spec(n_lat_tiles), _const_spec((1, D)), _const_spec(gdn_norm_row.shape),
                  _const_spec(head_ones.shape), _const_spec(w_gdn.shape), _const_spec(w_swa.shape)],
        out_specs=_row_spec(D),
        out_shape=jax.ShapeDtypeStruct((B, S, D), F32),
        compiler_params=_params("parallel", "parallel"),
        name="mix_out0",
    )(o_f, o_b, z, swa, xa, mod, norm_w, gdn_norm_row, head_ones, w_gdn, w_swa)


def _ffn_kernel(x_ref, mod_ref, n_in_ref, n_out_ref, wg_ref, wu_ref, wd_ref, o_ref):
    x = x_ref[0]
    f = (_rms(x, n_in_ref[...]) * (1.0 + mod_ref[0, 0, 4:5, :]) + mod_ref[0, 0, 3:4, :]).astype(BF16)
    h = _silu(_dot(f, wg_ref[...])) * _dot(f, wu_ref[...])
    y = _dot(h.astype(BF16), wd_ref[...])
    o_ref[0] = x + mod_ref[0, 0, 5:6, :] * _rms(y, n_out_ref[...])


def _ffn(xa, mod, n_in, n_out, wg, wu, wd, n_lat_tiles):
    B, S, D = xa.shape
    return pl.pallas_call(
        _ffn_kernel,
        grid=(B, S // ROW_TILE),
        in_specs=[_row_spec(D), _mod_spec(n_lat_tiles), _const_spec((1, D)), _const_spec((1, D)),
                  _const_spec(wg.shape), _const_spec(wu.shape), _const_spec(wd.shape)],
        out_specs=_row_spec(D),
        out_shape=jax.ShapeDtypeStruct((B, S, D), F32),
        compiler_params=_params("parallel", "parallel"),
        name="dense_ffn",
    )(xa, mod, n_in, n_out, wg, wu, wd)


O_CQ = (0, MLA_Q_RANK)
O_CKV = (O_CQ[1], O_CQ[1] + MLA_KV_RANK)
O_KR = (O_CKV[1], O_CKV[1] + LANES)
O_DQ = (O_KR[1], O_KR[1] + DIFF_HEADS * LANES)
O_DK = (O_DQ[1], O_DQ[1] + DIFF_HEADS * LANES)
O_DV = (O_DK[1], O_DK[1] + DIFF_HEADS * LANES)


def _odd_proj_kernel(x_ref, mod_ref, nw_ref, w_ref, cos_ref, sin_ref, cosr_ref, sinr_ref,
                     cq_ref, ckv_ref, kr_ref, dq_ref, dk_ref, dv_ref):
    a = (_rms(x_ref[0], nw_ref[...]) * (1.0 + mod_ref[0, 0, 1:2, :]) + mod_ref[0, 0, 0:1, :]).astype(BF16)

    def proj(cols):
        return _dot(a, w_ref[:, cols[0]:cols[1]])

    cq_ref[0] = proj(O_CQ)
    ckv_ref[0] = proj(O_CKV)
    kr_ref[0] = _rope_slab(proj(O_KR), cosr_ref[...], sinr_ref[...], MLA_ROPE // 4)
    cos = cos_ref[...]
    sin = sin_ref[...]
    quarter = DIFF_HEAD_DIM // 4
    dq = proj(O_DQ)
    dk = proj(O_DK)
    dv = proj(O_DV)
    ones = jnp.ones((ROW_TILE, LANES), BF16)
    for h in range(DIFF_HEADS):
        ls = slice(h * LANES, (h + 1) * LANES)
        dq_ref[0, :, ls] = _rope_slab(dq[:, ls], cos, sin, quarter).astype(BF16)
        dk_ref[0, :, ls] = _rope_slab(dk[:, ls], cos, sin, quarter).astype(BF16)
        dv_ref[0, :, 2 * h * LANES:(2 * h + 1) * LANES] = dv[:, ls].astype(BF16)
        dv_ref[0, :, (2 * h + 1) * LANES:(2 * h + 2) * LANES] = ones


def _odd_proj(xa, mod, norm_w, w, cos, sin, cos_r, sin_r, n_lat_tiles):
    B, S, D = xa.shape
    widths = (MLA_Q_RANK, MLA_KV_RANK, LANES, DIFF_HEADS * LANES, DIFF_HEADS * LANES, 2 * DIFF_HEADS * LANES)
    dtypes = (F32, F32, F32, BF16, BF16, BF16)
    return pl.pallas_call(
        _odd_proj_kernel,
        grid=(B, S // ROW_TILE),
        in_specs=[_row_spec(D), _mod_spec(n_lat_tiles), _const_spec((1, D)), _const_spec(w.shape),
                  _table_spec(), _table_spec(), _table_spec(), _table_spec()],
        out_specs=[_row_spec(n) for n in widths],
        out_shape=[jax.ShapeDtypeStruct((B, S, n), dt) for n, dt in zip(widths, dtypes)],
        compiler_params=_params("parallel", "parallel"),
        name="odd_proj",
    )(xa, mod, norm_w, w, cos, sin, cos_r, sin_r)


def _mla_up_kernel(cq_ref, ckv_ref, kr_ref, qn_ref, kvn_ref, wq_ref, wk_ref, wv_ref, cos_ref, sin_ref,
                   q_ref, k_ref, v_ref):
    cq = _rms(cq_ref[0], qn_ref[...]).astype(BF16)
    ckv = _rms(ckv_ref[0], kvn_ref[...]).astype(BF16)
    qp = _dot(cq, wq_ref[...])
    kp = _dot(ckv, wk_ref[...])
    vp = _dot(ckv, wv_ref[...])
    kr = kr_ref[0]
    cos = cos_ref[...]
    sin = sin_ref[...]
    lane = lax.broadcasted_iota(jnp.int32, (ROW_TILE, LANES), 1)
    ones_hi = jnp.where(lane >= MLA_V, 1.0, 0.0)
    for h in range(MLA_HEADS):
        ls = slice(h * LANES, (h + 1) * LANES)
        q_ref[0, :, ls] = _rope_slab(qp[:, ls], cos, sin, MLA_ROPE // 4).astype(BF16)
        k_ref[0, :, ls] = (kp[:, ls] + kr).astype(BF16)
        v_ref[0, :, ls] = (vp[:, ls] + ones_hi).astype(BF16)


def _mla_up(cq, ckv, kr, q_norm, kv_norm, wq, wk, wv, cos_r, sin_r):
    B, S, _ = cq.shape
    HW = MLA_HEADS * LANES
    return pl.pallas_call(
        _mla_up_kernel,
        grid=(B, S // ROW_TILE),
        in_specs=[_row_spec(MLA_Q_RANK), _row_spec(MLA_KV_RANK), _row_spec(LANES),
                  _const_spec(q_norm.shape), _const_spec(kv_norm.shape),
                  _const_spec(wq.shape), _const_spec(wk.shape), _const_spec(wv.shape),
                  _table_spec(), _table_spec()],
        out_specs=[_row_spec(HW)] * 3,
        out_shape=[jax.ShapeDtypeStruct((B, S, HW), BF16)] * 3,
        compiler_params=_params("parallel", "parallel"),
        name="mla_up",
    )(cq, ckv, kr, q_norm, kv_norm, wq, wk, wv, cos_r, sin_r)


def _one_ahead(n_sub, scores, finish):
    s_prev = scores(0)
    for i in range(1, n_sub):
        s_next = scores(i)
        finish(i - 1, s_prev)
        s_prev = s_next
    finish(n_sub - 1, s_prev)


def _mla_kernel(q_ref, k_ref, v_ref, o_ref):
    k = k_ref[0]
    v = v_ref[0]

    def scores(i):
        return _dot_nt(q_ref[0, i * ATTN_SUB:(i + 1) * ATTN_SUB, :], k)

    def finish(i, s):
        m = jnp.max(s, axis=-1, keepdims=True)
        acc = _dot(jnp.exp(s - m).astype(BF16), v)
        o_ref[0, i * ATTN_SUB:(i + 1) * ATTN_SUB, :] = (acc / acc[:, MLA_V:MLA_V + 1]).astype(o_ref.dtype)

    _one_ahead(q_ref.shape[1] // ATTN_SUB, scores, finish)


def _mla(q, k, v_ext, T):
    B, S, _ = k.shape
    return pl.pallas_call(
        _mla_kernel,
        grid=(B, MLA_HEADS, T // ATTN_TQ),
        in_specs=[pl.BlockSpec((1, ATTN_TQ, LANES), lambda b, h, t: (b, t, h)),
                  pl.BlockSpec((1, S, LANES), lambda b, h, t: (b, 0, h)),
                  pl.BlockSpec((1, S, LANES), lambda b, h, t: (b, 0, h))],
        out_specs=pl.BlockSpec((1, ATTN_TQ, LANES), lambda b, h, t: (b, t, h)),
        out_shape=jax.ShapeDtypeStruct((B, T, MLA_HEADS * LANES), BF16),
        compiler_params=_params("parallel", "parallel", "parallel"),
        name="mla_attn",
    )(q, k, v_ext)


def _diff_kernel(q_ref, k_ref, v_ref, lam_ref, sub_ref, o_ref, *, post_scale):
    k = k_ref[0]
    v = v_ref[0]
    sub = ATTN_SUB
    vw = 2 * DIFF_HEAD_DIM
    lo = lax.broadcasted_iota(jnp.int32, (sub, LANES), 1) < DIFF_HEAD_DIM
    zero = jnp.zeros((sub, LANES), BF16)

    def scores(i):
        q = q_ref[0, i * sub:(i + 1) * sub, :]
        return _dot_nt(jnp.concatenate([jnp.where(lo, q, zero), jnp.where(lo, zero, q)], axis=0), k)

    def finish(i, s):
        m = jnp.max(s, axis=-1, keepdims=True)
        acc = _dot(jnp.exp(s - m).astype(BF16), v)
        att = acc[:, :vw] / acc[:, vw:vw + 1]
        a = att[:sub] - lam_ref[...] * att[sub:]
        o_ref[0, i * sub:(i + 1) * sub, :] = (_rms(a, sub_ref[...]) * post_scale).astype(o_ref.dtype)

    _one_ahead(q_ref.shape[1] // sub, scores, finish)


def _diff(q, k, v_ext, lam, subln, post_scale, T):
    B, S, _ = k.shape
    tq = ATTN_TQ // 2
    return pl.pallas_call(
        functools.partial(_diff_kernel, post_scale=post_scale),
        grid=(B, DIFF_HEADS, T // tq),
        in_specs=[pl.BlockSpec((1, tq, LANES), lambda b, h, t: (b, t, h)),
                  pl.BlockSpec((1, S, LANES), lambda b, h, t: (b, 0, h)),
                  pl.BlockSpec((1, S, 2 * LANES), lambda b, h, t: (b, 0, h)),
                  pl.BlockSpec((1, 1), lambda b, h, t: (0, 0)),
                  pl.BlockSpec((1, LANES), lambda b, h, t: (0, 0))],
        out_specs=pl.BlockSpec((1, tq, LANES), lambda b, h, t: (b, t, h)),
        out_shape=jax.ShapeDtypeStruct((B, T, DIFF_HEADS * LANES), BF16),
        compiler_params=_params("parallel", "parallel", "parallel"),
        name="diff_attn",
    )(q, k, v_ext, lam, subln)


def _mix_out1_kernel(mla_ref, diff_ref, x_ref, mod_ref, n1_ref, n2_ref, w1_ref, w2_ref, wrh_ref, wrl_ref,
                     xo_ref, f_ref, lg_ref):
    y = _dot(mla_ref[0], w1_ref[...]) + _dot(diff_ref[0], w2_ref[...])
    x = x_ref[0] + mod_ref[0, 0, 2:3, :] * _rms(y, n1_ref[...])
    xo_ref[0] = x
    f = _rms(x, n2_ref[...]) * (1.0 + mod_ref[0, 0, 4:5, :]) + mod_ref[0, 0, 3:4, :]
    f_ref[0] = f
    f_hi, f_lo = _split_bf16(f)
    lg_ref[0] = _dot(f_hi, wrh_ref[...]) + (_dot(f_hi, wrl_ref[...]) + _dot(f_lo, wrh_ref[...]))


def _mix_out1(mla, diff, xa, mod, n1, n2, w1, w2, wr, n_lat_tiles):
    B, T, _ = mla.shape
    D = xa.shape[-1]
    wr_hi, wr_lo = _split_bf16(wr)
    return pl.pallas_call(
        _mix_out1_kernel,
        grid=(B, T // ROW_TILE),
        in_specs=[_row_spec(mla.shape[-1]), _row_spec(diff.shape[-1]), _row_spec(D), _mod_spec(n_lat_tiles),
                  _const_spec((1, D)), _const_spec((1, D)), _const_spec(w1.shape), _const_spec(w2.shape),
                  _const_spec(wr.shape), _const_spec(wr.shape)],
        out_specs=[_row_spec(D), _row_spec(D), _row_spec(LANES)],
        out_shape=[jax.ShapeDtypeStruct((B, T, D), F32), jax.ShapeDtypeStruct((B, T, D), F32),
                   jax.ShapeDtypeStruct((B, T, LANES), F32)],
        compiler_params=_params("parallel", "parallel"),
        name="mix_out1",
    )(mla, diff, xa, mod, n1, n2, w1, w2, wr_hi, wr_lo)


MOE_TM = 1024
MOE_TF = 512


def _moe_kernel(te_ref, ta_ref, x_ref, rw_ref, wg_ref, wu_ref, wd_ref, o_ref, acc_ref):
    i = pl.program_id(0)
    f = pl.program_id(1)

    @pl.when(f == 0)
    def _():
        acc_ref[...] = jnp.zeros_like(acc_ref)

    @pl.when(ta_ref[i] > 0)
    def _():
        x = x_ref[...].astype(BF16)
        h = _silu(_dot(x, wg_ref[0].astype(BF16))) * _dot(x, wu_ref[0].astype(BF16))
        acc_ref[...] += _dot(h.astype(BF16), wd_ref[0].astype(BF16))

    @pl.when(f == pl.num_programs(1) - 1)
    def _():
        o_ref[...] = acc_ref[...] * rw_ref[...]


def _moe_experts(tile_expert, tile_active, x_sorted, row_w, wg, wu, wd):
    R, D = x_sorted.shape
    E, _, F = wg.shape
    nf = F // MOE_TF

    def f_eff(i, f, ta):
        return jnp.where(ta[i] > 0, f, nf - 1)

    grid_spec = pltpu.PrefetchScalarGridSpec(
        num_scalar_prefetch=2,
        grid=(R // MOE_TM, nf),
        in_specs=[pl.BlockSpec((MOE_TM, D), lambda i, f, te, ta: (i, 0)),
                  pl.BlockSpec((MOE_TM, 1), lambda i, f, te, ta: (i, 0)),
                  pl.BlockSpec((1, D, MOE_TF), lambda i, f, te, ta: (te[i], 0, f_eff(i, f, ta))),
                  pl.BlockSpec((1, D, MOE_TF), lambda i, f, te, ta: (te[i], 0, f_eff(i, f, ta))),
                  pl.BlockSpec((1, MOE_TF, D), lambda i, f, te, ta: (te[i], f_eff(i, f, ta), 0))],
        out_specs=pl.BlockSpec((MOE_TM, D), lambda i, f, te, ta: (i, 0)),
        scratch_shapes=[pltpu.VMEM((MOE_TM, D), F32)],
    )
    return pl.pallas_call(
        _moe_kernel,
        grid_spec=grid_spec,
        out_shape=jax.ShapeDtypeStruct((R, D), F32),
        compiler_params=_params("arbitrary", "arbitrary"),
        name="moe_experts",
    )(tile_expert, tile_active, x_sorted, row_w, wg, wu, wd)


def _norm_residual_kernel(a_ref, b_ref, x_ref, mod_ref, nw_ref, o_ref):
    y = a_ref[0] + b_ref[0]
    o_ref[0] = x_ref[0] + mod_ref[0, 0, 5:6, :] * _rms(y, nw_ref[...])


def _norm_residual(a, b, x, mod, norm_w, n_lat_tiles):
    B, T, D = x.shape
    return pl.pallas_call(
        _norm_residual_kernel,
        grid=(B, T // ROW_TILE),
        in_specs=[_row_spec(D), _row_spec(D), _row_spec(D), _mod_spec(n_lat_tiles), _const_spec((1, D))],
        out_specs=_row_spec(D),
        out_shape=jax.ShapeDtypeStruct((B, T, D), F32),
        compiler_params=_params("parallel", "parallel"),
        name="norm_residual",
    )(a, b, x, mod, norm_w)


def _route(logits):
    N = logits.shape[0]
    M = N * TOP_K
    R = M + N_EXPERTS * MOE_TM
    top_val, top_idx = lax.top_k(logits, TOP_K)
    top_w = jax.nn.softmax(top_val, axis=-1).reshape(-1)
    e_flat = top_idx.reshape(-1)
    counts = jnp.sum((e_flat[:, None] == jnp.arange(N_EXPERTS)[None, :]).astype(jnp.int32), axis=0)
    padded = ((counts + MOE_TM - 1) // MOE_TM) * MOE_TM
    ends = jnp.cumsum(padded)
    starts = ends - padded
    ustarts = jnp.cumsum(counts) - counts
    order = jnp.argsort(e_flat, stable=True).astype(jnp.int32)
    rank = jnp.argsort(order).astype(jnp.int32)
    dest = rank + (starts - ustarts)[e_flat]
    tile_start = jnp.arange(R // MOE_TM, dtype=jnp.int32) * MOE_TM
    tile_active = (tile_start < ends[-1]).astype(jnp.int32)
    te = jnp.minimum(jnp.searchsorted(ends, tile_start, side="right"), N_EXPERTS - 1).astype(jnp.int32)
    last_e = te[jnp.maximum(ends[-1] // MOE_TM - 1, 0)]
    tile_expert = jnp.where(tile_active > 0, te, last_e)
    rows = jnp.arange(R, dtype=jnp.int32)
    e_row = tile_expert[rows // MOE_TM]
    within = rows - starts[e_row]
    valid = (within < counts[e_row]) & (rows < ends[-1])
    src = order[jnp.clip(ustarts[e_row] + within, 0, M - 1)]
    row_token = jnp.where(valid, src // TOP_K, rows % N)
    row_w = jnp.where(valid, top_w[src], 0.0)
    return dest.reshape(N, TOP_K), row_token, row_w[:, None], tile_expert, tile_active


def _rope_tables(rows, rot_dim):
    row = jnp.repeat(jnp.arange(rows, dtype=F32), GRID_W)
    col = jnp.tile(jnp.arange(GRID_W, dtype=F32), rows)
    half = rot_dim // 2
    inv_freq = ROPE_THETA ** (-jnp.arange(0, half, 2, dtype=F32) / half)
    ang_r = row[:, None] * inv_freq[None, :]
    ang_c = col[:, None] * inv_freq[None, :]
    ang = jnp.concatenate([ang_r, ang_r, ang_c, ang_c], axis=-1)
    return jnp.cos(ang), jnp.sin(ang)


def _stream_tables(cos, sin, lane0):
    T, R = cos.shape
    if lane0 == 0:
        cos_l, sin_l = jnp.tile(cos, (1, LANES // R)), jnp.tile(sin, (1, LANES // R))
    else:
        pad = ((0, 0), (lane0, LANES - lane0 - R))
        cos_l = jnp.pad(cos - 1.0, pad) + 1.0
        sin_l = jnp.pad(sin, pad)
    ctx = ((0, CTX_LEN), (0, 0))
    return jnp.pad(cos_l - 1.0, ctx) + 1.0, jnp.pad(sin_l, ctx)


def _mod_table(c, c_ctx, mod_w, mod_b):
    B = c.shape[0]
    hp = lax.Precision.HIGHEST
    m_l = (jnp.dot(jax.nn.silu(c), mod_w, precision=hp) + mod_b).reshape(B, 1, 6, D_MODEL)
    m_c = (jnp.dot(jax.nn.silu(c_ctx), mod_w, precision=hp) + mod_b).reshape(1, 1, 6, D_MODEL)
    return jnp.concatenate([jnp.broadcast_to(m_c, m_l.shape), m_l], axis=1)


def _pad_cols(w, n):
    return jnp.pad(w, ((0, 0), (0, n - w.shape[1])))


def _even_layer(xa, mod, norms, w_in, conv_w, a_log, dt_bias, gdn_norm, sink, w_out, ffn_gate, ffn_up, ffn_down,
                cos, sin, n_lat_tiles):
    B, S, D = xa.shape
    T = n_lat_tiles * ROW_TILE
    n_gate = 2 * GDN_HEADS
    c_qkv = 3 * GDN_HD
    c_z = c_qkv + GDN_HD
    c_ga = c_z + n_gate
    c_gb = c_ga + n_gate
    c_sq = c_gb + SWA_Q_HEADS * SWA_HEAD_DIM
    c_sk = c_sq + SWA_KV_HEADS * SWA_HEAD_DIM
    grp = SWA_Q_HEADS // SWA_KV_HEADS
    wq = w_in[:, c_gb:c_sq].reshape(D, SWA_Q_HEADS, SWA_HEAD_DIM) * SWA_HEAD_DIM ** -0.5
    kv_of_head = (jnp.arange(SWA_Q_HEADS) // grp)[None, :, None]
    wq = jnp.concatenate([jnp.where(kv_of_head == 0, wq, 0.0), jnp.where(kv_of_head == 1, wq, 0.0)], axis=-1)
    w_all = jnp.concatenate([w_in[:, :c_z], _pad_cols(w_in[:, c_z:c_gb], LANES),
                             wq.reshape(D, SWA_Q_HEADS * LANES), w_in[:, c_sq:]], axis=1).astype(BF16)
    qkv, z, gate, q_pad, k, v = _even_proj(xa, mod, norms[0:1], w_all, cos, sin, n_lat_tiles)

    head_ones = (jnp.arange(GDN_HD)[:, None] // GDN_DK == jnp.arange(GDN_HD)[None, :] // GDN_DK).astype(BF16)
    conv_w8 = jnp.pad(conv_w, ((0, SUBLANES - CONV_K), (0, 0)))
    gq, gk, gv = _gdn_prep(qkv, conv_w8, head_ones, n_lat_tiles)
    a_log_row = _pad_cols(a_log.reshape(1, n_gate), LANES)
    dt_bias_row = _pad_cols(dt_bias.reshape(1, n_gate), LANES)
    o_f, o_b = _gdn(gq, gk, gv, gate, a_log_row, dt_bias_row, T // GDN_CHUNK)

    sink_rows = jnp.repeat(sink, SWA_BLOCK)[:, None]
    swa = _swa(q_pad, k, v, sink_rows, T // SWA_BLOCK)

    w_s = w_out[GDN_HD:].reshape(SWA_Q_HEADS, SWA_HEAD_DIM, D)
    w_s = jnp.concatenate([jnp.where(kv_of_head.reshape(-1, 1, 1) == 0, w_s, 0.0),
                           jnp.where(kv_of_head.reshape(-1, 1, 1) == 1, w_s, 0.0)], axis=1)
    gdn_norm_row = jnp.tile(gdn_norm, GDN_HEADS)[None, :]
    xa = _mix_out0(o_f, o_b, z, swa, xa, mod, norms[1:2], gdn_norm_row, head_ones,
                   w_out[:GDN_HD].astype(BF16), w_s.reshape(SWA_Q_HEADS * LANES, D).astype(BF16), n_lat_tiles)
    return _ffn(xa, mod, norms[2:3], norms[3:4], ffn_gate.astype(BF16), ffn_up.astype(BF16),
                ffn_down.astype(BF16), n_lat_tiles)


def _odd_layer_last(xa, mod, norms, w_in, q_norm, kv_norm, w_uq, w_ukv, lam_p, lam_init, subln, w_out,
                    router, exp_gate, exp_up, exp_down, cos, sin, cos_r, sin_r, n_lat_tiles):
    B, S, D = xa.shape
    T = n_lat_tiles * ROW_TILE
    H = MLA_HEADS
    c0 = MLA_Q_RANK
    c1 = c0 + MLA_KV_RANK
    c2 = c1 + MLA_ROPE
    dw = DIFF_HEADS * 2 * DIFF_HEAD_DIM
    w_kr = jnp.pad(w_in[:, c1:c2], ((0, 0), (MLA_NOPE, LANES - MLA_NOPE - MLA_ROPE)))
    w_all = jnp.concatenate([w_in[:, :c1], w_kr, w_in[:, c2:c2 + dw] * DIFF_HEAD_DIM ** -0.5,
                             w_in[:, c2 + dw:]], axis=1).astype(BF16)
    cq, ckv, kr, dq, dk, dv = _odd_proj(xa, mod, norms[0:1], w_all, cos, sin, cos_r, sin_r, n_lat_tiles)

    qd = MLA_NOPE + MLA_ROPE
    wq = jnp.pad(w_uq.reshape(MLA_Q_RANK, H, qd) * qd ** -0.5, ((0, 0), (0, 0), (0, LANES - qd)))
    wkv = w_ukv.reshape(MLA_KV_RANK, H, MLA_NOPE + MLA_V)
    wk = jnp.pad(wkv[..., :MLA_NOPE], ((0, 0), (0, 0), (0, LANES - MLA_NOPE)))
    wv = jnp.pad(wkv[..., MLA_NOPE:], ((0, 0), (0, 0), (0, LANES - MLA_V)))
    q_cat, k_cat, v_ext = _mla_up(cq, ckv, kr, q_norm[None, :], kv_norm[None, :],
                                  wq.reshape(MLA_Q_RANK, H * LANES).astype(BF16),
                                  wk.reshape(MLA_KV_RANK, H * LANES).astype(BF16),
                                  wv.reshape(MLA_KV_RANK, H * LANES).astype(BF16), cos_r, sin_r)
    mla = _mla(q_cat, k_cat, v_ext, T)

    lam = (jnp.exp(jnp.sum(lam_p[0] * lam_p[1])) - jnp.exp(jnp.sum(lam_p[2] * lam_p[3])) + lam_init).reshape(1, 1)
    diff = _diff(dq, dk, dv, lam, subln[None, :], 1.0 - lam_init, T)

    w_mla = jnp.pad(w_out[:H * MLA_V].reshape(H, MLA_V, D), ((0, 0), (0, LANES - MLA_V), (0, 0)))
    x, f_in, logits = _mix_out1(mla, diff, xa, mod, norms[1:2], norms[2:3],
                                w_mla.reshape(H * LANES, D).astype(BF16), w_out[H * MLA_V:].astype(BF16),
                                _pad_cols(router, LANES), n_lat_tiles)

    dest, row_token, row_w, tile_expert, tile_active = _route(logits.reshape(B * T, LANES)[:, :N_EXPERTS])
    x_sorted = f_in.reshape(B * T, D)[row_token]
    y = _moe_experts(tile_expert, tile_active, x_sorted, row_w, exp_gate, exp_up, exp_down)
    y0 = y[dest[:, 0]].reshape(B, T, D)
    y1 = y[dest[:, 1]].reshape(B, T, D)
    return _norm_residual(y0, y1, x, mod, norms[3:4], n_lat_tiles)


def kernel(x, c, ctx, c_ctx, e_mod_w, e_mod_b, e_norms, e_w_in, e_conv_w, e_a_log, e_dt_bias, e_gdn_norm, e_sink, e_w_out, e_ffn_gate, e_ffn_up, e_ffn_down, o_mod_w, o_mod_b, o_norms, o_w_in, o_q_norm, o_kv_norm, o_w_uq, o_w_ukv, o_lambda, o_subln, o_w_out, o_router, o_exp_gate, o_exp_up, o_exp_down):
    B, T, D = x.shape
    n_lat_tiles = T // ROW_TILE
    rows = T // GRID_W
    cos64, sin64 = _stream_tables(*_rope_tables(rows, SWA_HEAD_DIM), 0)
    cos_r, sin_r = _stream_tables(*_rope_tables(rows, MLA_ROPE), MLA_NOPE)
    xa = jnp.concatenate([x, ctx], axis=1)
    mod_e = _mod_table(c, c_ctx, e_mod_w[0], e_mod_b[0])
    xa = _even_layer(xa, mod_e, e_norms[0], e_w_in[0], e_conv_w[0], e_a_log[0], e_dt_bias[0], e_gdn_norm[0],
                     e_sink[0], e_w_out[0], e_ffn_gate[0], e_ffn_up[0], e_ffn_down[0], cos64, sin64, n_lat_tiles)
    mod_o = _mod_table(c, c_ctx, o_mod_w[0], o_mod_b[0])
    lam_init = 0.8 - 0.6 * math.exp(-0.3 * 1)
    return _odd_layer_last(xa, mod_o, o_norms[0], o_w_in[0], o_q_norm[0], o_kv_norm[0], o_w_uq[0], o_w_ukv[0],
                           o_lambda[0], lam_init, o_subln[0], o_w_out[0], o_router[0], o_exp_gate[0],
                           o_exp_up[0], o_exp_down[0], cos64, sin64, cos_r, sin_r, n_lat_tiles)
```

```python
import functools
import math

import jax
import jax.numpy as jnp
from jax import lax
from jax.experimental import pallas as pl
from jax.experimental.pallas import tpu as pltpu

F32 = jnp.float32
BF16 = jnp.bfloat16

D_MODEL = 1024
CTX_LEN = 256
GRID_W = 64
NORM_EPS = 1e-6
ROPE_THETA = 10000.0

GDN_HEADS = 8
GDN_DK = 64
GDN_DV = 64
GDN_CHUNK = 64
CONV_K = 5
GDN_GROUP = 4
GDN_LANES = GDN_GROUP * GDN_DK
GDN_HD = GDN_HEADS * GDN_DK
GDN_NB = 4

SWA_Q_HEADS = 8
SWA_KV_HEADS = 2
SWA_HEAD_DIM = 64
SWA_WINDOW = 128
SWA_BLOCK = 128

MLA_HEADS = 8
MLA_Q_RANK = 384
MLA_KV_RANK = 256
MLA_NOPE = 64
MLA_ROPE = 32
MLA_V = 64

DIFF_HEADS = 4
DIFF_HEAD_DIM = 64

D_FF = 2816
N_EXPERTS = 8
TOP_K = 2
D_FF_EXPERT = 3584

LANES = 128
SUBLANES = 8
ROW_TILE = 256
SWA_HEADS_PER_SUB = 2
ATTN_TQ = 1024
ATTN_SUB = 128
VMEM_LIMIT = 56 * 1024 * 1024
NEG_BIG = -1e30


def _dot(a, b):
    return jnp.dot(a, b, preferred_element_type=F32)


def _dot_nt(a, b):
    return lax.dot_general(a, b, (((1,), (1,)), ((), ())), preferred_element_type=F32)


def _dot_tn(a, b):
    return lax.dot_general(a, b, (((0,), (0,)), ((), ())), preferred_element_type=F32)


def _split_bf16(x):
    hi = x.astype(BF16)
    lo = (x - hi.astype(F32)).astype(BF16)
    return hi, lo


def _dot_split(x, w):
    hi, lo = _split_bf16(x)
    return _dot(hi, w) + _dot(lo, w)


def _dot_rsplit(w, x):
    hi, lo = _split_bf16(x)
    return _dot(w, hi) + _dot(w, lo)


def _rms(x, w):
    return x * lax.rsqrt(jnp.mean(x * x, axis=-1, keepdims=True) + NORM_EPS) * w


def _silu(x):
    return x * jax.nn.sigmoid(x)


def _rope_slab(x, cos, sin, quarter):
    lane = lax.broadcasted_iota(jnp.int32, x.shape, 1)
    fwd = pltpu.roll(x, quarter, 1)
    back = pltpu.roll(x, LANES - quarter, 1)
    rot = jnp.where(lane % (2 * quarter) < quarter, -back, fwd)
    return x * cos + rot * sin


def _params(*sem):
    return pltpu.CompilerParams(dimension_semantics=sem, vmem_limit_bytes=VMEM_LIMIT)


def _row_spec(width, tile=ROW_TILE):
    return pl.BlockSpec((1, tile, width), lambda b, t: (b, t, 0))


def _mod_spec(n_lat_tiles):
    return pl.BlockSpec((1, 1, 6, D_MODEL), lambda b, t: (b, jnp.where(t < n_lat_tiles, 1, 0), 0, 0))


def _const_spec(shape):
    return pl.BlockSpec(shape, lambda b, t: (0,) * len(shape))


def _table_spec():
    return pl.BlockSpec((ROW_TILE, LANES), lambda b, t: (t, 0))


E_QKV = (0, 3 * GDN_HD)
E_Z = (E_QKV[1], E_QKV[1] + GDN_HD)
E_GATE = (E_Z[1], E_Z[1] + LANES)
E_Q = (E_GATE[1], E_GATE[1] + SWA_Q_HEADS * LANES)
E_K = (E_Q[1], E_Q[1] + LANES)
E_V = (E_K[1], E_K[1] + LANES)


def _even_proj_kernel(x_ref, mod_ref, nw_ref, w_ref, cos_ref, sin_ref,
                      qkv_ref, z_ref, gate_ref, q_ref, k_ref, v_ref):
    a = (_rms(x_ref[0], nw_ref[...]) * (1.0 + mod_ref[0, 0, 1:2, :]) + mod_ref[0, 0, 0:1, :]).astype(BF16)

    def proj(cols):
        return _dot(a, w_ref[:, cols[0]:cols[1]])

    qkv_ref[0] = proj(E_QKV)
    z_ref[0] = proj(E_Z).astype(BF16)
    gate_ref[0] = proj(E_GATE)
    cos = cos_ref[...]
    sin = sin_ref[...]
    quarter = SWA_HEAD_DIM // 4
    qp = proj(E_Q)
    for h in range(SWA_Q_HEADS):
        ls = slice(h * LANES, (h + 1) * LANES)
        q_ref[0, :, ls] = _rope_slab(qp[:, ls], cos, sin, quarter).astype(BF16)
    k_ref[0] = _rope_slab(proj(E_K), cos, sin, quarter).astype(BF16)
    v_ref[0] = proj(E_V).astype(BF16)


def _even_proj(xa, mod, norm_w, w, cos, sin, n_lat_tiles):
    B, S, D = xa.shape
    widths = (E_QKV[1] - E_QKV[0], GDN_HD, LANES, SWA_Q_HEADS * LANES, LANES, LANES)
    dtypes = (F32, BF16, F32, BF16, BF16, BF16)
    return pl.pallas_call(
        _even_proj_kernel,
        grid=(B, S // ROW_TILE),
        in_specs=[_row_spec(D), _mod_spec(n_lat_tiles), _const_spec((1, D)), _const_spec(w.shape),
                  _table_spec(), _table_spec()],
        out_specs=[_row_spec(n) for n in widths],
        out_shape=[jax.ShapeDtypeStruct((B, S, n), dt) for n, dt in zip(widths, dtypes)],
        compiler_params=_params("parallel", "parallel"),
        name="even_proj",
    )(xa, mod, norm_w, w, cos, sin)


def _gdn_prep_kernel(x_ref, prev_ref, next_ref, cw_ref, bd_ref, q_ref, k_ref, v_ref, *, n_lat_tiles):
    t = pl.program_id(1)
    has_prev = jnp.logical_and(t > 0, t < n_lat_tiles)
    has_next = t < n_lat_tiles - 1
    prev = jnp.where(has_prev, prev_ref[0], 0.0)
    nxt = jnp.where(has_next, next_ref[0], 0.0)
    xe = jnp.concatenate([prev, x_ref[0], nxt], axis=0)
    pad = CONV_K // 2
    y = jnp.zeros(x_ref.shape[1:], F32)
    for kk in range(CONV_K):
        lo = SUBLANES + kk - pad
        y = y + cw_ref[kk:kk + 1, :] * xe[lo:lo + ROW_TILE, :]
    u = _silu(y)

    def headnorm(a):
        return a * lax.rsqrt(_dot_split(a * a, bd_ref[...]) + NORM_EPS)

    q_ref[0] = headnorm(u[:, :GDN_HD]) * GDN_DK ** -0.5
    k_ref[0] = headnorm(u[:, GDN_HD:2 * GDN_HD])
    v_ref[0] = u[:, 2 * GDN_HD:]


def _gdn_prep(qkv, conv_w, head_ones, n_lat_tiles):
    B, S, W = qkv.shape
    per_tile = ROW_TILE // SUBLANES
    n8 = S // SUBLANES
    return pl.pallas_call(
        functools.partial(_gdn_prep_kernel, n_lat_tiles=n_lat_tiles),
        grid=(B, S // ROW_TILE),
        in_specs=[_row_spec(W),
                  pl.BlockSpec((1, SUBLANES, W), lambda b, t: (b, jnp.maximum(t * per_tile - 1, 0), 0)),
                  pl.BlockSpec((1, SUBLANES, W), lambda b, t: (b, jnp.minimum((t + 1) * per_tile, n8 - 1), 0)),
                  _const_spec(conv_w.shape), _const_spec(head_ones.shape)],
        out_specs=[_row_spec(GDN_HD)] * 3,
        out_shape=[jax.ShapeDtypeStruct((B, S, GDN_HD), F32)] * 3,
        compiler_params=_params("parallel", "parallel"),
        name="gdn_prep",
    )(qkv, qkv, qkv, conv_w, head_ones)


def _block_diag(x, head_of_lane):
    zero = jnp.zeros_like(x)
    return jnp.concatenate([jnp.where(head_of_lane == h, x, zero) for h in range(GDN_GROUP)], axis=0)


def _expand_heads(x4, head_of_lane):
    c = x4.shape[0]
    out = jnp.broadcast_to(x4[:, GDN_GROUP - 1:GDN_GROUP], (c, GDN_LANES))
    for h in range(GDN_GROUP - 2, -1, -1):
        out = jnp.where(head_of_lane == h, jnp.broadcast_to(x4[:, h:h + 1], (c, GDN_LANES)), out)
    return out


def _gdn_groups(items):
    C = GDN_CHUNK
    row = lax.broadcasted_iota(jnp.int32, (C, GDN_LANES), 0)
    lane = lax.broadcasted_iota(jnp.int32, (C, GDN_LANES), 1)
    hol = lane // GDN_DK
    col = lane % GDN_DK
    r64 = lax.broadcasted_iota(jnp.int32, (C, C), 0)
    c64 = lax.broadcasted_iota(jnp.int32, (C, C), 1)
    ones = jnp.ones((C, C), BF16)
    eye = jnp.where(row == col, 1.0, 0.0)

    def bd(x):
        return _block_diag(x, hol)

    st = []
    for (q4, k4, v4, g4, b4, s4, reverse) in items:
        sgn = -1 if reverse else 1
        ahead = (row - col) * sgn
        incl = ahead >= 0
        strict = ahead > 0
        tri = jnp.where((r64 - c64) * sgn >= 0, 1.0, 0.0).astype(BF16)
        gE = _expand_heads(g4, hol)
        bE = _expand_heads(b4, hol)
        gc = _dot_rsplit(tri, gE)
        gc_row = _dot_rsplit(ones, jnp.where((col - row) * sgn >= 0, gE, 0.0))
        decay = jnp.where(incl, jnp.exp(gc - gc_row), 0.0)
        g_last = gc[0:1, :] if reverse else gc[C - 1:C, :]
        eg = jnp.exp(gc)
        kb = k4 * bE
        st.append(dict(q=q4, k=k4, s=s4, incl=incl, strict=strict, decay=decay, g_last=g_last,
                       kb=kb, vb=v4 * bE, kbg=kb * eg, qh=q4 * eg, kt=k4 * jnp.exp(g_last - gc)))

    for d in st:
        gram = _dot_nt(jnp.concatenate([d["kb"], d["q"]], axis=0).astype(BF16), bd(d["k"].astype(BF16)))
        L = jnp.where(d["strict"], gram[:C] * d["decay"], 0.0)
        d["A"] = gram[C:] * d["decay"]
        d["T"] = eye - L
        d["P"] = L.astype(BF16)
    for _ in range(int(math.log2(C)) - 1):
        for d in st:
            d["P"] = _dot(d["P"], bd(d["P"])).astype(BF16)
        for d in st:
            d["T"] = d["T"] + _dot(d["T"].astype(BF16), bd(d["P"]))
    for d in st:
        Tb = d["T"].astype(BF16)
        d["u"] = _dot(Tb, bd(d["vb"].astype(BF16)))
        d["w"] = _dot(Tb, bd(d["kbg"].astype(BF16)))
    for d in st:
        d["ws_qs"] = _dot(jnp.concatenate([d["w"], d["qh"]], axis=0).astype(BF16), bd(d["s"].astype(BF16)))
    out = []
    for d in st:
        v_new = d["u"] - d["ws_qs"][:C]
        vb16 = v_new.astype(BF16)
        o = d["ws_qs"][C:] + _dot(d["A"].astype(BF16), bd(vb16))
        full = _dot_tn(d["kt"].astype(BF16), vb16)
        upd = jnp.zeros((GDN_DK, GDN_LANES), F32)
        for h in range(GDN_GROUP):
            upd = upd + jnp.where(hol == h, full[h * GDN_DK:(h + 1) * GDN_DK, :], 0.0)
        out.append((o, d["s"] * jnp.exp(d["g_last"]) + upd))
    return out


def _gdn_kernel(qf_ref, kf_ref, vf_ref, gf_ref, qb_ref, kb_ref, vb_ref, gb_ref, al_ref, dtb_ref,
                of_ref, ob_ref, s_ref):
    @pl.when(pl.program_id(1) == 0)
    def _():
        s_ref[...] = jnp.zeros_like(s_ref)

    n_grp = GDN_HEADS // GDN_GROUP
    n_gate = 2 * GDN_HEADS
    items = []
    slots = []
    for nb in range(GDN_NB):
        for d, (q_ref, k_ref, v_ref, g_ref, o_ref) in enumerate(((qf_ref, kf_ref, vf_ref, gf_ref, of_ref),
                                                                 (qb_ref, kb_ref, vb_ref, gb_ref, ob_ref))):
            raw = g_ref[nb]
            xs = raw + dtb_ref[...]
            softplus = jnp.maximum(xs, 0.0) + jnp.log(1.0 + jnp.exp(-jnp.abs(xs)))
            g = -jnp.exp(al_ref[...]) * softplus
            beta = jax.nn.sigmoid(raw)
            for grp in range(n_grp):
                ls = slice(grp * GDN_LANES, (grp + 1) * GDN_LANES)
                c0 = d * GDN_HEADS + grp * GDN_GROUP
                slot = (nb * 2 + d) * n_grp + grp
                items.append((q_ref[nb, :, ls], k_ref[nb, :, ls], v_ref[nb, :, ls],
                              g[:, c0:c0 + GDN_GROUP], beta[:, n_gate + c0:n_gate + c0 + GDN_GROUP],
                              s_ref[slot], d == 1))
                slots.append((o_ref, nb, ls, slot))
    for (o_ref, nb, ls, slot), (o, s_new) in zip(slots, _gdn_groups(items)):
        o_ref[nb, :, ls] = o
        s_ref[slot] = s_new


def _gdn(q, k, v, gate, a_log_row, dt_bias_row, n_lat_chunks):
    B, S, HD = q.shape
    n_chunks = S // GDN_CHUNK
    n_ctx = n_chunks - n_lat_chunks

    def fwd_c(s):
        return jnp.where(s < n_ctx, n_lat_chunks + s, s - n_ctx)

    def bwd_c(s):
        return n_chunks - 1 - s

    def spec(cmap, width):
        return pl.BlockSpec((GDN_NB, GDN_CHUNK, width), lambda b, s: (b, cmap(s), 0))

    return pl.pallas_call(
        _gdn_kernel,
        grid=(B // GDN_NB, n_chunks),
        in_specs=[spec(fwd_c, HD), spec(fwd_c, HD), spec(fwd_c, HD), spec(fwd_c, LANES),
                  spec(bwd_c, HD), spec(bwd_c, HD), spec(bwd_c, HD), spec(bwd_c, LANES),
                  _const_spec((1, LANES)), _const_spec((1, LANES))],
        out_specs=[spec(fwd_c, HD), spec(bwd_c, HD)],
        out_shape=[jax.ShapeDtypeStruct((B, S, HD), F32)] * 2,
        scratch_shapes=[pltpu.VMEM((GDN_NB * 2 * GDN_HEADS // GDN_GROUP, GDN_DK, GDN_LANES), F32)],
        compiler_params=_params("parallel", "arbitrary"),
        name="gdn_scan",
    )(q, k, v, gate, q, k, v, gate, a_log_row, dt_bias_row)


def _swa_kernel(q_ref, k_ref, v_ref, sink_ref, o_ref, *, n_lat_blocks):
    W = SWA_BLOCK
    H = SWA_Q_HEADS
    n = pl.program_id(1)
    is_lat = n < n_lat_blocks
    nl = jnp.minimum(n, n_lat_blocks - 1)
    prev = jnp.maximum(nl - 1, 0)
    nxt = jnp.minimum(nl + 1, n_lat_blocks - 1)
    ctx0 = n_lat_blocks * W

    def keys(ref):
        def blk(i):
            return ref[0, pl.ds(pl.multiple_of(i * W, W), W), :]
        return jnp.concatenate([ref[0, ctx0:ctx0 + CTX_LEN, :], blk(prev), blk(nl), blk(nxt)], axis=0)

    k_all = keys(k_ref)
    v_all = keys(v_ref)
    nk = CTX_LEN + 3 * W
    v_ext = jnp.concatenate([v_all, jnp.ones((nk, LANES), BF16)], axis=1)
    hp = SWA_HEADS_PER_SUB
    ii = lax.broadcasted_iota(jnp.int32, (hp * W, nk), 0) % W
    jj = lax.broadcasted_iota(jnp.int32, (hp * W, nk), 1) - CTX_LEN
    in_window = (((jj >= 0) & (jj < W) & (jj >= ii) & (nl > 0))
                 | ((jj >= W) & (jj < 2 * W))
                 | ((jj >= 2 * W) & (jj - 2 * W <= ii) & (nl < n_lat_blocks - 1)))
    valid = (jj < 0) | (in_window & is_lat)

    def scores(i):
        q = jnp.concatenate([q_ref[0, :, h * LANES:(h + 1) * LANES] for h in range(i * hp, (i + 1) * hp)], axis=0)
        return jnp.where(valid, _dot_nt(q, k_all), NEG_BIG)

    def finish(i, s):
        sk = sink_ref[i * hp * W:(i + 1) * hp * W, :]
        m = jnp.maximum(jnp.max(s, axis=-1, keepdims=True), sk)
        acc = _dot(jnp.exp(s - m).astype(BF16), v_ext)
        o = acc[:, :LANES] / (acc[:, LANES:LANES + 1] + jnp.exp(sk - m))
        for j in range(hp):
            h = i * hp + j
            o_ref[0, :, h * LANES:(h + 1) * LANES] = o[j * W:(j + 1) * W].astype(o_ref.dtype)

    _one_ahead(H // hp, scores, finish)


def _swa(q_pad, k, v, sink_rows, n_lat_blocks):
    B, S, _ = k.shape
    W = SWA_BLOCK
    QW = SWA_Q_HEADS * LANES
    return pl.pallas_call(
        functools.partial(_swa_kernel, n_lat_blocks=n_lat_blocks),
        grid=(B, S // W),
        in_specs=[pl.BlockSpec((1, W, QW), lambda b, n: (b, n, 0)),
                  pl.BlockSpec((1, S, LANES), lambda b, n: (b, 0, 0)),
                  pl.BlockSpec((1, S, LANES), lambda b, n: (b, 0, 0)),
                  _const_spec(sink_rows.shape)],
        out_specs=pl.BlockSpec((1, W, QW), lambda b, n: (b, n, 0)),
        out_shape=jax.ShapeDtypeStruct((B, S, QW), BF16),
        compiler_params=_params("parallel", "parallel"),
        name="swa",
    )(q_pad, k, v, sink_rows)


def _mix_out0_kernel(of_ref, ob_ref, z_ref, swa_ref, x_ref, mod_ref, nw_ref, gn_ref, bd_ref, wg_ref, ws_ref,
                     o_ref):
    o = of_ref[0] + ob_ref[0]
    ms = _dot_split(o * o, bd_ref[...]) * (1.0 / GDN_DV)
    gdn = o * lax.rsqrt(ms + NORM_EPS) * gn_ref[...] * _silu(z_ref[0].astype(F32))
    y = _dot(gdn.astype(BF16), wg_ref[...]) + _dot(swa_ref[0], ws_ref[...])
    o_ref[0] = x_ref[0] + mod_ref[0, 0, 2:3, :] * _rms(y, nw_ref[...])


def _mix_out0(o_f, o_b, z, swa, xa, mod, norm_w, gdn_norm_row, head_ones, w_gdn, w_swa, n_lat_tiles):
    B, S, D = xa.shape
    return pl.pallas_call(
        _mix_out0_kernel,
        grid=(B, S // ROW_TILE),
        in_specs=[_row_spec(GDN_HD), _row_spec(GDN_HD), _row_spec(GDN_HD), _row_spec(swa.shape[-1]), _row_spec(D),
                  _mod_spec(n_lat_tiles), _const_spec((1, D)), _const_spec(gdn_norm_row.shape),
                  _const_spec(head_ones.shape), _const_spec(w_gdn.shape), _const_spec(w_swa.shape)],
        out_specs=_row_spec(D),
        out_shape=jax.ShapeDtypeStruct((B, S, D), F32),
        compiler_params=_params("parallel", "parallel"),
        name="mix_out0",
    )(o_f, o_b, z, swa, xa, mod, norm_w, gdn_norm_row, head_ones, w_gdn, w_swa)


def _ffn_kernel(x_ref, mod_ref, n_in_ref, n_out_ref, wg_ref, wu_ref, wd_ref, o_ref):
    x = x_ref[0]
    f = (_rms(x, n_in_ref[...]) * (1.0 + mod_ref[0, 0, 4:5, :]) + mod_ref[0, 0, 3:4, :]).astype(BF16)
    h = _silu(_dot(f, wg_ref[...])) * _dot(f, wu_ref[...])
    y = _dot(h.astype(BF16), wd_ref[...])
    o_ref[0] = x + mod_ref[0, 0, 5:6, :] * _rms(y, n_out_ref[...])


def _ffn(xa, mod, n_in, n_out, wg, wu, wd, n_lat_tiles):
    B, S, D = xa.shape
    return pl.pallas_call(
        _ffn_kernel,
        grid=(B, S // ROW_TILE),
        in_specs=[_row_spec(D), _mod_spec(n_lat_tiles), _const_spec((1, D)), _const_spec((1, D)),
                  _const_spec(wg.shape), _const_spec(wu.shape), _const_spec(wd.shape)],
        out_specs=_row_spec(D),
        out_shape=jax.ShapeDtypeStruct((B, S, D), F32),
        compiler_params=_params("parallel", "parallel"),
        name="dense_ffn",
    )(xa, mod, n_in, n_out, wg, wu, wd)


O_CQ = (0, MLA_Q_RANK)
O_CKV = (O_CQ[1], O_CQ[1] + MLA_KV_RANK)
O_KR = (O_CKV[1], O_CKV[1] + LANES)
O_DQ = (O_KR[1], O_KR[1] + DIFF_HEADS * LANES)
O_DK = (O_DQ[1], O_DQ[1] + DIFF_HEADS * LANES)
O_DV = (O_DK[1], O_DK[1] + DIFF_HEADS * LANES)


def _odd_proj_kernel(x_ref, mod_ref, nw_ref, w_ref, cos_ref, sin_ref, cosr_ref, sinr_ref,
                     cq_ref, ckv_ref, kr_ref, dq_ref, dk_ref, dv_ref):
    a = (_rms(x_ref[0], nw_ref[...]) * (1.0 + mod_ref[0, 0, 1:2, :]) + mod_ref[0, 0, 0:1, :]).astype(BF16)

    def proj(cols):
        return _dot(a, w_ref[:, cols[0]:cols[1]])

    cq_ref[0] = proj(O_CQ)
    ckv_ref[0] = proj(O_CKV)
    kr_ref[0] = _rope_slab(proj(O_KR), cosr_ref[...], sinr_ref[...], MLA_ROPE // 4)
    cos = cos_ref[...]
    sin = sin_ref[...]
    quarter = DIFF_HEAD_DIM // 4
    dq = proj(O_DQ)
    dk = proj(O_DK)
    dv = proj(O_DV)
    ones = jnp.ones((ROW_TILE, LANES), BF16)
    for h in range(DIFF_HEADS):
        ls = slice(h * LANES, (h + 1) * LANES)
        dq_ref[0, :, ls] = _rope_slab(dq[:, ls], cos, sin, quarter).astype(BF16)
        dk_ref[0, :, ls] = _rope_slab(dk[:, ls], cos, sin, quarter).astype(BF16)
        dv_ref[0, :, 2 * h * LANES:(2 * h + 1) * LANES] = dv[:, ls].astype(BF16)
        dv_ref[0, :, (2 * h + 1) * LANES:(2 * h + 2) * LANES] = ones


def _odd_proj(xa, mod, norm_w, w, cos, sin, cos_r, sin_r, n_lat_tiles):
    B, S, D = xa.shape
    widths = (MLA_Q_RANK, MLA_KV_RANK, LANES, DIFF_HEADS * LANES, DIFF_HEADS * LANES, 2 * DIFF_HEADS * LANES)
    dtypes = (F32, F32, F32, BF16, BF16, BF16)
    return pl.pallas_call(
        _odd_proj_kernel,
        grid=(B, S // ROW_TILE),
        in_specs=[_row_spec(D), _mod_spec(n_lat_tiles), _const_spec((1, D)), _const_spec(w.shape),
                  _table_spec(), _table_spec(), _table_spec(), _table_spec()],
        out_specs=[_row_spec(n) for n in widths],
        out_shape=[jax.ShapeDtypeStruct((B, S, n), dt) for n, dt in zip(widths, dtypes)],
        compiler_params=_params("parallel", "parallel"),
        name="odd_proj",
    )(xa, mod, norm_w, w, cos, sin, cos_r, sin_r)


def _mla_up_kernel(cq_ref, ckv_ref, kr_ref, qn_ref, kvn_ref, wq_ref, wk_ref, wv_ref, cos_ref, sin_ref,
                   q_ref, k_ref, v_ref):
    cq = _rms(cq_ref[0], qn_ref[...]).astype(BF16)
    ckv = _rms(ckv_ref[0], kvn_ref[...]).astype(BF16)
    qp = _dot(cq, wq_ref[...])
    kp = _dot(ckv, wk_ref[...])
    vp = _dot(ckv, wv_ref[...])
    kr = kr_ref[0]
    cos = cos_ref[...]
    sin = sin_ref[...]
    lane = lax.broadcasted_iota(jnp.int32, (ROW_TILE, LANES), 1)
    ones_hi = jnp.where(lane >= MLA_V, 1.0, 0.0)
    for h in range(MLA_HEADS):
        ls = slice(h * LANES, (h + 1) * LANES)
        q_ref[0, :, ls] = _rope_slab(qp[:, ls], cos, sin, MLA_ROPE // 4).astype(BF16)
        k_ref[0, :, ls] = (kp[:, ls] + kr).astype(BF16)
        v_ref[0, :, ls] = (vp[:, ls] + ones_hi).astype(BF16)


def _mla_up(cq, ckv, kr, q_norm, kv_norm, wq, wk, wv, cos_r, sin_r):
    B, S, _ = cq.shape
    HW = MLA_HEADS * LANES
    return pl.pallas_call(
        _mla_up_kernel,
        grid=(B, S // ROW_TILE),
        in_specs=[_row_spec(MLA_Q_RANK), _row_spec(MLA_KV_RANK), _row_spec(LANES),
                  _const_spec(q_norm.shape), _const_spec(kv_norm.shape),
                  _const_spec(wq.shape), _const_spec(wk.shape), _const_spec(wv.shape),
                  _table_spec(), _table_spec()],
        out_specs=[_row_spec(HW)] * 3,
        out_shape=[jax.ShapeDtypeStruct((B, S, HW), BF16)] * 3,
        compiler_params=_params("parallel", "parallel"),
        name="mla_up",
    )(cq, ckv, kr, q_norm, kv_norm, wq, wk, wv, cos_r, sin_r)


def _one_ahead(n_sub, scores, finish):
    s_prev = scores(0)
    for i in range(1, n_sub):
        s_next = scores(i)
        finish(i - 1, s_prev)
        s_prev = s_next
    finish(n_sub - 1, s_prev)


def _mla_kernel(q_ref, k_ref, v_ref, o_ref):
    k = k_ref[0]
    v = v_ref[0]

    def scores(i):
        return _dot_nt(q_ref[0, i * ATTN_SUB:(i + 1) * ATTN_SUB, :], k)

    def finish(i, s):
        m = jnp.max(s, axis=-1, keepdims=True)
        acc = _dot(jnp.exp(s - m).astype(BF16), v)
        o_ref[0, i * ATTN_SUB:(i + 1) * ATTN_SUB, :] = (acc / acc[:, MLA_V:MLA_V + 1]).astype(o_ref.dtype)

    _one_ahead(q_ref.shape[1] // ATTN_SUB, scores, finish)


def _mla(q, k, v_ext, T):
    B, S, _ = k.shape
    return pl.pallas_call(
        _mla_kernel,
        grid=(B, MLA_HEADS, T // ATTN_TQ),
        in_specs=[pl.BlockSpec((1, ATTN_TQ, LANES), lambda b, h, t: (b, t, h)),
                  pl.BlockSpec((1, S, LANES), lambda b, h, t: (b, 0, h)),
                  pl.BlockSpec((1, S, LANES), lambda b, h, t: (b, 0, h))],
        out_specs=pl.BlockSpec((1, ATTN_TQ, LANES), lambda b, h, t: (b, t, h)),
        out_shape=jax.ShapeDtypeStruct((B, T, MLA_HEADS * LANES), BF16),
        compiler_params=_params("parallel", "parallel", "parallel"),
        name="mla_attn",
    )(q, k, v_ext)


def _diff_kernel(q_ref, k_ref, v_ref, lam_ref, sub_ref, o_ref, *, post_scale):
    k = k_ref[0]
    v = v_ref[0]
    sub = ATTN_SUB
    vw = 2 * DIFF_HEAD_DIM
    lo = lax.broadcasted_iota(jnp.int32, (sub, LANES), 1) < DIFF_HEAD_DIM
    zero = jnp.zeros((sub, LANES), BF16)

    def scores(i):
        q = q_ref[0, i * sub:(i + 1) * sub, :]
        return _dot_nt(jnp.concatenate([jnp.where(lo, q, zero), jnp.where(lo, zero, q)], axis=0), k)

    def finish(i, s):
        m = jnp.max(s, axis=-1, keepdims=True)
        acc = _dot(jnp.exp(s - m).astype(BF16), v)
        att = acc[:, :vw] / acc[:, vw:vw + 1]
        a = att[:sub] - lam_ref[...] * att[sub:]
        o_ref[0, i * sub:(i + 1) * sub, :] = (_rms(a, sub_ref[...]) * post_scale).astype(o_ref.dtype)

    _one_ahead(q_ref.shape[1] // sub, scores, finish)


def _diff(q, k, v_ext, lam, subln, post_scale, T):
    B, S, _ = k.shape
    tq = ATTN_TQ // 2
    return pl.pallas_call(
        functools.partial(_diff_kernel, post_scale=post_scale),
        grid=(B, DIFF_HEADS, T // tq),
        in_specs=[pl.BlockSpec((1, tq, LANES), lambda b, h, t: (b, t, h)),
                  pl.BlockSpec((1, S, LANES), lambda b, h, t: (b, 0, h)),
                  pl.BlockSpec((1, S, 2 * LANES), lambda b, h, t: (b, 0, h)),
                  pl.BlockSpec((1, 1), lambda b, h, t: (0, 0)),
                  pl.BlockSpec((1, LANES), lambda b, h, t: (0, 0))],
        out_specs=pl.BlockSpec((1, tq, LANES), lambda b, h, t: (b, t, h)),
        out_shape=jax.ShapeDtypeStruct((B, T, DIFF_HEADS * LANES), BF16),
        compiler_params=_params("parallel", "parallel", "parallel"),
        name="diff_attn",
    )(q, k, v_ext, lam, subln)


def _mix_out1_kernel(mla_ref, diff_ref, x_ref, mod_ref, n1_ref, n2_ref, w1_ref, w2_ref, wrh_ref, wrl_ref,
                     xo_ref, f_ref, lg_ref):
    y = _dot(mla_ref[0], w1_ref[...]) + _dot(diff_ref[0], w2_ref[...])
    x = x_ref[0] + mod_ref[0, 0, 2:3, :] * _rms(y, n1_ref[...])
    xo_ref[0] = x
    f = _rms(x, n2_ref[...]) * (1.0 + mod_ref[0, 0, 4:5, :]) + mod_ref[0, 0, 3:4, :]
    f_ref[0] = f
    f_hi, f_lo = _split_bf16(f)
    lg_ref[0] = _dot(f_hi, wrh_ref[...]) + (_dot(f_hi, wrl_ref[...]) + _dot(f_lo, wrh_ref[...]))


def _mix_out1(mla, diff, xa, mod, n1, n2, w1, w2, wr, n_lat_tiles):
    B, T, _ = mla.shape
    D = xa.shape[-1]
    wr_hi, wr_lo = _split_bf16(wr)
    return pl.pallas_call(
        _mix_out1_kernel,
        grid=(B, T // ROW_TILE),
        in_specs=[_row_spec(mla.shape[-1]), _row_spec(diff.shape[-1]), _row_spec(D), _mod_spec(n_lat_tiles),
                  _const_spec((1, D)), _const_spec((1, D)), _const_spec(w1.shape), _const_spec(w2.shape),
                  _const_spec(wr.shape), _const_spec(wr.shape)],
        out_specs=[_row_spec(D), _row_spec(D), _row_spec(LANES)],
        out_shape=[jax.ShapeDtypeStruct((B, T, D), F32), jax.ShapeDtypeStruct((B, T, D), F32),
                   jax.ShapeDtypeStruct((B, T, LANES), F32)],
        compiler_params=_params("parallel", "parallel"),
        name="mix_out1",
    )(mla, diff, xa, mod, n1, n2, w1, w2, wr_hi, wr_lo)


MOE_TM = 1024
MOE_TF = 512
MOE_SUB = 256


def _moe_kernel(te_ref, ta_ref, x_ref, rw_ref, wg_ref, wu_ref, wd_ref, o_ref, acc_ref):
    i = pl.program_id(0)
    f = pl.program_id(1)

    @pl.when(f == 0)
    def _():
        acc_ref[...] = jnp.zeros_like(acc_ref)

    @pl.when(ta_ref[i] > 0)
    def _():
        wg = wg_ref[0].astype(BF16)
        wu = wu_ref[0].astype(BF16)
        wd = wd_ref[0].astype(BF16)

        def gate_up(j):
            x = x_ref[j * MOE_SUB:(j + 1) * MOE_SUB, :].astype(BF16)
            return _dot(x, wg), _dot(x, wu)

        def down(j, gu):
            h = (_silu(gu[0]) * gu[1]).astype(BF16)
            acc_ref[j * MOE_SUB:(j + 1) * MOE_SUB, :] += _dot(h, wd)

        _one_ahead(MOE_TM // MOE_SUB, gate_up, down)

    @pl.when(f == pl.num_programs(1) - 1)
    def _():
        o_ref[...] = acc_ref[...] * rw_ref[...]


def _moe_experts(tile_expert, tile_active, x_sorted, row_w, wg, wu, wd):
    R, D = x_sorted.shape
    E, _, F = wg.shape
    nf = F // MOE_TF

    def f_eff(i, f, ta):
        return jnp.where(ta[i] > 0, f, nf - 1)

    grid_spec = pltpu.PrefetchScalarGridSpec(
        num_scalar_prefetch=2,
        grid=(R // MOE_TM, nf),
        in_specs=[pl.BlockSpec((MOE_TM, D), lambda i, f, te, ta: (i, 0)),
                  pl.BlockSpec((MOE_TM, 1), lambda i, f, te, ta: (i, 0)),
                  pl.BlockSpec((1, D, MOE_TF), lambda i, f, te, ta: (te[i], 0, f_eff(i, f, ta))),
                  pl.BlockSpec((1, D, MOE_TF), lambda i, f, te, ta: (te[i], 0, f_eff(i, f, ta))),
                  pl.BlockSpec((1, MOE_TF, D), lambda i, f, te, ta: (te[i], f_eff(i, f, ta), 0))],
        out_specs=pl.BlockSpec((MOE_TM, D), lambda i, f, te, ta: (i, 0)),
        scratch_shapes=[pltpu.VMEM((MOE_TM, D), F32)],
    )
    return pl.pallas_call(
        _moe_kernel,
        grid_spec=grid_spec,
        out_shape=jax.ShapeDtypeStruct((R, D), F32),
        compiler_params=_params("arbitrary", "arbitrary"),
        name="moe_experts",
    )(tile_expert, tile_active, x_sorted, row_w, wg, wu, wd)


def _norm_residual_kernel(a_ref, b_ref, x_ref, mod_ref, nw_ref, o_ref):
    y = a_ref[0] + b_ref[0]
    o_ref[0] = x_ref[0] + mod_ref[0, 0, 5:6, :] * _rms(y, nw_ref[...])


def _norm_residual(a, b, x, mod, norm_w, n_lat_tiles):
    B, T, D = x.shape
    return pl.pallas_call(
        _norm_residual_kernel,
        grid=(B, T // ROW_TILE),
        in_specs=[_row_spec(D), _row_spec(D), _row_spec(D), _mod_spec(n_lat_tiles), _const_spec((1, D))],
        out_specs=_row_spec(D),
        out_shape=jax.ShapeDtypeStruct((B, T, D), F32),
        compiler_params=_params("parallel", "parallel"),
        name="norm_residual",
    )(a, b, x, mod, norm_w)


def _route(logits):
    N = logits.shape[0]
    M = N * TOP_K
    R = M + N_EXPERTS * MOE_TM
    top_val, top_idx = lax.top_k(logits, TOP_K)
    top_w = jax.nn.softmax(top_val, axis=-1).reshape(-1)
    e_flat = top_idx.reshape(-1)
    onehot = (e_flat[:, None] == jnp.arange(N_EXPERTS)[None, :]).astype(jnp.int32)
    counts = jnp.sum(onehot, axis=0)
    padded = ((counts + MOE_TM - 1) // MOE_TM) * MOE_TM
    ends = jnp.cumsum(padded)
    starts = ends - padded
    ustarts = jnp.cumsum(counts) - counts
    order = jnp.argsort(e_flat, stable=True).astype(jnp.int32)
    rank = jnp.argsort(order).astype(jnp.int32)
    dest = rank + jnp.sum(onehot * (starts - ustarts)[None, :], axis=-1)
    n_tiles = R // MOE_TM
    tile_start = jnp.arange(n_tiles, dtype=jnp.int32) * MOE_TM
    tile_active = (tile_start < ends[-1]).astype(jnp.int32)
    te = jnp.minimum(jnp.searchsorted(ends, tile_start, side="right"), N_EXPERTS - 1).astype(jnp.int32)
    last_e = te[jnp.maximum(ends[-1] // MOE_TM - 1, 0)]
    tile_expert = jnp.where(tile_active > 0, te, last_e)
    in_tile = jnp.arange(MOE_TM, dtype=jnp.int32)[None, :]
    within = (tile_start - starts[tile_expert])[:, None] + in_tile
    valid = (within < counts[tile_expert][:, None]) & (tile_active[:, None] > 0)
    src = order[jnp.clip(ustarts[tile_expert][:, None] + within, 0, M - 1).reshape(R)]
    valid = valid.reshape(R)
    row_token = jnp.where(valid, src // TOP_K, jnp.arange(R, dtype=jnp.int32) % N)
    row_w = jnp.where(valid, top_w[src], 0.0)
    return dest.reshape(N, TOP_K), row_token, row_w[:, None], tile_expert, tile_active


def _rope_tables(rows, rot_dim):
    row = jnp.repeat(jnp.arange(rows, dtype=F32), GRID_W)
    col = jnp.tile(jnp.arange(GRID_W, dtype=F32), rows)
    half = rot_dim // 2
    inv_freq = ROPE_THETA ** (-jnp.arange(0, half, 2, dtype=F32) / half)
    ang_r = row[:, None] * inv_freq[None, :]
    ang_c = col[:, None] * inv_freq[None, :]
    ang = jnp.concatenate([ang_r, ang_r, ang_c, ang_c], axis=-1)
    return jnp.cos(ang), jnp.sin(ang)


def _stream_tables(cos, sin, lane0):
    T, R = cos.shape
    if lane0 == 0:
        cos_l, sin_l = jnp.tile(cos, (1, LANES // R)), jnp.tile(sin, (1, LANES // R))
    else:
        pad = ((0, 0), (lane0, LANES - lane0 - R))
        cos_l = jnp.pad(cos - 1.0, pad) + 1.0
        sin_l = jnp.pad(sin, pad)
    ctx = ((0, CTX_LEN), (0, 0))
    return jnp.pad(cos_l - 1.0, ctx) + 1.0, jnp.pad(sin_l, ctx)


def _mod_table(c, c_ctx, mod_w, mod_b):
    B = c.shape[0]
    hp = lax.Precision.HIGHEST
    m_l = (jnp.dot(jax.nn.silu(c), mod_w, precision=hp) + mod_b).reshape(B, 1, 6, D_MODEL)
    m_c = (jnp.dot(jax.nn.silu(c_ctx), mod_w, precision=hp) + mod_b).reshape(1, 1, 6, D_MODEL)
    return jnp.concatenate([jnp.broadcast_to(m_c, m_l.shape), m_l], axis=1)


def _pad_cols(w, n):
    return jnp.pad(w, ((0, 0), (0, n - w.shape[1])))


def _even_layer(xa, mod, norms, w_in, conv_w, a_log, dt_bias, gdn_norm, sink, w_out, ffn_gate, ffn_up, ffn_down,
                cos, sin, n_lat_tiles):
    B, S, D = xa.shape
    T = n_lat_tiles * ROW_TILE
    n_gate = 2 * GDN_HEADS
    c_qkv = 3 * GDN_HD
    c_z = c_qkv + GDN_HD
    c_ga = c_z + n_gate
    c_gb = c_ga + n_gate
    c_sq = c_gb + SWA_Q_HEADS * SWA_HEAD_DIM
    c_sk = c_sq + SWA_KV_HEADS * SWA_HEAD_DIM
    grp = SWA_Q_HEADS // SWA_KV_HEADS
    wq = w_in[:, c_gb:c_sq].reshape(D, SWA_Q_HEADS, SWA_HEAD_DIM) * SWA_HEAD_DIM ** -0.5
    kv_of_head = (jnp.arange(SWA_Q_HEADS) // grp)[None, :, None]
    wq = jnp.concatenate([jnp.where(kv_of_head == 0, wq, 0.0), jnp.where(kv_of_head == 1, wq, 0.0)], axis=-1)
    w_all = jnp.concatenate([w_in[:, :c_z], _pad_cols(w_in[:, c_z:c_gb], LANES),
                             wq.reshape(D, SWA_Q_HEADS * LANES), w_in[:, c_sq:]], axis=1).astype(BF16)
    qkv, z, gate, q_pad, k, v = _even_proj(xa, mod, norms[0:1], w_all, cos, sin, n_lat_tiles)

    head_ones = (jnp.arange(GDN_HD)[:, None] // GDN_DK == jnp.arange(GDN_HD)[None, :] // GDN_DK).astype(BF16)
    conv_w8 = jnp.pad(conv_w, ((0, SUBLANES - CONV_K), (0, 0)))
    gq, gk, gv = _gdn_prep(qkv, conv_w8, head_ones, n_lat_tiles)
    a_log_row = _pad_cols(a_log.reshape(1, n_gate), LANES)
    dt_bias_row = _pad_cols(dt_bias.reshape(1, n_gate), LANES)
    o_f, o_b = _gdn(gq, gk, gv, gate, a_log_row, dt_bias_row, T // GDN_CHUNK)

    sink_rows = jnp.repeat(sink, SWA_BLOCK)[:, None]
    swa = _swa(q_pad, k, v, sink_rows, T // SWA_BLOCK)

    w_s = w_out[GDN_HD:].reshape(SWA_Q_HEADS, SWA_HEAD_DIM, D)
    w_s = jnp.concatenate([jnp.where(kv_of_head.reshape(-1, 1, 1) == 0, w_s, 0.0),
                           jnp.where(kv_of_head.reshape(-1, 1, 1) == 1, w_s, 0.0)], axis=1)
    gdn_norm_row = jnp.tile(gdn_norm, GDN_HEADS)[None, :]
    xa = _mix_out0(o_f, o_b, z, swa, xa, mod, norms[1:2], gdn_norm_row, head_ones,
                   w_out[:GDN_HD].astype(BF16), w_s.reshape(SWA_Q_HEADS * LANES, D).astype(BF16), n_lat_tiles)
    return _ffn(xa, mod, norms[2:3], norms[3:4], ffn_gate.astype(BF16), ffn_up.astype(BF16),
                ffn_down.astype(BF16), n_lat_tiles)


def _odd_layer_last(xa, mod, norms, w_in, q_norm, kv_norm, w_uq, w_ukv, lam_p, lam_init, subln, w_out,
                    router, exp_gate, exp_up, exp_down, cos, sin, cos_r, sin_r, n_lat_tiles):
    B, S, D = xa.shape
    T = n_lat_tiles * ROW_TILE
    H = MLA_HEADS
    c0 = MLA_Q_RANK
    c1 = c0 + MLA_KV_RANK
    c2 = c1 + MLA_ROPE
    dw = DIFF_HEADS * 2 * DIFF_HEAD_DIM
    w_kr = jnp.pad(w_in[:, c1:c2], ((0, 0), (MLA_NOPE, LANES - MLA_NOPE - MLA_ROPE)))
    w_all = jnp.concatenate([w_in[:, :c1], w_kr, w_in[:, c2:c2 + dw] * DIFF_HEAD_DIM ** -0.5,
                             w_in[:, c2 + dw:]], axis=1).astype(BF16)
    cq, ckv, kr, dq, dk, dv = _odd_proj(xa, mod, norms[0:1], w_all, cos, sin, cos_r, sin_r, n_lat_tiles)

    qd = MLA_NOPE + MLA_ROPE
    wq = jnp.pad(w_uq.reshape(MLA_Q_RANK, H, qd) * qd ** -0.5, ((0, 0), (0, 0), (0, LANES - qd)))
    wkv = w_ukv.reshape(MLA_KV_RANK, H, MLA_NOPE + MLA_V)
    wk = jnp.pad(wkv[..., :MLA_NOPE], ((0, 0), (0, 0), (0, LANES - MLA_NOPE)))
    wv = jnp.pad(wkv[..., MLA_NOPE:], ((0, 0), (0, 0), (0, LANES - MLA_V)))
    q_cat, k_cat, v_ext = _mla_up(cq, ckv, kr, q_norm[None, :], kv_norm[None, :],
                                  wq.reshape(MLA_Q_RANK, H * LANES).astype(BF16),
                                  wk.reshape(MLA_KV_RANK, H * LANES).astype(BF16),
                                  wv.reshape(MLA_KV_RANK, H * LANES).astype(BF16), cos_r, sin_r)
    mla = _mla(q_cat, k_cat, v_ext, T)

    lam = (jnp.exp(jnp.sum(lam_p[0] * lam_p[1])) - jnp.exp(jnp.sum(lam_p[2] * lam_p[3])) + lam_init).reshape(1, 1)
    diff = _diff(dq, dk, dv, lam, subln[None, :], 1.0 - lam_init, T)

    w_mla = jnp.pad(w_out[:H * MLA_V].reshape(H, MLA_V, D), ((0, 0), (0, LANES - MLA_V), (0, 0)))
    x, f_in, logits = _mix_out1(mla, diff, xa, mod, norms[1:2], norms[2:3],
                                w_mla.reshape(H * LANES, D).astype(BF16), w_out[H * MLA_V:].astype(BF16),
                                _pad_cols(router, LANES), n_lat_tiles)

    dest, row_token, row_w, tile_expert, tile_active = _route(logits.reshape(B * T, LANES)[:, :N_EXPERTS])
    x_sorted = f_in.reshape(B * T, D)[row_token]
    y = _moe_experts(tile_expert, tile_active, x_sorted, row_w, exp_gate, exp_up, exp_down)
    y0 = y[dest[:, 0]].reshape(B, T, D)
    y1 = y[dest[:, 1]].reshape(B, T, D)
    return _norm_residual(y0, y1, x, mod, norms[3:4], n_lat_tiles)


def kernel(x, c, ctx, c_ctx, e_mod_w, e_mod_b, e_norms, e_w_in, e_conv_w, e_a_log, e_dt_bias, e_gdn_norm, e_sink, e_w_out, e_ffn_gate, e_ffn_up, e_ffn_down, o_mod_w, o_mod_b, o_norms, o_w_in, o_q_norm, o_kv_norm, o_w_uq, o_w_ukv, o_lambda, o_subln, o_w_out, o_router, o_exp_gate, o_exp_up, o_exp_down):
    B, T, D = x.shape
    n_lat_tiles = T // ROW_TILE
    rows = T // GRID_W
    cos64, sin64 = _stream_tables(*_rope_tables(rows, SWA_HEAD_DIM), 0)
    cos_r, sin_r = _stream_tables(*_rope_tables(rows, MLA_ROPE), MLA_NOPE)
    xa = jnp.concatenate([x, ctx], axis=1)
    mod_e = _mod_table(c, c_ctx, e_mod_w[0], e_mod_b[0])
    xa = _even_layer(xa, mod_e, e_norms[0], e_w_in[0], e_conv_w[0], e_a_log[0], e_dt_bias[0], e_gdn_norm[0],
                     e_sink[0], e_w_out[0], e_ffn_gate[0], e_ffn_up[0], e_ffn_down[0], cos64, sin64, n_lat_tiles)
    mod_o = _mod_table(c, c_ctx, o_mod_w[0], o_mod_b[0])
    lam_init = 0.8 - 0.6 * math.exp(-0.3 * 1)
    return _odd_layer_last(xa, mod_o, o_norms[0], o_w_in[0], o_q_norm[0], o_kv_norm[0], o_w_uq[0], o_w_ukv[0],
                           o_lambda[0], lam_init, o_subln[0], o_w_out[0], o_router[0], o_exp_gate[0],
                           o_exp_up[0], o_exp_down[0], cos64, sin64, cos_r, sin_r, n_lat_tiles)
```

```python
import functools
import math

import jax
import jax.numpy as jnp
from jax import lax
from jax.experimental import pallas as pl
from jax.experimental.pallas import tpu as pltpu

F32 = jnp.float32
BF16 = jnp.bfloat16

D_MODEL = 1024
CTX_LEN = 256
GRID_W = 64
NORM_EPS = 1e-6
ROPE_THETA = 10000.0

GDN_HEADS = 8
GDN_DK = 64
GDN_DV = 64
GDN_CHUNK = 64
CONV_K = 5
GDN_GROUP = 4
GDN_LANES = GDN_GROUP * GDN_DK
GDN_HD = GDN_HEADS * GDN_DK
GDN_NB = 4

SWA_Q_HEADS = 8
SWA_KV_HEADS = 2
SWA_HEAD_DIM = 64
SWA_WINDOW = 128
SWA_BLOCK = 128

MLA_HEADS = 8
MLA_Q_RANK = 384
MLA_KV_RANK = 256
MLA_NOPE = 64
MLA_ROPE = 32
MLA_V = 64

DIFF_HEADS = 4
DIFF_HEAD_DIM = 64

D_FF = 2816
N_EXPERTS = 8
TOP_K = 2
D_FF_EXPERT = 3584

LANES = 128
SUBLANES = 8
ROW_TILE = 256
SWA_HEADS_PER_SUB = 1
ATTN_TQ = 1024
ATTN_SUB = 128
VMEM_LIMIT = 56 * 1024 * 1024
NEG_BIG = -1e30


def _dot(a, b):
    return jnp.dot(a, b, preferred_element_type=F32)


def _dot_nt(a, b):
    return lax.dot_general(a, b, (((1,), (1,)), ((), ())), preferred_element_type=F32)


def _dot_tn(a, b):
    return lax.dot_general(a, b, (((0,), (0,)), ((), ())), preferred_element_type=F32)


def _split_bf16(x):
    hi = x.astype(BF16)
    lo = (x - hi.astype(F32)).astype(BF16)
    return hi, lo


def _dot_split(x, w):
    hi, lo = _split_bf16(x)
    return _dot(hi, w) + _dot(lo, w)


def _dot_rsplit(w, x):
    hi, lo = _split_bf16(x)
    return _dot(w, hi) + _dot(w, lo)


def _rms(x, w):
    return x * lax.rsqrt(jnp.mean(x * x, axis=-1, keepdims=True) + NORM_EPS) * w


def _silu(x):
    return x * jax.nn.sigmoid(x)


def _rope_slab(x, cos, sin, quarter):
    lane = lax.broadcasted_iota(jnp.int32, x.shape, 1)
    fwd = pltpu.roll(x, quarter, 1)
    back = pltpu.roll(x, LANES - quarter, 1)
    rot = jnp.where(lane % (2 * quarter) < quarter, -back, fwd)
    return x * cos + rot * sin


def _params(*sem):
    return pltpu.CompilerParams(dimension_semantics=sem, vmem_limit_bytes=VMEM_LIMIT)


def _row_spec(width, tile=ROW_TILE):
    return pl.BlockSpec((1, tile, width), lambda b, t: (b, t, 0))


def _mod_spec(n_lat_tiles):
    return pl.BlockSpec((1, 1, 6, D_MODEL), lambda b, t: (b, jnp.where(t < n_lat_tiles, 1, 0), 0, 0))


def _const_spec(shape):
    return pl.BlockSpec(shape, lambda b, t: (0,) * len(shape))


def _table_spec():
    return pl.BlockSpec((ROW_TILE, LANES), lambda b, t: (t, 0))


E_QKV = (0, 3 * GDN_HD)
E_Z = (E_QKV[1], E_QKV[1] + GDN_HD)
E_GATE = (E_Z[1], E_Z[1] + LANES)
E_Q = (E_GATE[1], E_GATE[1] + SWA_Q_HEADS * LANES)
E_K = (E_Q[1], E_Q[1] + LANES)
E_V = (E_K[1], E_K[1] + LANES)


def _even_proj_kernel(x_ref, mod_ref, nw_ref, w_ref, cos_ref, sin_ref,
                      qkv_ref, z_ref, gate_ref, q_ref, k_ref, v_ref):
    a = (_rms(x_ref[0], nw_ref[...]) * (1.0 + mod_ref[0, 0, 1:2, :]) + mod_ref[0, 0, 0:1, :]).astype(BF16)

    def proj(cols):
        return _dot(a, w_ref[:, cols[0]:cols[1]])

    qkv_ref[0] = proj(E_QKV)
    z_ref[0] = proj(E_Z).astype(BF16)
    gate_ref[0] = proj(E_GATE)
    cos = cos_ref[...]
    sin = sin_ref[...]
    quarter = SWA_HEAD_DIM // 4
    qp = proj(E_Q)
    for h in range(SWA_Q_HEADS):
        ls = slice(h * LANES, (h + 1) * LANES)
        q_ref[0, :, ls] = _rope_slab(qp[:, ls], cos, sin, quarter).astype(BF16)
    k_ref[0] = _rope_slab(proj(E_K), cos, sin, quarter).astype(BF16)
    v_ref[0] = proj(E_V).astype(BF16)


def _even_proj(xa, mod, norm_w, w, cos, sin, n_lat_tiles):
    B, S, D = xa.shape
    widths = (E_QKV[1] - E_QKV[0], GDN_HD, LANES, SWA_Q_HEADS * LANES, LANES, LANES)
    dtypes = (F32, BF16, F32, BF16, BF16, BF16)
    return pl.pallas_call(
        _even_proj_kernel,
        grid=(B, S // ROW_TILE),
        in_specs=[_row_spec(D), _mod_spec(n_lat_tiles), _const_spec((1, D)), _const_spec(w.shape),
                  _table_spec(), _table_spec()],
        out_specs=[_row_spec(n) for n in widths],
        out_shape=[jax.ShapeDtypeStruct((B, S, n), dt) for n, dt in zip(widths, dtypes)],
        compiler_params=_params("parallel", "parallel"),
        name="even_proj",
    )(xa, mod, norm_w, w, cos, sin)


def _gdn_prep_kernel(x_ref, prev_ref, next_ref, cw_ref, bd_ref, q_ref, k_ref, v_ref, *, n_lat_tiles):
    t = pl.program_id(1)
    has_prev = jnp.logical_and(t > 0, t < n_lat_tiles)
    has_next = t < n_lat_tiles - 1
    prev = jnp.where(has_prev, prev_ref[0], 0.0)
    nxt = jnp.where(has_next, next_ref[0], 0.0)
    xe = jnp.concatenate([prev, x_ref[0], nxt], axis=0)
    pad = CONV_K // 2
    y = jnp.zeros(x_ref.shape[1:], F32)
    for kk in range(CONV_K):
        lo = SUBLANES + kk - pad
        y = y + cw_ref[kk:kk + 1, :] * xe[lo:lo + ROW_TILE, :]
    u = _silu(y)

    def headnorm(a):
        return a * lax.rsqrt(_dot_split(a * a, bd_ref[...]) + NORM_EPS)

    q_ref[0] = headnorm(u[:, :GDN_HD]) * GDN_DK ** -0.5
    k_ref[0] = headnorm(u[:, GDN_HD:2 * GDN_HD])
    v_ref[0] = u[:, 2 * GDN_HD:]


def _gdn_prep(qkv, conv_w, head_ones, n_lat_tiles):
    B, S, W = qkv.shape
    per_tile = ROW_TILE // SUBLANES
    n8 = S // SUBLANES
    return pl.pallas_call(
        functools.partial(_gdn_prep_kernel, n_lat_tiles=n_lat_tiles),
        grid=(B, S // ROW_TILE),
        in_specs=[_row_spec(W),
                  pl.BlockSpec((1, SUBLANES, W), lambda b, t: (b, jnp.maximum(t * per_tile - 1, 0), 0)),
                  pl.BlockSpec((1, SUBLANES, W), lambda b, t: (b, jnp.minimum((t + 1) * per_tile, n8 - 1), 0)),
                  _const_spec(conv_w.shape), _const_spec(head_ones.shape)],
        out_specs=[_row_spec(GDN_HD)] * 3,
        out_shape=[jax.ShapeDtypeStruct((B, S, GDN_HD), F32)] * 3,
        compiler_params=_params("parallel", "parallel"),
        name="gdn_prep",
    )(qkv, qkv, qkv, conv_w, head_ones)


def _block_diag(x, head_of_lane):
    zero = jnp.zeros_like(x)
    return jnp.concatenate([jnp.where(head_of_lane == h, x, zero) for h in range(GDN_GROUP)], axis=0)


def _expand_heads(x4, head_of_lane):
    c = x4.shape[0]
    out = jnp.broadcast_to(x4[:, GDN_GROUP - 1:GDN_GROUP], (c, GDN_LANES))
    for h in range(GDN_GROUP - 2, -1, -1):
        out = jnp.where(head_of_lane == h, jnp.broadcast_to(x4[:, h:h + 1], (c, GDN_LANES)), out)
    return out


def _gdn_groups(items):
    C = GDN_CHUNK
    row = lax.broadcasted_iota(jnp.int32, (C, GDN_LANES), 0)
    lane = lax.broadcasted_iota(jnp.int32, (C, GDN_LANES), 1)
    hol = lane // GDN_DK
    col = lane % GDN_DK
    eye = jnp.where(row == col, 1.0, 0.0)
    rr = lax.broadcasted_iota(jnp.int32, (2 * C, 4 * C), 0)
    cc = lax.broadcasted_iota(jnp.int32, (2 * C, 4 * C), 1)
    cum_lhs = {}
    for rev in (False, True):
        before = (rr - cc % C) * (-1 if rev else 1) >= 0
        cum_lhs[rev] = jnp.where(((rr < C) & (cc < 2 * C) & before) | ((rr >= C) & (cc >= 2 * C)),
                                 1.0, 0.0).astype(BF16)

    def bd(x):
        return _block_diag(x, hol)

    st = []
    for (q4, k4, v4, g4, b4, s4, reverse) in items:
        sgn = -1 if reverse else 1
        ahead = (row - col) * sgn
        incl = ahead >= 0
        strict = ahead > 0
        gE = _expand_heads(g4, hol)
        bE = _expand_heads(b4, hol)
        g_hi, g_lo = _split_bf16(gE)
        m_hi, m_lo = _split_bf16(jnp.where((col - row) * sgn >= 0, gE, 0.0))
        both = _dot(cum_lhs[reverse], jnp.concatenate([g_hi, g_lo, m_hi, m_lo], axis=0))
        gc = both[:C]
        gc_row = both[C:]
        decay = jnp.where(incl, jnp.exp(gc - gc_row), 0.0)
        g_last = gc[0:1, :] if reverse else gc[C - 1:C, :]
        eg = jnp.exp(gc)
        kb = k4 * bE
        st.append(dict(q=q4, k=k4, s=s4, incl=incl, strict=strict, decay=decay, g_last=g_last,
                       kb=kb, vb=v4 * bE, kbg=kb * eg, qh=q4 * eg, kt=k4 * jnp.exp(g_last - gc)))

    for d in st:
        gram = _dot_nt(jnp.concatenate([d["kb"], d["q"]], axis=0).astype(BF16), bd(d["k"].astype(BF16)))
        L = jnp.where(d["strict"], gram[:C] * d["decay"], 0.0)
        d["A"] = gram[C:] * d["decay"]
        d["T"] = eye - L
        d["P"] = L.astype(BF16)
    n_factors = int(math.log2(C)) - 1
    for d in st:
        d["P"] = _dot(d["P"], bd(d["P"])).astype(BF16)
    for it in range(n_factors):
        last = it == n_factors - 1
        for d in st:
            lhs = d["T"].astype(BF16) if last else jnp.concatenate([d["T"].astype(BF16), d["P"]], axis=0)
            prod = _dot(lhs, bd(d["P"]))
            d["T"] = d["T"] + prod[:C]
            if not last:
                d["P"] = prod[C:].astype(BF16)
    for d in st:
        Tb = d["T"].astype(BF16)
        d["u"] = _dot(Tb, bd(d["vb"].astype(BF16)))
        d["w"] = _dot(Tb, bd(d["kbg"].astype(BF16)))
    for d in st:
        d["ws_qs"] = _dot(jnp.concatenate([d["w"], d["qh"]], axis=0).astype(BF16), bd(d["s"].astype(BF16)))
    out = []
    for d in st:
        v_new = d["u"] - d["ws_qs"][:C]
        vb16 = v_new.astype(BF16)
        o = d["ws_qs"][C:] + _dot(d["A"].astype(BF16), bd(vb16))
        full = _dot_tn(d["kt"].astype(BF16), vb16)
        upd = jnp.zeros((GDN_DK, GDN_LANES), F32)
        for h in range(GDN_GROUP):
            upd = upd + jnp.where(hol == h, full[h * GDN_DK:(h + 1) * GDN_DK, :], 0.0)
        out.append((o, d["s"] * jnp.exp(d["g_last"]) + upd))
    return out


def _gdn_kernel(qf_ref, kf_ref, vf_ref, gf_ref, qb_ref, kb_ref, vb_ref, gb_ref, al_ref, dtb_ref,
                of_ref, ob_ref, s_ref):
    @pl.when(pl.program_id(1) == 0)
    def _():
        s_ref[...] = jnp.zeros_like(s_ref)

    n_grp = GDN_HEADS // GDN_GROUP
    n_gate = 2 * GDN_HEADS
    items = []
    slots = []
    for nb in range(GDN_NB):
        for d, (q_ref, k_ref, v_ref, g_ref, o_ref) in enumerate(((qf_ref, kf_ref, vf_ref, gf_ref, of_ref),
                                                                 (qb_ref, kb_ref, vb_ref, gb_ref, ob_ref))):
            raw = g_ref[nb]
            xs = raw + dtb_ref[...]
            softplus = jnp.maximum(xs, 0.0) + jnp.log(1.0 + jnp.exp(-jnp.abs(xs)))
            g = -jnp.exp(al_ref[...]) * softplus
            beta = jax.nn.sigmoid(raw)
            for grp in range(n_grp):
                ls = slice(grp * GDN_LANES, (grp + 1) * GDN_LANES)
                c0 = d * GDN_HEADS + grp * GDN_GROUP
                slot = (nb * 2 + d) * n_grp + grp
                items.append((q_ref[nb, :, ls], k_ref[nb, :, ls], v_ref[nb, :, ls],
                              g[:, c0:c0 + GDN_GROUP], beta[:, n_gate + c0:n_gate + c0 + GDN_GROUP],
                              s_ref[slot], d == 1))
                slots.append((o_ref, nb, ls, slot))
    for (o_ref, nb, ls, slot), (o, s_new) in zip(slots, _gdn_groups(items)):
        o_ref[nb, :, ls] = o
        s_ref[slot] = s_new


def _gdn(q, k, v, gate, a_log_row, dt_bias_row, n_lat_chunks):
    B, S, HD = q.shape
    n_chunks = S // GDN_CHUNK
    n_ctx = n_chunks - n_lat_chunks

    def fwd_c(s):
        return jnp.where(s < n_ctx, n_lat_chunks + s, s - n_ctx)

    def bwd_c(s):
        return n_chunks - 1 - s

    def spec(cmap, width):
        return pl.BlockSpec((GDN_NB, GDN_CHUNK, width), lambda b, s: (b, cmap(s), 0))

    return pl.pallas_call(
        _gdn_kernel,
        grid=(B // GDN_NB, n_chunks),
        in_specs=[spec(fwd_c, HD), spec(fwd_c, HD), spec(fwd_c, HD), spec(fwd_c, LANES),
                  spec(bwd_c, HD), spec(bwd_c, HD), spec(bwd_c, HD), spec(bwd_c, LANES),
                  _const_spec((1, LANES)), _const_spec((1, LANES))],
        out_specs=[spec(fwd_c, HD), spec(bwd_c, HD)],
        out_shape=[jax.ShapeDtypeStruct((B, S, HD), F32)] * 2,
        scratch_shapes=[pltpu.VMEM((GDN_NB * 2 * GDN_HEADS // GDN_GROUP, GDN_DK, GDN_LANES), F32)],
        compiler_params=_params("parallel", "arbitrary"),
        name="gdn_scan",
    )(q, k, v, gate, q, k, v, gate, a_log_row, dt_bias_row)


def _swa_kernel(q_ref, k_ref, v_ref, sink_ref, o_ref, *, n_lat_blocks):
    W = SWA_BLOCK
    H = SWA_Q_HEADS
    n = pl.program_id(1)
    is_lat = n < n_lat_blocks
    nl = jnp.minimum(n, n_lat_blocks - 1)
    prev = jnp.maximum(nl - 1, 0)
    nxt = jnp.minimum(nl + 1, n_lat_blocks - 1)
    ctx0 = n_lat_blocks * W

    def keys(ref):
        def blk(i):
            return ref[0, pl.ds(pl.multiple_of(i * W, W), W), :]
        return jnp.concatenate([ref[0, ctx0:ctx0 + CTX_LEN, :], blk(prev), blk(nl), blk(nxt)], axis=0)

    k_all = keys(k_ref)
    v_all = keys(v_ref)
    nk = CTX_LEN + 3 * W
    v_ext = jnp.concatenate([v_all, jnp.ones((nk, LANES), BF16)], axis=1)
    hp = SWA_HEADS_PER_SUB
    ii = lax.broadcasted_iota(jnp.int32, (hp * W, nk), 0) % W
    jj = lax.broadcasted_iota(jnp.int32, (hp * W, nk), 1) - CTX_LEN
    in_window = (((jj >= 0) & (jj < W) & (jj >= ii) & (nl > 0))
                 | ((jj >= W) & (jj < 2 * W))
                 | ((jj >= 2 * W) & (jj - 2 * W <= ii) & (nl < n_lat_blocks - 1)))
    valid = (jj < 0) | (in_window & is_lat)

    def scores(i):
        q = jnp.concatenate([q_ref[0, :, h * LANES:(h + 1) * LANES] for h in range(i * hp, (i + 1) * hp)], axis=0)
        return jnp.where(valid, _dot_nt(q, k_all), NEG_BIG)

    def finish(i, s):
        sk = sink_ref[i * hp * W:(i + 1) * hp * W, :]
        m = jnp.maximum(jnp.max(s, axis=-1, keepdims=True), sk)
        acc = _dot(jnp.exp(s - m).astype(BF16), v_ext)
        o = acc[:, :LANES] / (acc[:, LANES:LANES + 1] + jnp.exp(sk - m))
        for j in range(hp):
            h = i * hp + j
            o_ref[0, :, h * LANES:(h + 1) * LANES] = o[j * W:(j + 1) * W].astype(o_ref.dtype)

    _one_ahead(H // hp, scores, finish)


def _swa(q_pad, k, v, sink_rows, n_lat_blocks):
    B, S, _ = k.shape
    W = SWA_BLOCK
    QW = SWA_Q_HEADS * LANES
    return pl.pallas_call(
        functools.partial(_swa_kernel, n_lat_blocks=n_lat_blocks),
        grid=(B, S // W),
        in_specs=[pl.BlockSpec((1, W, QW), lambda b, n: (b, n, 0)),
                  pl.BlockSpec((1, S, LANES), lambda b, n: (b, 0, 0)),
                  pl.BlockSpec((1, S, LANES), lambda b, n: (b, 0, 0)),
                  _const_spec(sink_rows.shape)],
        out_specs=pl.BlockSpec((1, W, QW), lambda b, n: (b, n, 0)),
        out_shape=jax.ShapeDtypeStruct((B, S, QW), BF16),
        compiler_params=_params("parallel", "parallel"),
        name="swa",
    )(q_pad, k, v, sink_rows)


def _mix_out0_kernel(of_ref, ob_ref, z_ref, swa_ref, x_ref, mod_ref, nw_ref, gn_ref, bd_ref, wg_ref, ws_ref,
                     o_ref):
    o = of_ref[0] + ob_ref[0]
    ms = _dot_split(o * o, bd_ref[...]) * (1.0 / GDN_DV)
    gdn = o * lax.rsqrt(ms + NORM_EPS) * gn_ref[...] * _silu(z_ref[0].astype(F32))
    y = _dot(gdn.astype(BF16), wg_ref[...]) + _dot(swa_ref[0], ws_ref[...])
    o_ref[0] = x_ref[0] + mod_ref[0, 0, 2:3, :] * _rms(y, nw_ref[...])


def _mix_out0(o_f, o_b, z, swa, xa, mod, norm_w, gdn_norm_row, head_ones, w_gdn, w_swa, n_lat_tiles):
    B, S, D = xa.shape
    return pl.pallas_call(
        _mix_out0_kernel,
        grid=(B, S // ROW_TILE),
        in_specs=[_row_spec(GDN_HD), _row_spec(GDN_HD), _row_spec(GDN_HD), _row_spec(swa.shape[-1]), _row_spec(D),
                  _mod_spec(n_lat_tiles), _const_spec((1, D)), _const_spec(gdn_norm_row.shape),
                  _const_spec(head_ones.shape), _const_spec(w_gdn.shape), _const_spec(w_swa.shape)],
        out_specs=_row_spec(D),
        out_shape=jax.ShapeDtypeStruct((B, S, D), F32),
        compiler_params=_params("parallel", "parallel"),
        name="mix_out0",
    )(o_f, o_b, z, swa, xa, mod, norm_w, gdn_norm_row, head_ones, w_gdn, w_swa)


def _ffn_kernel(x_ref, mod_ref, n_in_ref, n_out_ref, wg_ref, wu_ref, wd_ref, o_ref):
    x = x_ref[0]
    f = (_rms(x, n_in_ref[...]) * (1.0 + mod_ref[0, 0, 4:5, :]) + mod_ref[0, 0, 3:4, :]).astype(BF16)
    h = _silu(_dot(f, wg_ref[...])) * _dot(f, wu_ref[...])
    y = _dot(h.astype(BF16), wd_ref[...])
    o_ref[0] = x + mod_ref[0, 0, 5:6, :] * _rms(y, n_out_ref[...])


def _ffn(xa, mod, n_in, n_out, wg, wu, wd, n_lat_tiles):
    B, S, D = xa.shape
    return pl.pallas_call(
        _ffn_kernel,
        grid=(B, S // ROW_TILE),
        in_specs=[_row_spec(D), _mod_spec(n_lat_tiles), _const_spec((1, D)), _const_spec((1, D)),
                  _const_spec(wg.shape), _const_spec(wu.shape), _const_spec(wd.shape)],
        out_specs=_row_spec(D),
        out_shape=jax.ShapeDtypeStruct((B, S, D), F32),
        compiler_params=_params("parallel", "parallel"),
        name="dense_ffn",
    )(xa, mod, n_in, n_out, wg, wu, wd)


O_CQ = (0, MLA_Q_RANK)
O_CKV = (O_CQ[1], O_CQ[1] + MLA_KV_RANK)
O_KR = (O_CKV[1], O_CKV[1] + LANES)
O_DQ = (O_KR[1], O_KR[1] + DIFF_HEADS * LANES)
O_DK = (O_DQ[1], O_DQ[1] + DIFF_HEADS * LANES)
O_DV = (O_DK[1], O_DK[1] + DIFF_HEADS * LANES)


def _odd_proj_kernel(x_ref, mod_ref, nw_ref, w_ref, cos_ref, sin_ref, cosr_ref, sinr_ref,
                     cq_ref, ckv_ref, kr_ref, dq_ref, dk_ref, dv_ref):
    a = (_rms(x_ref[0], nw_ref[...]) * (1.0 + mod_ref[0, 0, 1:2, :]) + mod_ref[0, 0, 0:1, :]).astype(BF16)

    def proj(cols):
        return _dot(a, w_ref[:, cols[0]:cols[1]])

    cq_ref[0] = proj(O_CQ)
    ckv_ref[0] = proj(O_CKV)
    kr_ref[0] = _rope_slab(proj(O_KR), cosr_ref[...], sinr_ref[...], MLA_ROPE // 4)
    cos = cos_ref[...]
    sin = sin_ref[...]
    quarter = DIFF_HEAD_DIM // 4
    dq = proj(O_DQ)
    dk = proj(O_DK)
    dv = proj(O_DV)
    ones = jnp.ones((ROW_TILE, LANES), BF16)
    for h in range(DIFF_HEADS):
        ls = slice(h * LANES, (h + 1) * LANES)
        dq_ref[0, :, ls] = _rope_slab(dq[:, ls], cos, sin, quarter).astype(BF16)
        dk_ref[0, :, ls] = _rope_slab(dk[:, ls], cos, sin, quarter).astype(BF16)
        dv_ref[0, :, 2 * h * LANES:(2 * h + 1) * LANES] = dv[:, ls].astype(BF16)
        dv_ref[0, :, (2 * h + 1) * LANES:(2 * h + 2) * LANES] = ones


def _odd_proj(xa, mod, norm_w, w, cos, sin, cos_r, sin_r, n_lat_tiles):
    B, S, D = xa.shape
    widths = (MLA_Q_RANK, MLA_KV_RANK, LANES, DIFF_HEADS * LANES, DIFF_HEADS * LANES, 2 * DIFF_HEADS * LANES)
    dtypes = (F32, F32, F32, BF16, BF16, BF16)
    return pl.pallas_call(
        _odd_proj_kernel,
        grid=(B, S // ROW_TILE),
        in_specs=[_row_spec(D), _mod_spec(n_lat_tiles), _const_spec((1, D)), _const_spec(w.shape),
                  _table_spec(), _table_spec(), _table_spec(), _table_spec()],
        out_specs=[_row_spec(n) for n in widths],
        out_shape=[jax.ShapeDtypeStruct((B, S, n), dt) for n, dt in zip(widths, dtypes)],
        compiler_params=_params("parallel", "parallel"),
        name="odd_proj",
    )(xa, mod, norm_w, w, cos, sin, cos_r, sin_r)


def _mla_up_kernel(cq_ref, ckv_ref, kr_ref, qn_ref, kvn_ref, wq_ref, wk_ref, wv_ref, cos_ref, sin_ref,
                   q_ref, k_ref, v_ref):
    cq = _rms(cq_ref[0], qn_ref[...]).astype(BF16)
    ckv = _rms(ckv_ref[0], kvn_ref[...]).astype(BF16)
    qp = _dot(cq, wq_ref[...])
    kp = _dot(ckv, wk_ref[...])
    vp = _dot(ckv, wv_ref[...])
    kr = kr_ref[0]
    cos = cos_ref[...]
    sin = sin_ref[...]
    lane = lax.broadcasted_iota(jnp.int32, (ROW_TILE, LANES), 1)
    ones_hi = jnp.where(lane >= MLA_V, 1.0, 0.0)
    for h in range(MLA_HEADS):
        ls = slice(h * LANES, (h + 1) * LANES)
        q_ref[0, :, ls] = _rope_slab(qp[:, ls], cos, sin, MLA_ROPE // 4).astype(BF16)
        k_ref[0, :, ls] = (kp[:, ls] + kr).astype(BF16)
        v_ref[0, :, ls] = (vp[:, ls] + ones_hi).astype(BF16)


def _mla_up(cq, ckv, kr, q_norm, kv_norm, wq, wk, wv, cos_r, sin_r):
    B, S, _ = cq.shape
    HW = MLA_HEADS * LANES
    return pl.pallas_call(
        _mla_up_kernel,
        grid=(B, S // ROW_TILE),
        in_specs=[_row_spec(MLA_Q_RANK), _row_spec(MLA_KV_RANK), _row_spec(LANES),
                  _const_spec(q_norm.shape), _const_spec(kv_norm.shape),
                  _const_spec(wq.shape), _const_spec(wk.shape), _const_spec(wv.shape),
                  _table_spec(), _table_spec()],
        out_specs=[_row_spec(HW)] * 3,
        out_shape=[jax.ShapeDtypeStruct((B, S, HW), BF16)] * 3,
        compiler_params=_params("parallel", "parallel"),
        name="mla_up",
    )(cq, ckv, kr, q_norm, kv_norm, wq, wk, wv, cos_r, sin_r)


def _one_ahead(n_sub, scores, finish):
    s_prev = scores(0)
    for i in range(1, n_sub):
        s_next = scores(i)
        finish(i - 1, s_prev)
        s_prev = s_next
    finish(n_sub - 1, s_prev)


def _mla_kernel(q_ref, k_ref, v_ref, o_ref):
    k = k_ref[0]
    v = v_ref[0]

    def scores(i):
        return _dot_nt(q_ref[0, i * ATTN_SUB:(i + 1) * ATTN_SUB, :], k)

    def finish(i, s):
        m = jnp.max(s, axis=-1, keepdims=True)
        acc = _dot(jnp.exp(s - m).astype(BF16), v)
        o_ref[0, i * ATTN_SUB:(i + 1) * ATTN_SUB, :] = (acc / acc[:, MLA_V:MLA_V + 1]).astype(o_ref.dtype)

    _one_ahead(q_ref.shape[1] // ATTN_SUB, scores, finish)


def _mla(q, k, v_ext, T):
    B, S, _ = k.shape
    return pl.pallas_call(
        _mla_kernel,
        grid=(B, MLA_HEADS, T // ATTN_TQ),
        in_specs=[pl.BlockSpec((1, ATTN_TQ, LANES), lambda b, h, t: (b, t, h)),
                  pl.BlockSpec((1, S, LANES), lambda b, h, t: (b, 0, h)),
                  pl.BlockSpec((1, S, LANES), lambda b, h, t: (b, 0, h))],
        out_specs=pl.BlockSpec((1, ATTN_TQ, LANES), lambda b, h, t: (b, t, h)),
        out_shape=jax.ShapeDtypeStruct((B, T, MLA_HEADS * LANES), BF16),
        compiler_params=_params("parallel", "parallel", "parallel"),
        name="mla_attn",
    )(q, k, v_ext)


def _diff_kernel(q_ref, k_ref, v_ref, lam_ref, sub_ref, o_ref, *, post_scale):
    k = k_ref[0]
    v = v_ref[0]
    sub = ATTN_SUB
    vw = 2 * DIFF_HEAD_DIM
    lo = lax.broadcasted_iota(jnp.int32, (sub, LANES), 1) < DIFF_HEAD_DIM
    zero = jnp.zeros((sub, LANES), BF16)

    def scores(i):
        q = q_ref[0, i * sub:(i + 1) * sub, :]
        return _dot_nt(jnp.concatenate([jnp.where(lo, q, zero), jnp.where(lo, zero, q)], axis=0), k)

    def finish(i, s):
        m = jnp.max(s, axis=-1, keepdims=True)
        acc = _dot(jnp.exp(s - m).astype(BF16), v)
        att = acc[:, :vw] / acc[:, vw:vw + 1]
        a = att[:sub] - lam_ref[...] * att[sub:]
        o_ref[0, i * sub:(i + 1) * sub, :] = (_rms(a, sub_ref[...]) * post_scale).astype(o_ref.dtype)

    _one_ahead(q_ref.shape[1] // sub, scores, finish)


def _diff(q, k, v_ext, lam, subln, post_scale, T):
    B, S, _ = k.shape
    tq = ATTN_TQ // 2
    return pl.pallas_call(
        functools.partial(_diff_kernel, post_scale=post_scale),
        grid=(B, DIFF_HEADS, T // tq),
        in_specs=[pl.BlockSpec((1, tq, LANES), lambda b, h, t: (b, t, h)),
                  pl.BlockSpec((1, S, LANES), lambda b, h, t: (b, 0, h)),
                  pl.BlockSpec((1, S, 2 * LANES), lambda b, h, t: (b, 0, h)),
                  pl.BlockSpec((1, 1), lambda b, h, t: (0, 0)),
                  pl.BlockSpec((1, LANES), lambda b, h, t: (0, 0))],
        out_specs=pl.BlockSpec((1, tq, LANES), lambda b, h, t: (b, t, h)),
        out_shape=jax.ShapeDtypeStruct((B, T, DIFF_HEADS * LANES), BF16),
        compiler_params=_params("parallel", "parallel", "parallel"),
        name="diff_attn",
    )(q, k, v_ext, lam, subln)


def _mix_out1_kernel(mla_ref, diff_ref, x_ref, mod_ref, n1_ref, n2_ref, w1_ref, w2_ref, wrh_ref, wrl_ref,
                     xo_ref, f_ref, lg_ref):
    y = _dot(mla_ref[0], w1_ref[...]) + _dot(diff_ref[0], w2_ref[...])
    x = x_ref[0] + mod_ref[0, 0, 2:3, :] * _rms(y, n1_ref[...])
    xo_ref[0] = x
    f = _rms(x, n2_ref[...]) * (1.0 + mod_ref[0, 0, 4:5, :]) + mod_ref[0, 0, 3:4, :]
    f_ref[0] = f
    f_hi, f_lo = _split_bf16(f)
    lg_ref[0] = _dot(f_hi, wrh_ref[...]) + (_dot(f_hi, wrl_ref[...]) + _dot(f_lo, wrh_ref[...]))


def _mix_out1(mla, diff, xa, mod, n1, n2, w1, w2, wr, n_lat_tiles):
    B, T, _ = mla.shape
    D = xa.shape[-1]
    wr_hi, wr_lo = _split_bf16(wr)
    return pl.pallas_call(
        _mix_out1_kernel,
        grid=(B, T // ROW_TILE),
        in_specs=[_row_spec(mla.shape[-1]), _row_spec(diff.shape[-1]), _row_spec(D), _mod_spec(n_lat_tiles),
                  _const_spec((1, D)), _const_spec((1, D)), _const_spec(w1.shape), _const_spec(w2.shape),
                  _const_spec(wr.shape), _const_spec(wr.shape)],
        out_specs=[_row_spec(D), _row_spec(D), _row_spec(LANES)],
        out_shape=[jax.ShapeDtypeStruct((B, T, D), F32), jax.ShapeDtypeStruct((B, T, D), F32),
                   jax.ShapeDtypeStruct((B, T, LANES), F32)],
        compiler_params=_params("parallel", "parallel"),
        name="mix_out1",
    )(mla, diff, xa, mod, n1, n2, w1, w2, wr_hi, wr_lo)


MOE_TM = 1024
MOE_TF = 512
MOE_SUB = 256


def _moe_kernel(te_ref, tr_ref, x_ref, rw_ref, wg_ref, wu_ref, wd_ref, o_ref, acc_ref):
    i = pl.program_id(0)
    f = pl.program_id(1)
    n_rows = tr_ref[i]
    n_sub = MOE_TM // MOE_SUB

    @pl.when(f == 0)
    def _():
        acc_ref[...] = jnp.zeros_like(acc_ref)

    def sub_tile_fns():
        wg = wg_ref[0].astype(BF16)
        wu = wu_ref[0].astype(BF16)
        wd = wd_ref[0].astype(BF16)

        def gate_up(j):
            x = x_ref[j * MOE_SUB:(j + 1) * MOE_SUB, :].astype(BF16)
            return _dot(x, wg), _dot(x, wu)

        def down(j, gu):
            h = (_silu(gu[0]) * gu[1]).astype(BF16)
            acc_ref[j * MOE_SUB:(j + 1) * MOE_SUB, :] += _dot(h, wd)

        return gate_up, down

    all_subs = n_rows > (n_sub - 1) * MOE_SUB

    @pl.when(all_subs)
    def _():
        _one_ahead(n_sub, *sub_tile_fns())

    for j in range(n_sub - 1):
        @pl.when(jnp.logical_and(jnp.logical_not(all_subs), n_rows > j * MOE_SUB))
        def _(j=j):
            gate_up, down = sub_tile_fns()
            down(j, gate_up(j))

    @pl.when(f == pl.num_programs(1) - 1)
    def _():
        o_ref[...] = acc_ref[...] * rw_ref[...]


def _moe_experts(tile_expert, tile_rows, x_sorted, row_w, wg, wu, wd):
    R, D = x_sorted.shape
    E, _, F = wg.shape
    nf = F // MOE_TF

    def f_eff(i, f, ta):
        return jnp.where(ta[i] > 0, f, nf - 1)

    grid_spec = pltpu.PrefetchScalarGridSpec(
        num_scalar_prefetch=2,
        grid=(R // MOE_TM, nf),
        in_specs=[pl.BlockSpec((MOE_TM, D), lambda i, f, te, ta: (i, 0)),
                  pl.BlockSpec((MOE_TM, 1), lambda i, f, te, ta: (i, 0)),
                  pl.BlockSpec((1, D, MOE_TF), lambda i, f, te, ta: (te[i], 0, f_eff(i, f, ta))),
                  pl.BlockSpec((1, D, MOE_TF), lambda i, f, te, ta: (te[i], 0, f_eff(i, f, ta))),
                  pl.BlockSpec((1, MOE_TF, D), lambda i, f, te, ta: (te[i], f_eff(i, f, ta), 0))],
        out_specs=pl.BlockSpec((MOE_TM, D), lambda i, f, te, ta: (i, 0)),
        scratch_shapes=[pltpu.VMEM((MOE_TM, D), F32)],
    )
    return pl.pallas_call(
        _moe_kernel,
        grid_spec=grid_spec,
        out_shape=jax.ShapeDtypeStruct((R, D), F32),
        compiler_params=_params("arbitrary", "arbitrary"),
        name="moe_experts",
    )(tile_expert, tile_rows, x_sorted, row_w, wg, wu, wd)


def _norm_residual_kernel(a_ref, b_ref, x_ref, mod_ref, nw_ref, o_ref):
    y = a_ref[0] + b_ref[0]
    o_ref[0] = x_ref[0] + mod_ref[0, 0, 5:6, :] * _rms(y, nw_ref[...])


def _norm_residual(a, b, x, mod, norm_w, n_lat_tiles):
    B, T, D = x.shape
    return pl.pallas_call(
        _norm_residual_kernel,
        grid=(B, T // ROW_TILE),
        in_specs=[_row_spec(D), _row_spec(D), _row_spec(D), _mod_spec(n_lat_tiles), _const_spec((1, D))],
        out_specs=_row_spec(D),
        out_shape=jax.ShapeDtypeStruct((B, T, D), F32),
        compiler_params=_params("parallel", "parallel"),
        name="norm_residual",
    )(a, b, x, mod, norm_w)


def _route(logits):
    N = logits.shape[0]
    M = N * TOP_K
    R = M + N_EXPERTS * MOE_TM
    top_val, top_idx = lax.top_k(logits, TOP_K)
    top_w = jax.nn.softmax(top_val, axis=-1).reshape(-1)
    e_flat = top_idx.reshape(-1)
    onehot = (e_flat[:, None] == jnp.arange(N_EXPERTS)[None, :]).astype(jnp.int32)
    counts = jnp.sum(onehot, axis=0)
    padded = ((counts + MOE_TM - 1) // MOE_TM) * MOE_TM
    ends = jnp.cumsum(padded)
    starts = ends - padded
    ustarts = jnp.cumsum(counts) - counts
    order = jnp.argsort(e_flat, stable=True).astype(jnp.int32)
    rank = jnp.argsort(order).astype(jnp.int32)
    dest = rank + jnp.sum(onehot * (starts - ustarts)[None, :], axis=-1)
    n_tiles = R // MOE_TM
    tile_start = jnp.arange(n_tiles, dtype=jnp.int32) * MOE_TM
    tile_active = (tile_start < ends[-1]).astype(jnp.int32)
    te = jnp.minimum(jnp.searchsorted(ends, tile_start, side="right"), N_EXPERTS - 1).astype(jnp.int32)
    last_e = te[jnp.maximum(ends[-1] // MOE_TM - 1, 0)]
    tile_expert = jnp.where(tile_active > 0, te, last_e)
    in_tile = jnp.arange(MOE_TM, dtype=jnp.int32)[None, :]
    within = (tile_start - starts[tile_expert])[:, None] + in_tile
    valid = (within < counts[tile_expert][:, None]) & (tile_active[:, None] > 0)
    src = order[jnp.clip(ustarts[tile_expert][:, None] + within, 0, M - 1).reshape(R)]
    valid = valid.reshape(R)
    tile_rows = jnp.clip(counts[tile_expert] - (tile_start - starts[tile_expert]), 0, MOE_TM) * tile_active
    row_token = jnp.where(valid, src // TOP_K, jnp.arange(R, dtype=jnp.int32) % N)
    row_w = jnp.where(valid, top_w[src], 0.0)
    return dest.reshape(N, TOP_K), row_token, row_w[:, None], tile_expert, tile_rows.astype(jnp.int32)


def _rope_tables(rows, rot_dim):
    row = jnp.repeat(jnp.arange(rows, dtype=F32), GRID_W)
    col = jnp.tile(jnp.arange(GRID_W, dtype=F32), rows)
    half = rot_dim // 2
    inv_freq = ROPE_THETA ** (-jnp.arange(0, half, 2, dtype=F32) / half)
    ang_r = row[:, None] * inv_freq[None, :]
    ang_c = col[:, None] * inv_freq[None, :]
    ang = jnp.concatenate([ang_r, ang_r, ang_c, ang_c], axis=-1)
    return jnp.cos(ang), jnp.sin(ang)


def _stream_tables(cos, sin, lane0):
    T, R = cos.shape
    if lane0 == 0:
        cos_l, sin_l = jnp.tile(cos, (1, LANES // R)), jnp.tile(sin, (1, LANES // R))
    else:
        pad = ((0, 0), (lane0, LANES - lane0 - R))
        cos_l = jnp.pad(cos - 1.0, pad) + 1.0
        sin_l = jnp.pad(sin, pad)
    ctx = ((0, CTX_LEN), (0, 0))
    return jnp.pad(cos_l - 1.0, ctx) + 1.0, jnp.pad(sin_l, ctx)


def _mod_table(c, c_ctx, mod_w, mod_b):
    B = c.shape[0]
    hp = lax.Precision.HIGHEST
    m_l = (jnp.dot(jax.nn.silu(c), mod_w, precision=hp) + mod_b).reshape(B, 1, 6, D_MODEL)
    m_c = (jnp.dot(jax.nn.silu(c_ctx), mod_w, precision=hp) + mod_b).reshape(1, 1, 6, D_MODEL)
    return jnp.concatenate([jnp.broadcast_to(m_c, m_l.shape), m_l], axis=1)


def _pad_cols(w, n):
    return jnp.pad(w, ((0, 0), (0, n - w.shape[1])))


def _even_layer(xa, mod, norms, w_in, conv_w, a_log, dt_bias, gdn_norm, sink, w_out, ffn_gate, ffn_up, ffn_down,
                cos, sin, n_lat_tiles):
    B, S, D = xa.shape
    T = n_lat_tiles * ROW_TILE
    n_gate = 2 * GDN_HEADS
    c_qkv = 3 * GDN_HD
    c_z = c_qkv + GDN_HD
    c_ga = c_z + n_gate
    c_gb = c_ga + n_gate
    c_sq = c_gb + SWA_Q_HEADS * SWA_HEAD_DIM
    c_sk = c_sq + SWA_KV_HEADS * SWA_HEAD_DIM
    grp = SWA_Q_HEADS // SWA_KV_HEADS
    wq = w_in[:, c_gb:c_sq].reshape(D, SWA_Q_HEADS, SWA_HEAD_DIM) * SWA_HEAD_DIM ** -0.5
    kv_of_head = (jnp.arange(SWA_Q_HEADS) // grp)[None, :, None]
    wq = jnp.concatenate([jnp.where(kv_of_head == 0, wq, 0.0), jnp.where(kv_of_head == 1, wq, 0.0)], axis=-1)
    w_all = jnp.concatenate([w_in[:, :c_z], _pad_cols(w_in[:, c_z:c_gb], LANES),
                             wq.reshape(D, SWA_Q_HEADS * LANES), w_in[:, c_sq:]], axis=1).astype(BF16)
    qkv, z, gate, q_pad, k, v = _even_proj(xa, mod, norms[0:1], w_all, cos, sin, n_lat_tiles)

    head_ones = (jnp.arange(GDN_HD)[:, None] // GDN_DK == jnp.arange(GDN_HD)[None, :] // GDN_DK).astype(BF16)
    conv_w8 = jnp.pad(conv_w, ((0, SUBLANES - CONV_K), (0, 0)))
    gq, gk, gv = _gdn_prep(qkv, conv_w8, head_ones, n_lat_tiles)
    a_log_row = _pad_cols(a_log.reshape(1, n_gate), LANES)
    dt_bias_row = _pad_cols(dt_bias.reshape(1, n_gate), LANES)
    o_f, o_b = _gdn(gq, gk, gv, gate, a_log_row, dt_bias_row, T // GDN_CHUNK)

    sink_rows = jnp.repeat(sink, SWA_BLOCK)[:, None]
    swa = _swa(q_pad, k, v, sink_rows, T // SWA_BLOCK)

    w_s = w_out[GDN_HD:].reshape(SWA_Q_HEADS, SWA_HEAD_DIM, D)
    w_s = jnp.concatenate([jnp.where(kv_of_head.reshape(-1, 1, 1) == 0, w_s, 0.0),
                           jnp.where(kv_of_head.reshape(-1, 1, 1) == 1, w_s, 0.0)], axis=1)
    gdn_norm_row = jnp.tile(gdn_norm, GDN_HEADS)[None, :]
    xa = _mix_out0(o_f, o_b, z, swa, xa, mod, norms[1:2], gdn_norm_row, head_ones,
                   w_out[:GDN_HD].astype(BF16), w_s.reshape(SWA_Q_HEADS * LANES, D).astype(BF16), n_lat_tiles)
    return _ffn(xa, mod, norms[2:3], norms[3:4], ffn_gate.astype(BF16), ffn_up.astype(BF16),
                ffn_down.astype(BF16), n_lat_tiles)


def _odd_layer_last(xa, mod, norms, w_in, q_norm, kv_norm, w_uq, w_ukv, lam_p, lam_init, subln, w_out,
                    router, exp_gate, exp_up, exp_down, cos, sin, cos_r, sin_r, n_lat_tiles):
    B, S, D = xa.shape
    T = n_lat_tiles * ROW_TILE
    H = MLA_HEADS
    c0 = MLA_Q_RANK
    c1 = c0 + MLA_KV_RANK
    c2 = c1 + MLA_ROPE
    dw = DIFF_HEADS * 2 * DIFF_HEAD_DIM
    w_kr = jnp.pad(w_in[:, c1:c2], ((0, 0), (MLA_NOPE, LANES - MLA_NOPE - MLA_ROPE)))
    w_all = jnp.concatenate([w_in[:, :c1], w_kr, w_in[:, c2:c2 + dw] * DIFF_HEAD_DIM ** -0.5,
                             w_in[:, c2 + dw:]], axis=1).astype(BF16)
    cq, ckv, kr, dq, dk, dv = _odd_proj(xa, mod, norms[0:1], w_all, cos, sin, cos_r, sin_r, n_lat_tiles)

    qd = MLA_NOPE + MLA_ROPE
    wq = jnp.pad(w_uq.reshape(MLA_Q_RANK, H, qd) * qd ** -0.5, ((0, 0), (0, 0), (0, LANES - qd)))
    wkv = w_ukv.reshape(MLA_KV_RANK, H, MLA_NOPE + MLA_V)
    wk = jnp.pad(wkv[..., :MLA_NOPE], ((0, 0), (0, 0), (0, LANES - MLA_NOPE)))
    wv = jnp.pad(wkv[..., MLA_NOPE:], ((0, 0), (0, 0), (0, LANES - MLA_V)))
    q_cat, k_cat, v_ext = _mla_up(cq, ckv, kr, q_norm[None, :], kv_norm[None, :],
                                  wq.reshape(MLA_Q_RANK, H * LANES).astype(BF16),
                                  wk.reshape(MLA_KV_RANK, H * LANES).astype(BF16),
                                  wv.reshape(MLA_KV_RANK, H * LANES).astype(BF16), cos_r, sin_r)
    mla = _mla(q_cat, k_cat, v_ext, T)

    lam = (jnp.exp(jnp.sum(lam_p[0] * lam_p[1])) - jnp.exp(jnp.sum(lam_p[2] * lam_p[3])) + lam_init).reshape(1, 1)
    diff = _diff(dq, dk, dv, lam, subln[None, :], 1.0 - lam_init, T)

    w_mla = jnp.pad(w_out[:H * MLA_V].reshape(H, MLA_V, D), ((0, 0), (0, LANES - MLA_V), (0, 0)))
    x, f_in, logits = _mix_out1(mla, diff, xa, mod, norms[1:2], norms[2:3],
                                w_mla.reshape(H * LANES, D).astype(BF16), w_out[H * MLA_V:].astype(BF16),
                                _pad_cols(router, LANES), n_lat_tiles)

    dest, row_token, row_w, tile_expert, tile_rows = _route(logits.reshape(B * T, LANES)[:, :N_EXPERTS])
    x_sorted = f_in.reshape(B * T, D)[row_token]
    y = _moe_experts(tile_expert, tile_rows, x_sorted, row_w, exp_gate, exp_up, exp_down)
    y0 = y[dest[:, 0]].reshape(B, T, D)
    y1 = y[dest[:, 1]].reshape(B, T, D)
    return _norm_residual(y0, y1, x, mod, norms[3:4], n_lat_tiles)


def kernel(x, c, ctx, c_ctx, e_mod_w, e_mod_b, e_norms, e_w_in, e_conv_w, e_a_log, e_dt_bias, e_gdn_norm, e_sink, e_w_out, e_ffn_gate, e_ffn_up, e_ffn_down, o_mod_w, o_mod_b, o_norms, o_w_in, o_q_norm, o_kv_norm, o_w_uq, o_w_ukv, o_lambda, o_subln, o_w_out, o_router, o_exp_gate, o_exp_up, o_exp_down):
    B, T, D = x.shape
    n_lat_tiles = T // ROW_TILE
    rows = T // GRID_W
    cos64, sin64 = _stream_tables(*_rope_tables(rows, SWA_HEAD_DIM), 0)
    cos_r, sin_r = _stream_tables(*_rope_tables(rows, MLA_ROPE), MLA_NOPE)
    xa = jnp.concatenate([x, ctx], axis=1)
    mod_e = _mod_table(c, c_ctx, e_mod_w[0], e_mod_b[0])
    xa = _even_layer(xa, mod_e, e_norms[0], e_w_in[0], e_conv_w[0], e_a_log[0], e_dt_bias[0], e_gdn_norm[0],
                     e_sink[0], e_w_out[0], e_ffn_gate[0], e_ffn_up[0], e_ffn_down[0], cos64, sin64, n_lat_tiles)
    mod_o = _mod_table(c, c_ctx, o_mod_w[0], o_mod_b[0])
    lam_init = 0.8 - 0.6 * math.exp(-0.3 * 1)
    return _odd_layer_last(xa, mod_o, o_norms[0], o_w_in[0], o_q_norm[0], o_kv_norm[0], o_w_uq[0], o_w_ukv[0],
                           o_lambda[0], lam_init, o_subln[0], o_w_out[0], o_router[0], o_exp_gate[0],
                           o_exp_up[0], o_exp_down[0], cos64, sin64, cos_r, sin_r, n_lat_tiles)
```

```python
import functools
import math

import jax
import jax.numpy as jnp
from jax import lax
from jax.experimental import pallas as pl
from jax.experimental.pallas import tpu as pltpu

F32 = jnp.float32
BF16 = jnp.bfloat16

D_MODEL = 1024
CTX_LEN = 256
GRID_W = 64
NORM_EPS = 1e-6
ROPE_THETA = 10000.0

GDN_HEADS = 8
GDN_DK = 64
GDN_DV = 64
GDN_CHUNK = 64
CONV_K = 5
GDN_GROUP = 4
GDN_LANES = GDN_GROUP * GDN_DK
GDN_HD = GDN_HEADS * GDN_DK
GDN_NB = 4

SWA_Q_HEADS = 8
SWA_KV_HEADS = 2
SWA_HEAD_DIM = 64
SWA_WINDOW = 128
SWA_BLOCK = 128

MLA_HEADS = 8
MLA_Q_RANK = 384
MLA_KV_RANK = 256
MLA_NOPE = 64
MLA_ROPE = 32
MLA_V = 64

DIFF_HEADS = 4
DIFF_HEAD_DIM = 64

D_FF = 2816
N_EXPERTS = 8
TOP_K = 2
D_FF_EXPERT = 3584

LANES = 128
SUBLANES = 8
ROW_TILE = 256
LAT_TILE = 512
SWA_HEADS_PER_SUB = 1
ATTN_TQ = 2048
ATTN_SUB = 128
VMEM_LIMIT = 56 * 1024 * 1024
NEG_BIG = -1e30


def _dot(a, b):
    return jnp.dot(a, b, preferred_element_type=F32)


def _dot_nt(a, b):
    return lax.dot_general(a, b, (((1,), (1,)), ((), ())), preferred_element_type=F32)


def _dot_tn(a, b):
    return lax.dot_general(a, b, (((0,), (0,)), ((), ())), preferred_element_type=F32)


def _split_bf16(x):
    hi = x.astype(BF16)
    lo = (x - hi.astype(F32)).astype(BF16)
    return hi, lo


def _dot_split(x, w):
    hi, lo = _split_bf16(x)
    return _dot(hi, w) + _dot(lo, w)


def _dot_rsplit(w, x):
    hi, lo = _split_bf16(x)
    return _dot(w, hi) + _dot(w, lo)


def _rms(x, w):
    return x * lax.rsqrt(jnp.mean(x * x, axis=-1, keepdims=True) + NORM_EPS) * w


def _silu(x):
    return x * jax.nn.sigmoid(x)


def _rope_slab(x, cos, sin, quarter):
    lane = lax.broadcasted_iota(jnp.int32, x.shape, 1)
    fwd = pltpu.roll(x, quarter, 1)
    back = pltpu.roll(x, LANES - quarter, 1)
    rot = jnp.where(lane % (2 * quarter) < quarter, -back, fwd)
    return x * cos + rot * sin


def _params(*sem):
    return pltpu.CompilerParams(dimension_semantics=sem, vmem_limit_bytes=VMEM_LIMIT)


def _row_spec(width, tile=ROW_TILE):
    return pl.BlockSpec((1, tile, width), lambda b, t: (b, t, 0))


def _mod_spec(n_lat_tiles):
    return pl.BlockSpec((1, 1, 6, D_MODEL), lambda b, t: (b, jnp.where(t < n_lat_tiles, 1, 0), 0, 0))


def _const_spec(shape):
    return pl.BlockSpec(shape, lambda b, t: (0,) * len(shape))


def _table_spec():
    return pl.BlockSpec((ROW_TILE, LANES), lambda b, t: (t, 0))


E_QKV = (0, 3 * GDN_HD)
E_Z = (E_QKV[1], E_QKV[1] + GDN_HD)
E_GATE = (E_Z[1], E_Z[1] + LANES)
E_Q = (E_GATE[1], E_GATE[1] + SWA_Q_HEADS * LANES)
E_K = (E_Q[1], E_Q[1] + LANES)
E_V = (E_K[1], E_K[1] + LANES)


def _stream_in_specs(n_lat_tiles):
    assert CTX_LEN == ROW_TILE
    return [pl.BlockSpec((1, ROW_TILE, D_MODEL), lambda b, t: (b, jnp.minimum(t, n_lat_tiles - 1), 0)),
            pl.BlockSpec((1, ROW_TILE, D_MODEL), lambda b, t: (b, 0, 0))]


def _stream_tile(x_ref, c_ref, n_lat_tiles):
    return jnp.where(pl.program_id(1) < n_lat_tiles, x_ref[0], c_ref[0])


def _even_proj_kernel(x_ref, c_ref, mod_ref, nw_ref, w_ref, cos_ref, sin_ref,
                      qkv_ref, z_ref, gate_ref, q_ref, k_ref, v_ref, *, n_lat_tiles):
    xin = _stream_tile(x_ref, c_ref, n_lat_tiles)
    a = (_rms(xin, nw_ref[...]) * (1.0 + mod_ref[0, 0, 1:2, :]) + mod_ref[0, 0, 0:1, :]).astype(BF16)

    def proj(cols):
        return _dot(a, w_ref[:, cols[0]:cols[1]])

    qkv_ref[0] = proj(E_QKV)
    z_ref[0] = proj(E_Z).astype(BF16)
    gate_ref[0] = proj(E_GATE)
    cos = cos_ref[...]
    sin = sin_ref[...]
    quarter = SWA_HEAD_DIM // 4
    qp = proj(E_Q)
    for h in range(SWA_Q_HEADS):
        ls = slice(h * LANES, (h + 1) * LANES)
        q_ref[0, :, ls] = _rope_slab(qp[:, ls], cos, sin, quarter).astype(BF16)
    k_ref[0] = _rope_slab(proj(E_K), cos, sin, quarter).astype(BF16)
    v_ref[0] = proj(E_V).astype(BF16)


def _even_proj(x, ctx, mod, norm_w, w, cos, sin, n_lat_tiles):
    B, T, D = x.shape
    S = T + ctx.shape[1]
    widths = (E_QKV[1] - E_QKV[0], GDN_HD, LANES, SWA_Q_HEADS * LANES, LANES, LANES)
    dtypes = (F32, BF16, F32, BF16, BF16, BF16)
    return pl.pallas_call(
        functools.partial(_even_proj_kernel, n_lat_tiles=n_lat_tiles),
        grid=(B, S // ROW_TILE),
        in_specs=_stream_in_specs(n_lat_tiles) + [_mod_spec(n_lat_tiles), _const_spec((1, D)), _const_spec(w.shape),
                                                  _table_spec(), _table_spec()],
        out_specs=[_row_spec(n) for n in widths],
        out_shape=[jax.ShapeDtypeStruct((B, S, n), dt) for n, dt in zip(widths, dtypes)],
        compiler_params=_params("parallel", "parallel"),
        name="even_proj",
    )(x, ctx, mod, norm_w, w, cos, sin)


def _gdn_prep_kernel(x_ref, prev_ref, next_ref, cw_ref, bd_ref, q_ref, k_ref, v_ref, *, n_lat_tiles):
    t = pl.program_id(1)
    has_prev = jnp.logical_and(t > 0, t < n_lat_tiles)
    has_next = t < n_lat_tiles - 1
    prev = jnp.where(has_prev, prev_ref[0], 0.0)
    nxt = jnp.where(has_next, next_ref[0], 0.0)
    xe = jnp.concatenate([prev, x_ref[0], nxt], axis=0)
    pad = CONV_K // 2
    y = jnp.zeros(x_ref.shape[1:], F32)
    for kk in range(CONV_K):
        lo = SUBLANES + kk - pad
        y = y + cw_ref[kk:kk + 1, :] * xe[lo:lo + ROW_TILE, :]
    u = _silu(y)

    def headnorm(a):
        return a * lax.rsqrt(_dot_split(a * a, bd_ref[...]) + NORM_EPS)

    q_ref[0] = headnorm(u[:, :GDN_HD]) * GDN_DK ** -0.5
    k_ref[0] = headnorm(u[:, GDN_HD:2 * GDN_HD])
    v_ref[0] = u[:, 2 * GDN_HD:]


def _gdn_prep(qkv, conv_w, head_ones, n_lat_tiles):
    B, S, W = qkv.shape
    per_tile = ROW_TILE // SUBLANES
    n8 = S // SUBLANES
    return pl.pallas_call(
        functools.partial(_gdn_prep_kernel, n_lat_tiles=n_lat_tiles),
        grid=(B, S // ROW_TILE),
        in_specs=[_row_spec(W),
                  pl.BlockSpec((1, SUBLANES, W), lambda b, t: (b, jnp.maximum(t * per_tile - 1, 0), 0)),
                  pl.BlockSpec((1, SUBLANES, W), lambda b, t: (b, jnp.minimum((t + 1) * per_tile, n8 - 1), 0)),
                  _const_spec(conv_w.shape), _const_spec(head_ones.shape)],
        out_specs=[_row_spec(GDN_HD)] * 3,
        out_shape=[jax.ShapeDtypeStruct((B, S, GDN_HD), F32)] * 3,
        compiler_params=_params("parallel", "parallel"),
        name="gdn_prep",
    )(qkv, qkv, qkv, conv_w, head_ones)


def _block_diag(x, head_of_lane):
    zero = jnp.zeros_like(x)
    return jnp.concatenate([jnp.where(head_of_lane == h, x, zero) for h in range(GDN_GROUP)], axis=0)


def _expand_heads(x4, head_of_lane):
    c = x4.shape[0]
    out = jnp.broadcast_to(x4[:, GDN_GROUP - 1:GDN_GROUP], (c, GDN_LANES))
    for h in range(GDN_GROUP - 2, -1, -1):
        out = jnp.where(head_of_lane == h, jnp.broadcast_to(x4[:, h:h + 1], (c, GDN_LANES)), out)
    return out


def _gdn_groups(items):
    C = GDN_CHUNK
    row = lax.broadcasted_iota(jnp.int32, (C, GDN_LANES), 0)
    lane = lax.broadcasted_iota(jnp.int32, (C, GDN_LANES), 1)
    hol = lane // GDN_DK
    col = lane % GDN_DK
    eye = jnp.where(row == col, 1.0, 0.0)
    rr = lax.broadcasted_iota(jnp.int32, (2 * C, 4 * C), 0)
    cc = lax.broadcasted_iota(jnp.int32, (2 * C, 4 * C), 1)
    cum_lhs = {}
    for rev in (False, True):
        before = (rr - cc % C) * (-1 if rev else 1) >= 0
        cum_lhs[rev] = jnp.where(((rr < C) & (cc < 2 * C) & before) | ((rr >= C) & (cc >= 2 * C)),
                                 1.0, 0.0).astype(BF16)

    def bd(x):
        return _block_diag(x, hol)

    st = []
    for (q4, k4, v4, g4, b4, s4, reverse) in items:
        sgn = -1 if reverse else 1
        ahead = (row - col) * sgn
        incl = ahead >= 0
        strict = ahead > 0
        gE = _expand_heads(g4, hol)
        bE = _expand_heads(b4, hol)
        g_hi, g_lo = _split_bf16(gE)
        m_hi, m_lo = _split_bf16(jnp.where((col - row) * sgn >= 0, gE, 0.0))
        both = _dot(cum_lhs[reverse], jnp.concatenate([g_hi, g_lo, m_hi, m_lo], axis=0))
        gc = both[:C]
        gc_row = both[C:]
        decay = jnp.where(incl, jnp.exp(gc - gc_row), 0.0)
        g_last = gc[0:1, :] if reverse else gc[C - 1:C, :]
        eg = jnp.exp(gc)
        kb = k4 * bE
        st.append(dict(q=q4, k=k4, s=s4, incl=incl, strict=strict, decay=decay, g_last=g_last,
                       kb=kb, vb=v4 * bE, kbg=kb * eg, qh=q4 * eg, kt=k4 * jnp.exp(g_last - gc)))

    for d in st:
        gram = _dot_nt(jnp.concatenate([d["kb"], d["q"]], axis=0).astype(BF16), bd(d["k"].astype(BF16)))
        L = jnp.where(d["strict"], gram[:C] * d["decay"], 0.0)
        d["A"] = gram[C:] * d["decay"]
        d["T"] = eye - L
        d["P"] = L.astype(BF16)
    n_factors = int(math.log2(C)) - 1
    for d in st:
        d["P"] = _dot(d["P"], bd(d["P"])).astype(BF16)
    for it in range(n_factors):
        last = it == n_factors - 1
        for d in st:
            lhs = d["T"].astype(BF16) if last else jnp.concatenate([d["T"].astype(BF16), d["P"]], axis=0)
            prod = _dot(lhs, bd(d["P"]))
            d["T"] = d["T"] + prod[:C]
            if not last:
                d["P"] = prod[C:].astype(BF16)
    for d in st:
        Tb = d["T"].astype(BF16)
        d["u"] = _dot(Tb, bd(d["vb"].astype(BF16)))
        d["w"] = _dot(Tb, bd(d["kbg"].astype(BF16)))
    for d in st:
        d["ws_qs"] = _dot(jnp.concatenate([d["w"], d["qh"]], axis=0).astype(BF16), bd(d["s"].astype(BF16)))
    out = []
    for d in st:
        v_new = d["u"] - d["ws_qs"][:C]
        vb16 = v_new.astype(BF16)
        o = d["ws_qs"][C:] + _dot(d["A"].astype(BF16), bd(vb16))
        full = _dot_tn(d["kt"].astype(BF16), vb16)
        upd = jnp.zeros((GDN_DK, GDN_LANES), F32)
        for h in range(GDN_GROUP):
            upd = upd + jnp.where(hol == h, full[h * GDN_DK:(h + 1) * GDN_DK, :], 0.0)
        out.append((o, d["s"] * jnp.exp(d["g_last"]) + upd))
    return out


def _gdn_kernel(qf_ref, kf_ref, vf_ref, gf_ref, qb_ref, kb_ref, vb_ref, gb_ref, al_ref, dtb_ref,
                of_ref, ob_ref, s_ref):
    @pl.when(pl.program_id(1) == 0)
    def _():
        s_ref[...] = jnp.zeros_like(s_ref)

    n_grp = GDN_HEADS // GDN_GROUP
    n_gate = 2 * GDN_HEADS
    items = []
    slots = []
    for nb in range(GDN_NB):
        for d, (q_ref, k_ref, v_ref, g_ref, o_ref) in enumerate(((qf_ref, kf_ref, vf_ref, gf_ref, of_ref),
                                                                 (qb_ref, kb_ref, vb_ref, gb_ref, ob_ref))):
            raw = g_ref[nb]
            xs = raw + dtb_ref[...]
            softplus = jnp.maximum(xs, 0.0) + jnp.log(1.0 + jnp.exp(-jnp.abs(xs)))
            g = -jnp.exp(al_ref[...]) * softplus
            beta = jax.nn.sigmoid(raw)
            for grp in range(n_grp):
                ls = slice(grp * GDN_LANES, (grp + 1) * GDN_LANES)
                c0 = d * GDN_HEADS + grp * GDN_GROUP
                slot = (nb * 2 + d) * n_grp + grp
                items.append((q_ref[nb, :, ls], k_ref[nb, :, ls], v_ref[nb, :, ls],
                              g[:, c0:c0 + GDN_GROUP], beta[:, n_gate + c0:n_gate + c0 + GDN_GROUP],
                              s_ref[slot], d == 1))
                slots.append((o_ref, nb, ls, slot))
    for (o_ref, nb, ls, slot), (o, s_new) in zip(slots, _gdn_groups(items)):
        o_ref[nb, :, ls] = o
        s_ref[slot] = s_new


def _gdn(q, k, v, gate, a_log_row, dt_bias_row, n_lat_chunks):
    B, S, HD = q.shape
    n_chunks = S // GDN_CHUNK
    n_ctx = n_chunks - n_lat_chunks

    def fwd_c(s):
        return jnp.where(s < n_ctx, n_lat_chunks + s, s - n_ctx)

    def bwd_c(s):
        return n_chunks - 1 - s

    def spec(cmap, width):
        return pl.BlockSpec((GDN_NB, GDN_CHUNK, width), lambda b, s: (b, cmap(s), 0))

    return pl.pallas_call(
        _gdn_kernel,
        grid=(B // GDN_NB, n_chunks),
        in_specs=[spec(fwd_c, HD), spec(fwd_c, HD), spec(fwd_c, HD), spec(fwd_c, LANES),
                  spec(bwd_c, HD), spec(bwd_c, HD), spec(bwd_c, HD), spec(bwd_c, LANES),
                  _const_spec((1, LANES)), _const_spec((1, LANES))],
        out_specs=[spec(fwd_c, HD), spec(bwd_c, HD)],
        out_shape=[jax.ShapeDtypeStruct((B, S, HD), F32)] * 2,
        scratch_shapes=[pltpu.VMEM((GDN_NB * 2 * GDN_HEADS // GDN_GROUP, GDN_DK, GDN_LANES), F32)],
        compiler_params=_params("parallel", "arbitrary"),
        name="gdn_scan",
    )(q, k, v, gate, q, k, v, gate, a_log_row, dt_bias_row)


def _swa_kernel(q_ref, k_ref, v_ref, sink_ref, o_ref, *, n_lat_blocks):
    W = SWA_BLOCK
    H = SWA_Q_HEADS
    n = pl.program_id(1)
    is_lat = n < n_lat_blocks
    nl = jnp.minimum(n, n_lat_blocks - 1)
    prev = jnp.maximum(nl - 1, 0)
    nxt = jnp.minimum(nl + 1, n_lat_blocks - 1)
    ctx0 = n_lat_blocks * W

    def keys(ref):
        def blk(i):
            return ref[0, pl.ds(pl.multiple_of(i * W, W), W), :]
        return jnp.concatenate([ref[0, ctx0:ctx0 + CTX_LEN, :], blk(prev), blk(nl), blk(nxt)], axis=0)

    k_all = keys(k_ref)
    v_all = keys(v_ref)
    nk = CTX_LEN + 3 * W
    v_ext = jnp.concatenate([v_all, jnp.ones((nk, LANES), BF16)], axis=1)
    hp = SWA_HEADS_PER_SUB
    ii = lax.broadcasted_iota(jnp.int32, (hp * W, nk), 0) % W
    jj = lax.broadcasted_iota(jnp.int32, (hp * W, nk), 1) - CTX_LEN
    in_window = (((jj >= 0) & (jj < W) & (jj >= ii) & (nl > 0))
                 | ((jj >= W) & (jj < 2 * W))
                 | ((jj >= 2 * W) & (jj - 2 * W <= ii) & (nl < n_lat_blocks - 1)))
    valid = (jj < 0) | (in_window & is_lat)

    def scores(i):
        q = jnp.concatenate([q_ref[0, :, h * LANES:(h + 1) * LANES] for h in range(i * hp, (i + 1) * hp)], axis=0)
        return jnp.where(valid, _dot_nt(q, k_all), NEG_BIG)

    def finish(i, s):
        sk = sink_ref[i * hp * W:(i + 1) * hp * W, :]
        m = jnp.maximum(jnp.max(s, axis=-1, keepdims=True), sk)
        acc = _dot(jnp.exp(s - m).astype(BF16), v_ext)
        o = acc[:, :LANES] / (acc[:, LANES:LANES + 1] + jnp.exp(sk - m))
        for j in range(hp):
            h = i * hp + j
            o_ref[0, :, h * LANES:(h + 1) * LANES] = o[j * W:(j + 1) * W].astype(o_ref.dtype)

    _one_ahead(H // hp, scores, finish)


def _swa(q_pad, k, v, sink_rows, n_lat_blocks):
    B, S, _ = k.shape
    W = SWA_BLOCK
    QW = SWA_Q_HEADS * LANES
    return pl.pallas_call(
        functools.partial(_swa_kernel, n_lat_blocks=n_lat_blocks),
        grid=(B, S // W),
        in_specs=[pl.BlockSpec((1, W, QW), lambda b, n: (b, n, 0)),
                  pl.BlockSpec((1, S, LANES), lambda b, n: (b, 0, 0)),
                  pl.BlockSpec((1, S, LANES), lambda b, n: (b, 0, 0)),
                  _const_spec(sink_rows.shape)],
        out_specs=pl.BlockSpec((1, W, QW), lambda b, n: (b, n, 0)),
        out_shape=jax.ShapeDtypeStruct((B, S, QW), BF16),
        compiler_params=_params("parallel", "parallel"),
        name="swa",
    )(q_pad, k, v, sink_rows)


def _mix_out0_kernel(of_ref, ob_ref, z_ref, swa_ref, x_ref, c_ref, mod_ref, nw_ref, gn_ref, bd_ref, wg_ref, ws_ref,
                     o_ref, *, n_lat_tiles):
    o = of_ref[0] + ob_ref[0]
    ms = _dot_split(o * o, bd_ref[...]) * (1.0 / GDN_DV)
    gdn = o * lax.rsqrt(ms + NORM_EPS) * gn_ref[...] * _silu(z_ref[0].astype(F32))
    y = _dot(gdn.astype(BF16), wg_ref[...]) + _dot(swa_ref[0], ws_ref[...])
    o_ref[0] = _stream_tile(x_ref, c_ref, n_lat_tiles) + mod_ref[0, 0, 2:3, :] * _rms(y, nw_ref[...])


def _mix_out0(o_f, o_b, z, swa, x, ctx, mod, norm_w, gdn_norm_row, head_ones, w_gdn, w_swa, n_lat_tiles):
    B, S, _ = o_f.shape
    D = x.shape[-1]
    return pl.pallas_call(
        functools.partial(_mix_out0_kernel, n_lat_tiles=n_lat_tiles),
        grid=(B, S // ROW_TILE),
        in_specs=[_row_spec(GDN_HD), _row_spec(GDN_HD), _row_spec(GDN_HD), _row_spec(swa.shape[-1])]
        + _stream_in_specs(n_lat_tiles)
        + [_mod_spec(n_lat_tiles), _const_spec((1, D)), _const_spec(gdn_norm_row.shape),
           _const_spec(head_ones.shape), _const_spec(w_gdn.shape), _const_spec(w_swa.shape)],
        out_specs=_row_spec(D),
        out_shape=jax.ShapeDtypeStruct((B, S, D), F32),
        compiler_params=_params("parallel", "parallel"),
        name="mix_out0",
    )(o_f, o_b, z, swa, x, ctx, mod, norm_w, gdn_norm_row, head_ones, w_gdn, w_swa)


def _ffn_kernel(x_ref, mod_ref, n_in_ref, n_out_ref, wg_ref, wu_ref, wd_ref, o_ref):
    x = x_ref[0]
    f = (_rms(x, n_in_ref[...]) * (1.0 + mod_ref[0, 0, 4:5, :]) + mod_ref[0, 0, 3:4, :]).astype(BF16)
    h = _silu(_dot(f, wg_ref[...])) * _dot(f, wu_ref[...])
    y = _dot(h.astype(BF16), wd_ref[...])
    o_ref[0] = x + mod_ref[0, 0, 5:6, :] * _rms(y, n_out_ref[...])


def _ffn(xa, mod, n_in, n_out, wg, wu, wd, n_lat_tiles):
    B, S, D = xa.shape
    return pl.pallas_call(
        _ffn_kernel,
        grid=(B, S // ROW_TILE),
        in_specs=[_row_spec(D), _mod_spec(n_lat_tiles), _const_spec((1, D)), _const_spec((1, D)),
                  _const_spec(wg.shape), _const_spec(wu.shape), _const_spec(wd.shape)],
        out_specs=_row_spec(D),
        out_shape=jax.ShapeDtypeStruct((B, S, D), F32),
        compiler_params=_params("parallel", "parallel"),
        name="dense_ffn",
    )(xa, mod, n_in, n_out, wg, wu, wd)


O_CQ = (0, MLA_Q_RANK)
O_CKV = (O_CQ[1], O_CQ[1] + MLA_KV_RANK)
O_KR = (O_CKV[1], O_CKV[1] + LANES)
O_DQ = (O_KR[1], O_KR[1] + DIFF_HEADS * LANES)
O_DK = (O_DQ[1], O_DQ[1] + DIFF_HEADS * LANES)
O_DV = (O_DK[1], O_DK[1] + DIFF_HEADS * LANES)


def _odd_proj_kernel(x_ref, mod_ref, nw_ref, w_ref, cos_ref, sin_ref, cosr_ref, sinr_ref,
                     qn_ref, kvn_ref, wq_ref, wk_ref, wv_ref,
                     q_ref, k_ref, v_ref, dq_ref, dk_ref, dv_ref):
    a = (_rms(x_ref[0], nw_ref[...]) * (1.0 + mod_ref[0, 0, 1:2, :]) + mod_ref[0, 0, 0:1, :]).astype(BF16)

    def proj(cols):
        return _dot(a, w_ref[:, cols[0]:cols[1]])

    cos_r = cosr_ref[...]
    sin_r = sinr_ref[...]
    rq = MLA_ROPE // 4
    cq = _rms(proj(O_CQ), qn_ref[...]).astype(BF16)
    ckv = _rms(proj(O_CKV), kvn_ref[...]).astype(BF16)
    kr = _rope_slab(proj(O_KR), cos_r, sin_r, rq)
    qp = _dot(cq, wq_ref[...])
    kp = _dot(ckv, wk_ref[...])
    vp = _dot(ckv, wv_ref[...])
    lane = lax.broadcasted_iota(jnp.int32, (ROW_TILE, LANES), 1)
    ones_hi = jnp.where(lane >= MLA_V, 1.0, 0.0)
    for h in range(MLA_HEADS):
        ls = slice(h * LANES, (h + 1) * LANES)
        q_ref[0, :, ls] = _rope_slab(qp[:, ls], cos_r, sin_r, rq).astype(BF16)
        k_ref[0, :, ls] = (kp[:, ls] + kr).astype(BF16)
        v_ref[0, :, ls] = (vp[:, ls] + ones_hi).astype(BF16)
    cos = cos_ref[...]
    sin = sin_ref[...]
    quarter = DIFF_HEAD_DIM // 4
    dq = proj(O_DQ)
    dk = proj(O_DK)
    dv = proj(O_DV)
    ones = jnp.ones((ROW_TILE, LANES), BF16)
    for h in range(DIFF_HEADS):
        ls = slice(h * LANES, (h + 1) * LANES)
        dq_ref[0, :, ls] = _rope_slab(dq[:, ls], cos, sin, quarter).astype(BF16)
        dk_ref[0, :, ls] = _rope_slab(dk[:, ls], cos, sin, quarter).astype(BF16)
        dv_ref[0, :, 2 * h * LANES:(2 * h + 1) * LANES] = dv[:, ls].astype(BF16)
        dv_ref[0, :, (2 * h + 1) * LANES:(2 * h + 2) * LANES] = ones


def _odd_proj(xa, mod, norm_w, w, cos, sin, cos_r, sin_r, q_norm, kv_norm, wq, wk, wv, n_lat_tiles):
    B, S, D = xa.shape
    HW = MLA_HEADS * LANES
    widths = (HW, HW, HW, DIFF_HEADS * LANES, DIFF_HEADS * LANES, 2 * DIFF_HEADS * LANES)
    return pl.pallas_call(
        _odd_proj_kernel,
        grid=(B, S // ROW_TILE),
        in_specs=[_row_spec(D), _mod_spec(n_lat_tiles), _const_spec((1, D)), _const_spec(w.shape),
                  _table_spec(), _table_spec(), _table_spec(), _table_spec(),
                  _const_spec(q_norm.shape), _const_spec(kv_norm.shape),
                  _const_spec(wq.shape), _const_spec(wk.shape), _const_spec(wv.shape)],
        out_specs=[_row_spec(n) for n in widths],
        out_shape=[jax.ShapeDtypeStruct((B, S, n), BF16) for n in widths],
        compiler_params=_params("parallel", "parallel"),
        name="odd_proj",
    )(xa, mod, norm_w, w, cos, sin, cos_r, sin_r, q_norm, kv_norm, wq, wk, wv)


def _one_ahead(n_sub, scores, finish):
    s_prev = scores(0)
    for i in range(1, n_sub):
        s_next = scores(i)
        finish(i - 1, s_prev)
        s_prev = s_next
    finish(n_sub - 1, s_prev)


def _mla_kernel(q_ref, k_ref, v_ref, o_ref):
    k = k_ref[0]
    v = v_ref[0]

    def scores(i):
        return _dot_nt(q_ref[0, i * ATTN_SUB:(i + 1) * ATTN_SUB, :], k)

    def finish(i, s):
        m = jnp.max(s, axis=-1, keepdims=True)
        acc = _dot(jnp.exp(s - m).astype(BF16), v)
        o_ref[0, i * ATTN_SUB:(i + 1) * ATTN_SUB, :] = (acc / acc[:, MLA_V:MLA_V + 1]).astype(o_ref.dtype)

    _one_ahead(q_ref.shape[1] // ATTN_SUB, scores, finish)


def _mla(q, k, v_ext, T):
    B, S, _ = k.shape
    return pl.pallas_call(
        _mla_kernel,
        grid=(B, MLA_HEADS, T // ATTN_TQ),
        in_specs=[pl.BlockSpec((1, ATTN_TQ, LANES), lambda b, h, t: (b, t, h)),
                  pl.BlockSpec((1, S, LANES), lambda b, h, t: (b, 0, h)),
                  pl.BlockSpec((1, S, LANES), lambda b, h, t: (b, 0, h))],
        out_specs=pl.BlockSpec((1, ATTN_TQ, LANES), lambda b, h, t: (b, t, h)),
        out_shape=jax.ShapeDtypeStruct((B, T, MLA_HEADS * LANES), BF16),
        compiler_params=_params("parallel", "parallel", "parallel"),
        name="mla_attn",
    )(q, k, v_ext)


def _diff_kernel(q_ref, k_ref, v_ref, lam_ref, sub_ref, o_ref, *, post_scale):
    k = k_ref[0]
    v = v_ref[0]
    sub = ATTN_SUB
    vw = 2 * DIFF_HEAD_DIM
    lo = lax.broadcasted_iota(jnp.int32, (sub, LANES), 1) < DIFF_HEAD_DIM
    zero = jnp.zeros((sub, LANES), BF16)

    def scores(i):
        q = q_ref[0, i * sub:(i + 1) * sub, :]
        return _dot_nt(jnp.concatenate([jnp.where(lo, q, zero), jnp.where(lo, zero, q)], axis=0), k)

    def finish(i, s):
        m = jnp.max(s, axis=-1, keepdims=True)
        acc = _dot(jnp.exp(s - m).astype(BF16), v)
        att = acc[:, :vw] / acc[:, vw:vw + 1]
        a = att[:sub] - lam_ref[...] * att[sub:]
        o_ref[0, i * sub:(i + 1) * sub, :] = (_rms(a, sub_ref[...]) * post_scale).astype(o_ref.dtype)

    _one_ahead(q_ref.shape[1] // sub, scores, finish)


def _diff(q, k, v_ext, lam, subln, post_scale, T):
    B, S, _ = k.shape
    tq = ATTN_TQ // 2
    return pl.pallas_call(
        functools.partial(_diff_kernel, post_scale=post_scale),
        grid=(B, DIFF_HEADS, T // tq),
        in_specs=[pl.BlockSpec((1, tq, LANES), lambda b, h, t: (b, t, h)),
                  pl.BlockSpec((1, S, LANES), lambda b, h, t: (b, 0, h)),
                  pl.BlockSpec((1, S, 2 * LANES), lambda b, h, t: (b, 0, h)),
                  pl.BlockSpec((1, 1), lambda b, h, t: (0, 0)),
                  pl.BlockSpec((1, LANES), lambda b, h, t: (0, 0))],
        out_specs=pl.BlockSpec((1, tq, LANES), lambda b, h, t: (b, t, h)),
        out_shape=jax.ShapeDtypeStruct((B, T, DIFF_HEADS * LANES), BF16),
        compiler_params=_params("parallel", "parallel", "parallel"),
        name="diff_attn",
    )(q, k, v_ext, lam, subln)


def _mix_out1_kernel(mla_ref, diff_ref, x_ref, mod_ref, n1_ref, n2_ref, w1_ref, w2_ref, wrh_ref, wrl_ref,
                     xo_ref, f_ref, lg_ref):
    y = _dot(mla_ref[0], w1_ref[...]) + _dot(diff_ref[0], w2_ref[...])
    x = x_ref[0] + mod_ref[0, 0, 2:3, :] * _rms(y, n1_ref[...])
    xo_ref[0] = x
    f = _rms(x, n2_ref[...]) * (1.0 + mod_ref[0, 0, 4:5, :]) + mod_ref[0, 0, 3:4, :]
    f_ref[0] = f
    f_hi, f_lo = _split_bf16(f)
    lg_ref[0] = _dot(f_hi, wrh_ref[...]) + (_dot(f_hi, wrl_ref[...]) + _dot(f_lo, wrh_ref[...]))


def _mix_out1(mla, diff, xa, mod, n1, n2, w1, w2, wr, n_lat_tiles):
    B, T, _ = mla.shape
    D = xa.shape[-1]
    wr_hi, wr_lo = _split_bf16(wr)
    return pl.pallas_call(
        _mix_out1_kernel,
        grid=(B, T // ROW_TILE),
        in_specs=[_row_spec(mla.shape[-1]), _row_spec(diff.shape[-1]), _row_spec(D), _mod_spec(n_lat_tiles),
                  _const_spec((1, D)), _const_spec((1, D)), _const_spec(w1.shape), _const_spec(w2.shape),
                  _const_spec(wr.shape), _const_spec(wr.shape)],
        out_specs=[_row_spec(D), _row_spec(D), _row_spec(LANES)],
        out_shape=[jax.ShapeDtypeStruct((B, T, D), F32), jax.ShapeDtypeStruct((B, T, D), F32),
                   jax.ShapeDtypeStruct((B, T, LANES), F32)],
        compiler_params=_params("parallel", "parallel"),
        name="mix_out1",
    )(mla, diff, xa, mod, n1, n2, w1, w2, wr_hi, wr_lo)


MOE_TM = 1024
MOE_TF = 512
MOE_SUB = 256


def _moe_kernel(te_ref, tr_ref, x_ref, rw_ref, wg_ref, wu_ref, wd_ref, o_ref, acc_ref):
    i = pl.program_id(0)
    f = pl.program_id(1)
    n_rows = tr_ref[i]
    n_sub = MOE_TM // MOE_SUB

    @pl.when(f == 0)
    def _():
        acc_ref[...] = jnp.zeros_like(acc_ref)

    def sub_tile_fns():
        wg = wg_ref[0].astype(BF16)
        wu = wu_ref[0].astype(BF16)
        wd = wd_ref[0].astype(BF16)

        def gate_up(j):
            x = x_ref[j * MOE_SUB:(j + 1) * MOE_SUB, :].astype(BF16)
            return _dot(x, wg), _dot(x, wu)

        def down(j, gu):
            h = (_silu(gu[0]) * gu[1]).astype(BF16)
            acc_ref[j * MOE_SUB:(j + 1) * MOE_SUB, :] += _dot(h, wd)

        return gate_up, down

    all_subs = n_rows > (n_sub - 1) * MOE_SUB

    @pl.when(all_subs)
    def _():
        _one_ahead(n_sub, *sub_tile_fns())

    for j in range(n_sub - 1):
        @pl.when(jnp.logical_and(jnp.logical_not(all_subs), n_rows > j * MOE_SUB))
        def _(j=j):
            gate_up, down = sub_tile_fns()
            down(j, gate_up(j))

    @pl.when(f == pl.num_programs(1) - 1)
    def _():
        o_ref[...] = acc_ref[...] * rw_ref[...]


def _moe_experts(tile_expert, tile_rows, x_sorted, row_w, wg, wu, wd):
    R, D = x_sorted.shape
    E, _, F = wg.shape
    nf = F // MOE_TF

    def f_eff(i, f, ta):
        return jnp.where(ta[i] > 0, f, nf - 1)

    grid_spec = pltpu.PrefetchScalarGridSpec(
        num_scalar_prefetch=2,
        grid=(R // MOE_TM, nf),
        in_specs=[pl.BlockSpec((MOE_TM, D), lambda i, f, te, ta: (i, 0)),
                  pl.BlockSpec((MOE_TM, 1), lambda i, f, te, ta: (i, 0)),
                  pl.BlockSpec((1, D, MOE_TF), lambda i, f, te, ta: (te[i], 0, f_eff(i, f, ta))),
                  pl.BlockSpec((1, D, MOE_TF), lambda i, f, te, ta: (te[i], 0, f_eff(i, f, ta))),
                  pl.BlockSpec((1, MOE_TF, D), lambda i, f, te, ta: (te[i], f_eff(i, f, ta), 0))],
        out_specs=pl.BlockSpec((MOE_TM, D), lambda i, f, te, ta: (i, 0)),
        scratch_shapes=[pltpu.VMEM((MOE_TM, D), F32)],
    )
    return pl.pallas_call(
        _moe_kernel,
        grid_spec=grid_spec,
        out_shape=jax.ShapeDtypeStruct((R, D), F32),
        compiler_params=_params("arbitrary", "arbitrary"),
        name="moe_experts",
    )(tile_expert, tile_rows, x_sorted, row_w, wg, wu, wd)


def _norm_residual_kernel(a_ref, b_ref, x_ref, mod_ref, nw_ref, o_ref):
    y = a_ref[0] + b_ref[0]
    o_ref[0] = x_ref[0] + mod_ref[0, 0, 5:6, :] * _rms(y, nw_ref[...])


def _norm_residual(a, b, x, mod, norm_w, n_lat_tiles):
    B, T, D = x.shape
    return pl.pallas_call(
        _norm_residual_kernel,
        grid=(B, T // LAT_TILE),
        in_specs=[_row_spec(D, LAT_TILE), _row_spec(D, LAT_TILE), _row_spec(D, LAT_TILE), _mod_spec(n_lat_tiles),
                  _const_spec((1, D))],
        out_specs=_row_spec(D, LAT_TILE),
        out_shape=jax.ShapeDtypeStruct((B, T, D), F32),
        compiler_params=_params("parallel", "parallel"),
        name="norm_residual",
    )(a, b, x, mod, norm_w)


def _route(logits):
    N = logits.shape[0]
    M = N * TOP_K
    R = M + N_EXPERTS * MOE_TM
    top_val, top_idx = lax.top_k(logits, TOP_K)
    top_w = jax.nn.softmax(top_val, axis=-1).reshape(-1)
    e_flat = top_idx.reshape(-1)
    onehot = (e_flat[:, None] == jnp.arange(N_EXPERTS)[None, :]).astype(jnp.int32)
    counts = jnp.sum(onehot, axis=0)
    padded = ((counts + MOE_TM - 1) // MOE_TM) * MOE_TM
    ends = jnp.cumsum(padded)
    starts = ends - padded
    ustarts = jnp.cumsum(counts) - counts
    order = jnp.argsort(e_flat, stable=True).astype(jnp.int32)
    rank = jnp.argsort(order).astype(jnp.int32)
    dest = rank + jnp.sum(onehot * (starts - ustarts)[None, :], axis=-1)
    n_tiles = R // MOE_TM
    tile_start = jnp.arange(n_tiles, dtype=jnp.int32) * MOE_TM
    tile_active = (tile_start < ends[-1]).astype(jnp.int32)
    te = jnp.minimum(jnp.searchsorted(ends, tile_start, side="right"), N_EXPERTS - 1).astype(jnp.int32)
    last_e = te[jnp.maximum(ends[-1] // MOE_TM - 1, 0)]
    tile_expert = jnp.where(tile_active > 0, te, last_e)
    in_tile = jnp.arange(MOE_TM, dtype=jnp.int32)[None, :]
    within = (tile_start - starts[tile_expert])[:, None] + in_tile
    valid = (within < counts[tile_expert][:, None]) & (tile_active[:, None] > 0)
    src = order[jnp.clip(ustarts[tile_expert][:, None] + within, 0, M - 1).reshape(R)]
    valid = valid.reshape(R)
    tile_rows = jnp.clip(counts[tile_expert] - (tile_start - starts[tile_expert]), 0, MOE_TM) * tile_active
    row_token = jnp.where(valid, src // TOP_K, jnp.arange(R, dtype=jnp.int32) % N)
    row_w = jnp.where(valid, top_w[src], 0.0)
    return dest.reshape(N, TOP_K), row_token, row_w[:, None], tile_expert, tile_rows.astype(jnp.int32)


def _rope_tables(rows, rot_dim):
    row = jnp.repeat(jnp.arange(rows, dtype=F32), GRID_W)
    col = jnp.tile(jnp.arange(GRID_W, dtype=F32), rows)
    half = rot_dim // 2
    inv_freq = ROPE_THETA ** (-jnp.arange(0, half, 2, dtype=F32) / half)
    ang_r = row[:, None] * inv_freq[None, :]
    ang_c = col[:, None] * inv_freq[None, :]
    ang = jnp.concatenate([ang_r, ang_r, ang_c, ang_c], axis=-1)
    return jnp.cos(ang), jnp.sin(ang)


def _stream_tables(cos, sin, lane0):
    T, R = cos.shape
    if lane0 == 0:
        cos_l, sin_l = jnp.tile(cos, (1, LANES // R)), jnp.tile(sin, (1, LANES // R))
    else:
        pad = ((0, 0), (lane0, LANES - lane0 - R))
        cos_l = jnp.pad(cos - 1.0, pad) + 1.0
        sin_l = jnp.pad(sin, pad)
    ctx = ((0, CTX_LEN), (0, 0))
    return jnp.pad(cos_l - 1.0, ctx) + 1.0, jnp.pad(sin_l, ctx)


def _mod_table(c, c_ctx, mod_w, mod_b):
    B = c.shape[0]
    hp = lax.Precision.HIGHEST
    m_l = (jnp.dot(jax.nn.silu(c), mod_w, precision=hp) + mod_b).reshape(B, 1, 6, D_MODEL)
    m_c = (jnp.dot(jax.nn.silu(c_ctx), mod_w, precision=hp) + mod_b).reshape(1, 1, 6, D_MODEL)
    return jnp.concatenate([jnp.broadcast_to(m_c, m_l.shape), m_l], axis=1)


def _pad_cols(w, n):
    return jnp.pad(w, ((0, 0), (0, n - w.shape[1])))


def _even_layer(x, ctx, mod, norms, w_in, conv_w, a_log, dt_bias, gdn_norm, sink, w_out, ffn_gate, ffn_up,
                ffn_down, cos, sin, n_lat_tiles):
    B, T, D = x.shape
    n_gate = 2 * GDN_HEADS
    c_qkv = 3 * GDN_HD
    c_z = c_qkv + GDN_HD
    c_ga = c_z + n_gate
    c_gb = c_ga + n_gate
    c_sq = c_gb + SWA_Q_HEADS * SWA_HEAD_DIM
    c_sk = c_sq + SWA_KV_HEADS * SWA_HEAD_DIM
    grp = SWA_Q_HEADS // SWA_KV_HEADS
    wq = w_in[:, c_gb:c_sq].reshape(D, SWA_Q_HEADS, SWA_HEAD_DIM) * SWA_HEAD_DIM ** -0.5
    kv_of_head = (jnp.arange(SWA_Q_HEADS) // grp)[None, :, None]
    wq = jnp.concatenate([jnp.where(kv_of_head == 0, wq, 0.0), jnp.where(kv_of_head == 1, wq, 0.0)], axis=-1)
    w_all = jnp.concatenate([w_in[:, :c_z], _pad_cols(w_in[:, c_z:c_gb], LANES),
                             wq.reshape(D, SWA_Q_HEADS * LANES), w_in[:, c_sq:]], axis=1).astype(BF16)
    qkv, z, gate, q_pad, k, v = _even_proj(x, ctx, mod, norms[0:1], w_all, cos, sin, n_lat_tiles)

    head_ones = (jnp.arange(GDN_HD)[:, None] // GDN_DK == jnp.arange(GDN_HD)[None, :] // GDN_DK).astype(BF16)
    conv_w8 = jnp.pad(conv_w, ((0, SUBLANES - CONV_K), (0, 0)))
    gq, gk, gv = _gdn_prep(qkv, conv_w8, head_ones, n_lat_tiles)
    a_log_row = _pad_cols(a_log.reshape(1, n_gate), LANES)
    dt_bias_row = _pad_cols(dt_bias.reshape(1, n_gate), LANES)
    o_f, o_b = _gdn(gq, gk, gv, gate, a_log_row, dt_bias_row, T // GDN_CHUNK)

    sink_rows = jnp.repeat(sink, SWA_BLOCK)[:, None]
    swa = _swa(q_pad, k, v, sink_rows, T // SWA_BLOCK)

    w_s = w_out[GDN_HD:].reshape(SWA_Q_HEADS, SWA_HEAD_DIM, D)
    w_s = jnp.concatenate([jnp.where(kv_of_head.reshape(-1, 1, 1) == 0, w_s, 0.0),
                           jnp.where(kv_of_head.reshape(-1, 1, 1) == 1, w_s, 0.0)], axis=1)
    gdn_norm_row = jnp.tile(gdn_norm, GDN_HEADS)[None, :]
    xa = _mix_out0(o_f, o_b, z, swa, x, ctx, mod, norms[1:2], gdn_norm_row, head_ones,
                   w_out[:GDN_HD].astype(BF16), w_s.reshape(SWA_Q_HEADS * LANES, D).astype(BF16), n_lat_tiles)
    return _ffn(xa, mod, norms[2:3], norms[3:4], ffn_gate.astype(BF16), ffn_up.astype(BF16),
                ffn_down.astype(BF16), n_lat_tiles)


def _odd_layer_last(xa, mod, norms, w_in, q_norm, kv_norm, w_uq, w_ukv, lam_p, lam_init, subln, w_out,
                    router, exp_gate, exp_up, exp_down, cos, sin, cos_r, sin_r, n_lat_tiles):
    B, S, D = xa.shape
    T = n_lat_tiles * ROW_TILE
    H = MLA_HEADS
    c0 = MLA_Q_RANK
    c1 = c0 + MLA_KV_RANK
    c2 = c1 + MLA_ROPE
    dw = DIFF_HEADS * 2 * DIFF_HEAD_DIM
    w_kr = jnp.pad(w_in[:, c1:c2], ((0, 0), (MLA_NOPE, LANES - MLA_NOPE - MLA_ROPE)))
    w_all = jnp.concatenate([w_in[:, :c1], w_kr, w_in[:, c2:c2 + dw] * DIFF_HEAD_DIM ** -0.5,
                             w_in[:, c2 + dw:]], axis=1).astype(BF16)
    qd = MLA_NOPE + MLA_ROPE
    wq = jnp.pad(w_uq.reshape(MLA_Q_RANK, H, qd) * qd ** -0.5, ((0, 0), (0, 0), (0, LANES - qd)))
    wkv = w_ukv.reshape(MLA_KV_RANK, H, MLA_NOPE + MLA_V)
    wk = jnp.pad(wkv[..., :MLA_NOPE], ((0, 0), (0, 0), (0, LANES - MLA_NOPE)))
    wv = jnp.pad(wkv[..., MLA_NOPE:], ((0, 0), (0, 0), (0, LANES - MLA_V)))
    q_cat, k_cat, v_ext, dq, dk, dv = _odd_proj(
        xa, mod, norms[0:1], w_all, cos, sin, cos_r, sin_r, q_norm[None, :], kv_norm[None, :],
        wq.reshape(MLA_Q_RANK, H * LANES).astype(BF16), wk.reshape(MLA_KV_RANK, H * LANES).astype(BF16),
        wv.reshape(MLA_KV_RANK, H * LANES).astype(BF16), n_lat_tiles)
    mla = _mla(q_cat, k_cat, v_ext, T)

    lam = (jnp.exp(jnp.sum(lam_p[0] * lam_p[1])) - jnp.exp(jnp.sum(lam_p[2] * lam_p[3])) + lam_init).reshape(1, 1)
    diff = _diff(dq, dk, dv, lam, subln[None, :], 1.0 - lam_init, T)

    w_mla = jnp.pad(w_out[:H * MLA_V].reshape(H, MLA_V, D), ((0, 0), (0, LANES - MLA_V), (0, 0)))
    x, f_in, logits = _mix_out1(mla, diff, xa, mod, norms[1:2], norms[2:3],
                                w_mla.reshape(H * LANES, D).astype(BF16), w_out[H * MLA_V:].astype(BF16),
                                _pad_cols(router, LANES), n_lat_tiles)

    dest, row_token, row_w, tile_expert, tile_rows = _route(logits.reshape(B * T, LANES)[:, :N_EXPERTS])
    x_sorted = f_in.reshape(B * T, D)[row_token]
    y = _moe_experts(tile_expert, tile_rows, x_sorted, row_w, exp_gate, exp_up, exp_down)
    y0 = y[dest[:, 0]].reshape(B, T, D)
    y1 = y[dest[:, 1]].reshape(B, T, D)
    return _norm_residual(y0, y1, x, mod, norms[3:4], n_lat_tiles)


def kernel(x, c, ctx, c_ctx, e_mod_w, e_mod_b, e_norms, e_w_in, e_conv_w, e_a_log, e_dt_bias, e_gdn_norm, e_sink, e_w_out, e_ffn_gate, e_ffn_up, e_ffn_down, o_mod_w, o_mod_b, o_norms, o_w_in, o_q_norm, o_kv_norm, o_w_uq, o_w_ukv, o_lambda, o_subln, o_w_out, o_router, o_exp_gate, o_exp_up, o_exp_down):
    B, T, D = x.shape
    n_lat_tiles = T // ROW_TILE
    rows = T // GRID_W
    cos64, sin64 = _stream_tables(*_rope_tables(rows, SWA_HEAD_DIM), 0)
    cos_r, sin_r = _stream_tables(*_rope_tables(rows, MLA_ROPE), MLA_NOPE)
    mod_e = _mod_table(c, c_ctx, e_mod_w[0], e_mod_b[0])
    xa = _even_layer(x, ctx, mod_e, e_norms[0], e_w_in[0], e_conv_w[0], e_a_log[0], e_dt_bias[0], e_gdn_norm[0],
                     e_sink[0], e_w_out[0], e_ffn_gate[0], e_ffn_up[0], e_ffn_down[0], cos64, sin64, n_lat_tiles)
    mod_o = _mod_table(c, c_ctx, o_mod_w[0], o_mod_b[0])
    lam_init = 0.8 - 0.6 * math.exp(-0.3 * 1)
    return _odd_layer_last(xa, mod_o, o_norms[0], o_w_in[0], o_q_norm[0], o_kv_norm[0], o_w_uq[0], o_w_ukv[0],
                           o_lambda[0], lam_init, o_subln[0], o_w_out[0], o_router[0], o_exp_gate[0],
                           o_exp_up[0], o_exp_down[0], cos64, sin64, cos_r, sin_r, n_lat_tiles)
```

```python
import functools
import math

import jax
import jax.numpy as jnp
from jax import lax
from jax.experimental import pallas as pl
from jax.experimental.pallas import tpu as pltpu

F32 = jnp.float32
BF16 = jnp.bfloat16

D_MODEL = 1024
CTX_LEN = 256
GRID_W = 64
NORM_EPS = 1e-6
ROPE_THETA = 10000.0

GDN_HEADS = 8
GDN_DK = 64
GDN_DV = 64
GDN_CHUNK = 64
CONV_K = 5
GDN_GROUP = 4
GDN_LANES = GDN_GROUP * GDN_DK
GDN_HD = GDN_HEADS * GDN_DK
GDN_NB = 4

SWA_Q_HEADS = 8
SWA_KV_HEADS = 2
SWA_HEAD_DIM = 64
SWA_WINDOW = 128
SWA_BLOCK = 128

MLA_HEADS = 8
MLA_Q_RANK = 384
MLA_KV_RANK = 256
MLA_NOPE = 64
MLA_ROPE = 32
MLA_V = 64

DIFF_HEADS = 4
DIFF_HEAD_DIM = 64

D_FF = 2816
N_EXPERTS = 8
TOP_K = 2
D_FF_EXPERT = 3584

LANES = 128
SUBLANES = 8
ROW_TILE = 256
LAT_TILE = 512
SWA_HEADS_PER_SUB = 1
ATTN_TQ = 2048
ATTN_SUB = 128
VMEM_LIMIT = 56 * 1024 * 1024
NEG_BIG = -1e30


def _dot(a, b):
    return jnp.dot(a, b, preferred_element_type=F32)


def _dot_nt(a, b):
    return lax.dot_general(a, b, (((1,), (1,)), ((), ())), preferred_element_type=F32)


def _dot_tn(a, b):
    return lax.dot_general(a, b, (((0,), (0,)), ((), ())), preferred_element_type=F32)


def _split_bf16(x):
    hi = x.astype(BF16)
    lo = (x - hi.astype(F32)).astype(BF16)
    return hi, lo


def _dot_split(x, w):
    hi, lo = _split_bf16(x)
    return _dot(hi, w) + _dot(lo, w)


def _dot_rsplit(w, x):
    hi, lo = _split_bf16(x)
    return _dot(w, hi) + _dot(w, lo)


def _rms(x, w):
    return x * lax.rsqrt(jnp.mean(x * x, axis=-1, keepdims=True) + NORM_EPS) * w


def _silu(x):
    return x * jax.nn.sigmoid(x)


def _rope_slab(x, cos, sin, quarter):
    lane = lax.broadcasted_iota(jnp.int32, x.shape, 1)
    fwd = pltpu.roll(x, quarter, 1)
    back = pltpu.roll(x, LANES - quarter, 1)
    rot = jnp.where(lane % (2 * quarter) < quarter, -back, fwd)
    return x * cos + rot * sin


def _params(*sem):
    return pltpu.CompilerParams(dimension_semantics=sem, vmem_limit_bytes=VMEM_LIMIT)


def _row_spec(width, tile=ROW_TILE):
    return pl.BlockSpec((1, tile, width), lambda b, t: (b, t, 0))


def _mod_spec(n_lat_tiles):
    return pl.BlockSpec((1, 1, 6, D_MODEL), lambda b, t: (b, jnp.where(t < n_lat_tiles, 1, 0), 0, 0))


def _const_spec(shape):
    return pl.BlockSpec(shape, lambda b, t: (0,) * len(shape))


def _table_spec():
    return pl.BlockSpec((ROW_TILE, LANES), lambda b, t: (t, 0))


E_QKV = (0, 3 * GDN_HD)
E_Z = (E_QKV[1], E_QKV[1] + GDN_HD)
E_GATE = (E_Z[1], E_Z[1] + LANES)
E_Q = (E_GATE[1], E_GATE[1] + SWA_Q_HEADS * LANES)
E_K = (E_Q[1], E_Q[1] + LANES)
E_V = (E_K[1], E_K[1] + LANES)


def _stream_in_specs(n_lat_tiles):
    assert CTX_LEN == ROW_TILE
    return [pl.BlockSpec((1, ROW_TILE, D_MODEL), lambda b, t: (b, jnp.minimum(t, n_lat_tiles - 1), 0)),
            pl.BlockSpec((1, ROW_TILE, D_MODEL), lambda b, t: (b, 0, 0))]


def _stream_tile(x_ref, c_ref, n_lat_tiles):
    return jnp.where(pl.program_id(1) < n_lat_tiles, x_ref[0], c_ref[0])


def _even_proj_kernel(x_ref, c_ref, mod_ref, nw_ref, w_ref, cos_ref, sin_ref,
                      qkv_ref, z_ref, gate_ref, q_ref, k_ref, v_ref, *, n_lat_tiles):
    xin = _stream_tile(x_ref, c_ref, n_lat_tiles)
    a = (_rms(xin, nw_ref[...]) * (1.0 + mod_ref[0, 0, 1:2, :]) + mod_ref[0, 0, 0:1, :]).astype(BF16)

    def proj(cols):
        return _dot(a, w_ref[:, cols[0]:cols[1]])

    qkv_ref[0] = proj(E_QKV)
    z_ref[0] = proj(E_Z).astype(BF16)
    gate_ref[0] = proj(E_GATE)
    cos = cos_ref[...]
    sin = sin_ref[...]
    quarter = SWA_HEAD_DIM // 4
    qp = proj(E_Q)
    for h in range(SWA_Q_HEADS):
        ls = slice(h * LANES, (h + 1) * LANES)
        q_ref[0, :, ls] = _rope_slab(qp[:, ls], cos, sin, quarter).astype(BF16)
    k_ref[0] = _rope_slab(proj(E_K), cos, sin, quarter).astype(BF16)
    v_ref[0] = proj(E_V).astype(BF16)


def _even_proj(x, ctx, mod, norm_w, w, cos, sin, n_lat_tiles):
    B, T, D = x.shape
    S = T + ctx.shape[1]
    widths = (E_QKV[1] - E_QKV[0], GDN_HD, LANES, SWA_Q_HEADS * LANES, LANES, LANES)
    dtypes = (F32, BF16, F32, BF16, BF16, BF16)
    return pl.pallas_call(
        functools.partial(_even_proj_kernel, n_lat_tiles=n_lat_tiles),
        grid=(B, S // ROW_TILE),
        in_specs=_stream_in_specs(n_lat_tiles) + [_mod_spec(n_lat_tiles), _const_spec((1, D)), _const_spec(w.shape),
                                                  _table_spec(), _table_spec()],
        out_specs=[_row_spec(n) for n in widths],
        out_shape=[jax.ShapeDtypeStruct((B, S, n), dt) for n, dt in zip(widths, dtypes)],
        compiler_params=_params("parallel", "parallel"),
        name="even_proj",
    )(x, ctx, mod, norm_w, w, cos, sin)


def _gdn_prep_kernel(x_ref, prev_ref, next_ref, cw_ref, bd_ref, q_ref, k_ref, v_ref, *, n_lat_tiles):
    t = pl.program_id(1)
    has_prev = jnp.logical_and(t > 0, t < n_lat_tiles)
    has_next = t < n_lat_tiles - 1
    prev = jnp.where(has_prev, prev_ref[0], 0.0)
    nxt = jnp.where(has_next, next_ref[0], 0.0)
    xe = jnp.concatenate([prev, x_ref[0], nxt], axis=0)
    pad = CONV_K // 2
    y = jnp.zeros(x_ref.shape[1:], F32)
    for kk in range(CONV_K):
        lo = SUBLANES + kk - pad
        y = y + cw_ref[kk:kk + 1, :] * xe[lo:lo + ROW_TILE, :]
    u = _silu(y)

    def headnorm(a):
        return a * lax.rsqrt(_dot_split(a * a, bd_ref[...]) + NORM_EPS)

    q_ref[0] = headnorm(u[:, :GDN_HD]) * GDN_DK ** -0.5
    k_ref[0] = headnorm(u[:, GDN_HD:2 * GDN_HD])
    v_ref[0] = u[:, 2 * GDN_HD:]


def _gdn_prep(qkv, conv_w, head_ones, n_lat_tiles):
    B, S, W = qkv.shape
    per_tile = ROW_TILE // SUBLANES
    n8 = S // SUBLANES
    return pl.pallas_call(
        functools.partial(_gdn_prep_kernel, n_lat_tiles=n_lat_tiles),
        grid=(B, S // ROW_TILE),
        in_specs=[_row_spec(W),
                  pl.BlockSpec((1, SUBLANES, W), lambda b, t: (b, jnp.maximum(t * per_tile - 1, 0), 0)),
                  pl.BlockSpec((1, SUBLANES, W), lambda b, t: (b, jnp.minimum((t + 1) * per_tile, n8 - 1), 0)),
                  _const_spec(conv_w.shape), _const_spec(head_ones.shape)],
        out_specs=[_row_spec(GDN_HD)] * 3,
        out_shape=[jax.ShapeDtypeStruct((B, S, GDN_HD), F32)] * 3,
        compiler_params=_params("parallel", "parallel"),
        name="gdn_prep",
    )(qkv, qkv, qkv, conv_w, head_ones)


def _block_diag(x, head_of_lane):
    zero = jnp.zeros_like(x)
    return jnp.concatenate([jnp.where(head_of_lane == h, x, zero) for h in range(GDN_GROUP)], axis=0)


def _expand_heads(x4, head_of_lane):
    c = x4.shape[0]
    out = jnp.broadcast_to(x4[:, GDN_GROUP - 1:GDN_GROUP], (c, GDN_LANES))
    for h in range(GDN_GROUP - 2, -1, -1):
        out = jnp.where(head_of_lane == h, jnp.broadcast_to(x4[:, h:h + 1], (c, GDN_LANES)), out)
    return out


def _gdn_groups(items):
    C = GDN_CHUNK
    row = lax.broadcasted_iota(jnp.int32, (C, GDN_LANES), 0)
    lane = lax.broadcasted_iota(jnp.int32, (C, GDN_LANES), 1)
    hol = lane // GDN_DK
    col = lane % GDN_DK
    eye = jnp.where(row == col, 1.0, 0.0)
    rr = lax.broadcasted_iota(jnp.int32, (2 * C, 4 * C), 0)
    cc = lax.broadcasted_iota(jnp.int32, (2 * C, 4 * C), 1)
    cum_lhs = {}
    for rev in (False, True):
        before = (rr - cc % C) * (-1 if rev else 1) >= 0
        cum_lhs[rev] = jnp.where(((rr < C) & (cc < 2 * C) & before) | ((rr >= C) & (cc >= 2 * C)),
                                 1.0, 0.0).astype(BF16)

    def bd(x):
        return _block_diag(x, hol)

    st = []
    for (q4, k4, v4, g4, b4, s4, reverse) in items:
        sgn = -1 if reverse else 1
        ahead = (row - col) * sgn
        incl = ahead >= 0
        strict = ahead > 0
        gE = _expand_heads(g4, hol)
        bE = _expand_heads(b4, hol)
        g_hi, g_lo = _split_bf16(gE)
        m_hi, m_lo = _split_bf16(jnp.where((col - row) * sgn >= 0, gE, 0.0))
        both = _dot(cum_lhs[reverse], jnp.concatenate([g_hi, g_lo, m_hi, m_lo], axis=0))
        gc = both[:C]
        gc_row = both[C:]
        decay = jnp.where(incl, jnp.exp(gc - gc_row), 0.0)
        g_last = gc[0:1, :] if reverse else gc[C - 1:C, :]
        eg = jnp.exp(gc)
        kb = k4 * bE
        st.append(dict(q=q4, k=k4, s=s4, incl=incl, strict=strict, decay=decay, g_last=g_last,
                       kb=kb, vb=v4 * bE, kbg=kb * eg, qh=q4 * eg, kt=k4 * jnp.exp(g_last - gc)))

    for d in st:
        gram = _dot_nt(jnp.concatenate([d["kb"], d["q"]], axis=0).astype(BF16), bd(d["k"].astype(BF16)))
        L = jnp.where(d["strict"], gram[:C] * d["decay"], 0.0)
        d["A"] = gram[C:] * d["decay"]
        d["T"] = eye - L
        d["P"] = L.astype(BF16)
    n_factors = int(math.log2(C)) - 1
    for d in st:
        d["P"] = _dot(d["P"], bd(d["P"])).astype(BF16)
    for it in range(n_factors):
        last = it == n_factors - 1
        for d in st:
            lhs = d["T"].astype(BF16) if last else jnp.concatenate([d["T"].astype(BF16), d["P"]], axis=0)
            prod = _dot(lhs, bd(d["P"]))
            d["T"] = d["T"] + prod[:C]
            if not last:
                d["P"] = prod[C:].astype(BF16)
    for d in st:
        Tb = d["T"].astype(BF16)
        d["u"] = _dot(Tb, bd(d["vb"].astype(BF16)))
        d["w"] = _dot(Tb, bd(d["kbg"].astype(BF16)))
    for d in st:
        d["ws_qs"] = _dot(jnp.concatenate([d["w"], d["qh"]], axis=0).astype(BF16), bd(d["s"].astype(BF16)))
    out = []
    for d in st:
        v_new = d["u"] - d["ws_qs"][:C]
        vb16 = v_new.astype(BF16)
        o = d["ws_qs"][C:] + _dot(d["A"].astype(BF16), bd(vb16))
        full = _dot_tn(d["kt"].astype(BF16), vb16)
        upd = jnp.zeros((GDN_DK, GDN_LANES), F32)
        for h in range(GDN_GROUP):
            upd = upd + jnp.where(hol == h, full[h * GDN_DK:(h + 1) * GDN_DK, :], 0.0)
        out.append((o, d["s"] * jnp.exp(d["g_last"]) + upd))
    return out


def _gdn_kernel(qf_ref, kf_ref, vf_ref, gf_ref, qb_ref, kb_ref, vb_ref, gb_ref, al_ref, dtb_ref,
                of_ref, ob_ref, s_ref):
    @pl.when(pl.program_id(1) == 0)
    def _():
        s_ref[...] = jnp.zeros_like(s_ref)

    n_grp = GDN_HEADS // GDN_GROUP
    n_gate = 2 * GDN_HEADS
    items = []
    slots = []
    for nb in range(GDN_NB):
        for d, (q_ref, k_ref, v_ref, g_ref, o_ref) in enumerate(((qf_ref, kf_ref, vf_ref, gf_ref, of_ref),
                                                                 (qb_ref, kb_ref, vb_ref, gb_ref, ob_ref))):
            raw = g_ref[nb]
            xs = raw + dtb_ref[...]
            softplus = jnp.maximum(xs, 0.0) + jnp.log(1.0 + jnp.exp(-jnp.abs(xs)))
            g = -jnp.exp(al_ref[...]) * softplus
            beta = jax.nn.sigmoid(raw)
            for grp in range(n_grp):
                ls = slice(grp * GDN_LANES, (grp + 1) * GDN_LANES)
                c0 = d * GDN_HEADS + grp * GDN_GROUP
                slot = (nb * 2 + d) * n_grp + grp
                items.append((q_ref[nb, :, ls], k_ref[nb, :, ls], v_ref[nb, :, ls],
                              g[:, c0:c0 + GDN_GROUP], beta[:, n_gate + c0:n_gate + c0 + GDN_GROUP],
                              s_ref[slot], d == 1))
                slots.append((o_ref, nb, ls, slot))
    for (o_ref, nb, ls, slot), (o, s_new) in zip(slots, _gdn_groups(items)):
        o_ref[nb, :, ls] = o
        s_ref[slot] = s_new


def _gdn(q, k, v, gate, a_log_row, dt_bias_row, n_lat_chunks):
    B, S, HD = q.shape
    n_chunks = S // GDN_CHUNK
    n_ctx = n_chunks - n_lat_chunks

    def fwd_c(s):
        return jnp.where(s < n_ctx, n_lat_chunks + s, s - n_ctx)

    def bwd_c(s):
        return n_chunks - 1 - s

    def spec(cmap, width):
        return pl.BlockSpec((GDN_NB, GDN_CHUNK, width), lambda b, s: (b, cmap(s), 0))

    return pl.pallas_call(
        _gdn_kernel,
        grid=(B // GDN_NB, n_chunks),
        in_specs=[spec(fwd_c, HD), spec(fwd_c, HD), spec(fwd_c, HD), spec(fwd_c, LANES),
                  spec(bwd_c, HD), spec(bwd_c, HD), spec(bwd_c, HD), spec(bwd_c, LANES),
                  _const_spec((1, LANES)), _const_spec((1, LANES))],
        out_specs=[spec(fwd_c, HD), spec(bwd_c, HD)],
        out_shape=[jax.ShapeDtypeStruct((B, S, HD), F32)] * 2,
        scratch_shapes=[pltpu.VMEM((GDN_NB * 2 * GDN_HEADS // GDN_GROUP, GDN_DK, GDN_LANES), F32)],
        compiler_params=_params("parallel", "arbitrary"),
        name="gdn_scan",
    )(q, k, v, gate, q, k, v, gate, a_log_row, dt_bias_row)


def _swa_kernel(q_ref, k_ref, v_ref, sink_ref, o_ref, *, n_lat_blocks):
    W = SWA_BLOCK
    H = SWA_Q_HEADS
    n = pl.program_id(1)
    is_lat = n < n_lat_blocks
    nl = jnp.minimum(n, n_lat_blocks - 1)
    prev = jnp.maximum(nl - 1, 0)
    nxt = jnp.minimum(nl + 1, n_lat_blocks - 1)
    ctx0 = n_lat_blocks * W

    def keys(ref):
        def blk(i):
            return ref[0, pl.ds(pl.multiple_of(i * W, W), W), :]
        return jnp.concatenate([ref[0, ctx0:ctx0 + CTX_LEN, :], blk(prev), blk(nl), blk(nxt)], axis=0)

    k_all = keys(k_ref)
    v_all = keys(v_ref)
    nk = CTX_LEN + 3 * W
    v_ext = jnp.concatenate([v_all, jnp.ones((nk, LANES), BF16)], axis=1)
    hp = SWA_HEADS_PER_SUB
    ii = lax.broadcasted_iota(jnp.int32, (hp * W, nk), 0) % W
    jj = lax.broadcasted_iota(jnp.int32, (hp * W, nk), 1) - CTX_LEN
    in_window = (((jj >= 0) & (jj < W) & (jj >= ii) & (nl > 0))
                 | ((jj >= W) & (jj < 2 * W))
                 | ((jj >= 2 * W) & (jj - 2 * W <= ii) & (nl < n_lat_blocks - 1)))
    valid = (jj < 0) | (in_window & is_lat)

    def scores(i):
        q = jnp.concatenate([q_ref[0, :, h * LANES:(h + 1) * LANES] for h in range(i * hp, (i + 1) * hp)], axis=0)
        return jnp.where(valid, _dot_nt(q, k_all), NEG_BIG)

    def finish(i, s):
        sk = sink_ref[i * hp * W:(i + 1) * hp * W, :]
        m = jnp.maximum(jnp.max(s, axis=-1, keepdims=True), sk)
        acc = _dot(jnp.exp(s - m).astype(BF16), v_ext)
        o = acc[:, :LANES] / (acc[:, LANES:LANES + 1] + jnp.exp(sk - m))
        for j in range(hp):
            h = i * hp + j
            o_ref[0, :, h * LANES:(h + 1) * LANES] = o[j * W:(j + 1) * W].astype(o_ref.dtype)

    _one_ahead(H // hp, scores, finish)


def _swa(q_pad, k, v, sink_rows, n_lat_blocks):
    B, S, _ = k.shape
    W = SWA_BLOCK
    QW = SWA_Q_HEADS * LANES
    return pl.pallas_call(
        functools.partial(_swa_kernel, n_lat_blocks=n_lat_blocks),
        grid=(B, S // W),
        in_specs=[pl.BlockSpec((1, W, QW), lambda b, n: (b, n, 0)),
                  pl.BlockSpec((1, S, LANES), lambda b, n: (b, 0, 0)),
                  pl.BlockSpec((1, S, LANES), lambda b, n: (b, 0, 0)),
                  _const_spec(sink_rows.shape)],
        out_specs=pl.BlockSpec((1, W, QW), lambda b, n: (b, n, 0)),
        out_shape=jax.ShapeDtypeStruct((B, S, QW), BF16),
        compiler_params=_params("parallel", "parallel"),
        name="swa",
    )(q_pad, k, v, sink_rows)


def _mix_out0_kernel(of_ref, ob_ref, z_ref, swa_ref, x_ref, c_ref, mod_ref, nw_ref, gn_ref, bd_ref, wg_ref, ws_ref,
                     o_ref, *, n_lat_tiles):
    o = of_ref[0] + ob_ref[0]
    ms = _dot_split(o * o, bd_ref[...]) * (1.0 / GDN_DV)
    gdn = o * lax.rsqrt(ms + NORM_EPS) * gn_ref[...] * _silu(z_ref[0].astype(F32))
    y = _dot(gdn.astype(BF16), wg_ref[...]) + _dot(swa_ref[0], ws_ref[...])
    o_ref[0] = _stream_tile(x_ref, c_ref, n_lat_tiles) + mod_ref[0, 0, 2:3, :] * _rms(y, nw_ref[...])


def _mix_out0(o_f, o_b, z, swa, x, ctx, mod, norm_w, gdn_norm_row, head_ones, w_gdn, w_swa, n_lat_tiles):
    B, S, _ = o_f.shape
    D = x.shape[-1]
    return pl.pallas_call(
        functools.partial(_mix_out0_kernel, n_lat_tiles=n_lat_tiles),
        grid=(B, S // ROW_TILE),
        in_specs=[_row_spec(GDN_HD), _row_spec(GDN_HD), _row_spec(GDN_HD), _row_spec(swa.shape[-1])]
        + _stream_in_specs(n_lat_tiles)
        + [_mod_spec(n_lat_tiles), _const_spec((1, D)), _const_spec(gdn_norm_row.shape),
           _const_spec(head_ones.shape), _const_spec(w_gdn.shape), _const_spec(w_swa.shape)],
        out_specs=_row_spec(D),
        out_shape=jax.ShapeDtypeStruct((B, S, D), F32),
        compiler_params=_params("parallel", "parallel"),
        name="mix_out0",
    )(o_f, o_b, z, swa, x, ctx, mod, norm_w, gdn_norm_row, head_ones, w_gdn, w_swa)


def _ffn_kernel(x_ref, mod_ref, n_in_ref, n_out_ref, wg_ref, wu_ref, wd_ref, o_ref):
    x = x_ref[0]
    f = (_rms(x, n_in_ref[...]) * (1.0 + mod_ref[0, 0, 4:5, :]) + mod_ref[0, 0, 3:4, :]).astype(BF16)
    h = _silu(_dot(f, wg_ref[...])) * _dot(f, wu_ref[...])
    y = _dot(h.astype(BF16), wd_ref[...])
    o_ref[0] = x + mod_ref[0, 0, 5:6, :] * _rms(y, n_out_ref[...])


def _ffn(xa, mod, n_in, n_out, wg, wu, wd, n_lat_tiles):
    B, S, D = xa.shape
    return pl.pallas_call(
        _ffn_kernel,
        grid=(B, S // ROW_TILE),
        in_specs=[_row_spec(D), _mod_spec(n_lat_tiles), _const_spec((1, D)), _const_spec((1, D)),
                  _const_spec(wg.shape), _const_spec(wu.shape), _const_spec(wd.shape)],
        out_specs=_row_spec(D),
        out_shape=jax.ShapeDtypeStruct((B, S, D), F32),
        compiler_params=_params("parallel", "parallel"),
        name="dense_ffn",
    )(xa, mod, n_in, n_out, wg, wu, wd)


O_CQ = (0, MLA_Q_RANK)
O_CKV = (O_CQ[1], O_CQ[1] + MLA_KV_RANK)
O_KR = (O_CKV[1], O_CKV[1] + LANES)
O_DQ = (O_KR[1], O_KR[1] + DIFF_HEADS * LANES)
O_DK = (O_DQ[1], O_DQ[1] + DIFF_HEADS * LANES)
O_DV = (O_DK[1], O_DK[1] + DIFF_HEADS * LANES)


def _odd_proj_kernel(x_ref, mod_ref, nw_ref, w_ref, cos_ref, sin_ref, cosr_ref, sinr_ref,
                     qn_ref, kvn_ref, wq_ref, wk_ref, wv_ref,
                     q_ref, k_ref, v_ref, dq_ref, dk_ref, dv_ref):
    a = (_rms(x_ref[0], nw_ref[...]) * (1.0 + mod_ref[0, 0, 1:2, :]) + mod_ref[0, 0, 0:1, :]).astype(BF16)

    def proj(cols):
        return _dot(a, w_ref[:, cols[0]:cols[1]])

    cos_r = cosr_ref[...]
    sin_r = sinr_ref[...]
    rq = MLA_ROPE // 4
    cq = _rms(proj(O_CQ), qn_ref[...]).astype(BF16)
    ckv = _rms(proj(O_CKV), kvn_ref[...]).astype(BF16)
    kr = _rope_slab(proj(O_KR), cos_r, sin_r, rq)
    qp = _dot(cq, wq_ref[...])
    kp = _dot(ckv, wk_ref[...])
    vp = _dot(ckv, wv_ref[...])
    lane = lax.broadcasted_iota(jnp.int32, (ROW_TILE, LANES), 1)
    ones_hi = jnp.where(lane >= MLA_V, 1.0, 0.0)
    for h in range(MLA_HEADS):
        ls = slice(h * LANES, (h + 1) * LANES)
        q_ref[0, :, ls] = _rope_slab(qp[:, ls], cos_r, sin_r, rq).astype(BF16)
        k_ref[0, :, ls] = (kp[:, ls] + kr).astype(BF16)
        v_ref[0, :, ls] = (vp[:, ls] + ones_hi).astype(BF16)
    cos = cos_ref[...]
    sin = sin_ref[...]
    quarter = DIFF_HEAD_DIM // 4
    dq = proj(O_DQ)
    dk = proj(O_DK)
    dv = proj(O_DV)
    ones = jnp.ones((ROW_TILE, LANES), BF16)
    for h in range(DIFF_HEADS):
        ls = slice(h * LANES, (h + 1) * LANES)
        dq_ref[0, :, ls] = _rope_slab(dq[:, ls], cos, sin, quarter).astype(BF16)
        dk_ref[0, :, ls] = _rope_slab(dk[:, ls], cos, sin, quarter).astype(BF16)
        dv_ref[0, :, 2 * h * LANES:(2 * h + 1) * LANES] = dv[:, ls].astype(BF16)
        dv_ref[0, :, (2 * h + 1) * LANES:(2 * h + 2) * LANES] = ones


def _odd_proj(xa, mod, norm_w, w, cos, sin, cos_r, sin_r, q_norm, kv_norm, wq, wk, wv, n_lat_tiles):
    B, S, D = xa.shape
    HW = MLA_HEADS * LANES
    widths = (HW, HW, HW, DIFF_HEADS * LANES, DIFF_HEADS * LANES, 2 * DIFF_HEADS * LANES)
    return pl.pallas_call(
        _odd_proj_kernel,
        grid=(B, S // ROW_TILE),
        in_specs=[_row_spec(D), _mod_spec(n_lat_tiles), _const_spec((1, D)), _const_spec(w.shape),
                  _table_spec(), _table_spec(), _table_spec(), _table_spec(),
                  _const_spec(q_norm.shape), _const_spec(kv_norm.shape),
                  _const_spec(wq.shape), _const_spec(wk.shape), _const_spec(wv.shape)],
        out_specs=[_row_spec(n) for n in widths],
        out_shape=[jax.ShapeDtypeStruct((B, S, n), BF16) for n in widths],
        compiler_params=_params("parallel", "parallel"),
        name="odd_proj",
    )(xa, mod, norm_w, w, cos, sin, cos_r, sin_r, q_norm, kv_norm, wq, wk, wv)


def _one_ahead(n_sub, scores, finish):
    s_prev = scores(0)
    for i in range(1, n_sub):
        s_next = scores(i)
        finish(i - 1, s_prev)
        s_prev = s_next
    finish(n_sub - 1, s_prev)


def _mla_kernel(q_ref, k_ref, v_ref, o_ref):
    k = k_ref[0]
    v = v_ref[0]

    def scores(i):
        return _dot_nt(q_ref[0, i * ATTN_SUB:(i + 1) * ATTN_SUB, :], k)

    def finish(i, s):
        m = jnp.max(s, axis=-1, keepdims=True)
        acc = _dot(jnp.exp(s - m).astype(BF16), v)
        o_ref[0, i * ATTN_SUB:(i + 1) * ATTN_SUB, :] = (acc / acc[:, MLA_V:MLA_V + 1]).astype(o_ref.dtype)

    _one_ahead(q_ref.shape[1] // ATTN_SUB, scores, finish)


def _mla(q, k, v_ext, T):
    B, S, _ = k.shape
    return pl.pallas_call(
        _mla_kernel,
        grid=(B, MLA_HEADS, T // ATTN_TQ),
        in_specs=[pl.BlockSpec((1, ATTN_TQ, LANES), lambda b, h, t: (b, t, h)),
                  pl.BlockSpec((1, S, LANES), lambda b, h, t: (b, 0, h)),
                  pl.BlockSpec((1, S, LANES), lambda b, h, t: (b, 0, h))],
        out_specs=pl.BlockSpec((1, ATTN_TQ, LANES), lambda b, h, t: (b, t, h)),
        out_shape=jax.ShapeDtypeStruct((B, T, MLA_HEADS * LANES), BF16),
        compiler_params=_params("parallel", "parallel", "parallel"),
        name="mla_attn",
    )(q, k, v_ext)


def _diff_kernel(q_ref, k_ref, v_ref, lam_ref, sub_ref, o_ref, *, post_scale):
    k = k_ref[0]
    v = v_ref[0]
    sub = ATTN_SUB
    vw = 2 * DIFF_HEAD_DIM
    lo = lax.broadcasted_iota(jnp.int32, (sub, LANES), 1) < DIFF_HEAD_DIM
    zero = jnp.zeros((sub, LANES), BF16)

    def scores(i):
        q = q_ref[0, i * sub:(i + 1) * sub, :]
        return _dot_nt(jnp.concatenate([jnp.where(lo, q, zero), jnp.where(lo, zero, q)], axis=0), k)

    def finish(i, s):
        m = jnp.max(s, axis=-1, keepdims=True)
        acc = _dot(jnp.exp(s - m).astype(BF16), v)
        att = acc[:, :vw] / acc[:, vw:vw + 1]
        a = att[:sub] - lam_ref[...] * att[sub:]
        o_ref[0, i * sub:(i + 1) * sub, :] = (_rms(a, sub_ref[...]) * post_scale).astype(o_ref.dtype)

    _one_ahead(q_ref.shape[1] // sub, scores, finish)


def _diff(q, k, v_ext, lam, subln, post_scale, T):
    B, S, _ = k.shape
    tq = ATTN_TQ // 2
    return pl.pallas_call(
        functools.partial(_diff_kernel, post_scale=post_scale),
        grid=(B, DIFF_HEADS, T // tq),
        in_specs=[pl.BlockSpec((1, tq, LANES), lambda b, h, t: (b, t, h)),
                  pl.BlockSpec((1, S, LANES), lambda b, h, t: (b, 0, h)),
                  pl.BlockSpec((1, S, 2 * LANES), lambda b, h, t: (b, 0, h)),
                  pl.BlockSpec((1, 1), lambda b, h, t: (0, 0)),
                  pl.BlockSpec((1, LANES), lambda b, h, t: (0, 0))],
        out_specs=pl.BlockSpec((1, tq, LANES), lambda b, h, t: (b, t, h)),
        out_shape=jax.ShapeDtypeStruct((B, T, DIFF_HEADS * LANES), BF16),
        compiler_params=_params("parallel", "parallel", "parallel"),
        name="diff_attn",
    )(q, k, v_ext, lam, subln)


def _mix_out1_kernel(mla_ref, diff_ref, x_ref, mod_ref, n1_ref, n2_ref, w1_ref, w2_ref, wrh_ref, wrl_ref,
                     xo_ref, f_ref, lg_ref):
    y = _dot(mla_ref[0], w1_ref[...]) + _dot(diff_ref[0], w2_ref[...])
    x = x_ref[0] + mod_ref[0, 0, 2:3, :] * _rms(y, n1_ref[...])
    xo_ref[0] = x
    f = _rms(x, n2_ref[...]) * (1.0 + mod_ref[0, 0, 4:5, :]) + mod_ref[0, 0, 3:4, :]
    f_ref[0] = f
    f_hi, f_lo = _split_bf16(f)
    lg_ref[0] = _dot(f_hi, wrh_ref[...]) + (_dot(f_hi, wrl_ref[...]) + _dot(f_lo, wrh_ref[...]))


def _mix_out1(mla, diff, xa, mod, n1, n2, w1, w2, wr, n_lat_tiles):
    B, T, _ = mla.shape
    D = xa.shape[-1]
    wr_hi, wr_lo = _split_bf16(wr)
    return pl.pallas_call(
        _mix_out1_kernel,
        grid=(B, T // ROW_TILE),
        in_specs=[_row_spec(mla.shape[-1]), _row_spec(diff.shape[-1]), _row_spec(D), _mod_spec(n_lat_tiles),
                  _const_spec((1, D)), _const_spec((1, D)), _const_spec(w1.shape), _const_spec(w2.shape),
                  _const_spec(wr.shape), _const_spec(wr.shape)],
        out_specs=[_row_spec(D), _row_spec(D), _row_spec(LANES)],
        out_shape=[jax.ShapeDtypeStruct((B, T, D), F32), jax.ShapeDtypeStruct((B, T, D), F32),
                   jax.ShapeDtypeStruct((B, T, LANES), F32)],
        compiler_params=_params("parallel", "parallel"),
        name="mix_out1",
    )(mla, diff, xa, mod, n1, n2, w1, w2, wr_hi, wr_lo)


MOE_TM = 1024
MOE_TF = 512
MOE_SUB = 256


def _moe_kernel(te_ref, tr_ref, x_ref, rw_ref, wg_ref, wu_ref, wd_ref, o_ref, acc_ref):
    i = pl.program_id(0)
    f = pl.program_id(1)
    n_rows = tr_ref[i]
    n_sub = MOE_TM // MOE_SUB

    @pl.when(f == 0)
    def _():
        acc_ref[...] = jnp.zeros_like(acc_ref)

    def sub_tile_fns():
        wg = wg_ref[0].astype(BF16)
        wu = wu_ref[0].astype(BF16)
        wd = wd_ref[0].astype(BF16)

        def gate_up(j):
            x = x_ref[j * MOE_SUB:(j + 1) * MOE_SUB, :].astype(BF16)
            return _dot(x, wg), _dot(x, wu)

        def down(j, gu):
            h = (_silu(gu[0]) * gu[1]).astype(BF16)
            acc_ref[j * MOE_SUB:(j + 1) * MOE_SUB, :] += _dot(h, wd)

        return gate_up, down

    all_subs = n_rows > (n_sub - 1) * MOE_SUB

    @pl.when(all_subs)
    def _():
        _one_ahead(n_sub, *sub_tile_fns())

    for j in range(n_sub - 1):
        @pl.when(jnp.logical_and(jnp.logical_not(all_subs), n_rows > j * MOE_SUB))
        def _(j=j):
            gate_up, down = sub_tile_fns()
            down(j, gate_up(j))

    @pl.when(f == pl.num_programs(1) - 1)
    def _():
        o_ref[...] = acc_ref[...] * rw_ref[:, 0:1]


def _moe_experts(tile_expert, tile_rows, x_sorted, row_w, wg, wu, wd):
    R, D = x_sorted.shape
    E, _, F = wg.shape
    nf = F // MOE_TF

    def f_eff(i, f, ta):
        return jnp.where(ta[i] > 0, f, nf - 1)

    grid_spec = pltpu.PrefetchScalarGridSpec(
        num_scalar_prefetch=2,
        grid=(R // MOE_TM, nf),
        in_specs=[pl.BlockSpec((MOE_TM, D), lambda i, f, te, ta: (i, 0)),
                  pl.BlockSpec((MOE_TM, LANES), lambda i, f, te, ta: (i, 0)),
                  pl.BlockSpec((1, D, MOE_TF), lambda i, f, te, ta: (te[i], 0, f_eff(i, f, ta))),
                  pl.BlockSpec((1, D, MOE_TF), lambda i, f, te, ta: (te[i], 0, f_eff(i, f, ta))),
                  pl.BlockSpec((1, MOE_TF, D), lambda i, f, te, ta: (te[i], f_eff(i, f, ta), 0))],
        out_specs=pl.BlockSpec((MOE_TM, D), lambda i, f, te, ta: (i, 0)),
        scratch_shapes=[pltpu.VMEM((MOE_TM, D), F32)],
    )
    return pl.pallas_call(
        _moe_kernel,
        grid_spec=grid_spec,
        out_shape=jax.ShapeDtypeStruct((R, D), F32),
        compiler_params=_params("arbitrary", "arbitrary"),
        name="moe_experts",
    )(tile_expert, tile_rows, x_sorted, row_w, wg, wu, wd)


def _norm_residual_kernel(a_ref, b_ref, x_ref, mod_ref, nw_ref, o_ref):
    y = a_ref[0] + b_ref[0]
    o_ref[0] = x_ref[0] + mod_ref[0, 0, 5:6, :] * _rms(y, nw_ref[...])


def _norm_residual(a, b, x, mod, norm_w, n_lat_tiles):
    B, T, D = x.shape
    return pl.pallas_call(
        _norm_residual_kernel,
        grid=(B, T // LAT_TILE),
        in_specs=[_row_spec(D, LAT_TILE), _row_spec(D, LAT_TILE), _row_spec(D, LAT_TILE), _mod_spec(n_lat_tiles),
                  _const_spec((1, D))],
        out_specs=_row_spec(D, LAT_TILE),
        out_shape=jax.ShapeDtypeStruct((B, T, D), F32),
        compiler_params=_params("parallel", "parallel"),
        name="norm_residual",
    )(a, b, x, mod, norm_w)


def _route(logits):
    N = logits.shape[0]
    M = N * TOP_K
    R = M + N_EXPERTS * MOE_TM
    top_val, top_idx = lax.top_k(logits, TOP_K)
    top_w = jax.nn.softmax(top_val, axis=-1).T.reshape(-1)
    e_flat = top_idx.T.reshape(-1)
    onehot = (e_flat[:, None] == jnp.arange(N_EXPERTS)[None, :]).astype(jnp.int32)
    counts = jnp.sum(onehot, axis=0)
    padded = ((counts + MOE_TM - 1) // MOE_TM) * MOE_TM
    ends = jnp.cumsum(padded)
    starts = ends - padded
    ustarts = jnp.cumsum(counts) - counts
    order = jnp.argsort(e_flat, stable=True).astype(jnp.int32)
    rank = jnp.argsort(order).astype(jnp.int32)
    dest = rank + jnp.sum(onehot * (starts - ustarts)[None, :], axis=-1)
    n_tiles = R // MOE_TM
    tile_start = jnp.arange(n_tiles, dtype=jnp.int32) * MOE_TM
    tile_active = (tile_start < ends[-1]).astype(jnp.int32)
    te = jnp.minimum(jnp.searchsorted(ends, tile_start, side="right"), N_EXPERTS - 1).astype(jnp.int32)
    last_e = te[jnp.maximum(ends[-1] // MOE_TM - 1, 0)]
    tile_expert = jnp.where(tile_active > 0, te, last_e)
    in_tile = jnp.arange(MOE_TM, dtype=jnp.int32)[None, :]
    within = (tile_start - starts[tile_expert])[:, None] + in_tile
    valid = (within < counts[tile_expert][:, None]) & (tile_active[:, None] > 0)
    src = order[jnp.clip(ustarts[tile_expert][:, None] + within, 0, M - 1).reshape(R)]
    valid = valid.reshape(R)
    tile_rows = jnp.clip(counts[tile_expert] - (tile_start - starts[tile_expert]), 0, MOE_TM) * tile_active
    row_token = jnp.where(valid, src % N, jnp.arange(R, dtype=jnp.int32) % N)
    row_w = jnp.broadcast_to(jnp.where(valid, top_w[src], 0.0)[:, None], (R, LANES))
    return dest.reshape(TOP_K, N), row_token, row_w, tile_expert, tile_rows.astype(jnp.int32)


def _rope_tables(rows, rot_dim):
    row = jnp.repeat(jnp.arange(rows, dtype=F32), GRID_W)
    col = jnp.tile(jnp.arange(GRID_W, dtype=F32), rows)
    half = rot_dim // 2
    inv_freq = ROPE_THETA ** (-jnp.arange(0, half, 2, dtype=F32) / half)
    ang_r = row[:, None] * inv_freq[None, :]
    ang_c = col[:, None] * inv_freq[None, :]
    ang = jnp.concatenate([ang_r, ang_r, ang_c, ang_c], axis=-1)
    return jnp.cos(ang), jnp.sin(ang)


def _stream_tables(cos, sin, lane0):
    T, R = cos.shape
    if lane0 == 0:
        cos_l, sin_l = jnp.tile(cos, (1, LANES // R)), jnp.tile(sin, (1, LANES // R))
    else:
        pad = ((0, 0), (lane0, LANES - lane0 - R))
        cos_l = jnp.pad(cos - 1.0, pad) + 1.0
        sin_l = jnp.pad(sin, pad)
    ctx = ((0, CTX_LEN), (0, 0))
    return jnp.pad(cos_l - 1.0, ctx) + 1.0, jnp.pad(sin_l, ctx)


def _mod_table(c, c_ctx, mod_w, mod_b):
    B = c.shape[0]
    hp = lax.Precision.HIGHEST
    m_l = (jnp.dot(jax.nn.silu(c), mod_w, precision=hp) + mod_b).reshape(B, 1, 6, D_MODEL)
    m_c = (jnp.dot(jax.nn.silu(c_ctx), mod_w, precision=hp) + mod_b).reshape(1, 1, 6, D_MODEL)
    return jnp.concatenate([jnp.broadcast_to(m_c, m_l.shape), m_l], axis=1)


def _pad_cols(w, n):
    return jnp.pad(w, ((0, 0), (0, n - w.shape[1])))


def _even_layer(x, ctx, mod, norms, w_in, conv_w, a_log, dt_bias, gdn_norm, sink, w_out, ffn_gate, ffn_up,
                ffn_down, cos, sin, n_lat_tiles):
    B, T, D = x.shape
    n_gate = 2 * GDN_HEADS
    c_qkv = 3 * GDN_HD
    c_z = c_qkv + GDN_HD
    c_ga = c_z + n_gate
    c_gb = c_ga + n_gate
    c_sq = c_gb + SWA_Q_HEADS * SWA_HEAD_DIM
    c_sk = c_sq + SWA_KV_HEADS * SWA_HEAD_DIM
    grp = SWA_Q_HEADS // SWA_KV_HEADS
    wq = w_in[:, c_gb:c_sq].reshape(D, SWA_Q_HEADS, SWA_HEAD_DIM) * SWA_HEAD_DIM ** -0.5
    kv_of_head = (jnp.arange(SWA_Q_HEADS) // grp)[None, :, None]
    wq = jnp.concatenate([jnp.where(kv_of_head == 0, wq, 0.0), jnp.where(kv_of_head == 1, wq, 0.0)], axis=-1)
    w_all = jnp.concatenate([w_in[:, :c_z], _pad_cols(w_in[:, c_z:c_gb], LANES),
                             wq.reshape(D, SWA_Q_HEADS * LANES), w_in[:, c_sq:]], axis=1).astype(BF16)
    qkv, z, gate, q_pad, k, v = _even_proj(x, ctx, mod, norms[0:1], w_all, cos, sin, n_lat_tiles)

    head_ones = (jnp.arange(GDN_HD)[:, None] // GDN_DK == jnp.arange(GDN_HD)[None, :] // GDN_DK).astype(BF16)
    conv_w8 = jnp.pad(conv_w, ((0, SUBLANES - CONV_K), (0, 0)))
    gq, gk, gv = _gdn_prep(qkv, conv_w8, head_ones, n_lat_tiles)
    a_log_row = _pad_cols(a_log.reshape(1, n_gate), LANES)
    dt_bias_row = _pad_cols(dt_bias.reshape(1, n_gate), LANES)
    o_f, o_b = _gdn(gq, gk, gv, gate, a_log_row, dt_bias_row, T // GDN_CHUNK)

    sink_rows = jnp.repeat(sink, SWA_BLOCK)[:, None]
    swa = _swa(q_pad, k, v, sink_rows, T // SWA_BLOCK)

    w_s = w_out[GDN_HD:].reshape(SWA_Q_HEADS, SWA_HEAD_DIM, D)
    w_s = jnp.concatenate([jnp.where(kv_of_head.reshape(-1, 1, 1) == 0, w_s, 0.0),
                           jnp.where(kv_of_head.reshape(-1, 1, 1) == 1, w_s, 0.0)], axis=1)
    gdn_norm_row = jnp.tile(gdn_norm, GDN_HEADS)[None, :]
    xa = _mix_out0(o_f, o_b, z, swa, x, ctx, mod, norms[1:2], gdn_norm_row, head_ones,
                   w_out[:GDN_HD].astype(BF16), w_s.reshape(SWA_Q_HEADS * LANES, D).astype(BF16), n_lat_tiles)
    return _ffn(xa, mod, norms[2:3], norms[3:4], ffn_gate.astype(BF16), ffn_up.astype(BF16),
                ffn_down.astype(BF16), n_lat_tiles)


def _odd_layer_last(xa, mod, norms, w_in, q_norm, kv_norm, w_uq, w_ukv, lam_p, lam_init, subln, w_out,
                    router, exp_gate, exp_up, exp_down, cos, sin, cos_r, sin_r, n_lat_tiles):
    B, S, D = xa.shape
    T = n_lat_tiles * ROW_TILE
    H = MLA_HEADS
    c0 = MLA_Q_RANK
    c1 = c0 + MLA_KV_RANK
    c2 = c1 + MLA_ROPE
    dw = DIFF_HEADS * 2 * DIFF_HEAD_DIM
    w_kr = jnp.pad(w_in[:, c1:c2], ((0, 0), (MLA_NOPE, LANES - MLA_NOPE - MLA_ROPE)))
    w_all = jnp.concatenate([w_in[:, :c1], w_kr, w_in[:, c2:c2 + dw] * DIFF_HEAD_DIM ** -0.5,
                             w_in[:, c2 + dw:]], axis=1).astype(BF16)
    qd = MLA_NOPE + MLA_ROPE
    wq = jnp.pad(w_uq.reshape(MLA_Q_RANK, H, qd) * qd ** -0.5, ((0, 0), (0, 0), (0, LANES - qd)))
    wkv = w_ukv.reshape(MLA_KV_RANK, H, MLA_NOPE + MLA_V)
    wk = jnp.pad(wkv[..., :MLA_NOPE], ((0, 0), (0, 0), (0, LANES - MLA_NOPE)))
    wv = jnp.pad(wkv[..., MLA_NOPE:], ((0, 0), (0, 0), (0, LANES - MLA_V)))
    q_cat, k_cat, v_ext, dq, dk, dv = _odd_proj(
        xa, mod, norms[0:1], w_all, cos, sin, cos_r, sin_r, q_norm[None, :], kv_norm[None, :],
        wq.reshape(MLA_Q_RANK, H * LANES).astype(BF16), wk.reshape(MLA_KV_RANK, H * LANES).astype(BF16),
        wv.reshape(MLA_KV_RANK, H * LANES).astype(BF16), n_lat_tiles)
    mla = _mla(q_cat, k_cat, v_ext, T)

    lam = (jnp.exp(jnp.sum(lam_p[0] * lam_p[1])) - jnp.exp(jnp.sum(lam_p[2] * lam_p[3])) + lam_init).reshape(1, 1)
    diff = _diff(dq, dk, dv, lam, subln[None, :], 1.0 - lam_init, T)

    w_mla = jnp.pad(w_out[:H * MLA_V].reshape(H, MLA_V, D), ((0, 0), (0, LANES - MLA_V), (0, 0)))
    x, f_in, logits = _mix_out1(mla, diff, xa, mod, norms[1:2], norms[2:3],
                                w_mla.reshape(H * LANES, D).astype(BF16), w_out[H * MLA_V:].astype(BF16),
                                _pad_cols(router, LANES), n_lat_tiles)

    dest, row_token, row_w, tile_expert, tile_rows = _route(logits.reshape(B * T, LANES)[:, :N_EXPERTS])
    x_sorted = f_in.reshape(B * T, D)[row_token]
    y = _moe_experts(tile_expert, tile_rows, x_sorted, row_w, exp_gate, exp_up, exp_down)
    y0 = y[dest[0]].reshape(B, T, D)
    y1 = y[dest[1]].reshape(B, T, D)
    return _norm_residual(y0, y1, x, mod, norms[3:4], n_lat_tiles)


def kernel(x, c, ctx, c_ctx, e_mod_w, e_mod_b, e_norms, e_w_in, e_conv_w, e_a_log, e_dt_bias, e_gdn_norm, e_sink, e_w_out, e_ffn_gate, e_ffn_up, e_ffn_down, o_mod_w, o_mod_b, o_norms, o_w_in, o_q_norm, o_kv_norm, o_w_uq, o_w_ukv, o_lambda, o_subln, o_w_out, o_router, o_exp_gate, o_exp_up, o_exp_down):
    B, T, D = x.shape
    n_lat_tiles = T // ROW_TILE
    rows = T // GRID_W
    cos64, sin64 = _stream_tables(*_rope_tables(rows, SWA_HEAD_DIM), 0)
    cos_r, sin_r = _stream_tables(*_rope_tables(rows, MLA_ROPE), MLA_NOPE)
    mod_e = _mod_table(c, c_ctx, e_mod_w[0], e_mod_b[0])
    xa = _even_layer(x, ctx, mod_e, e_norms[0], e_w_in[0], e_conv_w[0], e_a_log[0], e_dt_bias[0], e_gdn_norm[0],
                     e_sink[0], e_w_out[0], e_ffn_gate[0], e_ffn_up[0], e_ffn_down[0], cos64, sin64, n_lat_tiles)
    mod_o = _mod_table(c, c_ctx, o_mod_w[0], o_mod_b[0])
    lam_init = 0.8 - 0.6 * math.exp(-0.3 * 1)
    return _odd_layer_last(xa, mod_o, o_norms[0], o_w_in[0], o_q_norm[0], o_kv_norm[0], o_w_uq[0], o_w_ukv[0],
                           o_lambda[0], lam_init, o_subln[0], o_w_out[0], o_router[0], o_exp_gate[0],
                           o_exp_up[0], o_exp_down[0], cos64, sin64, cos_r, sin_r, n_lat_tiles)
```

```python
import functools
import math

import jax
import jax.numpy as jnp
from jax import lax
from jax.experimental import pallas as pl
from jax.experimental.pallas import tpu as pltpu

F32 = jnp.float32
BF16 = jnp.bfloat16

D_MODEL = 1024
CTX_LEN = 256
GRID_W = 64
NORM_EPS = 1e-6
ROPE_THETA = 10000.0

GDN_HEADS = 8
GDN_DK = 64
GDN_DV = 64
GDN_CHUNK = 64
CONV_K = 5
GDN_GROUP = 4
GDN_LANES = GDN_GROUP * GDN_DK
GDN_HD = GDN_HEADS * GDN_DK
GDN_NB = 4

SWA_Q_HEADS = 8
SWA_KV_HEADS = 2
SWA_HEAD_DIM = 64
SWA_WINDOW = 128
SWA_BLOCK = 128

MLA_HEADS = 8
MLA_Q_RANK = 384
MLA_KV_RANK = 256
MLA_NOPE = 64
MLA_ROPE = 32
MLA_V = 64

DIFF_HEADS = 4
DIFF_HEAD_DIM = 64

D_FF = 2816
N_EXPERTS = 8
TOP_K = 2
D_FF_EXPERT = 3584

LANES = 128
SUBLANES = 8
ROW_TILE = 256
LAT_TILE = 512
SWA_HEADS_PER_SUB = 1
ATTN_TQ = 2048
ATTN_SUB = 128
VMEM_LIMIT = 56 * 1024 * 1024
NEG_BIG = -1e30


def _dot(a, b):
    return jnp.dot(a, b, preferred_element_type=F32)


def _dot_nt(a, b):
    return lax.dot_general(a, b, (((1,), (1,)), ((), ())), preferred_element_type=F32)


def _dot_tn(a, b):
    return lax.dot_general(a, b, (((0,), (0,)), ((), ())), preferred_element_type=F32)


def _split_bf16(x):
    hi = x.astype(BF16)
    lo = (x - hi.astype(F32)).astype(BF16)
    return hi, lo


def _dot_split(x, w):
    hi, lo = _split_bf16(x)
    return _dot(hi, w) + _dot(lo, w)


def _dot_rsplit(w, x):
    hi, lo = _split_bf16(x)
    return _dot(w, hi) + _dot(w, lo)


def _rms(x, w):
    return x * lax.rsqrt(jnp.mean(x * x, axis=-1, keepdims=True) + NORM_EPS) * w


def _silu(x):
    return x * jax.nn.sigmoid(x)


def _rope_slab(x, cos, sin, quarter):
    lane = lax.broadcasted_iota(jnp.int32, x.shape, 1)
    fwd = pltpu.roll(x, quarter, 1)
    back = pltpu.roll(x, LANES - quarter, 1)
    rot = jnp.where(lane % (2 * quarter) < quarter, -back, fwd)
    return x * cos + rot * sin


def _params(*sem):
    return pltpu.CompilerParams(dimension_semantics=sem, vmem_limit_bytes=VMEM_LIMIT)


def _row_spec(width, tile=ROW_TILE):
    return pl.BlockSpec((1, tile, width), lambda b, t: (b, t, 0))


def _mod_spec(n_lat_tiles):
    return pl.BlockSpec((1, 1, 6, D_MODEL), lambda b, t: (b, jnp.where(t < n_lat_tiles, 1, 0), 0, 0))


def _const_spec(shape):
    return pl.BlockSpec(shape, lambda b, t: (0,) * len(shape))


def _table_spec():
    return pl.BlockSpec((ROW_TILE, LANES), lambda b, t: (t, 0))


E_QKV = (0, 3 * GDN_HD)
E_Z = (E_QKV[1], E_QKV[1] + GDN_HD)
E_GATE = (E_Z[1], E_Z[1] + LANES)
E_Q = (E_GATE[1], E_GATE[1] + SWA_Q_HEADS * LANES)
E_K = (E_Q[1], E_Q[1] + LANES)
E_V = (E_K[1], E_K[1] + LANES)


def _stream_in_specs(n_lat_tiles):
    assert CTX_LEN == ROW_TILE
    return [pl.BlockSpec((1, ROW_TILE, D_MODEL), lambda b, t: (b, jnp.minimum(t, n_lat_tiles - 1), 0)),
            pl.BlockSpec((1, ROW_TILE, D_MODEL), lambda b, t: (b, 0, 0))]


def _stream_tile(x_ref, c_ref, n_lat_tiles):
    return jnp.where(pl.program_id(1) < n_lat_tiles, x_ref[0], c_ref[0])


def _even_proj_kernel(x_ref, c_ref, mod_ref, nw_ref, w_ref, cos_ref, sin_ref,
                      qkv_ref, z_ref, gate_ref, q_ref, k_ref, v_ref, *, n_lat_tiles):
    xin = _stream_tile(x_ref, c_ref, n_lat_tiles)
    a = (_rms(xin, nw_ref[...]) * (1.0 + mod_ref[0, 0, 1:2, :]) + mod_ref[0, 0, 0:1, :]).astype(BF16)

    def proj(cols):
        return _dot(a, w_ref[:, cols[0]:cols[1]])

    qkv_ref[0] = proj(E_QKV)
    z_ref[0] = proj(E_Z).astype(BF16)
    gate_ref[0] = proj(E_GATE)
    cos = cos_ref[...]
    sin = sin_ref[...]
    quarter = SWA_HEAD_DIM // 4
    qp = proj(E_Q)
    for h in range(SWA_Q_HEADS):
        ls = slice(h * LANES, (h + 1) * LANES)
        q_ref[0, :, ls] = _rope_slab(qp[:, ls], cos, sin, quarter).astype(BF16)
    k_ref[0] = _rope_slab(proj(E_K), cos, sin, quarter).astype(BF16)
    v_ref[0] = proj(E_V).astype(BF16)


def _even_proj(x, ctx, mod, norm_w, w, cos, sin, n_lat_tiles):
    B, T, D = x.shape
    S = T + ctx.shape[1]
    widths = (E_QKV[1] - E_QKV[0], GDN_HD, LANES, SWA_Q_HEADS * LANES, LANES, LANES)
    dtypes = (F32, BF16, F32, BF16, BF16, BF16)
    return pl.pallas_call(
        functools.partial(_even_proj_kernel, n_lat_tiles=n_lat_tiles),
        grid=(B, S // ROW_TILE),
        in_specs=_stream_in_specs(n_lat_tiles) + [_mod_spec(n_lat_tiles), _const_spec((1, D)), _const_spec(w.shape),
                                                  _table_spec(), _table_spec()],
        out_specs=[_row_spec(n) for n in widths],
        out_shape=[jax.ShapeDtypeStruct((B, S, n), dt) for n, dt in zip(widths, dtypes)],
        compiler_params=_params("parallel", "parallel"),
        name="even_proj",
    )(x, ctx, mod, norm_w, w, cos, sin)


def _gdn_prep_kernel(x_ref, prev_ref, next_ref, cw_ref, bd_ref, q_ref, k_ref, v_ref, *, n_lat_tiles):
    t = pl.program_id(1)
    has_prev = jnp.logical_and(t > 0, t < n_lat_tiles)
    has_next = t < n_lat_tiles - 1
    prev = jnp.where(has_prev, prev_ref[0], 0.0)
    nxt = jnp.where(has_next, next_ref[0], 0.0)
    xe = jnp.concatenate([prev, x_ref[0], nxt], axis=0)
    pad = CONV_K // 2
    y = jnp.zeros(x_ref.shape[1:], F32)
    for kk in range(CONV_K):
        lo = SUBLANES + kk - pad
        y = y + cw_ref[kk:kk + 1, :] * xe[lo:lo + ROW_TILE, :]
    u = _silu(y)

    def headnorm(a):
        return a * lax.rsqrt(_dot_split(a * a, bd_ref[...]) + NORM_EPS)

    q_ref[0] = headnorm(u[:, :GDN_HD]) * GDN_DK ** -0.5
    k_ref[0] = headnorm(u[:, GDN_HD:2 * GDN_HD])
    v_ref[0] = u[:, 2 * GDN_HD:]


def _gdn_prep(qkv, conv_w, head_ones, n_lat_tiles):
    B, S, W = qkv.shape
    per_tile = ROW_TILE // SUBLANES
    n8 = S // SUBLANES
    return pl.pallas_call(
        functools.partial(_gdn_prep_kernel, n_lat_tiles=n_lat_tiles),
        grid=(B, S // ROW_TILE),
        in_specs=[_row_spec(W),
                  pl.BlockSpec((1, SUBLANES, W), lambda b, t: (b, jnp.maximum(t * per_tile - 1, 0), 0)),
                  pl.BlockSpec((1, SUBLANES, W), lambda b, t: (b, jnp.minimum((t + 1) * per_tile, n8 - 1), 0)),
                  _const_spec(conv_w.shape), _const_spec(head_ones.shape)],
        out_specs=[_row_spec(GDN_HD)] * 3,
        out_shape=[jax.ShapeDtypeStruct((B, S, GDN_HD), F32)] * 3,
        compiler_params=_params("parallel", "parallel"),
        name="gdn_prep",
    )(qkv, qkv, qkv, conv_w, head_ones)


def _block_diag(x, head_of_lane):
    zero = jnp.zeros_like(x)
    return jnp.concatenate([jnp.where(head_of_lane == h, x, zero) for h in range(GDN_GROUP)], axis=0)


def _expand_heads(x4, head_of_lane):
    c = x4.shape[0]
    out = jnp.broadcast_to(x4[:, GDN_GROUP - 1:GDN_GROUP], (c, GDN_LANES))
    for h in range(GDN_GROUP - 2, -1, -1):
        out = jnp.where(head_of_lane == h, jnp.broadcast_to(x4[:, h:h + 1], (c, GDN_LANES)), out)
    return out


def _gdn_groups(items):
    C = GDN_CHUNK
    row = lax.broadcasted_iota(jnp.int32, (C, GDN_LANES), 0)
    lane = lax.broadcasted_iota(jnp.int32, (C, GDN_LANES), 1)
    hol = lane // GDN_DK
    col = lane % GDN_DK
    eye = jnp.where(row == col, 1.0, 0.0)
    rr = lax.broadcasted_iota(jnp.int32, (2 * C, 4 * C), 0)
    cc = lax.broadcasted_iota(jnp.int32, (2 * C, 4 * C), 1)
    cum_lhs = {}
    for rev in (False, True):
        before = (rr - cc % C) * (-1 if rev else 1) >= 0
        cum_lhs[rev] = jnp.where(((rr < C) & (cc < 2 * C) & before) | ((rr >= C) & (cc >= 2 * C)),
                                 1.0, 0.0).astype(BF16)

    def bd(x):
        return _block_diag(x, hol)

    st = []
    for (q4, k4, v4, g4, b4, s4, reverse) in items:
        sgn = -1 if reverse else 1
        ahead = (row - col) * sgn
        incl = ahead >= 0
        strict = ahead > 0
        gE = _expand_heads(g4, hol)
        bE = _expand_heads(b4, hol)
        g_hi, g_lo = _split_bf16(gE)
        m_hi, m_lo = _split_bf16(jnp.where((col - row) * sgn >= 0, gE, 0.0))
        both = _dot(cum_lhs[reverse], jnp.concatenate([g_hi, g_lo, m_hi, m_lo], axis=0))
        gc = both[:C]
        gc_row = both[C:]
        decay = jnp.where(incl, jnp.exp(gc - gc_row), 0.0)
        g_last = gc[0:1, :] if reverse else gc[C - 1:C, :]
        eg = jnp.exp(gc)
        kb = k4 * bE
        st.append(dict(q=q4, k=k4, s=s4, incl=incl, strict=strict, decay=decay, g_last=g_last,
                       kb=kb, vb=v4 * bE, kbg=kb * eg, qh=q4 * eg, kt=k4 * jnp.exp(g_last - gc)))

    for d in st:
        gram = _dot_nt(jnp.concatenate([d["kb"], d["q"]], axis=0).astype(BF16), bd(d["k"].astype(BF16)))
        L = jnp.where(d["strict"], gram[:C] * d["decay"], 0.0)
        d["A"] = gram[C:] * d["decay"]
        d["T"] = eye - L
        d["P"] = L.astype(BF16)
    n_factors = int(math.log2(C)) - 1
    for d in st:
        d["P"] = _dot(d["P"], bd(d["P"])).astype(BF16)
    for it in range(n_factors):
        last = it == n_factors - 1
        for d in st:
            lhs = d["T"].astype(BF16) if last else jnp.concatenate([d["T"].astype(BF16), d["P"]], axis=0)
            prod = _dot(lhs, bd(d["P"]))
            d["T"] = d["T"] + prod[:C]
            if not last:
                d["P"] = prod[C:].astype(BF16)
    for d in st:
        Tb = d["T"].astype(BF16)
        d["u"] = _dot(Tb, bd(d["vb"].astype(BF16)))
        d["w"] = _dot(Tb, bd(d["kbg"].astype(BF16)))
    for d in st:
        d["ws_qs"] = _dot(jnp.concatenate([d["w"], d["qh"]], axis=0).astype(BF16), bd(d["s"].astype(BF16)))
    out = []
    for d in st:
        v_new = d["u"] - d["ws_qs"][:C]
        vb16 = v_new.astype(BF16)
        o = d["ws_qs"][C:] + _dot(d["A"].astype(BF16), bd(vb16))
        full = _dot_tn(d["kt"].astype(BF16), vb16)
        upd = jnp.zeros((GDN_DK, GDN_LANES), F32)
        for h in range(GDN_GROUP):
            upd = upd + jnp.where(hol == h, full[h * GDN_DK:(h + 1) * GDN_DK, :], 0.0)
        out.append((o, d["s"] * jnp.exp(d["g_last"]) + upd))
    return out


def _gdn_kernel(qf_ref, kf_ref, vf_ref, gf_ref, qb_ref, kb_ref, vb_ref, gb_ref, al_ref, dtb_ref,
                of_ref, ob_ref, s_ref):
    @pl.when(pl.program_id(1) == 0)
    def _():
        s_ref[...] = jnp.zeros_like(s_ref)

    n_grp = GDN_HEADS // GDN_GROUP
    n_gate = 2 * GDN_HEADS
    items = []
    slots = []
    for nb in range(GDN_NB):
        for d, (q_ref, k_ref, v_ref, g_ref, o_ref) in enumerate(((qf_ref, kf_ref, vf_ref, gf_ref, of_ref),
                                                                 (qb_ref, kb_ref, vb_ref, gb_ref, ob_ref))):
            raw = g_ref[nb]
            xs = raw + dtb_ref[...]
            softplus = jnp.maximum(xs, 0.0) + jnp.log(1.0 + jnp.exp(-jnp.abs(xs)))
            g = -jnp.exp(al_ref[...]) * softplus
            beta = jax.nn.sigmoid(raw)
            for grp in range(n_grp):
                ls = slice(grp * GDN_LANES, (grp + 1) * GDN_LANES)
                c0 = d * GDN_HEADS + grp * GDN_GROUP
                slot = (nb * 2 + d) * n_grp + grp
                items.append((q_ref[nb, :, ls], k_ref[nb, :, ls], v_ref[nb, :, ls],
                              g[:, c0:c0 + GDN_GROUP], beta[:, n_gate + c0:n_gate + c0 + GDN_GROUP],
                              s_ref[slot], d == 1))
                slots.append((o_ref, nb, ls, slot))
    for (o_ref, nb, ls, slot), (o, s_new) in zip(slots, _gdn_groups(items)):
        o_ref[nb, :, ls] = o
        s_ref[slot] = s_new


def _gdn(q, k, v, gate, a_log_row, dt_bias_row, n_lat_chunks):
    B, S, HD = q.shape
    n_chunks = S // GDN_CHUNK
    n_ctx = n_chunks - n_lat_chunks

    def fwd_c(s):
        return jnp.where(s < n_ctx, n_lat_chunks + s, s - n_ctx)

    def bwd_c(s):
        return n_chunks - 1 - s

    def spec(cmap, width):
        return pl.BlockSpec((GDN_NB, GDN_CHUNK, width), lambda b, s: (b, cmap(s), 0))

    return pl.pallas_call(
        _gdn_kernel,
        grid=(B // GDN_NB, n_chunks),
        in_specs=[spec(fwd_c, HD), spec(fwd_c, HD), spec(fwd_c, HD), spec(fwd_c, LANES),
                  spec(bwd_c, HD), spec(bwd_c, HD), spec(bwd_c, HD), spec(bwd_c, LANES),
                  _const_spec((1, LANES)), _const_spec((1, LANES))],
        out_specs=[spec(fwd_c, HD), spec(bwd_c, HD)],
        out_shape=[jax.ShapeDtypeStruct((B, S, HD), F32)] * 2,
        scratch_shapes=[pltpu.VMEM((GDN_NB * 2 * GDN_HEADS // GDN_GROUP, GDN_DK, GDN_LANES), F32)],
        compiler_params=_params("parallel", "arbitrary"),
        name="gdn_scan",
    )(q, k, v, gate, q, k, v, gate, a_log_row, dt_bias_row)


def _swa_kernel(q_ref, k_ref, v_ref, sink_ref, o_ref, *, n_lat_blocks):
    W = SWA_BLOCK
    H = SWA_Q_HEADS
    n = pl.program_id(1)
    is_lat = n < n_lat_blocks
    nl = jnp.minimum(n, n_lat_blocks - 1)
    prev = jnp.maximum(nl - 1, 0)
    nxt = jnp.minimum(nl + 1, n_lat_blocks - 1)
    ctx0 = n_lat_blocks * W

    def keys(ref):
        def blk(i):
            return ref[0, pl.ds(pl.multiple_of(i * W, W), W), :]
        return jnp.concatenate([ref[0, ctx0:ctx0 + CTX_LEN, :], blk(prev), blk(nl), blk(nxt)], axis=0)

    k_all = keys(k_ref)
    v_all = keys(v_ref)
    nk = CTX_LEN + 3 * W
    v_ext = jnp.concatenate([v_all, jnp.ones((nk, LANES), BF16)], axis=1)
    hp = SWA_HEADS_PER_SUB
    ii = lax.broadcasted_iota(jnp.int32, (hp * W, nk), 0) % W
    jj = lax.broadcasted_iota(jnp.int32, (hp * W, nk), 1) - CTX_LEN
    in_window = (((jj >= 0) & (jj < W) & (jj >= ii) & (nl > 0))
                 | ((jj >= W) & (jj < 2 * W))
                 | ((jj >= 2 * W) & (jj - 2 * W <= ii) & (nl < n_lat_blocks - 1)))
    valid = (jj < 0) | (in_window & is_lat)

    def scores(i):
        q = jnp.concatenate([q_ref[0, :, h * LANES:(h + 1) * LANES] for h in range(i * hp, (i + 1) * hp)], axis=0)
        return jnp.where(valid, _dot_nt(q, k_all), NEG_BIG)

    def finish(i, s):
        sk = sink_ref[i * hp * W:(i + 1) * hp * W, :]
        m = jnp.maximum(jnp.max(s, axis=-1, keepdims=True), sk)
        acc = _dot(jnp.exp(s - m).astype(BF16), v_ext)
        o = acc[:, :LANES] / (acc[:, LANES:LANES + 1] + jnp.exp(sk - m))
        for j in range(hp):
            h = i * hp + j
            o_ref[0, :, h * LANES:(h + 1) * LANES] = o[j * W:(j + 1) * W].astype(o_ref.dtype)

    _one_ahead(H // hp, scores, finish)


def _swa(q_pad, k, v, sink_rows, n_lat_blocks):
    B, S, _ = k.shape
    W = SWA_BLOCK
    QW = SWA_Q_HEADS * LANES
    return pl.pallas_call(
        functools.partial(_swa_kernel, n_lat_blocks=n_lat_blocks),
        grid=(B, S // W),
        in_specs=[pl.BlockSpec((1, W, QW), lambda b, n: (b, n, 0)),
                  pl.BlockSpec((1, S, LANES), lambda b, n: (b, 0, 0)),
                  pl.BlockSpec((1, S, LANES), lambda b, n: (b, 0, 0)),
                  _const_spec(sink_rows.shape)],
        out_specs=pl.BlockSpec((1, W, QW), lambda b, n: (b, n, 0)),
        out_shape=jax.ShapeDtypeStruct((B, S, QW), BF16),
        compiler_params=_params("parallel", "parallel"),
        name="swa",
    )(q_pad, k, v, sink_rows)


def _even_tail_kernel(of_ref, ob_ref, z_ref, swa_ref, x_ref, c_ref, mod_ref, norms_ref, gn_ref, bd_ref,
                      wo_g_ref, wo_s_ref, wg_ref, wu_ref, wd_ref, o_ref, *, n_lat_tiles):
    o = of_ref[0] + ob_ref[0]
    ms = _dot_split(o * o, bd_ref[...]) * (1.0 / GDN_DV)
    gdn = o * lax.rsqrt(ms + NORM_EPS) * gn_ref[...] * _silu(z_ref[0].astype(F32))
    y = _dot(gdn.astype(BF16), wo_g_ref[...]) + _dot(swa_ref[0], wo_s_ref[...])
    x1 = _stream_tile(x_ref, c_ref, n_lat_tiles) + mod_ref[0, 0, 2:3, :] * _rms(y, norms_ref[1:2, :])
    f = (_rms(x1, norms_ref[2:3, :]) * (1.0 + mod_ref[0, 0, 4:5, :]) + mod_ref[0, 0, 3:4, :]).astype(BF16)
    h = _silu(_dot(f, wg_ref[...])) * _dot(f, wu_ref[...])
    y2 = _dot(h.astype(BF16), wd_ref[...])
    o_ref[0] = x1 + mod_ref[0, 0, 5:6, :] * _rms(y2, norms_ref[3:4, :])


def _resident_spec(shape):
    return pl.BlockSpec(shape, lambda b, t: (0,) * len(shape), pipeline_mode=pl.Buffered(1))


def _even_tail(o_f, o_b, z, swa, x, ctx, mod, norms, gdn_norm_row, head_ones, wo_gdn, wo_swa, wg, wu, wd,
               n_lat_tiles):
    B, S, _ = o_f.shape
    D = x.shape[-1]
    consts = (norms, gdn_norm_row, head_ones, wo_gdn, wo_swa, wg, wu, wd)
    return pl.pallas_call(
        functools.partial(_even_tail_kernel, n_lat_tiles=n_lat_tiles),
        grid=(B, S // ROW_TILE),
        in_specs=[_row_spec(GDN_HD), _row_spec(GDN_HD), _row_spec(GDN_HD), _row_spec(swa.shape[-1])]
        + _stream_in_specs(n_lat_tiles) + [_mod_spec(n_lat_tiles)] + [_resident_spec(a.shape) for a in consts],
        out_specs=_row_spec(D),
        out_shape=jax.ShapeDtypeStruct((B, S, D), F32),
        compiler_params=_params("parallel", "parallel"),
        name="even_tail",
    )(o_f, o_b, z, swa, x, ctx, mod, *consts)


O_CQ = (0, MLA_Q_RANK)
O_CKV = (O_CQ[1], O_CQ[1] + MLA_KV_RANK)
O_KR = (O_CKV[1], O_CKV[1] + LANES)
O_DQ = (O_KR[1], O_KR[1] + DIFF_HEADS * LANES)
O_DK = (O_DQ[1], O_DQ[1] + DIFF_HEADS * LANES)
O_DV = (O_DK[1], O_DK[1] + DIFF_HEADS * LANES)


def _odd_proj_kernel(x_ref, mod_ref, nw_ref, w_ref, cos_ref, sin_ref, cosr_ref, sinr_ref,
                     qn_ref, kvn_ref, wq_ref, wk_ref, wv_ref,
                     q_ref, k_ref, v_ref, dq_ref, dk_ref, dv_ref):
    a = (_rms(x_ref[0], nw_ref[...]) * (1.0 + mod_ref[0, 0, 1:2, :]) + mod_ref[0, 0, 0:1, :]).astype(BF16)

    def proj(cols):
        return _dot(a, w_ref[:, cols[0]:cols[1]])

    cos_r = cosr_ref[...]
    sin_r = sinr_ref[...]
    rq = MLA_ROPE // 4
    cq = _rms(proj(O_CQ), qn_ref[...]).astype(BF16)
    ckv = _rms(proj(O_CKV), kvn_ref[...]).astype(BF16)
    kr = _rope_slab(proj(O_KR), cos_r, sin_r, rq)
    qp = _dot(cq, wq_ref[...])
    kp = _dot(ckv, wk_ref[...])
    vp = _dot(ckv, wv_ref[...])
    lane = lax.broadcasted_iota(jnp.int32, (ROW_TILE, LANES), 1)
    ones_hi = jnp.where(lane >= MLA_V, 1.0, 0.0)
    for h in range(MLA_HEADS):
        ls = slice(h * LANES, (h + 1) * LANES)
        q_ref[0, :, ls] = _rope_slab(qp[:, ls], cos_r, sin_r, rq).astype(BF16)
        k_ref[0, :, ls] = (kp[:, ls] + kr).astype(BF16)
        v_ref[0, :, ls] = (vp[:, ls] + ones_hi).astype(BF16)
    cos = cos_ref[...]
    sin = sin_ref[...]
    quarter = DIFF_HEAD_DIM // 4
    dq = proj(O_DQ)
    dk = proj(O_DK)
    dv = proj(O_DV)
    ones = jnp.ones((ROW_TILE, LANES), BF16)
    for h in range(DIFF_HEADS):
        ls = slice(h * LANES, (h + 1) * LANES)
        dq_ref[0, :, ls] = _rope_slab(dq[:, ls], cos, sin, quarter).astype(BF16)
        dk_ref[0, :, ls] = _rope_slab(dk[:, ls], cos, sin, quarter).astype(BF16)
        dv_ref[0, :, 2 * h * LANES:(2 * h + 1) * LANES] = dv[:, ls].astype(BF16)
        dv_ref[0, :, (2 * h + 1) * LANES:(2 * h + 2) * LANES] = ones


def _odd_proj(xa, mod, norm_w, w, cos, sin, cos_r, sin_r, q_norm, kv_norm, wq, wk, wv, n_lat_tiles):
    B, S, D = xa.shape
    HW = MLA_HEADS * LANES
    widths = (HW, HW, HW, DIFF_HEADS * LANES, DIFF_HEADS * LANES, 2 * DIFF_HEADS * LANES)
    return pl.pallas_call(
        _odd_proj_kernel,
        grid=(B, S // ROW_TILE),
        in_specs=[_row_spec(D), _mod_spec(n_lat_tiles), _const_spec((1, D)), _const_spec(w.shape),
                  _table_spec(), _table_spec(), _table_spec(), _table_spec(),
                  _const_spec(q_norm.shape), _const_spec(kv_norm.shape),
                  _const_spec(wq.shape), _const_spec(wk.shape), _const_spec(wv.shape)],
        out_specs=[_row_spec(n) for n in widths],
        out_shape=[jax.ShapeDtypeStruct((B, S, n), BF16) for n in widths],
        compiler_params=_params("parallel", "parallel"),
        name="odd_proj",
    )(xa, mod, norm_w, w, cos, sin, cos_r, sin_r, q_norm, kv_norm, wq, wk, wv)


def _one_ahead(n_sub, scores, finish):
    s_prev = scores(0)
    for i in range(1, n_sub):
        s_next = scores(i)
        finish(i - 1, s_prev)
        s_prev = s_next
    finish(n_sub - 1, s_prev)


def _mla_kernel(q_ref, k_ref, v_ref, o_ref):
    k = k_ref[0]
    v = v_ref[0]

    def scores(i):
        return _dot_nt(q_ref[0, i * ATTN_SUB:(i + 1) * ATTN_SUB, :], k)

    def finish(i, s):
        m = jnp.max(s, axis=-1, keepdims=True)
        acc = _dot(jnp.exp(s - m).astype(BF16), v)
        o_ref[0, i * ATTN_SUB:(i + 1) * ATTN_SUB, :] = (acc / acc[:, MLA_V:MLA_V + 1]).astype(o_ref.dtype)

    _one_ahead(q_ref.shape[1] // ATTN_SUB, scores, finish)


def _mla(q, k, v_ext, T):
    B, S, _ = k.shape
    return pl.pallas_call(
        _mla_kernel,
        grid=(B, MLA_HEADS, T // ATTN_TQ),
        in_specs=[pl.BlockSpec((1, ATTN_TQ, LANES), lambda b, h, t: (b, t, h)),
                  pl.BlockSpec((1, S, LANES), lambda b, h, t: (b, 0, h)),
                  pl.BlockSpec((1, S, LANES), lambda b, h, t: (b, 0, h))],
        out_specs=pl.BlockSpec((1, ATTN_TQ, LANES), lambda b, h, t: (b, t, h)),
        out_shape=jax.ShapeDtypeStruct((B, T, MLA_HEADS * LANES), BF16),
        compiler_params=_params("parallel", "parallel", "parallel"),
        name="mla_attn",
    )(q, k, v_ext)


def _diff_kernel(q_ref, k_ref, v_ref, lam_ref, sub_ref, o_ref, *, post_scale):
    k = k_ref[0]
    v = v_ref[0]
    sub = ATTN_SUB
    vw = 2 * DIFF_HEAD_DIM
    lo = lax.broadcasted_iota(jnp.int32, (sub, LANES), 1) < DIFF_HEAD_DIM
    zero = jnp.zeros((sub, LANES), BF16)

    def scores(i):
        q = q_ref[0, i * sub:(i + 1) * sub, :]
        return _dot_nt(jnp.concatenate([jnp.where(lo, q, zero), jnp.where(lo, zero, q)], axis=0), k)

    def finish(i, s):
        m = jnp.max(s, axis=-1, keepdims=True)
        acc = _dot(jnp.exp(s - m).astype(BF16), v)
        att = acc[:, :vw] / acc[:, vw:vw + 1]
        a = att[:sub] - lam_ref[...] * att[sub:]
        o_ref[0, i * sub:(i + 1) * sub, :] = (_rms(a, sub_ref[...]) * post_scale).astype(o_ref.dtype)

    _one_ahead(q_ref.shape[1] // sub, scores, finish)


def _diff(q, k, v_ext, lam, subln, post_scale, T):
    B, S, _ = k.shape
    tq = ATTN_TQ // 2
    return pl.pallas_call(
        functools.partial(_diff_kernel, post_scale=post_scale),
        grid=(B, DIFF_HEADS, T // tq),
        in_specs=[pl.BlockSpec((1, tq, LANES), lambda b, h, t: (b, t, h)),
                  pl.BlockSpec((1, S, LANES), lambda b, h, t: (b, 0, h)),
                  pl.BlockSpec((1, S, 2 * LANES), lambda b, h, t: (b, 0, h)),
                  pl.BlockSpec((1, 1), lambda b, h, t: (0, 0)),
                  pl.BlockSpec((1, LANES), lambda b, h, t: (0, 0))],
        out_specs=pl.BlockSpec((1, tq, LANES), lambda b, h, t: (b, t, h)),
        out_shape=jax.ShapeDtypeStruct((B, T, DIFF_HEADS * LANES), BF16),
        compiler_params=_params("parallel", "parallel", "parallel"),
        name="diff_attn",
    )(q, k, v_ext, lam, subln)


def _mix_out1_kernel(mla_ref, diff_ref, x_ref, mod_ref, n1_ref, n2_ref, w1_ref, w2_ref, wrh_ref, wrl_ref,
                     xo_ref, f_ref, lg_ref):
    y = _dot(mla_ref[0], w1_ref[...]) + _dot(diff_ref[0], w2_ref[...])
    x = x_ref[0] + mod_ref[0, 0, 2:3, :] * _rms(y, n1_ref[...])
    xo_ref[0] = x
    f = _rms(x, n2_ref[...]) * (1.0 + mod_ref[0, 0, 4:5, :]) + mod_ref[0, 0, 3:4, :]
    f_ref[0] = f
    f_hi, f_lo = _split_bf16(f)
    lg_ref[0] = _dot(f_hi, wrh_ref[...]) + (_dot(f_hi, wrl_ref[...]) + _dot(f_lo, wrh_ref[...]))


def _mix_out1(mla, diff, xa, mod, n1, n2, w1, w2, wr, n_lat_tiles):
    B, T, _ = mla.shape
    D = xa.shape[-1]
    wr_hi, wr_lo = _split_bf16(wr)
    return pl.pallas_call(
        _mix_out1_kernel,
        grid=(B, T // ROW_TILE),
        in_specs=[_row_spec(mla.shape[-1]), _row_spec(diff.shape[-1]), _row_spec(D), _mod_spec(n_lat_tiles),
                  _const_spec((1, D)), _const_spec((1, D)), _const_spec(w1.shape), _const_spec(w2.shape),
                  _const_spec(wr.shape), _const_spec(wr.shape)],
        out_specs=[_row_spec(D), _row_spec(D), _row_spec(LANES)],
        out_shape=[jax.ShapeDtypeStruct((B, T, D), F32), jax.ShapeDtypeStruct((B, T, D), F32),
                   jax.ShapeDtypeStruct((B, T, LANES), F32)],
        compiler_params=_params("parallel", "parallel"),
        name="mix_out1",
    )(mla, diff, xa, mod, n1, n2, w1, w2, wr_hi, wr_lo)


MOE_TM = 1024
MOE_TF = 512
MOE_SUB = 256


def _moe_kernel(te_ref, tr_ref, x_ref, rw_ref, wg_ref, wu_ref, wd_ref, o_ref, acc_ref):
    i = pl.program_id(0)
    f = pl.program_id(1)
    n_rows = tr_ref[i]
    n_sub = MOE_TM // MOE_SUB

    @pl.when(f == 0)
    def _():
        acc_ref[...] = jnp.zeros_like(acc_ref)

    def sub_tile_fns():
        wg = wg_ref[0].astype(BF16)
        wu = wu_ref[0].astype(BF16)
        wd = wd_ref[0].astype(BF16)

        def gate_up(j):
            x = x_ref[j * MOE_SUB:(j + 1) * MOE_SUB, :].astype(BF16)
            return _dot(x, wg), _dot(x, wu)

        def down(j, gu):
            h = (_silu(gu[0]) * gu[1]).astype(BF16)
            acc_ref[j * MOE_SUB:(j + 1) * MOE_SUB, :] += _dot(h, wd)

        return gate_up, down

    all_subs = n_rows > (n_sub - 1) * MOE_SUB

    @pl.when(all_subs)
    def _():
        _one_ahead(n_sub, *sub_tile_fns())

    for j in range(n_sub - 1):
        @pl.when(jnp.logical_and(jnp.logical_not(all_subs), n_rows > j * MOE_SUB))
        def _(j=j):
            gate_up, down = sub_tile_fns()
            down(j, gate_up(j))

    @pl.when(f == pl.num_programs(1) - 1)
    def _():
        o_ref[...] = acc_ref[...] * rw_ref[:, 0:1]


def _moe_experts(tile_expert, tile_rows, x_sorted, row_w, wg, wu, wd):
    R, D = x_sorted.shape
    E, _, F = wg.shape
    nf = F // MOE_TF

    def f_eff(i, f, ta):
        return jnp.where(ta[i] > 0, f, nf - 1)

    grid_spec = pltpu.PrefetchScalarGridSpec(
        num_scalar_prefetch=2,
        grid=(R // MOE_TM, nf),
        in_specs=[pl.BlockSpec((MOE_TM, D), lambda i, f, te, ta: (i, 0)),
                  pl.BlockSpec((MOE_TM, LANES), lambda i, f, te, ta: (i, 0)),
                  pl.BlockSpec((1, D, MOE_TF), lambda i, f, te, ta: (te[i], 0, f_eff(i, f, ta))),
                  pl.BlockSpec((1, D, MOE_TF), lambda i, f, te, ta: (te[i], 0, f_eff(i, f, ta))),
                  pl.BlockSpec((1, MOE_TF, D), lambda i, f, te, ta: (te[i], f_eff(i, f, ta), 0))],
        out_specs=pl.BlockSpec((MOE_TM, D), lambda i, f, te, ta: (i, 0)),
        scratch_shapes=[pltpu.VMEM((MOE_TM, D), F32)],
    )
    return pl.pallas_call(
        _moe_kernel,
        grid_spec=grid_spec,
        out_shape=jax.ShapeDtypeStruct((R, D), F32),
        compiler_params=_params("arbitrary", "arbitrary"),
        name="moe_experts",
    )(tile_expert, tile_rows, x_sorted, row_w, wg, wu, wd)


def _norm_residual_kernel(a_ref, b_ref, x_ref, mod_ref, nw_ref, o_ref):
    y = a_ref[0] + b_ref[0]
    o_ref[0] = x_ref[0] + mod_ref[0, 0, 5:6, :] * _rms(y, nw_ref[...])


def _norm_residual(a, b, x, mod, norm_w, n_lat_tiles):
    B, T, D = x.shape
    return pl.pallas_call(
        _norm_residual_kernel,
        grid=(B, T // LAT_TILE),
        in_specs=[_row_spec(D, LAT_TILE), _row_spec(D, LAT_TILE), _row_spec(D, LAT_TILE), _mod_spec(n_lat_tiles),
                  _const_spec((1, D))],
        out_specs=_row_spec(D, LAT_TILE),
        out_shape=jax.ShapeDtypeStruct((B, T, D), F32),
        compiler_params=_params("parallel", "parallel"),
        name="norm_residual",
    )(a, b, x, mod, norm_w)


def _route(logits):
    N = logits.shape[0]
    M = N * TOP_K
    R = M + N_EXPERTS * MOE_TM
    top_val, top_idx = lax.top_k(logits, TOP_K)
    top_w = jax.nn.softmax(top_val, axis=-1).T.reshape(-1)
    e_flat = top_idx.T.reshape(-1)
    onehot = (e_flat[:, None] == jnp.arange(N_EXPERTS)[None, :]).astype(jnp.int32)
    counts = jnp.sum(onehot, axis=0)
    padded = ((counts + MOE_TM - 1) // MOE_TM) * MOE_TM
    ends = jnp.cumsum(padded)
    starts = ends - padded
    ustarts = jnp.cumsum(counts) - counts
    order = jnp.argsort(e_flat, stable=True).astype(jnp.int32)
    rank = jnp.argsort(order).astype(jnp.int32)
    dest = rank + jnp.sum(onehot * (starts - ustarts)[None, :], axis=-1)
    n_tiles = R // MOE_TM
    tile_start = jnp.arange(n_tiles, dtype=jnp.int32) * MOE_TM
    tile_active = (tile_start < ends[-1]).astype(jnp.int32)
    te = jnp.minimum(jnp.searchsorted(ends, tile_start, side="right"), N_EXPERTS - 1).astype(jnp.int32)
    last_e = te[jnp.maximum(ends[-1] // MOE_TM - 1, 0)]
    tile_expert = jnp.where(tile_active > 0, te, last_e)
    in_tile = jnp.arange(MOE_TM, dtype=jnp.int32)[None, :]
    within = (tile_start - starts[tile_expert])[:, None] + in_tile
    valid = (within < counts[tile_expert][:, None]) & (tile_active[:, None] > 0)
    src = order[jnp.clip(ustarts[tile_expert][:, None] + within, 0, M - 1).reshape(R)]
    valid = valid.reshape(R)
    tile_rows = jnp.clip(counts[tile_expert] - (tile_start - starts[tile_expert]), 0, MOE_TM) * tile_active
    row_token = jnp.where(valid, src % N, jnp.arange(R, dtype=jnp.int32) % N)
    row_w = jnp.broadcast_to(jnp.where(valid, top_w[src], 0.0)[:, None], (R, LANES))
    return dest.reshape(TOP_K, N), row_token, row_w, tile_expert, tile_rows.astype(jnp.int32)


def _rope_tables(rows, rot_dim):
    row = jnp.repeat(jnp.arange(rows, dtype=F32), GRID_W)
    col = jnp.tile(jnp.arange(GRID_W, dtype=F32), rows)
    half = rot_dim // 2
    inv_freq = ROPE_THETA ** (-jnp.arange(0, half, 2, dtype=F32) / half)
    ang_r = row[:, None] * inv_freq[None, :]
    ang_c = col[:, None] * inv_freq[None, :]
    ang = jnp.concatenate([ang_r, ang_r, ang_c, ang_c], axis=-1)
    return jnp.cos(ang), jnp.sin(ang)


def _stream_tables(cos, sin, lane0):
    T, R = cos.shape
    if lane0 == 0:
        cos_l, sin_l = jnp.tile(cos, (1, LANES // R)), jnp.tile(sin, (1, LANES // R))
    else:
        pad = ((0, 0), (lane0, LANES - lane0 - R))
        cos_l = jnp.pad(cos - 1.0, pad) + 1.0
        sin_l = jnp.pad(sin, pad)
    ctx = ((0, CTX_LEN), (0, 0))
    return jnp.pad(cos_l - 1.0, ctx) + 1.0, jnp.pad(sin_l, ctx)


def _cond_proj_kernel(c_ref, w_ref, b_ref, o_ref):
    c = c_ref[...]
    c_hi, c_lo = _split_bf16(_silu(c))
    w_hi, w_lo = _split_bf16(w_ref[...])
    o_ref[...] = _dot(c_hi, w_hi) + (_dot(c_hi, w_lo) + _dot(c_lo, w_hi)) + b_ref[...]


def _cond_proj(cond, w, b):
    R, D = cond.shape
    N = w.shape[1]
    return pl.pallas_call(
        _cond_proj_kernel,
        grid=(N // D,),
        in_specs=[pl.BlockSpec((R, D), lambda j: (0, 0)), pl.BlockSpec((D, D), lambda j: (0, j)),
                  pl.BlockSpec((1, D), lambda j: (0, j))],
        out_specs=pl.BlockSpec((R, D), lambda j: (0, j)),
        out_shape=jax.ShapeDtypeStruct((R, N), F32),
        compiler_params=_params("parallel"),
        name="cond_proj",
    )(cond, w, b)


def _mod_table(c, c_ctx, mod_w, mod_b):
    B = c.shape[0]
    cond = jnp.concatenate([c, c_ctx[None, :]], axis=0)
    cond = jnp.pad(cond, ((0, SUBLANES - (B + 1) % SUBLANES), (0, 0)))
    m = _cond_proj(cond, mod_w, mod_b[None, :])[:B + 1].reshape(B + 1, 1, 6, D_MODEL)
    return jnp.concatenate([jnp.broadcast_to(m[B:], (B, 1, 6, D_MODEL)), m[:B]], axis=1)


def _pad_cols(w, n):
    return jnp.pad(w, ((0, 0), (0, n - w.shape[1])))


def _even_layer(x, ctx, mod, norms, w_in, conv_w, a_log, dt_bias, gdn_norm, sink, w_out, ffn_gate, ffn_up,
                ffn_down, cos, sin, n_lat_tiles):
    B, T, D = x.shape
    n_gate = 2 * GDN_HEADS
    c_qkv = 3 * GDN_HD
    c_z = c_qkv + GDN_HD
    c_ga = c_z + n_gate
    c_gb = c_ga + n_gate
    c_sq = c_gb + SWA_Q_HEADS * SWA_HEAD_DIM
    c_sk = c_sq + SWA_KV_HEADS * SWA_HEAD_DIM
    grp = SWA_Q_HEADS // SWA_KV_HEADS
    wq = w_in[:, c_gb:c_sq].reshape(D, SWA_Q_HEADS, SWA_HEAD_DIM) * SWA_HEAD_DIM ** -0.5
    kv_of_head = (jnp.arange(SWA_Q_HEADS) // grp)[None, :, None]
    wq = jnp.concatenate([jnp.where(kv_of_head == 0, wq, 0.0), jnp.where(kv_of_head == 1, wq, 0.0)], axis=-1)
    w_all = jnp.concatenate([w_in[:, :c_z], _pad_cols(w_in[:, c_z:c_gb], LANES),
                             wq.reshape(D, SWA_Q_HEADS * LANES), w_in[:, c_sq:]], axis=1).astype(BF16)
    qkv, z, gate, q_pad, k, v = _even_proj(x, ctx, mod, norms[0:1], w_all, cos, sin, n_lat_tiles)

    head_ones = (jnp.arange(GDN_HD)[:, None] // GDN_DK == jnp.arange(GDN_HD)[None, :] // GDN_DK).astype(BF16)
    conv_w8 = jnp.pad(conv_w, ((0, SUBLANES - CONV_K), (0, 0)))
    gq, gk, gv = _gdn_prep(qkv, conv_w8, head_ones, n_lat_tiles)
    a_log_row = _pad_cols(a_log.reshape(1, n_gate), LANES)
    dt_bias_row = _pad_cols(dt_bias.reshape(1, n_gate), LANES)
    o_f, o_b = _gdn(gq, gk, gv, gate, a_log_row, dt_bias_row, T // GDN_CHUNK)

    sink_rows = jnp.repeat(sink, SWA_BLOCK)[:, None]
    swa = _swa(q_pad, k, v, sink_rows, T // SWA_BLOCK)

    w_s = w_out[GDN_HD:].reshape(SWA_Q_HEADS, SWA_HEAD_DIM, D)
    w_s = jnp.concatenate([jnp.where(kv_of_head.reshape(-1, 1, 1) == 0, w_s, 0.0),
                           jnp.where(kv_of_head.reshape(-1, 1, 1) == 1, w_s, 0.0)], axis=1)
    gdn_norm_row = jnp.tile(gdn_norm, GDN_HEADS)[None, :]
    return _even_tail(o_f, o_b, z, swa, x, ctx, mod, norms, gdn_norm_row, head_ones,
                      w_out[:GDN_HD].astype(BF16), w_s.reshape(SWA_Q_HEADS * LANES, D).astype(BF16),
                      ffn_gate.astype(BF16), ffn_up.astype(BF16), ffn_down.astype(BF16), n_lat_tiles)


def _odd_layer_last(xa, mod, norms, w_in, q_norm, kv_norm, w_uq, w_ukv, lam_p, lam_init, subln, w_out,
                    router, exp_gate, exp_up, exp_down, cos, sin, cos_r, sin_r, n_lat_tiles):
    B, S, D = xa.shape
    T = n_lat_tiles * ROW_TILE
    H = MLA_HEADS
    c0 = MLA_Q_RANK
    c1 = c0 + MLA_KV_RANK
    c2 = c1 + MLA_ROPE
    dw = DIFF_HEADS * 2 * DIFF_HEAD_DIM
    w_kr = jnp.pad(w_in[:, c1:c2], ((0, 0), (MLA_NOPE, LANES - MLA_NOPE - MLA_ROPE)))
    w_all = jnp.concatenate([w_in[:, :c1], w_kr, w_in[:, c2:c2 + dw] * DIFF_HEAD_DIM ** -0.5,
                             w_in[:, c2 + dw:]], axis=1).astype(BF16)
    qd = MLA_NOPE + MLA_ROPE
    wq = jnp.pad(w_uq.reshape(MLA_Q_RANK, H, qd) * qd ** -0.5, ((0, 0), (0, 0), (0, LANES - qd)))
    wkv = w_ukv.reshape(MLA_KV_RANK, H, MLA_NOPE + MLA_V)
    wk = jnp.pad(wkv[..., :MLA_NOPE], ((0, 0), (0, 0), (0, LANES - MLA_NOPE)))
    wv = jnp.pad(wkv[..., MLA_NOPE:], ((0, 0), (0, 0), (0, LANES - MLA_V)))
    q_cat, k_cat, v_ext, dq, dk, dv = _odd_proj(
        xa, mod, norms[0:1], w_all, cos, sin, cos_r, sin_r, q_norm[None, :], kv_norm[None, :],
        wq.reshape(MLA_Q_RANK, H * LANES).astype(BF16), wk.reshape(MLA_KV_RANK, H * LANES).astype(BF16),
        wv.reshape(MLA_KV_RANK, H * LANES).astype(BF16), n_lat_tiles)
    mla = _mla(q_cat, k_cat, v_ext, T)

    lam = (jnp.exp(jnp.sum(lam_p[0] * lam_p[1])) - jnp.exp(jnp.sum(lam_p[2] * lam_p[3])) + lam_init).reshape(1, 1)
    diff = _diff(dq, dk, dv, lam, subln[None, :], 1.0 - lam_init, T)

    w_mla = jnp.pad(w_out[:H * MLA_V].reshape(H, MLA_V, D), ((0, 0), (0, LANES - MLA_V), (0, 0)))
    x, f_in, logits = _mix_out1(mla, diff, xa, mod, norms[1:2], norms[2:3],
                                w_mla.reshape(H * LANES, D).astype(BF16), w_out[H * MLA_V:].astype(BF16),
                                _pad_cols(router, LANES), n_lat_tiles)

    dest, row_token, row_w, tile_expert, tile_rows = _route(logits.reshape(B * T, LANES)[:, :N_EXPERTS])
    x_sorted = f_in.reshape(B * T, D)[row_token]
    y = _moe_experts(tile_expert, tile_rows, x_sorted, row_w, exp_gate, exp_up, exp_down)
    y0 = y[dest[0]].reshape(B, T, D)
    y1 = y[dest[1]].reshape(B, T, D)
    return _norm_residual(y0, y1, x, mod, norms[3:4], n_lat_tiles)


def kernel(x, c, ctx, c_ctx, e_mod_w, e_mod_b, e_norms, e_w_in, e_conv_w, e_a_log, e_dt_bias, e_gdn_norm, e_sink, e_w_out, e_ffn_gate, e_ffn_up, e_ffn_down, o_mod_w, o_mod_b, o_norms, o_w_in, o_q_norm, o_kv_norm, o_w_uq, o_w_ukv, o_lambda, o_subln, o_w_out, o_router, o_exp_gate, o_exp_up, o_exp_down):
    B, T, D = x.shape
    n_lat_tiles = T // ROW_TILE
    rows = T // GRID_W
    cos64, sin64 = _stream_tables(*_rope_tables(rows, SWA_HEAD_DIM), 0)
    cos_r, sin_r = _stream_tables(*_rope_tables(rows, MLA_ROPE), MLA_NOPE)
    mod_e = _mod_table(c, c_ctx, e_mod_w[0], e_mod_b[0])
    xa = _even_layer(x, ctx, mod_e, e_norms[0], e_w_in[0], e_conv_w[0], e_a_log[0], e_dt_bias[0], e_gdn_norm[0],
                     e_sink[0], e_w_out[0], e_ffn_gate[0], e_ffn_up[0], e_ffn_down[0], cos64, sin64, n_lat_tiles)
    mod_o = _mod_table(c, c_ctx, o_mod_w[0], o_mod_b[0])
    lam_init = 0.8 - 0.6 * math.exp(-0.3 * 1)
    return _odd_layer_last(xa, mod_o, o_norms[0], o_w_in[0], o_q_norm[0], o_kv_norm[0], o_w_uq[0], o_w_ukv[0],
                           o_lambda[0], lam_init, o_subln[0], o_w_out[0], o_router[0], o_exp_gate[0],
                           o_exp_up[0], o_exp_down[0], cos64, sin64, cos_r, sin_r, n_lat_tiles)
```

```python
import functools
import math

import jax
import jax.numpy as jnp
from jax import lax
from jax.experimental import pallas as pl
from jax.experimental.pallas import tpu as pltpu

F32 = jnp.float32
BF16 = jnp.bfloat16

D_MODEL = 1024
CTX_LEN = 256
GRID_W = 64
NORM_EPS = 1e-6
ROPE_THETA = 10000.0

GDN_HEADS = 8
GDN_DK = 64
GDN_DV = 64
GDN_CHUNK = 64
CONV_K = 5
GDN_GROUP = 4
GDN_LANES = GDN_GROUP * GDN_DK
GDN_HD = GDN_HEADS * GDN_DK
GDN_NB = 4

SWA_Q_HEADS = 8
SWA_KV_HEADS = 2
SWA_HEAD_DIM = 64
SWA_WINDOW = 128
SWA_BLOCK = 128

MLA_HEADS = 8
MLA_Q_RANK = 384
MLA_KV_RANK = 256
MLA_NOPE = 64
MLA_ROPE = 32
MLA_V = 64

DIFF_HEADS = 4
DIFF_HEAD_DIM = 64

D_FF = 2816
N_EXPERTS = 8
TOP_K = 2
D_FF_EXPERT = 3584

LANES = 128
SUBLANES = 8
ROW_TILE = 256
LAT_TILE = 512
SWA_HEADS_PER_SUB = 1
ATTN_TQ = 2048
ATTN_SUB = 128
VMEM_LIMIT = 56 * 1024 * 1024
NEG_BIG = -1e30


def _dot(a, b):
    return jnp.dot(a, b, preferred_element_type=F32)


def _dot_nt(a, b):
    return lax.dot_general(a, b, (((1,), (1,)), ((), ())), preferred_element_type=F32)


def _dot_tn(a, b):
    return lax.dot_general(a, b, (((0,), (0,)), ((), ())), preferred_element_type=F32)


def _split_bf16(x):
    hi = x.astype(BF16)
    lo = (x - hi.astype(F32)).astype(BF16)
    return hi, lo


def _dot_split(x, w):
    hi, lo = _split_bf16(x)
    return _dot(hi, w) + _dot(lo, w)


def _dot_rsplit(w, x):
    hi, lo = _split_bf16(x)
    return _dot(w, hi) + _dot(w, lo)


def _rms(x, w):
    return x * lax.rsqrt(jnp.mean(x * x, axis=-1, keepdims=True) + NORM_EPS) * w


def _silu(x):
    return x * jax.nn.sigmoid(x)


def _rope_slab(x, cos, sin, quarter):
    lane = lax.broadcasted_iota(jnp.int32, x.shape, 1)
    fwd = pltpu.roll(x, quarter, 1)
    back = pltpu.roll(x, LANES - quarter, 1)
    rot = jnp.where(lane % (2 * quarter) < quarter, -back, fwd)
    return x * cos + rot * sin


def _params(*sem):
    return pltpu.CompilerParams(dimension_semantics=sem, vmem_limit_bytes=VMEM_LIMIT)


def _row_spec(width, tile=ROW_TILE):
    return pl.BlockSpec((1, tile, width), lambda b, t: (b, t, 0))


def _mod_spec(n_lat_tiles):
    return pl.BlockSpec((1, 1, 6, D_MODEL), lambda b, t: (b, jnp.where(t < n_lat_tiles, 1, 0), 0, 0))


def _const_spec(shape):
    return pl.BlockSpec(shape, lambda b, t: (0,) * len(shape))


def _table_spec():
    return pl.BlockSpec((ROW_TILE, LANES), lambda b, t: (t, 0))


E_QKV = (0, 3 * GDN_HD)
E_Z = (E_QKV[1], E_QKV[1] + GDN_HD)
E_GATE = (E_Z[1], E_Z[1] + LANES)
E_Q = (E_GATE[1], E_GATE[1] + SWA_Q_HEADS * LANES)
E_K = (E_Q[1], E_Q[1] + LANES)
E_V = (E_K[1], E_K[1] + LANES)


def _stream_in_specs(n_lat_tiles):
    assert CTX_LEN == ROW_TILE
    return [pl.BlockSpec((1, ROW_TILE, D_MODEL), lambda b, t: (b, jnp.minimum(t, n_lat_tiles - 1), 0)),
            pl.BlockSpec((1, ROW_TILE, D_MODEL), lambda b, t: (b, 0, 0))]


def _stream_tile(x_ref, c_ref, n_lat_tiles):
    return jnp.where(pl.program_id(1) < n_lat_tiles, x_ref[0], c_ref[0])


def _even_proj_kernel(x_ref, c_ref, mod_ref, nw_ref, w_ref, cos_ref, sin_ref,
                      qkv_ref, z_ref, gate_ref, q_ref, k_ref, v_ref, *, n_lat_tiles):
    xin = _stream_tile(x_ref, c_ref, n_lat_tiles)
    a = (_rms(xin, nw_ref[...]) * (1.0 + mod_ref[0, 0, 1:2, :]) + mod_ref[0, 0, 0:1, :]).astype(BF16)

    def proj(cols):
        return _dot(a, w_ref[:, cols[0]:cols[1]])

    qkv_ref[0] = proj(E_QKV)
    z_ref[0] = proj(E_Z).astype(BF16)
    gate_ref[0] = proj(E_GATE)
    cos = cos_ref[...]
    sin = sin_ref[...]
    quarter = SWA_HEAD_DIM // 4
    qp = proj(E_Q)
    for h in range(SWA_Q_HEADS):
        ls = slice(h * LANES, (h + 1) * LANES)
        q_ref[0, :, ls] = _rope_slab(qp[:, ls], cos, sin, quarter).astype(BF16)
    k_ref[0] = _rope_slab(proj(E_K), cos, sin, quarter).astype(BF16)
    v_ref[0] = proj(E_V).astype(BF16)


def _even_proj(x, ctx, mod, norm_w, w, cos, sin, n_lat_tiles):
    B, T, D = x.shape
    S = T + ctx.shape[1]
    widths = (E_QKV[1] - E_QKV[0], GDN_HD, LANES, SWA_Q_HEADS * LANES, LANES, LANES)
    dtypes = (F32, BF16, F32, BF16, BF16, BF16)
    return pl.pallas_call(
        functools.partial(_even_proj_kernel, n_lat_tiles=n_lat_tiles),
        grid=(B, S // ROW_TILE),
        in_specs=_stream_in_specs(n_lat_tiles) + [_mod_spec(n_lat_tiles), _const_spec((1, D)), _const_spec(w.shape),
                                                  _table_spec(), _table_spec()],
        out_specs=[_row_spec(n) for n in widths],
        out_shape=[jax.ShapeDtypeStruct((B, S, n), dt) for n, dt in zip(widths, dtypes)],
        compiler_params=_params("parallel", "parallel"),
        name="even_proj",
    )(x, ctx, mod, norm_w, w, cos, sin)


def _gdn_prep_kernel(x_ref, prev_ref, next_ref, cw_ref, bd_ref, q_ref, k_ref, v_ref, *, n_lat_tiles):
    t = pl.program_id(1)
    has_prev = jnp.logical_and(t > 0, t < n_lat_tiles)
    has_next = t < n_lat_tiles - 1
    prev = jnp.where(has_prev, prev_ref[0], 0.0)
    nxt = jnp.where(has_next, next_ref[0], 0.0)
    xe = jnp.concatenate([prev, x_ref[0], nxt], axis=0)
    pad = CONV_K // 2
    y = jnp.zeros(x_ref.shape[1:], F32)
    for kk in range(CONV_K):
        lo = SUBLANES + kk - pad
        y = y + cw_ref[kk:kk + 1, :] * xe[lo:lo + ROW_TILE, :]
    u = _silu(y)

    def headnorm(a):
        return a * lax.rsqrt(_dot_split(a * a, bd_ref[...]) + NORM_EPS)

    q_ref[0] = headnorm(u[:, :GDN_HD]) * GDN_DK ** -0.5
    k_ref[0] = headnorm(u[:, GDN_HD:2 * GDN_HD])
    v_ref[0] = u[:, 2 * GDN_HD:]


def _gdn_prep(qkv, conv_w, head_ones, n_lat_tiles):
    B, S, W = qkv.shape
    per_tile = ROW_TILE // SUBLANES
    n8 = S // SUBLANES
    return pl.pallas_call(
        functools.partial(_gdn_prep_kernel, n_lat_tiles=n_lat_tiles),
        grid=(B, S // ROW_TILE),
        in_specs=[_row_spec(W),
                  pl.BlockSpec((1, SUBLANES, W), lambda b, t: (b, jnp.maximum(t * per_tile - 1, 0), 0)),
                  pl.BlockSpec((1, SUBLANES, W), lambda b, t: (b, jnp.minimum((t + 1) * per_tile, n8 - 1), 0)),
                  _const_spec(conv_w.shape), _const_spec(head_ones.shape)],
        out_specs=[_row_spec(GDN_HD)] * 3,
        out_shape=[jax.ShapeDtypeStruct((B, S, GDN_HD), F32)] * 3,
        compiler_params=_params("parallel", "parallel"),
        name="gdn_prep",
    )(qkv, qkv, qkv, conv_w, head_ones)


def _block_diag(x, head_of_lane):
    zero = jnp.zeros_like(x)
    return jnp.concatenate([jnp.where(head_of_lane == h, x, zero) for h in range(GDN_GROUP)], axis=0)


def _expand_heads(x4, head_of_lane):
    c = x4.shape[0]
    out = jnp.broadcast_to(x4[:, GDN_GROUP - 1:GDN_GROUP], (c, GDN_LANES))
    for h in range(GDN_GROUP - 2, -1, -1):
        out = jnp.where(head_of_lane == h, jnp.broadcast_to(x4[:, h:h + 1], (c, GDN_LANES)), out)
    return out


def _gdn_groups(items):
    C = GDN_CHUNK
    row = lax.broadcasted_iota(jnp.int32, (C, GDN_LANES), 0)
    lane = lax.broadcasted_iota(jnp.int32, (C, GDN_LANES), 1)
    hol = lane // GDN_DK
    col = lane % GDN_DK
    eye = jnp.where(row == col, 1.0, 0.0)
    rr = lax.broadcasted_iota(jnp.int32, (2 * C, 4 * C), 0)
    cc = lax.broadcasted_iota(jnp.int32, (2 * C, 4 * C), 1)
    cum_lhs = {}
    for rev in (False, True):
        before = (rr - cc % C) * (-1 if rev else 1) >= 0
        cum_lhs[rev] = jnp.where(((rr < C) & (cc < 2 * C) & before) | ((rr >= C) & (cc >= 2 * C)),
                                 1.0, 0.0).astype(BF16)

    def bd(x):
        return _block_diag(x, hol)

    st = []
    for (q4, k4, v4, g4, b4, s4, reverse) in items:
        sgn = -1 if reverse else 1
        ahead = (row - col) * sgn
        incl = ahead >= 0
        strict = ahead > 0
        gE = _expand_heads(g4, hol)
        bE = _expand_heads(b4, hol)
        g_hi, g_lo = _split_bf16(gE)
        m_hi, m_lo = _split_bf16(jnp.where((col - row) * sgn >= 0, gE, 0.0))
        both = _dot(cum_lhs[reverse], jnp.concatenate([g_hi, g_lo, m_hi, m_lo], axis=0))
        gc = both[:C]
        gc_row = both[C:]
        decay = jnp.where(incl, jnp.exp(gc - gc_row), 0.0)
        g_last = gc[0:1, :] if reverse else gc[C - 1:C, :]
        eg = jnp.exp(gc)
        kb = k4 * bE
        st.append(dict(q=q4, k=k4, s=s4, incl=incl, strict=strict, decay=decay, g_last=g_last,
                       kb=kb, vb=v4 * bE, kbg=kb * eg, qh=q4 * eg, kt=k4 * jnp.exp(g_last - gc)))

    for d in st:
        gram = _dot_nt(jnp.concatenate([d["kb"], d["q"]], axis=0).astype(BF16), bd(d["k"].astype(BF16)))
        L = jnp.where(d["strict"], gram[:C] * d["decay"], 0.0)
        d["A"] = gram[C:] * d["decay"]
        d["T"] = eye - L
        d["P"] = L.astype(BF16)
    n_factors = int(math.log2(C)) - 1
    for d in st:
        d["P"] = _dot(d["P"], bd(d["P"])).astype(BF16)
    for it in range(n_factors):
        last = it == n_factors - 1
        for d in st:
            lhs = d["T"].astype(BF16) if last else jnp.concatenate([d["T"].astype(BF16), d["P"]], axis=0)
            prod = _dot(lhs, bd(d["P"]))
            d["T"] = d["T"] + prod[:C]
            if not last:
                d["P"] = prod[C:].astype(BF16)
    for d in st:
        Tb = d["T"].astype(BF16)
        d["u"] = _dot(Tb, bd(d["vb"].astype(BF16)))
        d["w"] = _dot(Tb, bd(d["kbg"].astype(BF16)))
    for d in st:
        d["ws_qs"] = _dot(jnp.concatenate([d["w"], d["qh"]], axis=0).astype(BF16), bd(d["s"].astype(BF16)))
    out = []
    for d in st:
        v_new = d["u"] - d["ws_qs"][:C]
        vb16 = v_new.astype(BF16)
        o = d["ws_qs"][C:] + _dot(d["A"].astype(BF16), bd(vb16))
        full = _dot_tn(d["kt"].astype(BF16), vb16)
        upd = jnp.zeros((GDN_DK, GDN_LANES), F32)
        for h in range(GDN_GROUP):
            upd = upd + jnp.where(hol == h, full[h * GDN_DK:(h + 1) * GDN_DK, :], 0.0)
        out.append((o, d["s"] * jnp.exp(d["g_last"]) + upd))
    return out


def _gdn_kernel(qf_ref, kf_ref, vf_ref, gf_ref, qb_ref, kb_ref, vb_ref, gb_ref, al_ref, dtb_ref,
                of_ref, ob_ref, s_ref):
    @pl.when(pl.program_id(1) == 0)
    def _():
        s_ref[...] = jnp.zeros_like(s_ref)

    n_grp = GDN_HEADS // GDN_GROUP
    n_gate = 2 * GDN_HEADS
    items = []
    slots = []
    for nb in range(GDN_NB):
        for d, (q_ref, k_ref, v_ref, g_ref, o_ref) in enumerate(((qf_ref, kf_ref, vf_ref, gf_ref, of_ref),
                                                                 (qb_ref, kb_ref, vb_ref, gb_ref, ob_ref))):
            raw = g_ref[nb]
            xs = raw + dtb_ref[...]
            softplus = jnp.maximum(xs, 0.0) + jnp.log(1.0 + jnp.exp(-jnp.abs(xs)))
            g = -jnp.exp(al_ref[...]) * softplus
            beta = jax.nn.sigmoid(raw)
            for grp in range(n_grp):
                ls = slice(grp * GDN_LANES, (grp + 1) * GDN_LANES)
                c0 = d * GDN_HEADS + grp * GDN_GROUP
                slot = (nb * 2 + d) * n_grp + grp
                items.append((q_ref[nb, :, ls], k_ref[nb, :, ls], v_ref[nb, :, ls],
                              g[:, c0:c0 + GDN_GROUP], beta[:, n_gate + c0:n_gate + c0 + GDN_GROUP],
                              s_ref[slot], d == 1))
                slots.append((o_ref, nb, ls, slot))
    for (o_ref, nb, ls, slot), (o, s_new) in zip(slots, _gdn_groups(items)):
        o_ref[nb, :, ls] = o
        s_ref[slot] = s_new


def _gdn(q, k, v, gate, a_log_row, dt_bias_row, n_lat_chunks):
    B, S, HD = q.shape
    n_chunks = S // GDN_CHUNK
    n_ctx = n_chunks - n_lat_chunks

    def fwd_c(s):
        return jnp.where(s < n_ctx, n_lat_chunks + s, s - n_ctx)

    def bwd_c(s):
        return n_chunks - 1 - s

    def spec(cmap, width):
        return pl.BlockSpec((GDN_NB, GDN_CHUNK, width), lambda b, s: (b, cmap(s), 0))

    return pl.pallas_call(
        _gdn_kernel,
        grid=(B // GDN_NB, n_chunks),
        in_specs=[spec(fwd_c, HD), spec(fwd_c, HD), spec(fwd_c, HD), spec(fwd_c, LANES),
                  spec(bwd_c, HD), spec(bwd_c, HD), spec(bwd_c, HD), spec(bwd_c, LANES),
                  _const_spec((1, LANES)), _const_spec((1, LANES))],
        out_specs=[spec(fwd_c, HD), spec(bwd_c, HD)],
        out_shape=[jax.ShapeDtypeStruct((B, S, HD), F32)] * 2,
        scratch_shapes=[pltpu.VMEM((GDN_NB * 2 * GDN_HEADS // GDN_GROUP, GDN_DK, GDN_LANES), F32)],
        compiler_params=_params("parallel", "arbitrary"),
        name="gdn_scan",
    )(q, k, v, gate, q, k, v, gate, a_log_row, dt_bias_row)


def _swa_kernel(q_ref, k_ref, v_ref, sink_ref, o_ref, *, n_lat_blocks):
    W = SWA_BLOCK
    H = SWA_Q_HEADS
    n = pl.program_id(1)
    is_lat = n < n_lat_blocks
    nl = jnp.minimum(n, n_lat_blocks - 1)
    prev = jnp.maximum(nl - 1, 0)
    nxt = jnp.minimum(nl + 1, n_lat_blocks - 1)
    ctx0 = n_lat_blocks * W

    def keys(ref):
        def blk(i):
            return ref[0, pl.ds(pl.multiple_of(i * W, W), W), :]
        return jnp.concatenate([ref[0, ctx0:ctx0 + CTX_LEN, :], blk(prev), blk(nl), blk(nxt)], axis=0)

    k_all = keys(k_ref)
    v_all = keys(v_ref)
    nk = CTX_LEN + 3 * W
    v_ext = jnp.concatenate([v_all, jnp.ones((nk, LANES), BF16)], axis=1)
    hp = SWA_HEADS_PER_SUB
    ii = lax.broadcasted_iota(jnp.int32, (hp * W, nk), 0) % W
    jj = lax.broadcasted_iota(jnp.int32, (hp * W, nk), 1) - CTX_LEN
    in_window = (((jj >= 0) & (jj < W) & (jj >= ii) & (nl > 0))
                 | ((jj >= W) & (jj < 2 * W))
                 | ((jj >= 2 * W) & (jj - 2 * W <= ii) & (nl < n_lat_blocks - 1)))
    valid = (jj < 0) | (in_window & is_lat)

    def scores(i):
        q = jnp.concatenate([q_ref[0, :, h * LANES:(h + 1) * LANES] for h in range(i * hp, (i + 1) * hp)], axis=0)
        return jnp.where(valid, _dot_nt(q, k_all), NEG_BIG)

    def finish(i, s):
        sk = sink_ref[i * hp * W:(i + 1) * hp * W, :]
        m = jnp.maximum(jnp.max(s, axis=-1, keepdims=True), sk)
        acc = _dot(jnp.exp(s - m).astype(BF16), v_ext)
        o = acc[:, :LANES] / (acc[:, LANES:LANES + 1] + jnp.exp(sk - m))
        for j in range(hp):
            h = i * hp + j
            o_ref[0, :, h * LANES:(h + 1) * LANES] = o[j * W:(j + 1) * W].astype(o_ref.dtype)

    _one_ahead(H // hp, scores, finish)


def _swa(q_pad, k, v, sink_rows, n_lat_blocks):
    B, S, _ = k.shape
    W = SWA_BLOCK
    QW = SWA_Q_HEADS * LANES
    return pl.pallas_call(
        functools.partial(_swa_kernel, n_lat_blocks=n_lat_blocks),
        grid=(B, S // W),
        in_specs=[pl.BlockSpec((1, W, QW), lambda b, n: (b, n, 0)),
                  pl.BlockSpec((1, S, LANES), lambda b, n: (b, 0, 0)),
                  pl.BlockSpec((1, S, LANES), lambda b, n: (b, 0, 0)),
                  _const_spec(sink_rows.shape)],
        out_specs=pl.BlockSpec((1, W, QW), lambda b, n: (b, n, 0)),
        out_shape=jax.ShapeDtypeStruct((B, S, QW), BF16),
        compiler_params=_params("parallel", "parallel"),
        name="swa",
    )(q_pad, k, v, sink_rows)


def _even_tail_kernel(of_ref, ob_ref, z_ref, swa_ref, x_ref, c_ref, mod_ref, norms_ref, gn_ref, bd_ref,
                      wo_g_ref, wo_s_ref, wg_ref, wu_ref, wd_ref, o_ref, *, n_lat_tiles):
    o = of_ref[0] + ob_ref[0]
    ms = _dot_split(o * o, bd_ref[...]) * (1.0 / GDN_DV)
    gdn = o * lax.rsqrt(ms + NORM_EPS) * gn_ref[...] * _silu(z_ref[0].astype(F32))
    y = _dot(gdn.astype(BF16), wo_g_ref[...]) + _dot(swa_ref[0], wo_s_ref[...])
    x1 = _stream_tile(x_ref, c_ref, n_lat_tiles) + mod_ref[0, 0, 2:3, :] * _rms(y, norms_ref[1:2, :])
    f = (_rms(x1, norms_ref[2:3, :]) * (1.0 + mod_ref[0, 0, 4:5, :]) + mod_ref[0, 0, 3:4, :]).astype(BF16)
    h = _silu(_dot(f, wg_ref[...])) * _dot(f, wu_ref[...])
    y2 = _dot(h.astype(BF16), wd_ref[...])
    o_ref[0] = x1 + mod_ref[0, 0, 5:6, :] * _rms(y2, norms_ref[3:4, :])


def _resident_spec(shape):
    return pl.BlockSpec(shape, lambda b, t: (0,) * len(shape), pipeline_mode=pl.Buffered(1))


def _even_tail(o_f, o_b, z, swa, x, ctx, mod, norms, gdn_norm_row, head_ones, wo_gdn, wo_swa, wg, wu, wd,
               n_lat_tiles):
    B, S, _ = o_f.shape
    D = x.shape[-1]
    consts = (norms, gdn_norm_row, head_ones, wo_gdn, wo_swa, wg, wu, wd)
    return pl.pallas_call(
        functools.partial(_even_tail_kernel, n_lat_tiles=n_lat_tiles),
        grid=(B, S // ROW_TILE),
        in_specs=[_row_spec(GDN_HD), _row_spec(GDN_HD), _row_spec(GDN_HD), _row_spec(swa.shape[-1])]
        + _stream_in_specs(n_lat_tiles) + [_mod_spec(n_lat_tiles)] + [_resident_spec(a.shape) for a in consts],
        out_specs=_row_spec(D),
        out_shape=jax.ShapeDtypeStruct((B, S, D), F32),
        compiler_params=_params("parallel", "parallel"),
        name="even_tail",
    )(o_f, o_b, z, swa, x, ctx, mod, *consts)


O_CQ = (0, MLA_Q_RANK)
O_CKV = (O_CQ[1], O_CQ[1] + MLA_KV_RANK)
O_KR = (O_CKV[1], O_CKV[1] + LANES)
O_DQ = (O_KR[1], O_KR[1] + DIFF_HEADS * LANES)
O_DK = (O_DQ[1], O_DQ[1] + DIFF_HEADS * LANES)
O_DV = (O_DK[1], O_DK[1] + DIFF_HEADS * LANES)


def _odd_proj_kernel(x_ref, mod_ref, nw_ref, w_ref, cos_ref, sin_ref, cosr_ref, sinr_ref,
                     qn_ref, kvn_ref, wq_ref, wk_ref, wv_ref,
                     q_ref, k_ref, v_ref, dq_ref, dk_ref, dv_ref):
    a = (_rms(x_ref[0], nw_ref[...]) * (1.0 + mod_ref[0, 0, 1:2, :]) + mod_ref[0, 0, 0:1, :]).astype(BF16)

    def proj(cols):
        return _dot(a, w_ref[:, cols[0]:cols[1]])

    cos_r = cosr_ref[...]
    sin_r = sinr_ref[...]
    rq = MLA_ROPE // 4
    cq = _rms(proj(O_CQ), qn_ref[...]).astype(BF16)
    ckv = _rms(proj(O_CKV), kvn_ref[...]).astype(BF16)
    kr = _rope_slab(proj(O_KR), cos_r, sin_r, rq)
    qp = _dot(cq, wq_ref[...])
    kp = _dot(ckv, wk_ref[...])
    vp = _dot(ckv, wv_ref[...])
    lane = lax.broadcasted_iota(jnp.int32, (ROW_TILE, LANES), 1)
    ones_hi = jnp.where(lane >= MLA_V, 1.0, 0.0)
    for h in range(MLA_HEADS):
        ls = slice(h * LANES, (h + 1) * LANES)
        q_ref[0, :, ls] = _rope_slab(qp[:, ls], cos_r, sin_r, rq).astype(BF16)
        k_ref[0, :, ls] = (kp[:, ls] + kr).astype(BF16)
        v_ref[0, :, ls] = (vp[:, ls] + ones_hi).astype(BF16)
    cos = cos_ref[...]
    sin = sin_ref[...]
    quarter = DIFF_HEAD_DIM // 4
    dq = proj(O_DQ)
    dk = proj(O_DK)
    dv = proj(O_DV)
    ones = jnp.ones((ROW_TILE, LANES), BF16)
    for h in range(DIFF_HEADS):
        ls = slice(h * LANES, (h + 1) * LANES)
        dq_ref[0, :, ls] = _rope_slab(dq[:, ls], cos, sin, quarter).astype(BF16)
        dk_ref[0, :, ls] = _rope_slab(dk[:, ls], cos, sin, quarter).astype(BF16)
        dv_ref[0, :, 2 * h * LANES:(2 * h + 1) * LANES] = dv[:, ls].astype(BF16)
        dv_ref[0, :, (2 * h + 1) * LANES:(2 * h + 2) * LANES] = ones


def _odd_proj(xa, mod, norm_w, w, cos, sin, cos_r, sin_r, q_norm, kv_norm, wq, wk, wv, n_lat_tiles):
    B, S, D = xa.shape
    HW = MLA_HEADS * LANES
    widths = (HW, HW, HW, DIFF_HEADS * LANES, DIFF_HEADS * LANES, 2 * DIFF_HEADS * LANES)
    return pl.pallas_call(
        _odd_proj_kernel,
        grid=(B, S // ROW_TILE),
        in_specs=[_row_spec(D), _mod_spec(n_lat_tiles), _const_spec((1, D)), _const_spec(w.shape),
                  _table_spec(), _table_spec(), _table_spec(), _table_spec(),
                  _const_spec(q_norm.shape), _const_spec(kv_norm.shape),
                  _const_spec(wq.shape), _const_spec(wk.shape), _const_spec(wv.shape)],
        out_specs=[_row_spec(n) for n in widths],
        out_shape=[jax.ShapeDtypeStruct((B, S, n), BF16) for n in widths],
        compiler_params=_params("parallel", "parallel"),
        name="odd_proj",
    )(xa, mod, norm_w, w, cos, sin, cos_r, sin_r, q_norm, kv_norm, wq, wk, wv)


def _one_ahead(n_sub, scores, finish):
    s_prev = scores(0)
    for i in range(1, n_sub):
        s_next = scores(i)
        finish(i - 1, s_prev)
        s_prev = s_next
    finish(n_sub - 1, s_prev)


def _mla_kernel(q_ref, k_ref, v_ref, o_ref):
    k = k_ref[0]
    v = v_ref[0]

    def scores(i):
        return _dot_nt(q_ref[0, i * ATTN_SUB:(i + 1) * ATTN_SUB, :], k)

    def finish(i, s):
        m = jnp.max(s, axis=-1, keepdims=True)
        acc = _dot(jnp.exp(s - m).astype(BF16), v)
        o_ref[0, i * ATTN_SUB:(i + 1) * ATTN_SUB, :] = (acc / acc[:, MLA_V:MLA_V + 1]).astype(o_ref.dtype)

    _one_ahead(q_ref.shape[1] // ATTN_SUB, scores, finish)


def _mla(q, k, v_ext, T):
    B, S, _ = k.shape
    return pl.pallas_call(
        _mla_kernel,
        grid=(B, MLA_HEADS, T // ATTN_TQ),
        in_specs=[pl.BlockSpec((1, ATTN_TQ, LANES), lambda b, h, t: (b, t, h)),
                  pl.BlockSpec((1, S, LANES), lambda b, h, t: (b, 0, h)),
                  pl.BlockSpec((1, S, LANES), lambda b, h, t: (b, 0, h))],
        out_specs=pl.BlockSpec((1, ATTN_TQ, LANES), lambda b, h, t: (b, t, h)),
        out_shape=jax.ShapeDtypeStruct((B, T, MLA_HEADS * LANES), BF16),
        compiler_params=_params("parallel", "parallel", "parallel"),
        name="mla_attn",
    )(q, k, v_ext)


def _diff_kernel(q_ref, k_ref, v_ref, lam_ref, sub_ref, o_ref, *, post_scale):
    k = k_ref[0]
    v = v_ref[0]
    sub = ATTN_SUB
    vw = 2 * DIFF_HEAD_DIM
    lo = lax.broadcasted_iota(jnp.int32, (sub, LANES), 1) < DIFF_HEAD_DIM
    zero = jnp.zeros((sub, LANES), BF16)

    def scores(i):
        q = q_ref[0, i * sub:(i + 1) * sub, :]
        return _dot_nt(jnp.concatenate([jnp.where(lo, q, zero), jnp.where(lo, zero, q)], axis=0), k)

    def finish(i, s):
        m = jnp.max(s, axis=-1, keepdims=True)
        acc = _dot(jnp.exp(s - m).astype(BF16), v)
        att = acc[:, :vw] / acc[:, vw:vw + 1]
        a = att[:sub] - lam_ref[...] * att[sub:]
        o_ref[0, i * sub:(i + 1) * sub, :] = (_rms(a, sub_ref[...]) * post_scale).astype(o_ref.dtype)

    _one_ahead(q_ref.shape[1] // sub, scores, finish)


def _diff(q, k, v_ext, lam, subln, post_scale, T):
    B, S, _ = k.shape
    tq = ATTN_TQ // 2
    return pl.pallas_call(
        functools.partial(_diff_kernel, post_scale=post_scale),
        grid=(B, DIFF_HEADS, T // tq),
        in_specs=[pl.BlockSpec((1, tq, LANES), lambda b, h, t: (b, t, h)),
                  pl.BlockSpec((1, S, LANES), lambda b, h, t: (b, 0, h)),
                  pl.BlockSpec((1, S, 2 * LANES), lambda b, h, t: (b, 0, h)),
                  pl.BlockSpec((1, 1), lambda b, h, t: (0, 0)),
                  pl.BlockSpec((1, LANES), lambda b, h, t: (0, 0))],
        out_specs=pl.BlockSpec((1, tq, LANES), lambda b, h, t: (b, t, h)),
        out_shape=jax.ShapeDtypeStruct((B, T, DIFF_HEADS * LANES), BF16),
        compiler_params=_params("parallel", "parallel", "parallel"),
        name="diff_attn",
    )(q, k, v_ext, lam, subln)


def _mix_out1_kernel(mla_ref, diff_ref, x_ref, mod_ref, n1_ref, n2_ref, w1_ref, w2_ref, wrh_ref, wrl_ref,
                     xo_ref, f_ref, lg_ref):
    y = _dot(mla_ref[0], w1_ref[...]) + _dot(diff_ref[0], w2_ref[...])
    x = x_ref[0] + mod_ref[0, 0, 2:3, :] * _rms(y, n1_ref[...])
    xo_ref[0] = x
    f = _rms(x, n2_ref[...]) * (1.0 + mod_ref[0, 0, 4:5, :]) + mod_ref[0, 0, 3:4, :]
    f_ref[0] = f
    f_hi, f_lo = _split_bf16(f)
    lg_ref[0] = _dot(f_hi, wrh_ref[...]) + (_dot(f_hi, wrl_ref[...]) + _dot(f_lo, wrh_ref[...]))


def _mix_out1(mla, diff, xa, mod, n1, n2, w1, w2, wr, n_lat_tiles):
    B, T, _ = mla.shape
    D = xa.shape[-1]
    wr_hi, wr_lo = _split_bf16(wr)
    return pl.pallas_call(
        _mix_out1_kernel,
        grid=(B, T // ROW_TILE),
        in_specs=[_row_spec(mla.shape[-1]), _row_spec(diff.shape[-1]), _row_spec(D), _mod_spec(n_lat_tiles),
                  _const_spec((1, D)), _const_spec((1, D)), _const_spec(w1.shape), _const_spec(w2.shape),
                  _const_spec(wr.shape), _const_spec(wr.shape)],
        out_specs=[_row_spec(D), _row_spec(D), _row_spec(LANES)],
        out_shape=[jax.ShapeDtypeStruct((B, T, D), F32), jax.ShapeDtypeStruct((B, T, D), F32),
                   jax.ShapeDtypeStruct((B, T, LANES), F32)],
        compiler_params=_params("parallel", "parallel"),
        name="mix_out1",
    )(mla, diff, xa, mod, n1, n2, w1, w2, wr_hi, wr_lo)


MOE_TM = 1024
MOE_TF = 512
MOE_SUB = 256


def _moe_kernel(te_ref, tr_ref, x_ref, wg_ref, wu_ref, wd_ref, o_ref, acc_ref):
    i = pl.program_id(0)
    f = pl.program_id(1)
    n_rows = tr_ref[i]
    n_sub = MOE_TM // MOE_SUB

    @pl.when(f == 0)
    def _():
        acc_ref[...] = jnp.zeros_like(acc_ref)

    def sub_tile_fns():
        wg = wg_ref[0].astype(BF16)
        wu = wu_ref[0].astype(BF16)
        wd = wd_ref[0].astype(BF16)

        def gate_up(j):
            x = x_ref[j * MOE_SUB:(j + 1) * MOE_SUB, :].astype(BF16)
            return _dot(x, wg), _dot(x, wu)

        def down(j, gu):
            h = (_silu(gu[0]) * gu[1]).astype(BF16)
            acc_ref[j * MOE_SUB:(j + 1) * MOE_SUB, :] += _dot(h, wd)

        return gate_up, down

    all_subs = n_rows > (n_sub - 1) * MOE_SUB

    @pl.when(all_subs)
    def _():
        _one_ahead(n_sub, *sub_tile_fns())

    for j in range(n_sub - 1):
        @pl.when(jnp.logical_and(jnp.logical_not(all_subs), n_rows > j * MOE_SUB))
        def _(j=j):
            gate_up, down = sub_tile_fns()
            down(j, gate_up(j))

    @pl.when(f == pl.num_programs(1) - 1)
    def _():
        o_ref[...] = acc_ref[...]


def _moe_experts(tile_expert, tile_rows, x_sorted, wg, wu, wd):
    R, D = x_sorted.shape
    E, _, F = wg.shape
    nf = F // MOE_TF

    def f_eff(i, f, ta):
        return jnp.where(ta[i] > 0, f, nf - 1)

    grid_spec = pltpu.PrefetchScalarGridSpec(
        num_scalar_prefetch=2,
        grid=(R // MOE_TM, nf),
        in_specs=[pl.BlockSpec((MOE_TM, D), lambda i, f, te, ta: (i, 0)),
                  pl.BlockSpec((1, D, MOE_TF), lambda i, f, te, ta: (te[i], 0, f_eff(i, f, ta))),
                  pl.BlockSpec((1, D, MOE_TF), lambda i, f, te, ta: (te[i], 0, f_eff(i, f, ta))),
                  pl.BlockSpec((1, MOE_TF, D), lambda i, f, te, ta: (te[i], f_eff(i, f, ta), 0))],
        out_specs=pl.BlockSpec((MOE_TM, D), lambda i, f, te, ta: (i, 0)),
        scratch_shapes=[pltpu.VMEM((MOE_TM, D), F32)],
    )
    return pl.pallas_call(
        _moe_kernel,
        grid_spec=grid_spec,
        out_shape=jax.ShapeDtypeStruct((R, D), F32),
        compiler_params=_params("arbitrary", "arbitrary"),
        name="moe_experts",
    )(tile_expert, tile_rows, x_sorted, wg, wu, wd)


def _moe_combine_kernel(a_ref, b_ref, g_ref, x_ref, mod_ref, nw_ref, o_ref):
    g = g_ref[0]
    y = g[:, 0:1] * a_ref[0] + g[:, 1:2] * b_ref[0]
    o_ref[0] = x_ref[0] + mod_ref[0, 0, 5:6, :] * _rms(y, nw_ref[...])


def _moe_combine(a, b, gates, x, mod, norm_w, n_lat_tiles):
    B, T, D = x.shape
    return pl.pallas_call(
        _moe_combine_kernel,
        grid=(B, T // LAT_TILE),
        in_specs=[_row_spec(D, LAT_TILE), _row_spec(D, LAT_TILE), _row_spec(TOP_K, LAT_TILE), _row_spec(D, LAT_TILE),
                  _mod_spec(n_lat_tiles), _const_spec((1, D))],
        out_specs=_row_spec(D, LAT_TILE),
        out_shape=jax.ShapeDtypeStruct((B, T, D), F32),
        compiler_params=_params("parallel", "parallel"),
        name="moe_combine",
    )(a, b, gates, x, mod, norm_w)


def _route(logits):
    N = logits.shape[0]
    M = N * TOP_K
    R = M + N_EXPERTS * MOE_TM
    top_val, top_idx = lax.top_k(logits, TOP_K)
    gates = jax.nn.softmax(top_val, axis=-1)
    e_flat = top_idx.T.reshape(-1)
    onehot = (e_flat[:, None] == jnp.arange(N_EXPERTS)[None, :]).astype(jnp.int32)
    counts = jnp.sum(onehot, axis=0)
    padded = ((counts + MOE_TM - 1) // MOE_TM) * MOE_TM
    ends = jnp.cumsum(padded)
    starts = ends - padded
    ustarts = jnp.cumsum(counts) - counts
    order = jnp.argsort(e_flat, stable=True).astype(jnp.int32)
    rank = jnp.argsort(order).astype(jnp.int32)
    dest = rank + jnp.sum(onehot * (starts - ustarts)[None, :], axis=-1)
    n_tiles = R // MOE_TM
    tile_start = jnp.arange(n_tiles, dtype=jnp.int32) * MOE_TM
    tile_active = (tile_start < ends[-1]).astype(jnp.int32)
    te = jnp.minimum(jnp.searchsorted(ends, tile_start, side="right"), N_EXPERTS - 1).astype(jnp.int32)
    last_e = te[jnp.maximum(ends[-1] // MOE_TM - 1, 0)]
    tile_expert = jnp.where(tile_active > 0, te, last_e)
    in_tile = jnp.arange(MOE_TM, dtype=jnp.int32)[None, :]
    within = (tile_start - starts[tile_expert])[:, None] + in_tile
    valid = (within < counts[tile_expert][:, None]) & (tile_active[:, None] > 0)
    src = order[jnp.clip(ustarts[tile_expert][:, None] + within, 0, M - 1).reshape(R)]
    valid = valid.reshape(R)
    tile_rows = jnp.clip(counts[tile_expert] - (tile_start - starts[tile_expert]), 0, MOE_TM) * tile_active
    row_token = jnp.where(valid, src, jnp.arange(R, dtype=jnp.int32)) % N
    return gates, dest.reshape(TOP_K, N), row_token, tile_expert, tile_rows.astype(jnp.int32)


def _rope_tables(rows, rot_dim):
    row = jnp.repeat(jnp.arange(rows, dtype=F32), GRID_W)
    col = jnp.tile(jnp.arange(GRID_W, dtype=F32), rows)
    half = rot_dim // 2
    inv_freq = ROPE_THETA ** (-jnp.arange(0, half, 2, dtype=F32) / half)
    ang_r = row[:, None] * inv_freq[None, :]
    ang_c = col[:, None] * inv_freq[None, :]
    ang = jnp.concatenate([ang_r, ang_r, ang_c, ang_c], axis=-1)
    return jnp.cos(ang), jnp.sin(ang)


def _stream_tables(cos, sin, lane0):
    T, R = cos.shape
    if lane0 == 0:
        cos_l, sin_l = jnp.tile(cos, (1, LANES // R)), jnp.tile(sin, (1, LANES // R))
    else:
        pad = ((0, 0), (lane0, LANES - lane0 - R))
        cos_l = jnp.pad(cos - 1.0, pad) + 1.0
        sin_l = jnp.pad(sin, pad)
    ctx = ((0, CTX_LEN), (0, 0))
    return jnp.pad(cos_l - 1.0, ctx) + 1.0, jnp.pad(sin_l, ctx)


def _cond_proj_kernel(c_ref, w_ref, b_ref, o_ref):
    c = c_ref[...]
    c_hi, c_lo = _split_bf16(_silu(c))
    w_hi, w_lo = _split_bf16(w_ref[...])
    o_ref[...] = _dot(c_hi, w_hi) + (_dot(c_hi, w_lo) + _dot(c_lo, w_hi)) + b_ref[...]


def _cond_proj(cond, w, b):
    R, D = cond.shape
    N = w.shape[1]
    return pl.pallas_call(
        _cond_proj_kernel,
        grid=(N // D,),
        in_specs=[pl.BlockSpec((R, D), lambda j: (0, 0)), pl.BlockSpec((D, D), lambda j: (0, j)),
                  pl.BlockSpec((1, D), lambda j: (0, j))],
        out_specs=pl.BlockSpec((R, D), lambda j: (0, j)),
        out_shape=jax.ShapeDtypeStruct((R, N), F32),
        compiler_params=_params("parallel"),
        name="cond_proj",
    )(cond, w, b)


def _mod_table(c, c_ctx, mod_w, mod_b):
    B = c.shape[0]
    cond = jnp.concatenate([c, c_ctx[None, :]], axis=0)
    cond = jnp.pad(cond, ((0, SUBLANES - (B + 1) % SUBLANES), (0, 0)))
    m = _cond_proj(cond, mod_w, mod_b[None, :])[:B + 1].reshape(B + 1, 1, 6, D_MODEL)
    return jnp.concatenate([jnp.broadcast_to(m[B:], (B, 1, 6, D_MODEL)), m[:B]], axis=1)


def _pad_cols(w, n):
    return jnp.pad(w, ((0, 0), (0, n - w.shape[1])))


def _even_layer(x, ctx, mod, norms, w_in, conv_w, a_log, dt_bias, gdn_norm, sink, w_out, ffn_gate, ffn_up,
                ffn_down, cos, sin, n_lat_tiles):
    B, T, D = x.shape
    n_gate = 2 * GDN_HEADS
    c_qkv = 3 * GDN_HD
    c_z = c_qkv + GDN_HD
    c_ga = c_z + n_gate
    c_gb = c_ga + n_gate
    c_sq = c_gb + SWA_Q_HEADS * SWA_HEAD_DIM
    c_sk = c_sq + SWA_KV_HEADS * SWA_HEAD_DIM
    grp = SWA_Q_HEADS // SWA_KV_HEADS
    wq = w_in[:, c_gb:c_sq].reshape(D, SWA_Q_HEADS, SWA_HEAD_DIM) * SWA_HEAD_DIM ** -0.5
    kv_of_head = (jnp.arange(SWA_Q_HEADS) // grp)[None, :, None]
    wq = jnp.concatenate([jnp.where(kv_of_head == 0, wq, 0.0), jnp.where(kv_of_head == 1, wq, 0.0)], axis=-1)
    w_all = jnp.concatenate([w_in[:, :c_z], _pad_cols(w_in[:, c_z:c_gb], LANES),
                             wq.reshape(D, SWA_Q_HEADS * LANES), w_in[:, c_sq:]], axis=1).astype(BF16)
    qkv, z, gate, q_pad, k, v = _even_proj(x, ctx, mod, norms[0:1], w_all, cos, sin, n_lat_tiles)

    head_ones = (jnp.arange(GDN_HD)[:, None] // GDN_DK == jnp.arange(GDN_HD)[None, :] // GDN_DK).astype(BF16)
    conv_w8 = jnp.pad(conv_w, ((0, SUBLANES - CONV_K), (0, 0)))
    gq, gk, gv = _gdn_prep(qkv, conv_w8, head_ones, n_lat_tiles)
    a_log_row = _pad_cols(a_log.reshape(1, n_gate), LANES)
    dt_bias_row = _pad_cols(dt_bias.reshape(1, n_gate), LANES)
    o_f, o_b = _gdn(gq, gk, gv, gate, a_log_row, dt_bias_row, T // GDN_CHUNK)

    sink_rows = jnp.repeat(sink, SWA_BLOCK)[:, None]
    swa = _swa(q_pad, k, v, sink_rows, T // SWA_BLOCK)

    w_s = w_out[GDN_HD:].reshape(SWA_Q_HEADS, SWA_HEAD_DIM, D)
    w_s = jnp.concatenate([jnp.where(kv_of_head.reshape(-1, 1, 1) == 0, w_s, 0.0),
                           jnp.where(kv_of_head.reshape(-1, 1, 1) == 1, w_s, 0.0)], axis=1)
    gdn_norm_row = jnp.tile(gdn_norm, GDN_HEADS)[None, :]
    return _even_tail(o_f, o_b, z, swa, x, ctx, mod, norms, gdn_norm_row, head_ones,
                      w_out[:GDN_HD].astype(BF16), w_s.reshape(SWA_Q_HEADS * LANES, D).astype(BF16),
                      ffn_gate.astype(BF16), ffn_up.astype(BF16), ffn_down.astype(BF16), n_lat_tiles)


def _odd_layer_last(xa, mod, norms, w_in, q_norm, kv_norm, w_uq, w_ukv, lam_p, lam_init, subln, w_out,
                    router, exp_gate, exp_up, exp_down, cos, sin, cos_r, sin_r, n_lat_tiles):
    B, S, D = xa.shape
    T = n_lat_tiles * ROW_TILE
    H = MLA_HEADS
    c0 = MLA_Q_RANK
    c1 = c0 + MLA_KV_RANK
    c2 = c1 + MLA_ROPE
    dw = DIFF_HEADS * 2 * DIFF_HEAD_DIM
    w_kr = jnp.pad(w_in[:, c1:c2], ((0, 0), (MLA_NOPE, LANES - MLA_NOPE - MLA_ROPE)))
    w_all = jnp.concatenate([w_in[:, :c1], w_kr, w_in[:, c2:c2 + dw] * DIFF_HEAD_DIM ** -0.5,
                             w_in[:, c2 + dw:]], axis=1).astype(BF16)
    qd = MLA_NOPE + MLA_ROPE
    wq = jnp.pad(w_uq.reshape(MLA_Q_RANK, H, qd) * qd ** -0.5, ((0, 0), (0, 0), (0, LANES - qd)))
    wkv = w_ukv.reshape(MLA_KV_RANK, H, MLA_NOPE + MLA_V)
    wk = jnp.pad(wkv[..., :MLA_NOPE], ((0, 0), (0, 0), (0, LANES - MLA_NOPE)))
    wv = jnp.pad(wkv[..., MLA_NOPE:], ((0, 0), (0, 0), (0, LANES - MLA_V)))
    q_cat, k_cat, v_ext, dq, dk, dv = _odd_proj(
        xa, mod, norms[0:1], w_all, cos, sin, cos_r, sin_r, q_norm[None, :], kv_norm[None, :],
        wq.reshape(MLA_Q_RANK, H * LANES).astype(BF16), wk.reshape(MLA_KV_RANK, H * LANES).astype(BF16),
        wv.reshape(MLA_KV_RANK, H * LANES).astype(BF16), n_lat_tiles)
    mla = _mla(q_cat, k_cat, v_ext, T)

    lam = (jnp.exp(jnp.sum(lam_p[0] * lam_p[1])) - jnp.exp(jnp.sum(lam_p[2] * lam_p[3])) + lam_init).reshape(1, 1)
    diff = _diff(dq, dk, dv, lam, subln[None, :], 1.0 - lam_init, T)

    w_mla = jnp.pad(w_out[:H * MLA_V].reshape(H, MLA_V, D), ((0, 0), (0, LANES - MLA_V), (0, 0)))
    x, f_in, logits = _mix_out1(mla, diff, xa, mod, norms[1:2], norms[2:3],
                                w_mla.reshape(H * LANES, D).astype(BF16), w_out[H * MLA_V:].astype(BF16),
                                _pad_cols(router, LANES), n_lat_tiles)

    gates, dest, row_token, tile_expert, tile_rows = _route(logits.reshape(B * T, LANES)[:, :N_EXPERTS])
    x_sorted = f_in.reshape(B * T, D)[row_token]
    y = _moe_experts(tile_expert, tile_rows, x_sorted, exp_gate, exp_up, exp_down)
    y0 = y[dest[0]].reshape(B, T, D)
    y1 = y[dest[1]].reshape(B, T, D)
    return _moe_combine(y0, y1, gates.reshape(B, T, TOP_K), x, mod, norms[3:4], n_lat_tiles)


def kernel(x, c, ctx, c_ctx, e_mod_w, e_mod_b, e_norms, e_w_in, e_conv_w, e_a_log, e_dt_bias, e_gdn_norm, e_sink, e_w_out, e_ffn_gate, e_ffn_up, e_ffn_down, o_mod_w, o_mod_b, o_norms, o_w_in, o_q_norm, o_kv_norm, o_w_uq, o_w_ukv, o_lambda, o_subln, o_w_out, o_router, o_exp_gate, o_exp_up, o_exp_down):
    B, T, D = x.shape
    n_lat_tiles = T // ROW_TILE
    rows = T // GRID_W
    cos64, sin64 = _stream_tables(*_rope_tables(rows, SWA_HEAD_DIM), 0)
    cos_r, sin_r = _stream_tables(*_rope_tables(rows, MLA_ROPE), MLA_NOPE)
    mod_e = _mod_table(c, c_ctx, e_mod_w[0], e_mod_b[0])
    xa = _even_layer(x, ctx, mod_e, e_norms[0], e_w_in[0], e_conv_w[0], e_a_log[0], e_dt_bias[0], e_gdn_norm[0],
                     e_sink[0], e_w_out[0], e_ffn_gate[0], e_ffn_up[0], e_ffn_down[0], cos64, sin64, n_lat_tiles)
    mod_o = _mod_table(c, c_ctx, o_mod_w[0], o_mod_b[0])
    lam_init = 0.8 - 0.6 * math.exp(-0.3 * 1)
    return _odd_layer_last(xa, mod_o, o_norms[0], o_w_in[0], o_q_norm[0], o_kv_norm[0], o_w_uq[0], o_w_ukv[0],
                           o_lambda[0], lam_init, o_subln[0], o_w_out[0], o_router[0], o_exp_gate[0],
                           o_exp_up[0], o_exp_down[0], cos64, sin64, cos_r, sin_r, n_lat_tiles)
```

```python
import functools
import math

import jax
import jax.numpy as jnp
from jax import lax
from jax.experimental import pallas as pl
from jax.experimental.pallas import tpu as pltpu

F32 = jnp.float32
BF16 = jnp.bfloat16

D_MODEL = 1024
CTX_LEN = 256
GRID_W = 64
NORM_EPS = 1e-6
ROPE_THETA = 10000.0

GDN_HEADS = 8
GDN_DK = 64
GDN_DV = 64
GDN_CHUNK = 64
CONV_K = 5
GDN_GROUP = 4
GDN_LANES = GDN_GROUP * GDN_DK
GDN_HD = GDN_HEADS * GDN_DK
GDN_NB = 4

SWA_Q_HEADS = 8
SWA_KV_HEADS = 2
SWA_HEAD_DIM = 64
SWA_WINDOW = 128
SWA_BLOCK = 128

MLA_HEADS = 8
MLA_Q_RANK = 384
MLA_KV_RANK = 256
MLA_NOPE = 64
MLA_ROPE = 32
MLA_V = 64

DIFF_HEADS = 4
DIFF_HEAD_DIM = 64

D_FF = 2816
N_EXPERTS = 8
TOP_K = 2
D_FF_EXPERT = 3584

LANES = 128
SUBLANES = 8
ROW_TILE = 256
LAT_TILE = 512
SWA_HEADS_PER_SUB = 1
ATTN_TQ = 2048
ATTN_SUB = 128
VMEM_LIMIT = 56 * 1024 * 1024
NEG_BIG = -1e30


def _dot(a, b):
    return jnp.dot(a, b, preferred_element_type=F32)


def _dot_nt(a, b):
    return lax.dot_general(a, b, (((1,), (1,)), ((), ())), preferred_element_type=F32)


def _dot_tn(a, b):
    return lax.dot_general(a, b, (((0,), (0,)), ((), ())), preferred_element_type=F32)


def _split_bf16(x):
    hi = x.astype(BF16)
    lo = (x - hi.astype(F32)).astype(BF16)
    return hi, lo


def _dot_split(x, w):
    hi, lo = _split_bf16(x)
    return _dot(hi, w) + _dot(lo, w)


def _dot_rsplit(w, x):
    hi, lo = _split_bf16(x)
    return _dot(w, hi) + _dot(w, lo)


def _rms(x, w):
    return x * lax.rsqrt(jnp.mean(x * x, axis=-1, keepdims=True) + NORM_EPS) * w


def _silu(x):
    return x * jax.nn.sigmoid(x)


def _rope_slab(x, cos, sin, quarter):
    lane = lax.broadcasted_iota(jnp.int32, x.shape, 1)
    fwd = pltpu.roll(x, quarter, 1)
    back = pltpu.roll(x, LANES - quarter, 1)
    rot = jnp.where(lane % (2 * quarter) < quarter, -back, fwd)
    return x * cos + rot * sin


def _params(*sem):
    return pltpu.CompilerParams(dimension_semantics=sem, vmem_limit_bytes=VMEM_LIMIT)


def _row_spec(width, tile=ROW_TILE):
    return pl.BlockSpec((1, tile, width), lambda b, t: (b, t, 0))


def _mod_spec(n_lat_tiles):
    return pl.BlockSpec((1, 1, 6, D_MODEL), lambda b, t: (b, jnp.where(t < n_lat_tiles, 1, 0), 0, 0))


def _const_spec(shape):
    return pl.BlockSpec(shape, lambda b, t: (0,) * len(shape))


def _table_spec():
    return pl.BlockSpec((ROW_TILE, LANES), lambda b, t: (t, 0))


E_QKV = (0, 3 * GDN_HD)
E_Z = (E_QKV[1], E_QKV[1] + GDN_HD)
E_GATE = (E_Z[1], E_Z[1] + LANES)
E_Q = (E_GATE[1], E_GATE[1] + SWA_Q_HEADS * LANES)
E_K = (E_Q[1], E_Q[1] + LANES)
E_V = (E_K[1], E_K[1] + LANES)


def _stream_in_specs(n_lat_tiles):
    assert CTX_LEN == ROW_TILE
    return [pl.BlockSpec((1, ROW_TILE, D_MODEL), lambda b, t: (b, jnp.minimum(t, n_lat_tiles - 1), 0)),
            pl.BlockSpec((1, ROW_TILE, D_MODEL), lambda b, t: (b, 0, 0))]


def _stream_tile(x_ref, c_ref, n_lat_tiles):
    return jnp.where(pl.program_id(1) < n_lat_tiles, x_ref[0], c_ref[0])


def _even_proj_kernel(x_ref, c_ref, xp_ref, xn_ref, mod_ref, nw_ref, w_ref, cos_ref, sin_ref, cw_ref, bd_ref,
                      gq_ref, gk_ref, gv_ref, z_ref, gate_ref, q_ref, k_ref, v_ref, *, n_lat_tiles):
    t = pl.program_id(1)
    has_prev = jnp.logical_and(t > 0, t < n_lat_tiles)
    has_next = t < n_lat_tiles - 1

    def act(x):
        return _rms(x, nw_ref[...]) * (1.0 + mod_ref[0, 0, 1:2, :]) + mod_ref[0, 0, 0:1, :]

    a32 = act(_stream_tile(x_ref, c_ref, n_lat_tiles))
    a = a32.astype(BF16)
    a_ext = jnp.concatenate([jnp.where(has_prev, act(xp_ref[0]), 0.0), a32,
                             jnp.where(has_next, act(xn_ref[0]), 0.0)], axis=0).astype(BF16)
    qkv = _dot(a_ext, w_ref[:, E_QKV[0]:E_QKV[1]])
    pad = CONV_K // 2
    y = jnp.zeros((ROW_TILE, E_QKV[1] - E_QKV[0]), F32)
    for kk in range(CONV_K):
        lo = SUBLANES + kk - pad
        y = y + cw_ref[kk:kk + 1, :] * qkv[lo:lo + ROW_TILE, :]
    u = _silu(y)

    def headnorm(v):
        return v * lax.rsqrt(_dot_split(v * v, bd_ref[...]) + NORM_EPS)

    gq_ref[0] = headnorm(u[:, :GDN_HD]) * GDN_DK ** -0.5
    gk_ref[0] = headnorm(u[:, GDN_HD:2 * GDN_HD])
    gv_ref[0] = u[:, 2 * GDN_HD:]

    def proj(cols):
        return _dot(a, w_ref[:, cols[0]:cols[1]])

    z_ref[0] = proj(E_Z).astype(BF16)
    gate_ref[0] = proj(E_GATE)
    cos = cos_ref[...]
    sin = sin_ref[...]
    quarter = SWA_HEAD_DIM // 4
    qp = proj(E_Q)
    for h in range(SWA_Q_HEADS):
        ls = slice(h * LANES, (h + 1) * LANES)
        q_ref[0, :, ls] = _rope_slab(qp[:, ls], cos, sin, quarter).astype(BF16)
    k_ref[0] = _rope_slab(proj(E_K), cos, sin, quarter).astype(BF16)
    v_ref[0] = proj(E_V).astype(BF16)


def _even_proj(x, ctx, mod, norm_w, w, cos, sin, conv_w, head_ones, n_lat_tiles):
    B, T, D = x.shape
    S = T + ctx.shape[1]
    per_tile = ROW_TILE // SUBLANES
    n8 = T // SUBLANES
    widths = (GDN_HD, GDN_HD, GDN_HD, GDN_HD, LANES, SWA_Q_HEADS * LANES, LANES, LANES)
    dtypes = (F32, F32, F32, BF16, F32, BF16, BF16, BF16)
    halo = [pl.BlockSpec((1, SUBLANES, D), lambda b, t: (b, jnp.clip(t * per_tile - 1, 0, n8 - 1), 0)),
            pl.BlockSpec((1, SUBLANES, D), lambda b, t: (b, jnp.clip((t + 1) * per_tile, 0, n8 - 1), 0))]
    return pl.pallas_call(
        functools.partial(_even_proj_kernel, n_lat_tiles=n_lat_tiles),
        grid=(B, S // ROW_TILE),
        in_specs=_stream_in_specs(n_lat_tiles) + halo
        + [_mod_spec(n_lat_tiles), _const_spec((1, D)), _const_spec(w.shape), _table_spec(), _table_spec(),
           _const_spec(conv_w.shape), _const_spec(head_ones.shape)],
        out_specs=[_row_spec(n) for n in widths],
        out_shape=[jax.ShapeDtypeStruct((B, S, n), dt) for n, dt in zip(widths, dtypes)],
        compiler_params=_params("parallel", "parallel"),
        name="even_proj",
    )(x, ctx, x, x, mod, norm_w, w, cos, sin, conv_w, head_ones)


def _block_diag(x, head_of_lane):
    zero = jnp.zeros_like(x)
    return jnp.concatenate([jnp.where(head_of_lane == h, x, zero) for h in range(GDN_GROUP)], axis=0)


def _expand_heads(x4, head_of_lane):
    c = x4.shape[0]
    out = jnp.broadcast_to(x4[:, GDN_GROUP - 1:GDN_GROUP], (c, GDN_LANES))
    for h in range(GDN_GROUP - 2, -1, -1):
        out = jnp.where(head_of_lane == h, jnp.broadcast_to(x4[:, h:h + 1], (c, GDN_LANES)), out)
    return out


def _gdn_groups(items):
    C = GDN_CHUNK
    row = lax.broadcasted_iota(jnp.int32, (C, GDN_LANES), 0)
    lane = lax.broadcasted_iota(jnp.int32, (C, GDN_LANES), 1)
    hol = lane // GDN_DK
    col = lane % GDN_DK
    eye = jnp.where(row == col, 1.0, 0.0)
    rr = lax.broadcasted_iota(jnp.int32, (2 * C, 4 * C), 0)
    cc = lax.broadcasted_iota(jnp.int32, (2 * C, 4 * C), 1)
    cum_lhs = {}
    for rev in (False, True):
        before = (rr - cc % C) * (-1 if rev else 1) >= 0
        cum_lhs[rev] = jnp.where(((rr < C) & (cc < 2 * C) & before) | ((rr >= C) & (cc >= 2 * C)),
                                 1.0, 0.0).astype(BF16)

    def bd(x):
        return _block_diag(x, hol)

    st = []
    for (q4, k4, v4, g4, b4, s4, reverse) in items:
        sgn = -1 if reverse else 1
        ahead = (row - col) * sgn
        incl = ahead >= 0
        strict = ahead > 0
        gE = _expand_heads(g4, hol)
        bE = _expand_heads(b4, hol)
        g_hi, g_lo = _split_bf16(gE)
        m_hi, m_lo = _split_bf16(jnp.where((col - row) * sgn >= 0, gE, 0.0))
        both = _dot(cum_lhs[reverse], jnp.concatenate([g_hi, g_lo, m_hi, m_lo], axis=0))
        gc = both[:C]
        gc_row = both[C:]
        decay = jnp.where(incl, jnp.exp(gc - gc_row), 0.0)
        g_last = gc[0:1, :] if reverse else gc[C - 1:C, :]
        eg = jnp.exp(gc)
        kb = k4 * bE
        st.append(dict(q=q4, k=k4, s=s4, incl=incl, strict=strict, decay=decay, g_last=g_last,
                       kb=kb, vb=v4 * bE, kbg=kb * eg, qh=q4 * eg, kt=k4 * jnp.exp(g_last - gc)))

    for d in st:
        gram = _dot_nt(jnp.concatenate([d["kb"], d["q"]], axis=0).astype(BF16), bd(d["k"].astype(BF16)))
        L = jnp.where(d["strict"], gram[:C] * d["decay"], 0.0)
        d["A"] = gram[C:] * d["decay"]
        d["T"] = eye - L
        d["P"] = L.astype(BF16)
    n_factors = int(math.log2(C)) - 1
    for d in st:
        d["P"] = _dot(d["P"], bd(d["P"])).astype(BF16)
    for it in range(n_factors):
        last = it == n_factors - 1
        for d in st:
            lhs = d["T"].astype(BF16) if last else jnp.concatenate([d["T"].astype(BF16), d["P"]], axis=0)
            prod = _dot(lhs, bd(d["P"]))
            d["T"] = d["T"] + prod[:C]
            if not last:
                d["P"] = prod[C:].astype(BF16)
    for d in st:
        Tb = d["T"].astype(BF16)
        d["u"] = _dot(Tb, bd(d["vb"].astype(BF16)))
        d["w"] = _dot(Tb, bd(d["kbg"].astype(BF16)))
    for d in st:
        d["ws_qs"] = _dot(jnp.concatenate([d["w"], d["qh"]], axis=0).astype(BF16), bd(d["s"].astype(BF16)))
    out = []
    for d in st:
        v_new = d["u"] - d["ws_qs"][:C]
        vb16 = v_new.astype(BF16)
        o = d["ws_qs"][C:] + _dot(d["A"].astype(BF16), bd(vb16))
        full = _dot_tn(d["kt"].astype(BF16), vb16)
        upd = jnp.zeros((GDN_DK, GDN_LANES), F32)
        for h in range(GDN_GROUP):
            upd = upd + jnp.where(hol == h, full[h * GDN_DK:(h + 1) * GDN_DK, :], 0.0)
        out.append((o, d["s"] * jnp.exp(d["g_last"]) + upd))
    return out


def _gdn_kernel(qf_ref, kf_ref, vf_ref, gf_ref, qb_ref, kb_ref, vb_ref, gb_ref, al_ref, dtb_ref,
                of_ref, ob_ref, s_ref):
    @pl.when(pl.program_id(1) == 0)
    def _():
        s_ref[...] = jnp.zeros_like(s_ref)

    n_grp = GDN_HEADS // GDN_GROUP
    n_gate = 2 * GDN_HEADS
    items = []
    slots = []
    for nb in range(GDN_NB):
        for d, (q_ref, k_ref, v_ref, g_ref, o_ref) in enumerate(((qf_ref, kf_ref, vf_ref, gf_ref, of_ref),
                                                                 (qb_ref, kb_ref, vb_ref, gb_ref, ob_ref))):
            raw = g_ref[nb]
            xs = raw + dtb_ref[...]
            softplus = jnp.maximum(xs, 0.0) + jnp.log(1.0 + jnp.exp(-jnp.abs(xs)))
            g = -jnp.exp(al_ref[...]) * softplus
            beta = jax.nn.sigmoid(raw)
            for grp in range(n_grp):
                ls = slice(grp * GDN_LANES, (grp + 1) * GDN_LANES)
                c0 = d * GDN_HEADS + grp * GDN_GROUP
                slot = (nb * 2 + d) * n_grp + grp
                items.append((q_ref[nb, :, ls], k_ref[nb, :, ls], v_ref[nb, :, ls],
                              g[:, c0:c0 + GDN_GROUP], beta[:, n_gate + c0:n_gate + c0 + GDN_GROUP],
                              s_ref[slot], d == 1))
                slots.append((o_ref, nb, ls, slot))
    for (o_ref, nb, ls, slot), (o, s_new) in zip(slots, _gdn_groups(items)):
        o_ref[nb, :, ls] = o
        s_ref[slot] = s_new


def _gdn(q, k, v, gate, a_log_row, dt_bias_row, n_lat_chunks):
    B, S, HD = q.shape
    n_chunks = S // GDN_CHUNK
    n_ctx = n_chunks - n_lat_chunks

    def fwd_c(s):
        return jnp.where(s < n_ctx, n_lat_chunks + s, s - n_ctx)

    def bwd_c(s):
        return n_chunks - 1 - s

    def spec(cmap, width):
        return pl.BlockSpec((GDN_NB, GDN_CHUNK, width), lambda b, s: (b, cmap(s), 0))

    return pl.pallas_call(
        _gdn_kernel,
        grid=(B // GDN_NB, n_chunks),
        in_specs=[spec(fwd_c, HD), spec(fwd_c, HD), spec(fwd_c, HD), spec(fwd_c, LANES),
                  spec(bwd_c, HD), spec(bwd_c, HD), spec(bwd_c, HD), spec(bwd_c, LANES),
                  _const_spec((1, LANES)), _const_spec((1, LANES))],
        out_specs=[spec(fwd_c, HD), spec(bwd_c, HD)],
        out_shape=[jax.ShapeDtypeStruct((B, S, HD), F32)] * 2,
        scratch_shapes=[pltpu.VMEM((GDN_NB * 2 * GDN_HEADS // GDN_GROUP, GDN_DK, GDN_LANES), F32)],
        compiler_params=_params("parallel", "arbitrary"),
        name="gdn_scan",
    )(q, k, v, gate, q, k, v, gate, a_log_row, dt_bias_row)


def _swa_kernel(q_ref, k_ref, v_ref, sink_ref, o_ref, *, n_lat_blocks):
    W = SWA_BLOCK
    H = SWA_Q_HEADS
    n = pl.program_id(1)
    is_lat = n < n_lat_blocks
    nl = jnp.minimum(n, n_lat_blocks - 1)
    prev = jnp.maximum(nl - 1, 0)
    nxt = jnp.minimum(nl + 1, n_lat_blocks - 1)
    ctx0 = n_lat_blocks * W

    def keys(ref):
        def blk(i):
            return ref[0, pl.ds(pl.multiple_of(i * W, W), W), :]
        return jnp.concatenate([ref[0, ctx0:ctx0 + CTX_LEN, :], blk(prev), blk(nl), blk(nxt)], axis=0)

    k_all = keys(k_ref)
    v_all = keys(v_ref)
    nk = CTX_LEN + 3 * W
    v_ext = jnp.concatenate([v_all, jnp.ones((nk, LANES), BF16)], axis=1)
    hp = SWA_HEADS_PER_SUB
    ii = lax.broadcasted_iota(jnp.int32, (hp * W, nk), 0) % W
    jj = lax.broadcasted_iota(jnp.int32, (hp * W, nk), 1) - CTX_LEN
    in_window = (((jj >= 0) & (jj < W) & (jj >= ii) & (nl > 0))
                 | ((jj >= W) & (jj < 2 * W))
                 | ((jj >= 2 * W) & (jj - 2 * W <= ii) & (nl < n_lat_blocks - 1)))
    valid = (jj < 0) | (in_window & is_lat)

    def scores(i):
        q = jnp.concatenate([q_ref[0, :, h * LANES:(h + 1) * LANES] for h in range(i * hp, (i + 1) * hp)], axis=0)
        return jnp.where(valid, _dot_nt(q, k_all), NEG_BIG)

    def finish(i, s):
        sk = sink_ref[i * hp * W:(i + 1) * hp * W, :]
        m = jnp.maximum(jnp.max(s, axis=-1, keepdims=True), sk)
        acc = _dot(jnp.exp(s - m).astype(BF16), v_ext)
        o = acc[:, :LANES] / (acc[:, LANES:LANES + 1] + jnp.exp(sk - m))
        for j in range(hp):
            h = i * hp + j
            o_ref[0, :, h * LANES:(h + 1) * LANES] = o[j * W:(j + 1) * W].astype(o_ref.dtype)

    _one_ahead(H // hp, scores, finish)


def _swa(q_pad, k, v, sink_rows, n_lat_blocks):
    B, S, _ = k.shape
    W = SWA_BLOCK
    QW = SWA_Q_HEADS * LANES
    return pl.pallas_call(
        functools.partial(_swa_kernel, n_lat_blocks=n_lat_blocks),
        grid=(B, S // W),
        in_specs=[pl.BlockSpec((1, W, QW), lambda b, n: (b, n, 0)),
                  pl.BlockSpec((1, S, LANES), lambda b, n: (b, 0, 0)),
                  pl.BlockSpec((1, S, LANES), lambda b, n: (b, 0, 0)),
                  _const_spec(sink_rows.shape)],
        out_specs=pl.BlockSpec((1, W, QW), lambda b, n: (b, n, 0)),
        out_shape=jax.ShapeDtypeStruct((B, S, QW), BF16),
        compiler_params=_params("parallel", "parallel"),
        name="swa",
    )(q_pad, k, v, sink_rows)


def _even_tail_kernel(of_ref, ob_ref, z_ref, swa_ref, x_ref, c_ref, mod_ref, norms_ref, gn_ref, bd_ref,
                      wo_g_ref, wo_s_ref, wg_ref, wu_ref, wd_ref, o_ref, *, n_lat_tiles):
    o = of_ref[0] + ob_ref[0]
    ms = _dot_split(o * o, bd_ref[...]) * (1.0 / GDN_DV)
    gdn = o * lax.rsqrt(ms + NORM_EPS) * gn_ref[...] * _silu(z_ref[0].astype(F32))
    y = _dot(gdn.astype(BF16), wo_g_ref[...]) + _dot(swa_ref[0], wo_s_ref[...])
    x1 = _stream_tile(x_ref, c_ref, n_lat_tiles) + mod_ref[0, 0, 2:3, :] * _rms(y, norms_ref[1:2, :])
    f = (_rms(x1, norms_ref[2:3, :]) * (1.0 + mod_ref[0, 0, 4:5, :]) + mod_ref[0, 0, 3:4, :]).astype(BF16)
    h = _silu(_dot(f, wg_ref[...])) * _dot(f, wu_ref[...])
    y2 = _dot(h.astype(BF16), wd_ref[...])
    o_ref[0] = x1 + mod_ref[0, 0, 5:6, :] * _rms(y2, norms_ref[3:4, :])


def _resident_spec(shape):
    return pl.BlockSpec(shape, lambda b, t: (0,) * len(shape), pipeline_mode=pl.Buffered(1))


def _even_tail(o_f, o_b, z, swa, x, ctx, mod, norms, gdn_norm_row, head_ones, wo_gdn, wo_swa, wg, wu, wd,
               n_lat_tiles):
    B, S, _ = o_f.shape
    D = x.shape[-1]
    consts = (norms, gdn_norm_row, head_ones, wo_gdn, wo_swa, wg, wu, wd)
    return pl.pallas_call(
        functools.partial(_even_tail_kernel, n_lat_tiles=n_lat_tiles),
        grid=(B, S // ROW_TILE),
        in_specs=[_row_spec(GDN_HD), _row_spec(GDN_HD), _row_spec(GDN_HD), _row_spec(swa.shape[-1])]
        + _stream_in_specs(n_lat_tiles) + [_mod_spec(n_lat_tiles)] + [_resident_spec(a.shape) for a in consts],
        out_specs=_row_spec(D),
        out_shape=jax.ShapeDtypeStruct((B, S, D), F32),
        compiler_params=_params("parallel", "parallel"),
        name="even_tail",
    )(o_f, o_b, z, swa, x, ctx, mod, *consts)


O_CQ = (0, MLA_Q_RANK)
O_CKV = (O_CQ[1], O_CQ[1] + MLA_KV_RANK)
O_KR = (O_CKV[1], O_CKV[1] + LANES)
O_DQ = (O_KR[1], O_KR[1] + DIFF_HEADS * LANES)
O_DK = (O_DQ[1], O_DQ[1] + DIFF_HEADS * LANES)
O_DV = (O_DK[1], O_DK[1] + DIFF_HEADS * LANES)


def _odd_proj_kernel(x_ref, mod_ref, nw_ref, w_ref, cos_ref, sin_ref, cosr_ref, sinr_ref,
                     qn_ref, kvn_ref, wq_ref, wk_ref, wv_ref,
                     q_ref, k_ref, v_ref, dq_ref, dk_ref, dv_ref):
    a = (_rms(x_ref[0], nw_ref[...]) * (1.0 + mod_ref[0, 0, 1:2, :]) + mod_ref[0, 0, 0:1, :]).astype(BF16)

    def proj(cols):
        return _dot(a, w_ref[:, cols[0]:cols[1]])

    cos_r = cosr_ref[...]
    sin_r = sinr_ref[...]
    rq = MLA_ROPE // 4
    cq = _rms(proj(O_CQ), qn_ref[...]).astype(BF16)
    ckv = _rms(proj(O_CKV), kvn_ref[...]).astype(BF16)
    kr = _rope_slab(proj(O_KR), cos_r, sin_r, rq)
    qp = _dot(cq, wq_ref[...])
    kp = _dot(ckv, wk_ref[...])
    vp = _dot(ckv, wv_ref[...])
    lane = lax.broadcasted_iota(jnp.int32, (ROW_TILE, LANES), 1)
    ones_hi = jnp.where(lane >= MLA_V, 1.0, 0.0)
    for h in range(MLA_HEADS):
        ls = slice(h * LANES, (h + 1) * LANES)
        q_ref[0, :, ls] = _rope_slab(qp[:, ls], cos_r, sin_r, rq).astype(BF16)
        k_ref[0, :, ls] = (kp[:, ls] + kr).astype(BF16)
        v_ref[0, :, ls] = (vp[:, ls] + ones_hi).astype(BF16)
    cos = cos_ref[...]
    sin = sin_ref[...]
    quarter = DIFF_HEAD_DIM // 4
    dq = proj(O_DQ)
    dk = proj(O_DK)
    dv = proj(O_DV)
    ones = jnp.ones((ROW_TILE, LANES), BF16)
    for h in range(DIFF_HEADS):
        ls = slice(h * LANES, (h + 1) * LANES)
        dq_ref[0, :, ls] = _rope_slab(dq[:, ls], cos, sin, quarter).astype(BF16)
        dk_ref[0, :, ls] = _rope_slab(dk[:, ls], cos, sin, quarter).astype(BF16)
        dv_ref[0, :, 2 * h * LANES:(2 * h + 1) * LANES] = dv[:, ls].astype(BF16)
        dv_ref[0, :, (2 * h + 1) * LANES:(2 * h + 2) * LANES] = ones


def _odd_proj(xa, mod, norm_w, w, cos, sin, cos_r, sin_r, q_norm, kv_norm, wq, wk, wv, n_lat_tiles):
    B, S, D = xa.shape
    HW = MLA_HEADS * LANES
    widths = (HW, HW, HW, DIFF_HEADS * LANES, DIFF_HEADS * LANES, 2 * DIFF_HEADS * LANES)
    return pl.pallas_call(
        _odd_proj_kernel,
        grid=(B, S // ROW_TILE),
        in_specs=[_row_spec(D), _mod_spec(n_lat_tiles), _const_spec((1, D)), _const_spec(w.shape),
                  _table_spec(), _table_spec(), _table_spec(), _table_spec(),
                  _const_spec(q_norm.shape), _const_spec(kv_norm.shape),
                  _const_spec(wq.shape), _const_spec(wk.shape), _const_spec(wv.shape)],
        out_specs=[_row_spec(n) for n in widths],
        out_shape=[jax.ShapeDtypeStruct((B, S, n), BF16) for n in widths],
        compiler_params=_params("parallel", "parallel"),
        name="odd_proj",
    )(xa, mod, norm_w, w, cos, sin, cos_r, sin_r, q_norm, kv_norm, wq, wk, wv)


def _one_ahead(n_sub, scores, finish):
    s_prev = scores(0)
    for i in range(1, n_sub):
        s_next = scores(i)
        finish(i - 1, s_prev)
        s_prev = s_next
    finish(n_sub - 1, s_prev)


def _mla_kernel(q_ref, k_ref, v_ref, o_ref):
    k = k_ref[0]
    v = v_ref[0]

    def scores(i):
        return _dot_nt(q_ref[0, i * ATTN_SUB:(i + 1) * ATTN_SUB, :], k)

    def finish(i, s):
        m = jnp.max(s, axis=-1, keepdims=True)
        acc = _dot(jnp.exp(s - m).astype(BF16), v)
        o_ref[0, i * ATTN_SUB:(i + 1) * ATTN_SUB, :] = (acc / acc[:, MLA_V:MLA_V + 1]).astype(o_ref.dtype)

    _one_ahead(q_ref.shape[1] // ATTN_SUB, scores, finish)


def _mla(q, k, v_ext, T):
    B, S, _ = k.shape
    return pl.pallas_call(
        _mla_kernel,
        grid=(B, MLA_HEADS, T // ATTN_TQ),
        in_specs=[pl.BlockSpec((1, ATTN_TQ, LANES), lambda b, h, t: (b, t, h)),
                  pl.BlockSpec((1, S, LANES), lambda b, h, t: (b, 0, h)),
                  pl.BlockSpec((1, S, LANES), lambda b, h, t: (b, 0, h))],
        out_specs=pl.BlockSpec((1, ATTN_TQ, LANES), lambda b, h, t: (b, t, h)),
        out_shape=jax.ShapeDtypeStruct((B, T, MLA_HEADS * LANES), BF16),
        compiler_params=_params("parallel", "parallel", "parallel"),
        name="mla_attn",
    )(q, k, v_ext)


def _diff_kernel(q_ref, k_ref, v_ref, lam_ref, sub_ref, o_ref, *, post_scale):
    k = k_ref[0]
    v = v_ref[0]
    sub = ATTN_SUB
    vw = 2 * DIFF_HEAD_DIM
    lo = lax.broadcasted_iota(jnp.int32, (sub, LANES), 1) < DIFF_HEAD_DIM
    zero = jnp.zeros((sub, LANES), BF16)

    def scores(i):
        q = q_ref[0, i * sub:(i + 1) * sub, :]
        return _dot_nt(jnp.concatenate([jnp.where(lo, q, zero), jnp.where(lo, zero, q)], axis=0), k)

    def finish(i, s):
        m = jnp.max(s, axis=-1, keepdims=True)
        acc = _dot(jnp.exp(s - m).astype(BF16), v)
        att = acc[:, :vw] / acc[:, vw:vw + 1]
        a = att[:sub] - lam_ref[...] * att[sub:]
        o_ref[0, i * sub:(i + 1) * sub, :] = (_rms(a, sub_ref[...]) * post_scale).astype(o_ref.dtype)

    _one_ahead(q_ref.shape[1] // sub, scores, finish)


def _diff(q, k, v_ext, lam, subln, post_scale, T):
    B, S, _ = k.shape
    tq = ATTN_TQ // 2
    return pl.pallas_call(
        functools.partial(_diff_kernel, post_scale=post_scale),
        grid=(B, DIFF_HEADS, T // tq),
        in_specs=[pl.BlockSpec((1, tq, LANES), lambda b, h, t: (b, t, h)),
                  pl.BlockSpec((1, S, LANES), lambda b, h, t: (b, 0, h)),
                  pl.BlockSpec((1, S, 2 * LANES), lambda b, h, t: (b, 0, h)),
                  pl.BlockSpec((1, 1), lambda b, h, t: (0, 0)),
                  pl.BlockSpec((1, LANES), lambda b, h, t: (0, 0))],
        out_specs=pl.BlockSpec((1, tq, LANES), lambda b, h, t: (b, t, h)),
        out_shape=jax.ShapeDtypeStruct((B, T, DIFF_HEADS * LANES), BF16),
        compiler_params=_params("parallel", "parallel", "parallel"),
        name="diff_attn",
    )(q, k, v_ext, lam, subln)


def _mix_out1_kernel(mla_ref, diff_ref, x_ref, mod_ref, n1_ref, n2_ref, w1_ref, w2_ref, wrh_ref, wrl_ref,
                     xo_ref, f_ref, lg_ref):
    y = _dot(mla_ref[0], w1_ref[...]) + _dot(diff_ref[0], w2_ref[...])
    x = x_ref[0] + mod_ref[0, 0, 2:3, :] * _rms(y, n1_ref[...])
    xo_ref[0] = x
    f = _rms(x, n2_ref[...]) * (1.0 + mod_ref[0, 0, 4:5, :]) + mod_ref[0, 0, 3:4, :]
    f_ref[0] = f
    f_hi, f_lo = _split_bf16(f)
    lg_ref[0] = _dot(f_hi, wrh_ref[...]) + (_dot(f_hi, wrl_ref[...]) + _dot(f_lo, wrh_ref[...]))


def _mix_out1(mla, diff, xa, mod, n1, n2, w1, w2, wr, n_lat_tiles):
    B, T, _ = mla.shape
    D = xa.shape[-1]
    wr_hi, wr_lo = _split_bf16(wr)
    return pl.pallas_call(
        _mix_out1_kernel,
        grid=(B, T // ROW_TILE),
        in_specs=[_row_spec(mla.shape[-1]), _row_spec(diff.shape[-1]), _row_spec(D), _mod_spec(n_lat_tiles),
                  _const_spec((1, D)), _const_spec((1, D)), _const_spec(w1.shape), _const_spec(w2.shape),
                  _const_spec(wr.shape), _const_spec(wr.shape)],
        out_specs=[_row_spec(D), _row_spec(D), _row_spec(LANES)],
        out_shape=[jax.ShapeDtypeStruct((B, T, D), F32), jax.ShapeDtypeStruct((B, T, D), F32),
                   jax.ShapeDtypeStruct((B, T, LANES), F32)],
        compiler_params=_params("parallel", "parallel"),
        name="mix_out1",
    )(mla, diff, xa, mod, n1, n2, w1, w2, wr_hi, wr_lo)


MOE_TM = 1024
MOE_TF = 512
MOE_SUB = 256


def _moe_kernel(te_ref, tr_ref, x_ref, wg_ref, wu_ref, wd_ref, o_ref, acc_ref):
    i = pl.program_id(0)
    f = pl.program_id(1)
    n_rows = tr_ref[i]
    n_sub = MOE_TM // MOE_SUB

    @pl.when(f == 0)
    def _():
        acc_ref[...] = jnp.zeros_like(acc_ref)

    def sub_tile_fns():
        wg = wg_ref[0].astype(BF16)
        wu = wu_ref[0].astype(BF16)
        wd = wd_ref[0].astype(BF16)

        def gate_up(j):
            x = x_ref[j * MOE_SUB:(j + 1) * MOE_SUB, :].astype(BF16)
            return _dot(x, wg), _dot(x, wu)

        def down(j, gu):
            h = (_silu(gu[0]) * gu[1]).astype(BF16)
            acc_ref[j * MOE_SUB:(j + 1) * MOE_SUB, :] += _dot(h, wd)

        return gate_up, down

    all_subs = n_rows > (n_sub - 1) * MOE_SUB

    @pl.when(all_subs)
    def _():
        _one_ahead(n_sub, *sub_tile_fns())

    for j in range(n_sub - 1):
        @pl.when(jnp.logical_and(jnp.logical_not(all_subs), n_rows > j * MOE_SUB))
        def _(j=j):
            gate_up, down = sub_tile_fns()
            down(j, gate_up(j))

    @pl.when(f == pl.num_programs(1) - 1)
    def _():
        o_ref[...] = acc_ref[...]


def _moe_experts(tile_expert, tile_rows, x_sorted, wg, wu, wd):
    R, D = x_sorted.shape
    E, _, F = wg.shape
    nf = F // MOE_TF

    def f_eff(i, f, ta):
        return jnp.where(ta[i] > 0, f, nf - 1)

    grid_spec = pltpu.PrefetchScalarGridSpec(
        num_scalar_prefetch=2,
        grid=(R // MOE_TM, nf),
        in_specs=[pl.BlockSpec((MOE_TM, D), lambda i, f, te, ta: (i, 0)),
                  pl.BlockSpec((1, D, MOE_TF), lambda i, f, te, ta: (te[i], 0, f_eff(i, f, ta))),
                  pl.BlockSpec((1, D, MOE_TF), lambda i, f, te, ta: (te[i], 0, f_eff(i, f, ta))),
                  pl.BlockSpec((1, MOE_TF, D), lambda i, f, te, ta: (te[i], f_eff(i, f, ta), 0))],
        out_specs=pl.BlockSpec((MOE_TM, D), lambda i, f, te, ta: (i, 0)),
        scratch_shapes=[pltpu.VMEM((MOE_TM, D), F32)],
    )
    return pl.pallas_call(
        _moe_kernel,
        grid_spec=grid_spec,
        out_shape=jax.ShapeDtypeStruct((R, D), F32),
        compiler_params=_params("arbitrary", "arbitrary"),
        name="moe_experts",
    )(tile_expert, tile_rows, x_sorted, wg, wu, wd)


def _moe_combine_kernel(a_ref, b_ref, g_ref, x_ref, mod_ref, nw_ref, o_ref):
    g = g_ref[0]
    y = g[:, 0:1] * a_ref[0] + g[:, 1:2] * b_ref[0]
    o_ref[0] = x_ref[0] + mod_ref[0, 0, 5:6, :] * _rms(y, nw_ref[...])


def _moe_combine(a, b, gates, x, mod, norm_w, n_lat_tiles):
    B, T, D = x.shape
    return pl.pallas_call(
        _moe_combine_kernel,
        grid=(B, T // LAT_TILE),
        in_specs=[_row_spec(D, LAT_TILE), _row_spec(D, LAT_TILE), _row_spec(TOP_K, LAT_TILE), _row_spec(D, LAT_TILE),
                  _mod_spec(n_lat_tiles), _const_spec((1, D))],
        out_specs=_row_spec(D, LAT_TILE),
        out_shape=jax.ShapeDtypeStruct((B, T, D), F32),
        compiler_params=_params("parallel", "parallel"),
        name="moe_combine",
    )(a, b, gates, x, mod, norm_w)


def _route(logits):
    N = logits.shape[0]
    M = N * TOP_K
    R = M + N_EXPERTS * MOE_TM
    top_val, top_idx = lax.top_k(logits, TOP_K)
    gates = jax.nn.softmax(top_val, axis=-1)
    e_flat = top_idx.T.reshape(-1)
    onehot = (e_flat[:, None] == jnp.arange(N_EXPERTS)[None, :]).astype(jnp.int32)
    counts = jnp.sum(onehot, axis=0)
    padded = ((counts + MOE_TM - 1) // MOE_TM) * MOE_TM
    ends = jnp.cumsum(padded)
    starts = ends - padded
    ustarts = jnp.cumsum(counts) - counts
    order = jnp.argsort(e_flat, stable=True).astype(jnp.int32)
    rank = jnp.argsort(order).astype(jnp.int32)
    dest = rank + jnp.sum(onehot * (starts - ustarts)[None, :], axis=-1)
    n_tiles = R // MOE_TM
    tile_start = jnp.arange(n_tiles, dtype=jnp.int32) * MOE_TM
    tile_active = (tile_start < ends[-1]).astype(jnp.int32)
    te = jnp.minimum(jnp.searchsorted(ends, tile_start, side="right"), N_EXPERTS - 1).astype(jnp.int32)
    last_e = te[jnp.maximum(ends[-1] // MOE_TM - 1, 0)]
    tile_expert = jnp.where(tile_active > 0, te, last_e)
    in_tile = jnp.arange(MOE_TM, dtype=jnp.int32)[None, :]
    within = (tile_start - starts[tile_expert])[:, None] + in_tile
    valid = (within < counts[tile_expert][:, None]) & (tile_active[:, None] > 0)
    src = order[jnp.clip(ustarts[tile_expert][:, None] + within, 0, M - 1).reshape(R)]
    valid = valid.reshape(R)
    tile_rows = jnp.clip(counts[tile_expert] - (tile_start - starts[tile_expert]), 0, MOE_TM) * tile_active
    row_token = jnp.where(valid, src, jnp.arange(R, dtype=jnp.int32)) % N
    return gates, dest.reshape(TOP_K, N), row_token, tile_expert, tile_rows.astype(jnp.int32)


def _rope_tables(rows, rot_dim):
    row = jnp.repeat(jnp.arange(rows, dtype=F32), GRID_W)
    col = jnp.tile(jnp.arange(GRID_W, dtype=F32), rows)
    half = rot_dim // 2
    inv_freq = ROPE_THETA ** (-jnp.arange(0, half, 2, dtype=F32) / half)
    ang_r = row[:, None] * inv_freq[None, :]
    ang_c = col[:, None] * inv_freq[None, :]
    ang = jnp.concatenate([ang_r, ang_r, ang_c, ang_c], axis=-1)
    return jnp.cos(ang), jnp.sin(ang)


def _stream_tables(cos, sin, lane0):
    T, R = cos.shape
    if lane0 == 0:
        cos_l, sin_l = jnp.tile(cos, (1, LANES // R)), jnp.tile(sin, (1, LANES // R))
    else:
        pad = ((0, 0), (lane0, LANES - lane0 - R))
        cos_l = jnp.pad(cos - 1.0, pad) + 1.0
        sin_l = jnp.pad(sin, pad)
    ctx = ((0, CTX_LEN), (0, 0))
    return jnp.pad(cos_l - 1.0, ctx) + 1.0, jnp.pad(sin_l, ctx)


def _cond_proj_kernel(c_ref, w_ref, b_ref, o_ref):
    c = c_ref[...]
    c_hi, c_lo = _split_bf16(_silu(c))
    w_hi, w_lo = _split_bf16(w_ref[...])
    o_ref[...] = _dot(c_hi, w_hi) + (_dot(c_hi, w_lo) + _dot(c_lo, w_hi)) + b_ref[...]


def _cond_proj(cond, w, b):
    R, D = cond.shape
    N = w.shape[1]
    return pl.pallas_call(
        _cond_proj_kernel,
        grid=(N // D,),
        in_specs=[pl.BlockSpec((R, D), lambda j: (0, 0)), pl.BlockSpec((D, D), lambda j: (0, j)),
                  pl.BlockSpec((1, D), lambda j: (0, j))],
        out_specs=pl.BlockSpec((R, D), lambda j: (0, j)),
        out_shape=jax.ShapeDtypeStruct((R, N), F32),
        compiler_params=_params("parallel"),
        name="cond_proj",
    )(cond, w, b)


def _mod_table(c, c_ctx, mod_w, mod_b):
    B = c.shape[0]
    cond = jnp.concatenate([c, c_ctx[None, :]], axis=0)
    cond = jnp.pad(cond, ((0, SUBLANES - (B + 1) % SUBLANES), (0, 0)))
    m = _cond_proj(cond, mod_w, mod_b[None, :])[:B + 1].reshape(B + 1, 1, 6, D_MODEL)
    return jnp.concatenate([jnp.broadcast_to(m[B:], (B, 1, 6, D_MODEL)), m[:B]], axis=1)


def _pad_cols(w, n):
    return jnp.pad(w, ((0, 0), (0, n - w.shape[1])))


def _even_layer(x, ctx, mod, norms, w_in, conv_w, a_log, dt_bias, gdn_norm, sink, w_out, ffn_gate, ffn_up,
                ffn_down, cos, sin, n_lat_tiles):
    B, T, D = x.shape
    n_gate = 2 * GDN_HEADS
    c_qkv = 3 * GDN_HD
    c_z = c_qkv + GDN_HD
    c_ga = c_z + n_gate
    c_gb = c_ga + n_gate
    c_sq = c_gb + SWA_Q_HEADS * SWA_HEAD_DIM
    c_sk = c_sq + SWA_KV_HEADS * SWA_HEAD_DIM
    grp = SWA_Q_HEADS // SWA_KV_HEADS
    wq = w_in[:, c_gb:c_sq].reshape(D, SWA_Q_HEADS, SWA_HEAD_DIM) * SWA_HEAD_DIM ** -0.5
    kv_of_head = (jnp.arange(SWA_Q_HEADS) // grp)[None, :, None]
    wq = jnp.concatenate([jnp.where(kv_of_head == 0, wq, 0.0), jnp.where(kv_of_head == 1, wq, 0.0)], axis=-1)
    w_all = jnp.concatenate([w_in[:, :c_z], _pad_cols(w_in[:, c_z:c_gb], LANES),
                             wq.reshape(D, SWA_Q_HEADS * LANES), w_in[:, c_sq:]], axis=1).astype(BF16)
    head_ones = (jnp.arange(GDN_HD)[:, None] // GDN_DK == jnp.arange(GDN_HD)[None, :] // GDN_DK).astype(BF16)
    conv_w8 = jnp.pad(conv_w, ((0, SUBLANES - CONV_K), (0, 0)))
    gq, gk, gv, z, gate, q_pad, k, v = _even_proj(x, ctx, mod, norms[0:1], w_all, cos, sin, conv_w8, head_ones,
                                                  n_lat_tiles)
    a_log_row = _pad_cols(a_log.reshape(1, n_gate), LANES)
    dt_bias_row = _pad_cols(dt_bias.reshape(1, n_gate), LANES)
    o_f, o_b = _gdn(gq, gk, gv, gate, a_log_row, dt_bias_row, T // GDN_CHUNK)

    sink_rows = jnp.repeat(sink, SWA_BLOCK)[:, None]
    swa = _swa(q_pad, k, v, sink_rows, T // SWA_BLOCK)

    w_s = w_out[GDN_HD:].reshape(SWA_Q_HEADS, SWA_HEAD_DIM, D)
    w_s = jnp.concatenate([jnp.where(kv_of_head.reshape(-1, 1, 1) == 0, w_s, 0.0),
                           jnp.where(kv_of_head.reshape(-1, 1, 1) == 1, w_s, 0.0)], axis=1)
    gdn_norm_row = jnp.tile(gdn_norm, GDN_HEADS)[None, :]
    return _even_tail(o_f, o_b, z, swa, x, ctx, mod, norms, gdn_norm_row, head_ones,
                      w_out[:GDN_HD].astype(BF16), w_s.reshape(SWA_Q_HEADS * LANES, D).astype(BF16),
                      ffn_gate.astype(BF16), ffn_up.astype(BF16), ffn_down.astype(BF16), n_lat_tiles)


def _odd_layer_last(xa, mod, norms, w_in, q_norm, kv_norm, w_uq, w_ukv, lam_p, lam_init, subln, w_out,
                    router, exp_gate, exp_up, exp_down, cos, sin, cos_r, sin_r, n_lat_tiles):
    B, S, D = xa.shape
    T = n_lat_tiles * ROW_TILE
    H = MLA_HEADS
    c0 = MLA_Q_RANK
    c1 = c0 + MLA_KV_RANK
    c2 = c1 + MLA_ROPE
    dw = DIFF_HEADS * 2 * DIFF_HEAD_DIM
    w_kr = jnp.pad(w_in[:, c1:c2], ((0, 0), (MLA_NOPE, LANES - MLA_NOPE - MLA_ROPE)))
    w_all = jnp.concatenate([w_in[:, :c1], w_kr, w_in[:, c2:c2 + dw] * DIFF_HEAD_DIM ** -0.5,
                             w_in[:, c2 + dw:]], axis=1).astype(BF16)
    qd = MLA_NOPE + MLA_ROPE
    wq = jnp.pad(w_uq.reshape(MLA_Q_RANK, H, qd) * qd ** -0.5, ((0, 0), (0, 0), (0, LANES - qd)))
    wkv = w_ukv.reshape(MLA_KV_RANK, H, MLA_NOPE + MLA_V)
    wk = jnp.pad(wkv[..., :MLA_NOPE], ((0, 0), (0, 0), (0, LANES - MLA_NOPE)))
    wv = jnp.pad(wkv[..., MLA_NOPE:], ((0, 0), (0, 0), (0, LANES - MLA_V)))
    q_cat, k_cat, v_ext, dq, dk, dv = _odd_proj(
        xa, mod, norms[0:1], w_all, cos, sin, cos_r, sin_r, q_norm[None, :], kv_norm[None, :],
        wq.reshape(MLA_Q_RANK, H * LANES).astype(BF16), wk.reshape(MLA_KV_RANK, H * LANES).astype(BF16),
        wv.reshape(MLA_KV_RANK, H * LANES).astype(BF16), n_lat_tiles)
    mla = _mla(q_cat, k_cat, v_ext, T)

    lam = (jnp.exp(jnp.sum(lam_p[0] * lam_p[1])) - jnp.exp(jnp.sum(lam_p[2] * lam_p[3])) + lam_init).reshape(1, 1)
    diff = _diff(dq, dk, dv, lam, subln[None, :], 1.0 - lam_init, T)

    w_mla = jnp.pad(w_out[:H * MLA_V].reshape(H, MLA_V, D), ((0, 0), (0, LANES - MLA_V), (0, 0)))
    x, f_in, logits = _mix_out1(mla, diff, xa, mod, norms[1:2], norms[2:3],
                                w_mla.reshape(H * LANES, D).astype(BF16), w_out[H * MLA_V:].astype(BF16),
                                _pad_cols(router, LANES), n_lat_tiles)

    gates, dest, row_token, tile_expert, tile_rows = _route(logits.reshape(B * T, LANES)[:, :N_EXPERTS])
    x_sorted = f_in.reshape(B * T, D)[row_token]
    y = _moe_experts(tile_expert, tile_rows, x_sorted, exp_gate, exp_up, exp_down)
    y0 = y[dest[0]].reshape(B, T, D)
    y1 = y[dest[1]].reshape(B, T, D)
    return _moe_combine(y0, y1, gates.reshape(B, T, TOP_K), x, mod, norms[3:4], n_lat_tiles)


def kernel(x, c, ctx, c_ctx, e_mod_w, e_mod_b, e_norms, e_w_in, e_conv_w, e_a_log, e_dt_bias, e_gdn_norm, e_sink, e_w_out, e_ffn_gate, e_ffn_up, e_ffn_down, o_mod_w, o_mod_b, o_norms, o_w_in, o_q_norm, o_kv_norm, o_w_uq, o_w_ukv, o_lambda, o_subln, o_w_out, o_router, o_exp_gate, o_exp_up, o_exp_down):
    B, T, D = x.shape
    n_lat_tiles = T // ROW_TILE
    rows = T // GRID_W
    cos64, sin64 = _stream_tables(*_rope_tables(rows, SWA_HEAD_DIM), 0)
    cos_r, sin_r = _stream_tables(*_rope_tables(rows, MLA_ROPE), MLA_NOPE)
    mod_e = _mod_table(c, c_ctx, e_mod_w[0], e_mod_b[0])
    xa = _even_layer(x, ctx, mod_e, e_norms[0], e_w_in[0], e_conv_w[0], e_a_log[0], e_dt_bias[0], e_gdn_norm[0],
                     e_sink[0], e_w_out[0], e_ffn_gate[0], e_ffn_up[0], e_ffn_down[0], cos64, sin64, n_lat_tiles)
    mod_o = _mod_table(c, c_ctx, o_mod_w[0], o_mod_b[0])
    lam_init = 0.8 - 0.6 * math.exp(-0.3 * 1)
    return _odd_layer_last(xa, mod_o, o_norms[0], o_w_in[0], o_q_norm[0], o_kv_norm[0], o_w_uq[0], o_w_ukv[0],
                           o_lambda[0], lam_init, o_subln[0], o_w_out[0], o_router[0], o_exp_gate[0],
                           o_exp_up[0], o_exp_down[0], cos64, sin64, cos_r, sin_r, n_lat_tiles)
```

```python
import functools
import math

import jax
import jax.numpy as jnp
from jax import lax
from jax.experimental import pallas as pl
from jax.experimental.pallas import tpu as pltpu

F32 = jnp.float32
BF16 = jnp.bfloat16

D_MODEL = 1024
CTX_LEN = 256
GRID_W = 64
NORM_EPS = 1e-6
ROPE_THETA = 10000.0

GDN_HEADS = 8
GDN_DK = 64
GDN_DV = 64
GDN_CHUNK = 64
CONV_K = 5
GDN_GROUP = 4
GDN_LANES = GDN_GROUP * GDN_DK
GDN_HD = GDN_HEADS * GDN_DK
GDN_NB = 4

SWA_Q_HEADS = 8
SWA_KV_HEADS = 2
SWA_HEAD_DIM = 64
SWA_WINDOW = 128
SWA_BLOCK = 128

MLA_HEADS = 8
MLA_Q_RANK = 384
MLA_KV_RANK = 256
MLA_NOPE = 64
MLA_ROPE = 32
MLA_V = 64

DIFF_HEADS = 4
DIFF_HEAD_DIM = 64

D_FF = 2816
N_EXPERTS = 8
TOP_K = 2
D_FF_EXPERT = 3584

LANES = 128
SUBLANES = 8
ROW_TILE = 256
LAT_TILE = 512
SWA_HEADS_PER_SUB = 1
ATTN_TQ = 2048
ATTN_SUB = 256
VMEM_LIMIT = 56 * 1024 * 1024
NEG_BIG = -1e30


def _dot(a, b):
    return jnp.dot(a, b, preferred_element_type=F32)


def _dot_nt(a, b):
    return lax.dot_general(a, b, (((1,), (1,)), ((), ())), preferred_element_type=F32)


def _dot_tn(a, b):
    return lax.dot_general(a, b, (((0,), (0,)), ((), ())), preferred_element_type=F32)


def _split_bf16(x):
    hi = x.astype(BF16)
    lo = (x - hi.astype(F32)).astype(BF16)
    return hi, lo


def _dot_split(x, w):
    hi, lo = _split_bf16(x)
    return _dot(hi, w) + _dot(lo, w)


def _rms(x, w):
    return x * lax.rsqrt(jnp.mean(x * x, axis=-1, keepdims=True) + NORM_EPS) * w


def _silu(x):
    return x * jax.nn.sigmoid(x)


def _rope_slab(x, cos, sin, quarter):
    lane = lax.broadcasted_iota(jnp.int32, x.shape, 1)
    fwd = pltpu.roll(x, quarter, 1)
    back = pltpu.roll(x, LANES - quarter, 1)
    rot = jnp.where(lane % (2 * quarter) < quarter, -back, fwd)
    return x * cos + rot * sin


def _params(*sem):
    return pltpu.CompilerParams(dimension_semantics=sem, vmem_limit_bytes=VMEM_LIMIT)


def _row_spec(width, tile=ROW_TILE):
    return pl.BlockSpec((1, tile, width), lambda b, t: (b, t, 0))


def _mod_spec(n_lat_tiles):
    return pl.BlockSpec((1, 1, 6, D_MODEL), lambda b, t: (b, jnp.where(t < n_lat_tiles, 1, 0), 0, 0))


def _const_spec(shape):
    return pl.BlockSpec(shape, lambda b, t: (0,) * len(shape))


def _table_spec():
    return pl.BlockSpec((ROW_TILE, LANES), lambda b, t: (t, 0))


E_QKV = (0, 3 * GDN_HD)
E_Z = (E_QKV[1], E_QKV[1] + GDN_HD)
E_GATE = (E_Z[1], E_Z[1] + LANES)
E_Q = (E_GATE[1], E_GATE[1] + SWA_Q_HEADS * LANES)
E_K = (E_Q[1], E_Q[1] + LANES)
E_V = (E_K[1], E_K[1] + LANES)


def _stream_in_specs(n_lat_tiles):
    assert CTX_LEN == ROW_TILE
    return [pl.BlockSpec((1, ROW_TILE, D_MODEL), lambda b, t: (b, jnp.minimum(t, n_lat_tiles - 1), 0)),
            pl.BlockSpec((1, ROW_TILE, D_MODEL), lambda b, t: (b, 0, 0))]


def _stream_tile(x_ref, c_ref, n_lat_tiles):
    return jnp.where(pl.program_id(1) < n_lat_tiles, x_ref[0], c_ref[0])


def _even_proj_kernel(x_ref, c_ref, xp_ref, xn_ref, mod_ref, nw_ref, w_ref, cos_ref, sin_ref, cw_ref, bd_ref,
                      gq_ref, gk_ref, gv_ref, z_ref, gate_ref, q_ref, k_ref, v_ref, *, n_lat_tiles):
    t = pl.program_id(1)
    has_prev = jnp.logical_and(t > 0, t < n_lat_tiles)
    has_next = t < n_lat_tiles - 1

    def act(x):
        return _rms(x, nw_ref[...]) * (1.0 + mod_ref[0, 0, 1:2, :]) + mod_ref[0, 0, 0:1, :]

    a32 = act(_stream_tile(x_ref, c_ref, n_lat_tiles))
    a = a32.astype(BF16)
    a_ext = jnp.concatenate([jnp.where(has_prev, act(xp_ref[0]), 0.0), a32,
                             jnp.where(has_next, act(xn_ref[0]), 0.0)], axis=0).astype(BF16)
    qkv = _dot(a_ext, w_ref[:, E_QKV[0]:E_QKV[1]])
    pad = CONV_K // 2
    y = jnp.zeros((ROW_TILE, E_QKV[1] - E_QKV[0]), F32)
    for kk in range(CONV_K):
        lo = SUBLANES + kk - pad
        y = y + cw_ref[kk:kk + 1, :] * qkv[lo:lo + ROW_TILE, :]
    u = _silu(y)

    def headnorm(v):
        return v * lax.rsqrt(_dot_split(v * v, bd_ref[...]) + NORM_EPS)

    gq_ref[0] = headnorm(u[:, :GDN_HD]) * GDN_DK ** -0.5
    gk_ref[0] = headnorm(u[:, GDN_HD:2 * GDN_HD])
    gv_ref[0] = u[:, 2 * GDN_HD:]

    def proj(cols):
        return _dot(a, w_ref[:, cols[0]:cols[1]])

    z_ref[0] = proj(E_Z).astype(BF16)
    gate_ref[0] = proj(E_GATE)
    cos = cos_ref[...]
    sin = sin_ref[...]
    quarter = SWA_HEAD_DIM // 4
    qp = proj(E_Q)
    for h in range(SWA_Q_HEADS):
        ls = slice(h * LANES, (h + 1) * LANES)
        q_ref[0, :, ls] = _rope_slab(qp[:, ls], cos, sin, quarter).astype(BF16)
    k_ref[0] = _rope_slab(proj(E_K), cos, sin, quarter).astype(BF16)
    v_ref[0] = proj(E_V).astype(BF16)


def _even_proj(x, ctx, mod, norm_w, w, cos, sin, conv_w, head_ones, n_lat_tiles):
    B, T, D = x.shape
    S = T + ctx.shape[1]
    per_tile = ROW_TILE // SUBLANES
    n8 = T // SUBLANES
    widths = (GDN_HD, GDN_HD, GDN_HD, GDN_HD, LANES, SWA_Q_HEADS * LANES, LANES, LANES)
    dtypes = (F32, F32, F32, BF16, F32, BF16, BF16, BF16)
    halo = [pl.BlockSpec((1, SUBLANES, D), lambda b, t: (b, jnp.clip(t * per_tile - 1, 0, n8 - 1), 0)),
            pl.BlockSpec((1, SUBLANES, D), lambda b, t: (b, jnp.clip((t + 1) * per_tile, 0, n8 - 1), 0))]
    return pl.pallas_call(
        functools.partial(_even_proj_kernel, n_lat_tiles=n_lat_tiles),
        grid=(B, S // ROW_TILE),
        in_specs=_stream_in_specs(n_lat_tiles) + halo
        + [_mod_spec(n_lat_tiles), _const_spec((1, D)), _const_spec(w.shape), _table_spec(), _table_spec(),
           _const_spec(conv_w.shape), _const_spec(head_ones.shape)],
        out_specs=[_row_spec(n) for n in widths],
        out_shape=[jax.ShapeDtypeStruct((B, S, n), dt) for n, dt in zip(widths, dtypes)],
        compiler_params=_params("parallel", "parallel"),
        name="even_proj",
    )(x, ctx, x, x, mod, norm_w, w, cos, sin, conv_w, head_ones)


def _block_diag(x, head_of_lane):
    zero = jnp.zeros_like(x)
    return jnp.concatenate([jnp.where(head_of_lane == h, x, zero) for h in range(GDN_GROUP)], axis=0)


def _expand_heads(x4, head_of_lane):
    c = x4.shape[0]
    out = jnp.broadcast_to(x4[:, GDN_GROUP - 1:GDN_GROUP], (c, GDN_LANES))
    for h in range(GDN_GROUP - 2, -1, -1):
        out = jnp.where(head_of_lane == h, jnp.broadcast_to(x4[:, h:h + 1], (c, GDN_LANES)), out)
    return out


def _gdn_groups(items):
    C = GDN_CHUNK
    row = lax.broadcasted_iota(jnp.int32, (C, GDN_LANES), 0)
    lane = lax.broadcasted_iota(jnp.int32, (C, GDN_LANES), 1)
    hol = lane // GDN_DK
    col = lane % GDN_DK
    eye = jnp.where(row == col, 1.0, 0.0)
    rr = lax.broadcasted_iota(jnp.int32, (2 * C, 4 * C), 0)
    cc = lax.broadcasted_iota(jnp.int32, (2 * C, 4 * C), 1)
    cum_lhs = {}
    for rev in (False, True):
        before = (rr - cc % C) * (-1 if rev else 1) >= 0
        cum_lhs[rev] = jnp.where(((rr < C) & (cc < 2 * C) & before) | ((rr >= C) & (cc >= 2 * C)),
                                 1.0, 0.0).astype(BF16)

    def bd(x):
        return _block_diag(x, hol)

    st = []
    for (q4, k4, v4, g4, b4, s4, reverse) in items:
        sgn = -1 if reverse else 1
        ahead = (row - col) * sgn
        incl = ahead >= 0
        strict = ahead > 0
        gE = _expand_heads(g4, hol)
        bE = _expand_heads(b4, hol)
        g_hi, g_lo = _split_bf16(gE)
        m_hi, m_lo = _split_bf16(jnp.where((col - row) * sgn >= 0, gE, 0.0))
        both = _dot(cum_lhs[reverse], jnp.concatenate([g_hi, g_lo, m_hi, m_lo], axis=0))
        gc = both[:C]
        gc_row = both[C:]
        decay = jnp.where(incl, jnp.exp(gc - gc_row), 0.0)
        g_last = gc[0:1, :] if reverse else gc[C - 1:C, :]
        eg = jnp.exp(gc)
        kb = k4 * bE
        st.append(dict(q=q4, k=k4, s=s4, incl=incl, strict=strict, decay=decay, g_last=g_last,
                       kb=kb, vb=v4 * bE, kbg=kb * eg, qh=q4 * eg, kt=k4 * jnp.exp(g_last - gc)))

    for d in st:
        gram = _dot_nt(jnp.concatenate([d["kb"], d["q"]], axis=0).astype(BF16), bd(d["k"].astype(BF16)))
        L = jnp.where(d["strict"], gram[:C] * d["decay"], 0.0)
        d["A"] = gram[C:] * d["decay"]
        d["T"] = eye - L
        d["P"] = L.astype(BF16)
    n_factors = int(math.log2(C)) - 1
    for d in st:
        d["P"] = _dot(d["P"], bd(d["P"])).astype(BF16)
    for it in range(n_factors):
        last = it == n_factors - 1
        for d in st:
            lhs = d["T"].astype(BF16) if last else jnp.concatenate([d["T"].astype(BF16), d["P"]], axis=0)
            prod = _dot(lhs, bd(d["P"]))
            d["T"] = d["T"] + prod[:C]
            if not last:
                d["P"] = prod[C:].astype(BF16)
    for d in st:
        Tb = d["T"].astype(BF16)
        d["u"] = _dot(Tb, bd(d["vb"].astype(BF16)))
        d["w"] = _dot(Tb, bd(d["kbg"].astype(BF16)))
    for d in st:
        d["ws_qs"] = _dot(jnp.concatenate([d["w"], d["qh"]], axis=0).astype(BF16), bd(d["s"].astype(BF16)))
    out = []
    for d in st:
        v_new = d["u"] - d["ws_qs"][:C]
        vb16 = v_new.astype(BF16)
        o = d["ws_qs"][C:] + _dot(d["A"].astype(BF16), bd(vb16))
        full = _dot_tn(d["kt"].astype(BF16), vb16)
        upd = jnp.zeros((GDN_DK, GDN_LANES), F32)
        for h in range(GDN_GROUP):
            upd = upd + jnp.where(hol == h, full[h * GDN_DK:(h + 1) * GDN_DK, :], 0.0)
        out.append((o, d["s"] * jnp.exp(d["g_last"]) + upd))
    return out


def _gdn_kernel(qf_ref, kf_ref, vf_ref, gf_ref, qb_ref, kb_ref, vb_ref, gb_ref, al_ref, dtb_ref,
                of_ref, ob_ref, s_ref):
    @pl.when(pl.program_id(1) == 0)
    def _():
        s_ref[...] = jnp.zeros_like(s_ref)

    n_grp = GDN_HEADS // GDN_GROUP
    n_gate = 2 * GDN_HEADS
    items = []
    slots = []
    for nb in range(GDN_NB):
        for d, (q_ref, k_ref, v_ref, g_ref, o_ref) in enumerate(((qf_ref, kf_ref, vf_ref, gf_ref, of_ref),
                                                                 (qb_ref, kb_ref, vb_ref, gb_ref, ob_ref))):
            raw = g_ref[nb]
            xs = raw + dtb_ref[...]
            softplus = jnp.maximum(xs, 0.0) + jnp.log(1.0 + jnp.exp(-jnp.abs(xs)))
            g = -jnp.exp(al_ref[...]) * softplus
            beta = jax.nn.sigmoid(raw)
            for grp in range(n_grp):
                ls = slice(grp * GDN_LANES, (grp + 1) * GDN_LANES)
                c0 = d * GDN_HEADS + grp * GDN_GROUP
                slot = (nb * 2 + d) * n_grp + grp
                items.append((q_ref[nb, :, ls], k_ref[nb, :, ls], v_ref[nb, :, ls],
                              g[:, c0:c0 + GDN_GROUP], beta[:, n_gate + c0:n_gate + c0 + GDN_GROUP],
                              s_ref[slot], d == 1))
                slots.append((o_ref, nb, ls, slot))
    for (o_ref, nb, ls, slot), (o, s_new) in zip(slots, _gdn_groups(items)):
        o_ref[nb, :, ls] = o
        s_ref[slot] = s_new


def _gdn(q, k, v, gate, a_log_row, dt_bias_row, n_lat_chunks):
    B, S, HD = q.shape
    n_chunks = S // GDN_CHUNK
    n_ctx = n_chunks - n_lat_chunks

    def fwd_c(s):
        return jnp.where(s < n_ctx, n_lat_chunks + s, s - n_ctx)

    def bwd_c(s):
        return n_chunks - 1 - s

    def spec(cmap, width):
        return pl.BlockSpec((GDN_NB, GDN_CHUNK, width), lambda b, s: (b, cmap(s), 0))

    return pl.pallas_call(
        _gdn_kernel,
        grid=(B // GDN_NB, n_chunks),
        in_specs=[spec(fwd_c, HD), spec(fwd_c, HD), spec(fwd_c, HD), spec(fwd_c, LANES),
                  spec(bwd_c, HD), spec(bwd_c, HD), spec(bwd_c, HD), spec(bwd_c, LANES),
                  _const_spec((1, LANES)), _const_spec((1, LANES))],
        out_specs=[spec(fwd_c, HD), spec(bwd_c, HD)],
        out_shape=[jax.ShapeDtypeStruct((B, S, HD), F32)] * 2,
        scratch_shapes=[pltpu.VMEM((GDN_NB * 2 * GDN_HEADS // GDN_GROUP, GDN_DK, GDN_LANES), F32)],
        compiler_params=_params("parallel", "arbitrary"),
        name="gdn_scan",
    )(q, k, v, gate, q, k, v, gate, a_log_row, dt_bias_row)


def _swa_kernel(q_ref, k_ref, v_ref, sink_ref, o_ref, *, n_lat_blocks):
    W = SWA_BLOCK
    H = SWA_Q_HEADS
    n = pl.program_id(1)
    is_lat = n < n_lat_blocks
    nl = jnp.minimum(n, n_lat_blocks - 1)
    prev = jnp.maximum(nl - 1, 0)
    nxt = jnp.minimum(nl + 1, n_lat_blocks - 1)
    ctx0 = n_lat_blocks * W

    def keys(ref):
        def blk(i):
            return ref[0, pl.ds(pl.multiple_of(i * W, W), W), :]
        return jnp.concatenate([ref[0, ctx0:ctx0 + CTX_LEN, :], blk(prev), blk(nl), blk(nxt)], axis=0)

    k_all = keys(k_ref)
    v_all = keys(v_ref)
    nk = CTX_LEN + 3 * W
    v_ext = jnp.concatenate([v_all, jnp.ones((nk, LANES), BF16)], axis=1)
    hp = SWA_HEADS_PER_SUB
    ii = lax.broadcasted_iota(jnp.int32, (hp * W, nk), 0) % W
    jj = lax.broadcasted_iota(jnp.int32, (hp * W, nk), 1) - CTX_LEN
    in_window = (((jj >= 0) & (jj < W) & (jj >= ii) & (nl > 0))
                 | ((jj >= W) & (jj < 2 * W))
                 | ((jj >= 2 * W) & (jj - 2 * W <= ii) & (nl < n_lat_blocks - 1)))
    valid = (jj < 0) | (in_window & is_lat)

    def scores(i):
        q = jnp.concatenate([q_ref[0, :, h * LANES:(h + 1) * LANES] for h in range(i * hp, (i + 1) * hp)], axis=0)
        return jnp.where(valid, _dot_nt(q, k_all), NEG_BIG)

    def finish(i, s):
        sk = sink_ref[i * hp * W:(i + 1) * hp * W, :]
        m = jnp.maximum(jnp.max(s, axis=-1, keepdims=True), sk)
        acc = _dot(jnp.exp(s - m).astype(BF16), v_ext)
        o = acc[:, :LANES] / (acc[:, LANES:LANES + 1] + jnp.exp(sk - m))
        for j in range(hp):
            h = i * hp + j
            o_ref[0, :, h * LANES:(h + 1) * LANES] = o[j * W:(j + 1) * W].astype(o_ref.dtype)

    _one_ahead(H // hp, scores, finish)


def _swa(q_pad, k, v, sink_rows, n_lat_blocks):
    assert SWA_WINDOW == SWA_BLOCK
    B, S, _ = k.shape
    W = SWA_BLOCK
    QW = SWA_Q_HEADS * LANES
    return pl.pallas_call(
        functools.partial(_swa_kernel, n_lat_blocks=n_lat_blocks),
        grid=(B, S // W),
        in_specs=[pl.BlockSpec((1, W, QW), lambda b, n: (b, n, 0)),
                  pl.BlockSpec((1, S, LANES), lambda b, n: (b, 0, 0)),
                  pl.BlockSpec((1, S, LANES), lambda b, n: (b, 0, 0)),
                  _const_spec(sink_rows.shape)],
        out_specs=pl.BlockSpec((1, W, QW), lambda b, n: (b, n, 0)),
        out_shape=jax.ShapeDtypeStruct((B, S, QW), BF16),
        compiler_params=_params("parallel", "parallel"),
        name="swa",
    )(q_pad, k, v, sink_rows)


def _even_tail_kernel(of_ref, ob_ref, z_ref, swa_ref, x_ref, c_ref, mod_ref, norms_ref, gn_ref, bd_ref,
                      wo_g_ref, wo_s_ref, wg_ref, wu_ref, wd_ref, o_ref, *, n_lat_tiles):
    o = of_ref[0] + ob_ref[0]
    ms = _dot_split(o * o, bd_ref[...]) * (1.0 / GDN_DV)
    gdn = o * lax.rsqrt(ms + NORM_EPS) * gn_ref[...] * _silu(z_ref[0].astype(F32))
    y = _dot(gdn.astype(BF16), wo_g_ref[...]) + _dot(swa_ref[0], wo_s_ref[...])
    x1 = _stream_tile(x_ref, c_ref, n_lat_tiles) + mod_ref[0, 0, 2:3, :] * _rms(y, norms_ref[1:2, :])
    f = (_rms(x1, norms_ref[2:3, :]) * (1.0 + mod_ref[0, 0, 4:5, :]) + mod_ref[0, 0, 3:4, :]).astype(BF16)
    h = _silu(_dot(f, wg_ref[...])) * _dot(f, wu_ref[...])
    y2 = _dot(h.astype(BF16), wd_ref[...])
    o_ref[0] = x1 + mod_ref[0, 0, 5:6, :] * _rms(y2, norms_ref[3:4, :])


def _resident_spec(shape):
    return pl.BlockSpec(shape, lambda b, t: (0,) * len(shape), pipeline_mode=pl.Buffered(1))


def _even_tail(o_f, o_b, z, swa, x, ctx, mod, norms, gdn_norm_row, head_ones, wo_gdn, wo_swa, wg, wu, wd,
               n_lat_tiles):
    B, S, _ = o_f.shape
    D = x.shape[-1]
    consts = (norms, gdn_norm_row, head_ones, wo_gdn, wo_swa, wg, wu, wd)
    return pl.pallas_call(
        functools.partial(_even_tail_kernel, n_lat_tiles=n_lat_tiles),
        grid=(B, S // ROW_TILE),
        in_specs=[_row_spec(GDN_HD), _row_spec(GDN_HD), _row_spec(GDN_HD), _row_spec(swa.shape[-1])]
        + _stream_in_specs(n_lat_tiles) + [_mod_spec(n_lat_tiles)] + [_resident_spec(a.shape) for a in consts],
        out_specs=_row_spec(D),
        out_shape=jax.ShapeDtypeStruct((B, S, D), F32),
        compiler_params=_params("parallel", "parallel"),
        name="even_tail",
    )(o_f, o_b, z, swa, x, ctx, mod, *consts)


O_CQ = (0, MLA_Q_RANK)
O_CKV = (O_CQ[1], O_CQ[1] + MLA_KV_RANK)
O_KR = (O_CKV[1], O_CKV[1] + LANES)
O_DQ = (O_KR[1], O_KR[1] + DIFF_HEADS * LANES)
O_DK = (O_DQ[1], O_DQ[1] + DIFF_HEADS * LANES)
O_DV = (O_DK[1], O_DK[1] + DIFF_HEADS * LANES)


def _odd_proj_kernel(x_ref, mod_ref, nw_ref, w_ref, cos_ref, sin_ref, cosr_ref, sinr_ref,
                     qn_ref, kvn_ref, wq_ref, wk_ref, wv_ref,
                     q_ref, k_ref, v_ref, dq_ref, dk_ref, dv_ref):
    a = (_rms(x_ref[0], nw_ref[...]) * (1.0 + mod_ref[0, 0, 1:2, :]) + mod_ref[0, 0, 0:1, :]).astype(BF16)

    def proj(cols):
        return _dot(a, w_ref[:, cols[0]:cols[1]])

    cos_r = cosr_ref[...]
    sin_r = sinr_ref[...]
    rq = MLA_ROPE // 4
    cq = _rms(proj(O_CQ), qn_ref[...]).astype(BF16)
    ckv = _rms(proj(O_CKV), kvn_ref[...]).astype(BF16)
    kr = _rope_slab(proj(O_KR), cos_r, sin_r, rq)
    qp = _dot(cq, wq_ref[...])
    kp = _dot(ckv, wk_ref[...])
    vp = _dot(ckv, wv_ref[...])
    lane = lax.broadcasted_iota(jnp.int32, (ROW_TILE, LANES), 1)
    ones_hi = jnp.where(lane >= MLA_V, 1.0, 0.0)
    for h in range(MLA_HEADS):
        ls = slice(h * LANES, (h + 1) * LANES)
        q_ref[0, :, ls] = _rope_slab(qp[:, ls], cos_r, sin_r, rq).astype(BF16)
        k_ref[0, :, ls] = (kp[:, ls] + kr).astype(BF16)
        v_ref[0, :, ls] = (vp[:, ls] + ones_hi).astype(BF16)
    cos = cos_ref[...]
    sin = sin_ref[...]
    quarter = DIFF_HEAD_DIM // 4
    dq = proj(O_DQ)
    dk = proj(O_DK)
    dv = proj(O_DV)
    ones = jnp.ones((ROW_TILE, LANES), BF16)
    for h in range(DIFF_HEADS):
        ls = slice(h * LANES, (h + 1) * LANES)
        dq_ref[0, :, ls] = _rope_slab(dq[:, ls], cos, sin, quarter).astype(BF16)
        dk_ref[0, :, ls] = _rope_slab(dk[:, ls], cos, sin, quarter).astype(BF16)
        dv_ref[0, :, 2 * h * LANES:(2 * h + 1) * LANES] = dv[:, ls].astype(BF16)
        dv_ref[0, :, (2 * h + 1) * LANES:(2 * h + 2) * LANES] = ones


def _odd_proj(xa, mod, norm_w, w, cos, sin, cos_r, sin_r, q_norm, kv_norm, wq, wk, wv, n_lat_tiles):
    B, S, D = xa.shape
    HW = MLA_HEADS * LANES
    widths = (HW, HW, HW, DIFF_HEADS * LANES, DIFF_HEADS * LANES, 2 * DIFF_HEADS * LANES)
    return pl.pallas_call(
        _odd_proj_kernel,
        grid=(B, S // ROW_TILE),
        in_specs=[_row_spec(D), _mod_spec(n_lat_tiles), _const_spec((1, D)), _const_spec(w.shape),
                  _table_spec(), _table_spec(), _table_spec(), _table_spec(),
                  _const_spec(q_norm.shape), _const_spec(kv_norm.shape),
                  _const_spec(wq.shape), _const_spec(wk.shape), _const_spec(wv.shape)],
        out_specs=[_row_spec(n) for n in widths],
        out_shape=[jax.ShapeDtypeStruct((B, S, n), BF16) for n in widths],
        compiler_params=_params("parallel", "parallel"),
        name="odd_proj",
    )(xa, mod, norm_w, w, cos, sin, cos_r, sin_r, q_norm, kv_norm, wq, wk, wv)


def _one_ahead(n_sub, scores, finish):
    s_prev = scores(0)
    for i in range(1, n_sub):
        s_next = scores(i)
        finish(i - 1, s_prev)
        s_prev = s_next
    finish(n_sub - 1, s_prev)


def _mla_kernel(q_ref, k_ref, v_ref, o_ref):
    k = k_ref[0]
    v = v_ref[0]

    def scores(i):
        return _dot_nt(q_ref[0, i * ATTN_SUB:(i + 1) * ATTN_SUB, :], k)

    def finish(i, s):
        m = jnp.max(s, axis=-1, keepdims=True)
        acc = _dot(jnp.exp(s - m).astype(BF16), v)
        o_ref[0, i * ATTN_SUB:(i + 1) * ATTN_SUB, :] = (acc / acc[:, MLA_V:MLA_V + 1]).astype(o_ref.dtype)

    _one_ahead(q_ref.shape[1] // ATTN_SUB, scores, finish)


def _mla(q, k, v_ext, T):
    B, S, _ = k.shape
    return pl.pallas_call(
        _mla_kernel,
        grid=(B, MLA_HEADS, T // ATTN_TQ),
        in_specs=[pl.BlockSpec((1, ATTN_TQ, LANES), lambda b, h, t: (b, t, h)),
                  pl.BlockSpec((1, S, LANES), lambda b, h, t: (b, 0, h)),
                  pl.BlockSpec((1, S, LANES), lambda b, h, t: (b, 0, h))],
        out_specs=pl.BlockSpec((1, ATTN_TQ, LANES), lambda b, h, t: (b, t, h)),
        out_shape=jax.ShapeDtypeStruct((B, T, MLA_HEADS * LANES), BF16),
        compiler_params=_params("parallel", "parallel", "parallel"),
        name="mla_attn",
    )(q, k, v_ext)


def _diff_kernel(q_ref, k_ref, v_ref, lam_ref, sub_ref, o_ref, *, post_scale):
    k = k_ref[0]
    v = v_ref[0]
    sub = ATTN_SUB // 2
    vw = 2 * DIFF_HEAD_DIM
    lo = lax.broadcasted_iota(jnp.int32, (sub, LANES), 1) < DIFF_HEAD_DIM
    zero = jnp.zeros((sub, LANES), BF16)

    def scores(i):
        q = q_ref[0, i * sub:(i + 1) * sub, :]
        return _dot_nt(jnp.concatenate([jnp.where(lo, q, zero), jnp.where(lo, zero, q)], axis=0), k)

    def finish(i, s):
        m = jnp.max(s, axis=-1, keepdims=True)
        acc = _dot(jnp.exp(s - m).astype(BF16), v)
        att = acc[:, :vw] / acc[:, vw:vw + 1]
        a = att[:sub] - lam_ref[...] * att[sub:]
        o_ref[0, i * sub:(i + 1) * sub, :] = (_rms(a, sub_ref[...]) * post_scale).astype(o_ref.dtype)

    _one_ahead(q_ref.shape[1] // sub, scores, finish)


def _diff(q, k, v_ext, lam, subln, post_scale, T):
    B, S, _ = k.shape
    tq = ATTN_TQ // 2
    return pl.pallas_call(
        functools.partial(_diff_kernel, post_scale=post_scale),
        grid=(B, DIFF_HEADS, T // tq),
        in_specs=[pl.BlockSpec((1, tq, LANES), lambda b, h, t: (b, t, h)),
                  pl.BlockSpec((1, S, LANES), lambda b, h, t: (b, 0, h)),
                  pl.BlockSpec((1, S, 2 * LANES), lambda b, h, t: (b, 0, h)),
                  pl.BlockSpec((1, 1), lambda b, h, t: (0, 0)),
                  pl.BlockSpec((1, LANES), lambda b, h, t: (0, 0))],
        out_specs=pl.BlockSpec((1, tq, LANES), lambda b, h, t: (b, t, h)),
        out_shape=jax.ShapeDtypeStruct((B, T, DIFF_HEADS * LANES), BF16),
        compiler_params=_params("parallel", "parallel", "parallel"),
        name="diff_attn",
    )(q, k, v_ext, lam, subln)


def _mix_out1_kernel(mla_ref, diff_ref, x_ref, mod_ref, n1_ref, n2_ref, w1_ref, w2_ref, wrh_ref, wrl_ref,
                     xo_ref, f_ref, lg_ref):
    y = _dot(mla_ref[0], w1_ref[...]) + _dot(diff_ref[0], w2_ref[...])
    x = x_ref[0] + mod_ref[0, 0, 2:3, :] * _rms(y, n1_ref[...])
    xo_ref[0] = x
    f = _rms(x, n2_ref[...]) * (1.0 + mod_ref[0, 0, 4:5, :]) + mod_ref[0, 0, 3:4, :]
    f_ref[0] = f
    f_hi, f_lo = _split_bf16(f)
    lg_ref[0] = _dot(f_hi, wrh_ref[...]) + (_dot(f_hi, wrl_ref[...]) + _dot(f_lo, wrh_ref[...]))


def _mix_out1(mla, diff, xa, mod, n1, n2, w1, w2, wr, n_lat_tiles):
    B, T, _ = mla.shape
    D = xa.shape[-1]
    wr_hi, wr_lo = _split_bf16(wr)
    return pl.pallas_call(
        _mix_out1_kernel,
        grid=(B, T // ROW_TILE),
        in_specs=[_row_spec(mla.shape[-1]), _row_spec(diff.shape[-1]), _row_spec(D), _mod_spec(n_lat_tiles),
                  _const_spec((1, D)), _const_spec((1, D)), _const_spec(w1.shape), _const_spec(w2.shape),
                  _const_spec(wr.shape), _const_spec(wr.shape)],
        out_specs=[_row_spec(D), _row_spec(D), _row_spec(LANES)],
        out_shape=[jax.ShapeDtypeStruct((B, T, D), F32), jax.ShapeDtypeStruct((B, T, D), F32),
                   jax.ShapeDtypeStruct((B, T, LANES), F32)],
        compiler_params=_params("parallel", "parallel"),
        name="mix_out1",
    )(mla, diff, xa, mod, n1, n2, w1, w2, wr_hi, wr_lo)


MOE_TM = 1024
MOE_TF = 512
MOE_SUB = 256


def _moe_kernel(te_ref, tr_ref, x_ref, wg_ref, wu_ref, wd_ref, o_ref, acc_ref):
    i = pl.program_id(0)
    f = pl.program_id(1)
    n_rows = tr_ref[i]
    n_sub = MOE_TM // MOE_SUB

    @pl.when(f == 0)
    def _():
        acc_ref[...] = jnp.zeros_like(acc_ref)

    def sub_tile_fns():
        wg = wg_ref[0].astype(BF16)
        wu = wu_ref[0].astype(BF16)
        wd = wd_ref[0].astype(BF16)

        def gate_up(j):
            x = x_ref[j * MOE_SUB:(j + 1) * MOE_SUB, :].astype(BF16)
            return _dot(x, wg), _dot(x, wu)

        def down(j, gu):
            h = (_silu(gu[0]) * gu[1]).astype(BF16)
            acc_ref[j * MOE_SUB:(j + 1) * MOE_SUB, :] += _dot(h, wd)

        return gate_up, down

    all_subs = n_rows > (n_sub - 1) * MOE_SUB

    @pl.when(all_subs)
    def _():
        _one_ahead(n_sub, *sub_tile_fns())

    for j in range(n_sub - 1):
        @pl.when(jnp.logical_and(jnp.logical_not(all_subs), n_rows > j * MOE_SUB))
        def _(j=j):
            gate_up, down = sub_tile_fns()
            down(j, gate_up(j))

    @pl.when(f == pl.num_programs(1) - 1)
    def _():
        o_ref[...] = acc_ref[...]


def _moe_experts(tile_expert, tile_rows, x_sorted, wg, wu, wd):
    R, D = x_sorted.shape
    E, _, F = wg.shape
    nf = F // MOE_TF

    def f_eff(i, f, ta):
        return jnp.where(ta[i] > 0, f, nf - 1)

    grid_spec = pltpu.PrefetchScalarGridSpec(
        num_scalar_prefetch=2,
        grid=(R // MOE_TM, nf),
        in_specs=[pl.BlockSpec((MOE_TM, D), lambda i, f, te, ta: (i, 0)),
                  pl.BlockSpec((1, D, MOE_TF), lambda i, f, te, ta: (te[i], 0, f_eff(i, f, ta))),
                  pl.BlockSpec((1, D, MOE_TF), lambda i, f, te, ta: (te[i], 0, f_eff(i, f, ta))),
                  pl.BlockSpec((1, MOE_TF, D), lambda i, f, te, ta: (te[i], f_eff(i, f, ta), 0))],
        out_specs=pl.BlockSpec((MOE_TM, D), lambda i, f, te, ta: (i, 0)),
        scratch_shapes=[pltpu.VMEM((MOE_TM, D), F32)],
    )
    return pl.pallas_call(
        _moe_kernel,
        grid_spec=grid_spec,
        out_shape=jax.ShapeDtypeStruct((R, D), F32),
        compiler_params=_params("arbitrary", "arbitrary"),
        name="moe_experts",
    )(tile_expert, tile_rows, x_sorted, wg, wu, wd)


def _moe_combine_kernel(a_ref, b_ref, g_ref, x_ref, mod_ref, nw_ref, o_ref):
    g = g_ref[0]
    y = g[:, 0:1] * a_ref[0] + g[:, 1:2] * b_ref[0]
    o_ref[0] = x_ref[0] + mod_ref[0, 0, 5:6, :] * _rms(y, nw_ref[...])


def _moe_combine(a, b, gates, x, mod, norm_w, n_lat_tiles):
    B, T, D = x.shape
    return pl.pallas_call(
        _moe_combine_kernel,
        grid=(B, T // LAT_TILE),
        in_specs=[_row_spec(D, LAT_TILE), _row_spec(D, LAT_TILE), _row_spec(TOP_K, LAT_TILE), _row_spec(D, LAT_TILE),
                  _mod_spec(n_lat_tiles), _const_spec((1, D))],
        out_specs=_row_spec(D, LAT_TILE),
        out_shape=jax.ShapeDtypeStruct((B, T, D), F32),
        compiler_params=_params("parallel", "parallel"),
        name="moe_combine",
    )(a, b, gates, x, mod, norm_w)


def _route(logits):
    N = logits.shape[0]
    M = N * TOP_K
    R = M + N_EXPERTS * MOE_TM
    top_val, top_idx = lax.top_k(logits, TOP_K)
    gates = jax.nn.softmax(top_val, axis=-1)
    e_flat = top_idx.T.reshape(-1)
    onehot = (e_flat[:, None] == jnp.arange(N_EXPERTS)[None, :]).astype(jnp.int32)
    counts = jnp.sum(onehot, axis=0)
    padded = ((counts + MOE_TM - 1) // MOE_TM) * MOE_TM
    ends = jnp.cumsum(padded)
    starts = ends - padded
    ustarts = jnp.cumsum(counts) - counts
    order = jnp.argsort(e_flat, stable=True).astype(jnp.int32)
    rank = jnp.argsort(order).astype(jnp.int32)
    dest = rank + jnp.sum(onehot * (starts - ustarts)[None, :], axis=-1)
    n_tiles = R // MOE_TM
    tile_start = jnp.arange(n_tiles, dtype=jnp.int32) * MOE_TM
    tile_active = (tile_start < ends[-1]).astype(jnp.int32)
    te = jnp.minimum(jnp.searchsorted(ends, tile_start, side="right"), N_EXPERTS - 1).astype(jnp.int32)
    last_e = te[jnp.maximum(ends[-1] // MOE_TM - 1, 0)]
    tile_expert = jnp.where(tile_active > 0, te, last_e)
    in_tile = jnp.arange(MOE_TM, dtype=jnp.int32)[None, :]
    within = (tile_start - starts[tile_expert])[:, None] + in_tile
    valid = (within < counts[tile_expert][:, None]) & (tile_active[:, None] > 0)
    src = order[jnp.clip(ustarts[tile_expert][:, None] + within, 0, M - 1).reshape(R)]
    valid = valid.reshape(R)
    tile_rows = jnp.clip(counts[tile_expert] - (tile_start - starts[tile_expert]), 0, MOE_TM) * tile_active
    row_token = jnp.where(valid, src, jnp.arange(R, dtype=jnp.int32)) % N
    return gates, dest.reshape(TOP_K, N), row_token, tile_expert, tile_rows.astype(jnp.int32)


def _rope_tables(rows, rot_dim):
    row = jnp.repeat(jnp.arange(rows, dtype=F32), GRID_W)
    col = jnp.tile(jnp.arange(GRID_W, dtype=F32), rows)
    half = rot_dim // 2
    inv_freq = ROPE_THETA ** (-jnp.arange(0, half, 2, dtype=F32) / half)
    ang_r = row[:, None] * inv_freq[None, :]
    ang_c = col[:, None] * inv_freq[None, :]
    ang = jnp.concatenate([ang_r, ang_r, ang_c, ang_c], axis=-1)
    return jnp.cos(ang), jnp.sin(ang)


def _stream_tables(cos, sin, lane0):
    T, R = cos.shape
    if lane0 == 0:
        cos_l, sin_l = jnp.tile(cos, (1, LANES // R)), jnp.tile(sin, (1, LANES // R))
    else:
        pad = ((0, 0), (lane0, LANES - lane0 - R))
        cos_l = jnp.pad(cos - 1.0, pad) + 1.0
        sin_l = jnp.pad(sin, pad)
    ctx = ((0, CTX_LEN), (0, 0))
    return jnp.pad(cos_l - 1.0, ctx) + 1.0, jnp.pad(sin_l, ctx)


def _cond_proj_kernel(c_ref, w_ref, b_ref, o_ref):
    c = c_ref[...]
    c_hi, c_lo = _split_bf16(_silu(c))
    w_hi, w_lo = _split_bf16(w_ref[...])
    o_ref[...] = _dot(c_hi, w_hi) + (_dot(c_hi, w_lo) + _dot(c_lo, w_hi)) + b_ref[...]


def _cond_proj(cond, w, b):
    R, D = cond.shape
    N = w.shape[1]
    return pl.pallas_call(
        _cond_proj_kernel,
        grid=(N // D,),
        in_specs=[pl.BlockSpec((R, D), lambda j: (0, 0)), pl.BlockSpec((D, D), lambda j: (0, j)),
                  pl.BlockSpec((1, D), lambda j: (0, j))],
        out_specs=pl.BlockSpec((R, D), lambda j: (0, j)),
        out_shape=jax.ShapeDtypeStruct((R, N), F32),
        compiler_params=_params("parallel"),
        name="cond_proj",
    )(cond, w, b)


def _mod_table(c, c_ctx, mod_w, mod_b):
    B = c.shape[0]
    cond = jnp.concatenate([c, c_ctx[None, :]], axis=0)
    cond = jnp.pad(cond, ((0, SUBLANES - (B + 1) % SUBLANES), (0, 0)))
    m = _cond_proj(cond, mod_w, mod_b[None, :])[:B + 1].reshape(B + 1, 1, 6, D_MODEL)
    return jnp.concatenate([jnp.broadcast_to(m[B:], (B, 1, 6, D_MODEL)), m[:B]], axis=1)


def _pad_cols(w, n):
    return jnp.pad(w, ((0, 0), (0, n - w.shape[1])))


def _even_layer(x, ctx, mod, norms, w_in, conv_w, a_log, dt_bias, gdn_norm, sink, w_out, ffn_gate, ffn_up,
                ffn_down, cos, sin, n_lat_tiles):
    B, T, D = x.shape
    n_gate = 2 * GDN_HEADS
    c_qkv = 3 * GDN_HD
    c_z = c_qkv + GDN_HD
    c_ga = c_z + n_gate
    c_gb = c_ga + n_gate
    c_sq = c_gb + SWA_Q_HEADS * SWA_HEAD_DIM
    c_sk = c_sq + SWA_KV_HEADS * SWA_HEAD_DIM
    grp = SWA_Q_HEADS // SWA_KV_HEADS
    wq = w_in[:, c_gb:c_sq].reshape(D, SWA_Q_HEADS, SWA_HEAD_DIM) * SWA_HEAD_DIM ** -0.5
    kv_of_head = (jnp.arange(SWA_Q_HEADS) // grp)[None, :, None]
    wq = jnp.concatenate([jnp.where(kv_of_head == 0, wq, 0.0), jnp.where(kv_of_head == 1, wq, 0.0)], axis=-1)
    w_all = jnp.concatenate([w_in[:, :c_z], _pad_cols(w_in[:, c_z:c_gb], LANES),
                             wq.reshape(D, SWA_Q_HEADS * LANES), w_in[:, c_sq:]], axis=1).astype(BF16)
    head_ones = (jnp.arange(GDN_HD)[:, None] // GDN_DK == jnp.arange(GDN_HD)[None, :] // GDN_DK).astype(BF16)
    conv_w8 = jnp.pad(conv_w, ((0, SUBLANES - CONV_K), (0, 0)))
    gq, gk, gv, z, gate, q_pad, k, v = _even_proj(x, ctx, mod, norms[0:1], w_all, cos, sin, conv_w8, head_ones,
                                                  n_lat_tiles)
    a_log_row = _pad_cols(a_log.reshape(1, n_gate), LANES)
    dt_bias_row = _pad_cols(dt_bias.reshape(1, n_gate), LANES)
    o_f, o_b = _gdn(gq, gk, gv, gate, a_log_row, dt_bias_row, T // GDN_CHUNK)

    sink_rows = jnp.repeat(sink, SWA_BLOCK)[:, None]
    swa = _swa(q_pad, k, v, sink_rows, T // SWA_BLOCK)

    w_s = w_out[GDN_HD:].reshape(SWA_Q_HEADS, SWA_HEAD_DIM, D)
    w_s = jnp.concatenate([jnp.where(kv_of_head.reshape(-1, 1, 1) == 0, w_s, 0.0),
                           jnp.where(kv_of_head.reshape(-1, 1, 1) == 1, w_s, 0.0)], axis=1)
    gdn_norm_row = jnp.tile(gdn_norm, GDN_HEADS)[None, :]
    return _even_tail(o_f, o_b, z, swa, x, ctx, mod, norms, gdn_norm_row, head_ones,
                      w_out[:GDN_HD].astype(BF16), w_s.reshape(SWA_Q_HEADS * LANES, D).astype(BF16),
                      ffn_gate.astype(BF16), ffn_up.astype(BF16), ffn_down.astype(BF16), n_lat_tiles)


def _odd_layer_last(xa, mod, norms, w_in, q_norm, kv_norm, w_uq, w_ukv, lam_p, lam_init, subln, w_out,
                    router, exp_gate, exp_up, exp_down, cos, sin, cos_r, sin_r, n_lat_tiles):
    B, S, D = xa.shape
    T = n_lat_tiles * ROW_TILE
    H = MLA_HEADS
    c0 = MLA_Q_RANK
    c1 = c0 + MLA_KV_RANK
    c2 = c1 + MLA_ROPE
    dw = DIFF_HEADS * 2 * DIFF_HEAD_DIM
    w_kr = jnp.pad(w_in[:, c1:c2], ((0, 0), (MLA_NOPE, LANES - MLA_NOPE - MLA_ROPE)))
    w_all = jnp.concatenate([w_in[:, :c1], w_kr, w_in[:, c2:c2 + dw] * DIFF_HEAD_DIM ** -0.5,
                             w_in[:, c2 + dw:]], axis=1).astype(BF16)
    qd = MLA_NOPE + MLA_ROPE
    wq = jnp.pad(w_uq.reshape(MLA_Q_RANK, H, qd) * qd ** -0.5, ((0, 0), (0, 0), (0, LANES - qd)))
    wkv = w_ukv.reshape(MLA_KV_RANK, H, MLA_NOPE + MLA_V)
    wk = jnp.pad(wkv[..., :MLA_NOPE], ((0, 0), (0, 0), (0, LANES - MLA_NOPE)))
    wv = jnp.pad(wkv[..., MLA_NOPE:], ((0, 0), (0, 0), (0, LANES - MLA_V)))
    q_cat, k_cat, v_ext, dq, dk, dv = _odd_proj(
        xa, mod, norms[0:1], w_all, cos, sin, cos_r, sin_r, q_norm[None, :], kv_norm[None, :],
        wq.reshape(MLA_Q_RANK, H * LANES).astype(BF16), wk.reshape(MLA_KV_RANK, H * LANES).astype(BF16),
        wv.reshape(MLA_KV_RANK, H * LANES).astype(BF16), n_lat_tiles)
    mla = _mla(q_cat, k_cat, v_ext, T)

    lam = (jnp.exp(jnp.sum(lam_p[0] * lam_p[1])) - jnp.exp(jnp.sum(lam_p[2] * lam_p[3])) + lam_init).reshape(1, 1)
    diff = _diff(dq, dk, dv, lam, subln[None, :], 1.0 - lam_init, T)

    w_mla = jnp.pad(w_out[:H * MLA_V].reshape(H, MLA_V, D), ((0, 0), (0, LANES - MLA_V), (0, 0)))
    x, f_in, logits = _mix_out1(mla, diff, xa, mod, norms[1:2], norms[2:3],
                                w_mla.reshape(H * LANES, D).astype(BF16), w_out[H * MLA_V:].astype(BF16),
                                _pad_cols(router, LANES), n_lat_tiles)

    gates, dest, row_token, tile_expert, tile_rows = _route(logits.reshape(B * T, LANES)[:, :N_EXPERTS])
    x_sorted = f_in.reshape(B * T, D)[row_token]
    y = _moe_experts(tile_expert, tile_rows, x_sorted, exp_gate, exp_up, exp_down)
    y0 = y[dest[0]].reshape(B, T, D)
    y1 = y[dest[1]].reshape(B, T, D)
    return _moe_combine(y0, y1, gates.reshape(B, T, TOP_K), x, mod, norms[3:4], n_lat_tiles)


def kernel(x, c, ctx, c_ctx, e_mod_w, e_mod_b, e_norms, e_w_in, e_conv_w, e_a_log, e_dt_bias, e_gdn_norm, e_sink, e_w_out, e_ffn_gate, e_ffn_up, e_ffn_down, o_mod_w, o_mod_b, o_norms, o_w_in, o_q_norm, o_kv_norm, o_w_uq, o_w_ukv, o_lambda, o_subln, o_w_out, o_router, o_exp_gate, o_exp_up, o_exp_down):
    B, T, D = x.shape
    assert D == D_MODEL and ctx.shape == (B, CTX_LEN, D) and B % GDN_NB == 0
    assert T % ATTN_TQ == 0 and T % GRID_W == 0 and (B * T * TOP_K) % MOE_TM == 0
    n_lat_tiles = T // ROW_TILE
    rows = T // GRID_W
    cos64, sin64 = _stream_tables(*_rope_tables(rows, SWA_HEAD_DIM), 0)
    cos_r, sin_r = _stream_tables(*_rope_tables(rows, MLA_ROPE), MLA_NOPE)
    mod_e = _mod_table(c, c_ctx, e_mod_w[0], e_mod_b[0])
    xa = _even_layer(x, ctx, mod_e, e_norms[0], e_w_in[0], e_conv_w[0], e_a_log[0], e_dt_bias[0], e_gdn_norm[0],
                     e_sink[0], e_w_out[0], e_ffn_gate[0], e_ffn_up[0], e_ffn_down[0], cos64, sin64, n_lat_tiles)
    mod_o = _mod_table(c, c_ctx, o_mod_w[0], o_mod_b[0])
    lam_init = 0.8 - 0.6 * math.exp(-0.3 * 1)
    return _odd_layer_last(xa, mod_o, o_norms[0], o_w_in[0], o_q_norm[0], o_kv_norm[0], o_w_uq[0], o_w_ukv[0],
                           o_lambda[0], lam_init, o_subln[0], o_w_out[0], o_router[0], o_exp_gate[0],
                           o_exp_up[0], o_exp_down[0], cos64, sin64, cos_r, sin_r, n_lat_tiles)
```
